```python
import math
import jax, jax.numpy as jnp
from jax import lax
import numpy as np

D_MODEL = 1024
BATCH = 16
SEQ = 2048
DEPTH = 1
DEC_BATCH = 32
DEC_SEQ = 16
PAST_LEN = 2048

CHUNK = 64
N_PREV_CHUNKS = 8
BAND = (N_PREV_CHUNKS + 1) * CHUNK
SSM_WIDTH = D_MODEL // 2
SSM_GROUP = 16
SSM_GROUPS = SSM_WIDTH // SSM_GROUP
SSM_STATE = 64
ATT_WIDTH = D_MODEL - SSM_WIDTH
ATT_HEAD_DIM = 64
ATT_HEADS = ATT_WIDTH // ATT_HEAD_DIM
REL_CLIP = 128
N_MEM = 256
MEM_HEADS = 4
MEM_HEAD_DIM = D_MODEL // MEM_HEADS
N_GROUPS = 4
EXPERTS_PER_GROUP = 8
N_EXPERTS = N_GROUPS * EXPERTS_PER_GROUP
TOP_K = 2
D_FF_EXPERT = D_MODEL // 4
EPS = 1e-6
F32 = jnp.float32
NEG_INF = -1e30

kernel_name = 'hymba_s5_chunkband_hmoe_stream_step'


def rms_norm(x, gain):
    xf = x.astype(F32)
    y = xf * lax.rsqrt(jnp.mean(xf * xf, axis=-1, keepdims=True) + EPS)
    return (y * gain.astype(F32)).astype(x.dtype)


def rel_bias_lookup(rel_bias, dist):
    idx = jnp.clip(dist, -REL_CLIP, REL_CLIP) + REL_CLIP
    return rel_bias.astype(F32)[:, idx]


def ssm_discretise(lam_re, lam_im, log_step, b_re, b_im):
    lam = lax.complex(lam_re.astype(F32), lam_im.astype(F32))
    step = jnp.exp(log_step.astype(F32))[:, None]
    lam_bar = jnp.exp(lam * step)
    b = lax.complex(b_re.astype(F32), b_im.astype(F32))
    b_bar = ((lam_bar - 1.0) / lam)[..., None] * b
    return lam_bar, b_bar


def ssm_scan(u, lam_bar, b_bar, c, d_skip, s0):
    bsz, t = u.shape[:2]
    ug = u.astype(F32).reshape(bsz, t, SSM_GROUPS, SSM_GROUP)
    bu = jnp.einsum('gpc,btgc->btgp', b_bar, ug.astype(jnp.complex64))
    if s0 is not None:
        bu = bu.at[:, 0].add(lam_bar * s0)
    a = jnp.broadcast_to(lam_bar, bu.shape)

    def combine(e1, e2):
        return e1[0] * e2[0], e2[0] * e1[1] + e2[1]

    _, states = lax.associative_scan(combine, (a, bu), axis=1)
    y = jnp.einsum('gcp,btgp->btgc', c, states).real
    y = y + d_skip.astype(F32).reshape(SSM_GROUPS, SSM_GROUP) * ug
    return y.reshape(bsz, t, SSM_WIDTH).astype(u.dtype), states[:, -1]


def parallel_mixer_inputs(x, norm_mix, w_in, att_q_gain, att_k_gain):
    bsz, t = x.shape[:2]
    h = rms_norm(x, norm_mix)
    z = jnp.einsum('btd,de->bte', h, w_in)
    u = z[..., :SSM_WIDTH]
    q, k, v = jnp.split(z[..., SSM_WIDTH:], 3, axis=-1)
    shape = (bsz, t, ATT_HEADS, ATT_HEAD_DIM)
    q = rms_norm(q.reshape(shape), att_q_gain)
    k = rms_norm(k.reshape(shape), att_k_gain)
    return u, q, k, v.reshape(shape)


def parallel_mixer_output(y_ssm, y_att, w_glu, out_norm_ssm, out_norm_att, w_out):
    bsz, t = y_ssm.shape[:2]
    g = jnp.einsum('btc,ce->bte', jax.nn.gelu(y_ssm), w_glu)
    val, gate = jnp.split(g, 2, axis=-1)
    y_s = val * jax.nn.sigmoid(gate)
    cat = jnp.concatenate([rms_norm(y_s, out_norm_ssm),
                           rms_norm(y_att.reshape(bsz, t, ATT_WIDTH), out_norm_att)], axis=-1)
    return jnp.einsum('btc,cd->btd', cat, w_out)


def band_attention_prompt(q, k, v, rel_bias):
    bsz, t = q.shape[:2]
    n_chunks = t // CHUNK
    pad = N_PREV_CHUNKS * CHUNK
    kp = jnp.pad(k, ((0, 0), (pad, 0), (0, 0), (0, 0)))
    vp = jnp.pad(v, ((0, 0), (pad, 0), (0, 0), (0, 0)))
    dist = pad + jnp.arange(CHUNK)[:, None] - jnp.arange(BAND)[None, :]
    bias = rel_bias_lookup(rel_bias, dist)
    scale = ATT_HEAD_DIM ** -0.5

    def one_chunk(args):
        c, qc = args
        kb = lax.dynamic_slice_in_dim(kp, c * CHUNK, BAND, axis=1)
        vb = lax.dynamic_slice_in_dim(vp, c * CHUNK, BAND, axis=1)
        s = jnp.einsum('bqhd,bkhd->bhqk', qc, kb).astype(F32) * scale + bias
        key_pos = (c - N_PREV_CHUNKS) * CHUNK + jnp.arange(BAND)
        s = jnp.where((key_pos >= 0)[None, None, None, :], s, NEG_INF)
        p = jax.nn.softmax(s, axis=-1).astype(vb.dtype)
        return jnp.einsum('bhqk,bkhd->bqhd', p, vb)

    qc_all = q.reshape(bsz, n_chunks, CHUNK, ATT_HEADS, ATT_HEAD_DIM).transpose(1, 0, 2, 3, 4)
    out = lax.map(one_chunk, (jnp.arange(n_chunks), qc_all))
    return out.transpose(1, 0, 2, 3, 4).reshape(bsz, t, ATT_HEADS, ATT_HEAD_DIM)


def band_attention_sample(q, k, v, cache_k, cache_v, rel_bias):
    w = cache_k.shape[1]
    s_len = q.shape[1]
    kk = jnp.concatenate([cache_k.astype(k.dtype), k], axis=1)
    vv = jnp.concatenate([cache_v.astype(v.dtype), v], axis=1)
    dist = (w + jnp.arange(s_len))[:, None] - jnp.arange(w + s_len)[None, :]
    bias = rel_bias_lookup(rel_bias, dist)
    s = jnp.einsum('bqhd,bkhd->bhqk', q, kk).astype(F32) * ATT_HEAD_DIM ** -0.5 + bias
    p = jax.nn.softmax(s, axis=-1).astype(vv.dtype)
    return jnp.einsum('bhqk,bkhd->bqhd', p, vv)


def memory_kv(mem, mem_in_norm, w_mem_k, w_mem_v, mem_k_gain):
    bsz, n = mem.shape[:2]
    m = rms_norm(mem, mem_in_norm)
    shape = (bsz, n, MEM_HEADS, MEM_HEAD_DIM)
    k = rms_norm(jnp.einsum('bnd,de->bne', m, w_mem_k).reshape(shape), mem_k_gain)
    v = jnp.einsum('bnd,de->bne', m, w_mem_v).reshape(shape)
    return k, v


def memory_attend(x, mem_k, mem_v, norm_mem, w_mem_q, mem_q_gain, w_mem_o):
    bsz, t = x.shape[:2]
    h = rms_norm(x, norm_mem)
    q = rms_norm(jnp.einsum('btd,de->bte', h, w_mem_q).reshape(bsz, t, MEM_HEADS, MEM_HEAD_DIM), mem_q_gain)
    s = jnp.einsum('bqhd,bkhd->bhqk', q, mem_k.astype(q.dtype)).astype(F32) * MEM_HEAD_DIM ** -0.5
    p = jax.nn.softmax(s, axis=-1).astype(q.dtype)
    o = jnp.einsum('bhqk,bkhd->bqhd', p, mem_v.astype(q.dtype)).reshape(bsz, t, D_MODEL)
    return jnp.einsum('btd,de->bte', o, w_mem_o)


def hier_moe(x, router_g_w, router_g_b, router_e_w, router_e_b, exp_w_gate, exp_w_up, exp_w_down):
    shp = x.shape
    xt = x.reshape(-1, D_MODEL)
    n_tok = xt.shape[0]
    lg = jnp.einsum('td,dg->tg', xt, router_g_w).astype(F32) + router_g_b.astype(F32)
    pg = jax.nn.softmax(lg, axis=-1)
    _, g_top = lax.top_k(lg, 1)
    g_idx = g_top[:, 0]
    tok = jnp.arange(n_tok)
    p1 = pg[tok, g_idx]
    le = jnp.einsum('td,gde->tge', xt, router_e_w).astype(F32) + router_e_b.astype(F32)
    le_sel = le[tok, g_idx]
    top_v, top_i = lax.top_k(le_sel, TOP_K)
    w_sel = p1[:, None] * jax.nn.softmax(top_v, axis=-1)
    expert_id = g_idx[:, None] * EXPERTS_PER_GROUP + top_i
    gates = jnp.sum(jax.nn.one_hot(expert_id, N_EXPERTS, dtype=F32) * w_sel[..., None], axis=1)

    def body(acc, e):
        wg, wu, wd, ge = e
        h = jax.nn.silu(xt @ wg) * (xt @ wu)
        return acc + ge[:, None] * (h @ wd).astype(F32), None

    acc, _ = lax.scan(body, jnp.zeros(xt.shape, F32), (exp_w_gate, exp_w_up, exp_w_down, gates.T))
    return acc.astype(x.dtype).reshape(shp)


def setup_inputs(seed: int = 0) -> dict:
    key = jax.random.key(seed)
    ks = iter(jax.random.split(key, 64))
    nrm = lambda shape, scale=1.0: scale * jax.random.normal(next(ks), shape, F32)
    gain = lambda shape: 1.0 + 0.02 * jax.random.normal(next(ks), shape, F32)
    att_rows = min(N_PREV_CHUNKS * CHUNK, PAST_LEN)
    n_idx = jnp.arange(SSM_STATE, dtype=F32)
    d_in = D_MODEL ** -0.5
    return {
        'x_prompt': nrm((BATCH, SEQ, D_MODEL)),
        'x_sample': nrm((DEC_BATCH, DEC_SEQ, D_MODEL)),
        'mem_prompt': nrm((BATCH, N_MEM, D_MODEL)),
        'cache_attn_k': nrm((DEPTH, DEC_BATCH, att_rows, ATT_HEADS, ATT_HEAD_DIM)),
        'cache_attn_v': nrm((DEPTH, DEC_BATCH, att_rows, ATT_HEADS, ATT_HEAD_DIM)),
        'state_ssm_re': nrm((DEPTH, DEC_BATCH, SSM_GROUPS, SSM_STATE), 0.5),
        'state_ssm_im': nrm((DEPTH, DEC_BATCH, SSM_GROUPS, SSM_STATE), 0.5),
        'cache_mem_k': nrm((DEPTH, DEC_BATCH, N_MEM, MEM_HEADS, MEM_HEAD_DIM)),
        'cache_mem_v': nrm((DEPTH, DEC_BATCH, N_MEM, MEM_HEADS, MEM_HEAD_DIM)),
        'norm_mix': gain((DEPTH, D_MODEL)),
        'w_in': nrm((DEPTH, D_MODEL, SSM_WIDTH + 3 * ATT_WIDTH), d_in),
        'ssm_lambda_re': -0.5 + nrm((DEPTH, SSM_GROUPS, SSM_STATE), 0.01),
        'ssm_lambda_im': math.pi * n_idx + nrm((DEPTH, SSM_GROUPS, SSM_STATE), 0.01),
        'ssm_log_step': jax.random.uniform(next(ks), (DEPTH, SSM_GROUPS), F32, math.log(1e-3), math.log(1e-1)),
        'ssm_b_re': nrm((DEPTH, SSM_GROUPS, SSM_STATE, SSM_GROUP), (2 * SSM_GROUP) ** -0.5),
        'ssm_b_im': nrm((DEPTH, SSM_GROUPS, SSM_STATE, SSM_GROUP), (2 * SSM_GROUP) ** -0.5),
        'ssm_c_re': nrm((DEPTH, SSM_GROUPS, SSM_GROUP, SSM_STATE), SSM_STATE ** -0.5),
        'ssm_c_im': nrm((DEPTH, SSM_GROUPS, SSM_GROUP, SSM_STATE), SSM_STATE ** -0.5),
        'ssm_d': nrm((DEPTH, SSM_WIDTH)),
        'w_glu': nrm((DEPTH, SSM_WIDTH, 2 * SSM_WIDTH), SSM_WIDTH ** -0.5),
        'att_q_gain': gain((DEPTH, ATT_HEAD_DIM)),
        'att_k_gain': gain((DEPTH, ATT_HEAD_DIM)),
        'att_rel_bias': nrm((DEPTH, ATT_HEADS, 2 * REL_CLIP + 1), 0.1),
        'out_norm_ssm': gain((DEPTH, SSM_WIDTH)),
        'out_norm_att': gain((DEPTH, ATT_WIDTH)),
        'w_out': nrm((DEPTH, SSM_WIDTH + ATT_WIDTH, D_MODEL), (SSM_WIDTH + ATT_WIDTH) ** -0.5),
        'norm_mem': gain((DEPTH, D_MODEL)),
        'mem_in_norm': gain((DEPTH, D_MODEL)),
        'w_mem_q': nrm((DEPTH, D_MODEL, D_MODEL), d_in),
        'w_mem_k': nrm((DEPTH, D_MODEL, D_MODEL), d_in),
        'w_mem_v': nrm((DEPTH, D_MODEL, D_MODEL), d_in),
        'w_mem_o': nrm((DEPTH, D_MODEL, D_MODEL), d_in),
        'mem_q_gain': gain((DEPTH, MEM_HEAD_DIM)),
        'mem_k_gain': gain((DEPTH, MEM_HEAD_DIM)),
        'norm_ffn': gain((DEPTH, D_MODEL)),
        'router_g_w': nrm((DEPTH, D_MODEL, N_GROUPS), d_in),
        'router_g_b': nrm((DEPTH, N_GROUPS), 0.01),
        'router_e_w': nrm((DEPTH, N_GROUPS, D_MODEL, EXPERTS_PER_GROUP), d_in),
        'router_e_b': nrm((DEPTH, N_GROUPS, EXPERTS_PER_GROUP), 0.01),
        'exp_w_gate': nrm((DEPTH, N_EXPERTS, D_MODEL, D_FF_EXPERT), d_in),
        'exp_w_up': nrm((DEPTH, N_EXPERTS, D_MODEL, D_FF_EXPERT), d_in),
        'exp_w_down': nrm((DEPTH, N_EXPERTS, D_FF_EXPERT, D_MODEL), D_FF_EXPERT ** -0.5),
    }


def reference(x_prompt, x_sample, mem_prompt, cache_attn_k, cache_attn_v, state_ssm_re, state_ssm_im,
              cache_mem_k, cache_mem_v, norm_mix, w_in, ssm_lambda_re, ssm_lambda_im, ssm_log_step,
              ssm_b_re, ssm_b_im, ssm_c_re, ssm_c_im, ssm_d, w_glu, att_q_gain, att_k_gain, att_rel_bias,
              out_norm_ssm, out_norm_att, w_out, norm_mem, mem_in_norm, w_mem_q, w_mem_k, w_mem_v, w_mem_o,
              mem_q_gain, mem_k_gain, norm_ffn, router_g_w, router_g_b, router_e_w, router_e_b,
              exp_w_gate, exp_w_up, exp_w_down):
    yp, ys = x_prompt, x_sample
    sdt = state_ssm_re.dtype
    p_k, p_v, p_re, p_im, p_mk, p_mv = [], [], [], [], [], []
    s_k, s_v, s_re, s_im = [], [], [], []
    for l in range(DEPTH):
        lam_bar, b_bar = ssm_discretise(ssm_lambda_re[l], ssm_lambda_im[l], ssm_log_step[l], ssm_b_re[l], ssm_b_im[l])
        c_mat = lax.complex(ssm_c_re[l].astype(F32), ssm_c_im[l].astype(F32))
        moe_args = (router_g_w[l], router_g_b[l], router_e_w[l], router_e_b[l], exp_w_gate[l], exp_w_up[l], exp_w_down[l])

        u, q, k, v = parallel_mixer_inputs(yp, norm_mix[l], w_in[l], att_q_gain[l], att_k_gain[l])
        y_ssm, s_fin = ssm_scan(u, lam_bar, b_bar, c_mat, ssm_d[l], None)
        y_att = band_attention_prompt(q, k, v, att_rel_bias[l])
        yp = yp + parallel_mixer_output(y_ssm, y_att, w_glu[l], out_norm_ssm[l], out_norm_att[l], w_out[l])
        mk, mv = memory_kv(mem_prompt, mem_in_norm[l], w_mem_k[l], w_mem_v[l], mem_k_gain[l])
        yp = yp + memory_attend(yp, mk, mv, norm_mem[l], w_mem_q[l], mem_q_gain[l], w_mem_o[l])
        yp = yp + hier_moe(rms_norm(yp, norm_ffn[l]), *moe_args)
        keep = min(N_PREV_CHUNKS * CHUNK, k.shape[1])
        p_k.append(k[:, -keep:])
        p_v.append(v[:, -keep:])
        p_re.append(s_fin.real.astype(sdt))
        p_im.append(s_fin.imag.astype(sdt))
        p_mk.append(mk)
        p_mv.append(mv)

        u, q, k, v = parallel_mixer_inputs(ys, norm_mix[l], w_in[l], att_q_gain[l], att_k_gain[l])
        s0 = lax.complex(state_ssm_re[l].astype(F32), state_ssm_im[l].astype(F32))
        y_ssm, s_new = ssm_scan(u, lam_bar, b_bar, c_mat, ssm_d[l], s0)
        y_att = band_attention_sample(q, k, v, cache_attn_k[l], cache_attn_v[l], att_rel_bias[l])
        ys = ys + parallel_mixer_output(y_ssm, y_att, w_glu[l], out_norm_ssm[l], out_norm_att[l], w_out[l])
        ys = ys + memory_attend(ys, cache_mem_k[l], cache_mem_v[l], norm_mem[l], w_mem_q[l], mem_q_gain[l], w_mem_o[l])
        ys = ys + hier_moe(rms_norm(ys, norm_ffn[l]), *moe_args)
        s_k.append(k)
        s_v.append(v)
        s_re.append(s_new.real.astype(sdt))
        s_im.append(s_new.imag.astype(sdt))

    return (yp, ys,
            jnp.stack(p_k), jnp.stack(p_v), jnp.stack(p_re), jnp.stack(p_im), jnp.stack(p_mk), jnp.stack(p_mv),
            jnp.stack(s_k), jnp.stack(s_v), jnp.stack(s_re), jnp.stack(s_im))
```

```python
import functools
import math

import jax
import jax.numpy as jnp
from jax import lax
from jax.experimental import pallas as pl
from jax.experimental.pallas import tpu as pltpu

F32 = jnp.float32
BF16 = jnp.bfloat16

CHUNK = 64
N_PREV_CHUNKS = 8
BAND = (N_PREV_CHUNKS + 1) * CHUNK
PAD_ROWS = N_PREV_CHUNKS * CHUNK
SSM_GROUP = 16
SSM_STATE = 64
ATT_HEAD_DIM = 64
REL_CLIP = 128
MEM_HEADS = 4
N_GROUPS = 4
EXPERTS_PER_GROUP = 8
EPS = 1e-6
NEG_INF = -1e30

LANES = 128
MXU_N = 256
VMEM_LIMIT = 56 * 1024 * 1024
ROUTER_LANES = LANES
ROUTER_E0 = N_GROUPS


def _cparams(sem):
    return pltpu.CompilerParams(dimension_semantics=sem, vmem_limit_bytes=VMEM_LIMIT)


def _rms(x, gain):
    ms = jnp.mean(x * x, axis=-1, keepdims=True)
    return x * lax.rsqrt(ms + EPS) * gain


def _sigmoid(x):
    return 1.0 / (1.0 + jnp.exp(-x))


def _gelu_tanh(x):
    c = math.sqrt(2.0 / math.pi)
    return 0.5 * x * (1.0 + jnp.tanh(c * (x + 0.044715 * (x * x * x))))


def _dot(a, b):
    return jnp.dot(a, b, preferred_element_type=F32)


def _dot_nt(a, b):
    return lax.dot_general(a, b, (((1,), (1,)), ((), ())), preferred_element_type=F32)


def _split_bf16(x):
    hi = x.astype(BF16)
    lo = (x - hi.astype(F32)).astype(BF16)
    return hi, lo


def _full(shape):
    n = len(shape)
    return pl.BlockSpec(shape, lambda *_: (0,) * n)


def _in_proj_kernel(x_ref, g_ref, w_ref, qg_ref, kg_ref, hm_ref,
                    u_ref, q_ref, k_ref, v_ref, kk_ref, vk_ref, *, width, nt):
    h = _rms(x_ref[...], g_ref[...]).astype(BF16)
    z = _dot(h, w_ref[...])
    u_ref[...] = z[:, :width].astype(BF16)

    def head_norm(a, gain):
        hi, lo = _split_bf16(a * a)
        ms = _dot(hi, hm_ref[...]) + _dot(lo, hm_ref[...])
        return a * lax.rsqrt(ms + EPS) * gain

    q = head_norm(z[:, width:2 * width], qg_ref[...])
    k = head_norm(z[:, 2 * width:3 * width], kg_ref[...])
    v = z[:, 3 * width:]
    q_ref[...] = q.astype(BF16)
    k_ref[...] = k.astype(BF16)
    v_ref[...] = v.astype(BF16)

    @pl.when(pl.program_id(1) == nt - 1)
    def _():
        kk_ref[...] = k
        vk_ref[...] = v


def _in_proj(x2d, gain, w_bf, qg, kg, hm, *, nb, nt, tm):
    rows, d = x2d.shape
    width = w_bf.shape[1] // 4
    tok = lambda b, t: (b * nt + t, 0)
    out_shape = (
        jax.ShapeDtypeStruct((nt * tm, nb * width), BF16),
        jax.ShapeDtypeStruct((rows, width), BF16),
        jax.ShapeDtypeStruct((rows, width), BF16),
        jax.ShapeDtypeStruct((rows, width), BF16),
        jax.ShapeDtypeStruct((nb * tm, width), F32),
        jax.ShapeDtypeStruct((nb * tm, width), F32),
    )
    return pl.pallas_call(
        functools.partial(_in_proj_kernel, width=width, nt=nt),
        grid=(nb, nt),
        in_specs=[pl.BlockSpec((tm, d), tok), _full((1, d)), _full(w_bf.shape),
                  _full((1, width)), _full((1, width)), _full((width, width))],
        out_specs=(pl.BlockSpec((tm, width), lambda b, t: (t, b)),
                   pl.BlockSpec((tm, width), tok), pl.BlockSpec((tm, width), tok),
                   pl.BlockSpec((tm, width), tok),
                   pl.BlockSpec((tm, width), lambda b, t: (b, 0)),
                   pl.BlockSpec((tm, width), lambda b, t: (b, 0))),
        out_shape=out_shape,
        compiler_params=_cparams(("arbitrary", "arbitrary")),
        name="in_proj",
    )(x2d, gain, w_bf, qg, kg, hm)


def _ssm_kernel(u_ref, bc_ref, cre_ref, cim_ref, lre_ref, lim_ref, d_ref, s0re_ref, s0im_ref,
                y_ref, sre_ref, sim_ref, bu_ref, *, nb, tc, ns):
    @pl.when(pl.program_id(0) == 0)
    def _():
        sre_ref[...] = s0re_ref[...]
        sim_ref[...] = s0im_ref[...]

    u = u_ref[...]
    n_tiles = 2 * ns // MXU_N
    for j in range(n_tiles):
        slab = (j % (n_tiles // 2)) // 2
        bu_ref[:, j * MXU_N:(j + 1) * MXU_N] = _dot(u[:, slab * LANES:(slab + 1) * LANES], bc_ref[j])

    cw = 8 * 1024 // nb
    for cb in range(ns // cw):
        c0 = cb * cw
        lre = jnp.broadcast_to(lre_ref[:, c0:c0 + cw], (nb, cw))
        lim = jnp.broadcast_to(lim_ref[:, c0:c0 + cw], (nb, cw))

        def step(t, carry, c0=c0, lre=lre, lim=lim):
            sr, si = carry
            r0 = pl.multiple_of(t * nb, nb)
            nr = lre * sr - lim * si + bu_ref[pl.ds(r0, nb), c0:c0 + cw]
            ni = lre * si + lim * sr + bu_ref[pl.ds(r0, nb), ns + c0:ns + c0 + cw]
            bu_ref[pl.ds(r0, nb), c0:c0 + cw] = nr
            bu_ref[pl.ds(r0, nb), ns + c0:ns + c0 + cw] = ni
            return nr, ni

        sr, si = lax.fori_loop(0, tc, step, (sre_ref[:, c0:c0 + cw], sim_ref[:, c0:c0 + cw]))
        sre_ref[:, c0:c0 + cw] = sr
        sim_ref[:, c0:c0 + cw] = si

    width = u.shape[1]
    kt = ns * MXU_N // width
    for n in range(width // MXU_N):
        s_re = bu_ref[:, n * kt:(n + 1) * kt].astype(BF16)
        s_im = bu_ref[:, ns + n * kt:ns + (n + 1) * kt].astype(BF16)
        y = _dot(s_re, cre_ref[n]) + _dot(s_im, cim_ref[n])
        cols = slice(n * MXU_N, (n + 1) * MXU_N)
        y_ref[:, cols] = y + d_ref[:, cols] * u[:, cols].astype(F32)


def _ssm(u_rows, bc, cre, cim, lre, lim, dskip, s0re, s0im, *, nb, tc):
    rows, width = u_rows.shape
    ns = lre.shape[1]
    r = tc * nb
    return pl.pallas_call(
        functools.partial(_ssm_kernel, nb=nb, tc=tc, ns=ns),
        grid=(rows // r,),
        in_specs=[pl.BlockSpec((r, width), lambda i: (i, 0)), _full(bc.shape), _full(cre.shape),
                  _full(cim.shape), _full((1, ns)), _full((1, ns)), _full((1, width)),
                  _full((nb, ns)), _full((nb, ns))],
        out_specs=(pl.BlockSpec((r, width), lambda i: (i, 0)), _full((nb, ns)), _full((nb, ns))),
        out_shape=(jax.ShapeDtypeStruct((rows, width), F32),
                   jax.ShapeDtypeStruct((nb, ns), F32), jax.ShapeDtypeStruct((nb, ns), F32)),
        scratch_shapes=[pltpu.VMEM((r, 2 * ns), F32)],
        compiler_params=_cparams(("arbitrary",)),
        name="ssm",
    )(u_rows, bc, cre, cim, lre, lim, dskip, s0re, s0im)


def _half_select(shape):
    lane = lax.broadcasted_iota(jnp.int32, shape, 1)
    return lane < ATT_HEAD_DIM


def _head_masks(first):
    m0 = jnp.where(first, 1.0, 0.0).astype(BF16)
    return m0, (1.0 - m0.astype(F32)).astype(BF16)


def _band_prompt_kernel(q_ref, k_ref, v_ref, bias_ref, o_ref, kp_ref, vp_ref, *, t_len):
    width = q_ref.shape[1]
    kp_ref[0:PAD_ROWS, :] = jnp.zeros((PAD_ROWS, width), BF16)
    vp_ref[0:PAD_ROWS, :] = jnp.zeros((PAD_ROWS, width), BF16)
    kp_ref[PAD_ROWS:, :] = k_ref[...]
    vp_ref[PAD_ROWS:, :] = v_ref[...]
    scale = ATT_HEAD_DIM ** -0.5
    first = _half_select((CHUNK, LANES))
    head_mask = _head_masks(first)
    col = lax.broadcasted_iota(jnp.int32, (CHUNK, BAND), 1)

    def chunk(c, carry):
        r0 = pl.multiple_of(c * CHUNK, CHUNK)
        valid = col >= (N_PREV_CHUNKS - c) * CHUNK
        for hp in range(width // LANES):
            lanes = slice(hp * LANES, (hp + 1) * LANES)
            qp = q_ref[pl.ds(r0, CHUNK), lanes]
            kb = kp_ref[pl.ds(r0, BAND), lanes]
            vb = vp_ref[pl.ds(r0, BAND), lanes]
            outs = []
            for hh in range(2):
                qm = qp * head_mask[hh]
                s = _dot_nt(qm, kb) * scale + bias_ref[2 * hp + hh]
                s = jnp.where(valid, s, NEG_INF)
                p = jnp.exp(s - jnp.max(s, axis=-1, keepdims=True))
                l = jnp.sum(p, axis=-1, keepdims=True)
                outs.append(_dot(p.astype(BF16), vb) / l)
            o_ref[pl.ds(r0, CHUNK), lanes] = jnp.where(first, outs[0], outs[1]).astype(BF16)
        return carry

    lax.fori_loop(0, t_len // CHUNK, chunk, 0)


def _band_prompt(q, k, v, bias, *, nb, t_len):
    width = q.shape[1]
    blk = pl.BlockSpec((t_len, width), lambda b: (b, 0))
    return pl.pallas_call(
        functools.partial(_band_prompt_kernel, t_len=t_len),
        grid=(nb,),
        in_specs=[blk, blk, blk, _full(bias.shape)],
        out_specs=blk,
        out_shape=jax.ShapeDtypeStruct(q.shape, BF16),
        scratch_shapes=[pltpu.VMEM((t_len + PAD_ROWS, width), BF16),
                        pltpu.VMEM((t_len + PAD_ROWS, width), BF16)],
        compiler_params=_cparams(("arbitrary",)),
        name="band_prompt",
    )(q, k, v, bias)


def _band_sample_kernel(q_ref, k_ref, v_ref, ck_ref, cv_ref, bc_ref, bn_ref, o_ref):
    width = q_ref.shape[1]
    s_len = q_ref.shape[0]
    scale = ATT_HEAD_DIM ** -0.5
    first = _half_select((s_len, LANES))
    head_mask = _head_masks(first)
    for hp in range(width // LANES):
        lanes = slice(hp * LANES, (hp + 1) * LANES)
        qp = q_ref[:, lanes]
        kc = ck_ref[:, lanes].astype(BF16)
        vc = cv_ref[:, lanes].astype(BF16)
        kn = k_ref[:, lanes]
        vn = v_ref[:, lanes]
        outs = []
        for hh in range(2):
            qm = qp * head_mask[hh]
            sc =_dot_nt(qm, kc) * scale + bc_ref[2 * hp + hh]
            sn = _dot_nt(qm, kn) * scale + bn_ref[2 * hp + hh]
            m = jnp.maximum(jnp.max(sc, axis=-1, keepdims=True), jnp.max(sn, axis=-1, keepdims=True))
            pc = jnp.exp(sc - m)
            pn = jnp.exp(sn - m)
            l = jnp.sum(pc, axis=-1, keepdims=True) + jnp.sum(pn, axis=-1, keepdims=True)
            outs.append((_dot(pc.astype(BF16), vc) + _dot(pn.astype(BF16), vn)) / l)
        o_ref[:, lanes] = jnp.where(first, outs[0], outs[1]).astype(BF16)


def _band_sample(q_tm, k_tm, v_tm, cache_k, cache_v, bias_c, bias_n, *, nb, s_len):
    width = q_tm.shape[1] // nb
    w_rows = cache_k.shape[0] // nb
    col = pl.BlockSpec((s_len, width), lambda b: (0, b))
    cache = pl.BlockSpec((w_rows, width), lambda b: (b, 0))
    return pl.pallas_call(
        _band_sample_kernel,
        grid=(nb,),
        in_specs=[col, col, col, cache, cache, _full(bias_c.shape), _full(bias_n.shape)],
        out_specs=col,
        out_shape=jax.ShapeDtypeStruct(q_tm.shape, BF16),
        compiler_params=_cparams(("arbitrary",)),
        name="band_sample",
    )(q_tm, k_tm, v_tm, cache_k, cache_v, bias_c, bias_n)


def _mem_kv_kernel(m_ref, g_ref, wk_ref, wv_ref, kg_ref, k_ref, v_ref, kb_ref, vb_ref):
    m = _rms(m_ref[...], g_ref[...]).astype(BF16)
    k = _dot(m, wk_ref[...])
    v = _dot(m, wv_ref[...])
    hd = kg_ref.shape[1]
    for h in range(k.shape[1] // hd):
        cols = slice(h * hd, (h + 1) * hd)
        kh = _rms(k[:, cols], kg_ref[...])
        k_ref[:, cols] = kh
        kb_ref[:, cols] = kh.astype(BF16)
    v_ref[...] = v
    vb_ref[...] = v.astype(BF16)


def _mem_kv(mem2d, gain, wk, wv, kgain, *, tm):
    rows, d = mem2d.shape
    blk = pl.BlockSpec((tm, d), lambda i: (i, 0))
    return pl.pallas_call(
        _mem_kv_kernel,
        grid=(rows // tm,),
        in_specs=[blk, _full((1, d)), _full(wk.shape), _full(wv.shape), _full(kgain.shape)],
        out_specs=(blk, blk, blk, blk),
        out_shape=(jax.ShapeDtypeStruct((rows, d), F32), jax.ShapeDtypeStruct((rows, d), F32),
                   jax.ShapeDtypeStruct((rows, d), BF16), jax.ShapeDtypeStruct((rows, d), BF16)),
        compiler_params=_cparams(("arbitrary",)),
        name="mem_kv",
    )(mem2d, gain, wk, wv, kgain)


def _route(logits):
    lane = lax.broadcasted_iota(jnp.int32, logits.shape, 1).astype(F32)
    big = float(ROUTER_LANES)
    is_g = lane < N_GROUPS
    lg = jnp.where(is_g, logits, NEG_INF)
    gmax = jnp.max(lg, axis=-1, keepdims=True)
    p1 = 1.0 / jnp.sum(jnp.where(is_g, jnp.exp(lg - gmax), 0.0), axis=-1, keepdims=True)
    g_idx = jnp.min(jnp.where(lg == gmax, lane, big), axis=-1, keepdims=True)
    lo = ROUTER_E0 + g_idx * EXPERTS_PER_GROUP
    le = jnp.where((lane >= lo) & (lane < lo + EXPERTS_PER_GROUP), logits, NEG_INF)
    v1 = jnp.max(le, axis=-1, keepdims=True)
    i1 = jnp.min(jnp.where(le == v1, lane, big), axis=-1, keepdims=True)
    le2 = jnp.where(lane == i1, NEG_INF, le)
    v2 = jnp.max(le2, axis=-1, keepdims=True)
    i2 = jnp.min(jnp.where(le2 == v2, lane, big), axis=-1, keepdims=True)
    e2 = jnp.exp(v2 - v1)
    w1 = p1 / (1.0 + e2)
    w2 = p1 * e2 / (1.0 + e2)
    return jnp.where(lane == i1, w1, 0.0) + jnp.where(lane == i2, w2, 0.0)


def _mix_mem_kernel(x_ref, ys_ref, ya_ref, mk_ref, mv_ref, wglu_ref, ons_ref, ona_ref, wout_ref,
                    nmem_ref, wq_ref, qg_ref, wo_ref, nffn_ref, wrh_ref, wrl_ref, rb_ref,
                    x2_ref, xn_ref, gate_ref, o_scr):
    width = ys_ref.shape[1]
    g = _dot(_gelu_tanh(ys_ref[...]).astype(BF16), wglu_ref[...])
    y_s = g[:, :width] * _sigmoid(g[:, width:])
    cat_s = _rms(y_s, ons_ref[...]).astype(BF16)
    cat_a = _rms(ya_ref[...].astype(F32), ona_ref[...]).astype(BF16)
    x1 = x_ref[...] + _dot(cat_s, wout_ref[0:width, :]) + _dot(cat_a, wout_ref[width:, :])

    q = _dot(_rms(x1, nmem_ref[...]).astype(BF16), wq_ref[...])
    hd = qg_ref.shape[1]
    scale = hd ** -0.5
    for h in range(q.shape[1] // hd):
        cols = slice(h * hd, (h + 1) * hd)
        qh = _rms(q[:, cols], qg_ref[...]).astype(BF16)
        s = _dot_nt(qh, mk_ref[:, cols].astype(BF16)) * scale
        p = jnp.exp(s - jnp.max(s, axis=-1, keepdims=True))
        l = jnp.sum(p, axis=-1, keepdims=True)
        o_scr[:, cols] = (_dot(p.astype(BF16), mv_ref[:, cols].astype(BF16)) / l).astype(BF16)
    acc = x1 + _dot(o_scr[...], wo_ref[...])
    x2_ref[...] = acc

    xn = _rms(acc, nffn_ref[...])
    xh, xl = _split_bf16(xn)
    xn_ref[...] = xh
    logits = _dot(xh, wrh_ref[...]) + _dot(xl, wrh_ref[...]) + _dot(xh, wrl_ref[...]) + rb_ref[...]
    gate_ref[...] = _route(logits)


def _mix_mem(x, ys, ya, mk, mv, weights, *, grid, tm, row_map, ssm_map, mem_map):
    d = weights["w_out"].shape[1]
    width = weights["w_glu"].shape[0]
    names = ("w_glu", "out_norm_ssm", "out_norm_att", "w_out", "norm_mem", "w_mem_q", "mem_q_gain",
             "w_mem_o", "norm_ffn", "w_router_hi", "w_router_lo", "router_bias")
    ws = [weights[n] for n in names]
    mem_rows = weights["n_mem"]
    xspec = pl.BlockSpec((tm, d), row_map)
    hspec = pl.BlockSpec((tm, width), row_map)
    sspec = pl.BlockSpec((tm, width), ssm_map)
    mspec = pl.BlockSpec((mem_rows, d), mem_map)
    return pl.pallas_call(
        _mix_mem_kernel,
        grid=grid,
        in_specs=[xspec, sspec, hspec, mspec, mspec] + [_full(w.shape) for w in ws],
        out_specs=(xspec, xspec, pl.BlockSpec((tm, ROUTER_LANES), row_map)),
        out_shape=(jax.ShapeDtypeStruct(x.shape, F32), jax.ShapeDtypeStruct(x.shape, BF16),
                   jax.ShapeDtypeStruct((x.shape[0], x.shape[1] // d * ROUTER_LANES), F32)),
        scratch_shapes=[pltpu.VMEM((tm, d), BF16)],
        compiler_params=_cparams(("arbitrary",) * len(grid)),
        name="mix_mem",
    )(x, ys, ya, mk, mv, *ws)


def _moe_kernel(xn_ref, x2_ref, gate_ref, wg_ref, wu_ref, wd_ref, o_ref):
    e = pl.program_id(1)

    @pl.when(e == 0)
    def _():
        o_ref[...] = x2_ref[...]

    gates = gate_ref[...]
    lane = lax.broadcasted_iota(jnp.int32, gates.shape, 1)
    ge = jnp.sum(jnp.where(lane == e + ROUTER_E0, gates, 0.0), axis=-1, keepdims=True)
    xn = xn_ref[...]
    a = _dot(xn, wg_ref[...])
    h = a * _sigmoid(a) * _dot(xn, wu_ref[...])
    o_ref[...] += ge * _dot(h.astype(BF16), wd_ref[...])


def _moe(xn, x2, gates, wg, wu, wd, *, tm):
    rows, d = xn.shape
    n_exp, _, dff = wg.shape
    row = lambda i, e: (i, 0)
    return pl.pallas_call(
        _moe_kernel,
        grid=(rows // tm, n_exp),
        in_specs=[pl.BlockSpec((tm, d), row), pl.BlockSpec((tm, d), row),
                  pl.BlockSpec((tm, ROUTER_LANES), row),
                  pl.BlockSpec((None, d, dff), lambda i, e: (e, 0, 0)),
                  pl.BlockSpec((None, d, dff), lambda i, e: (e, 0, 0)),
                  pl.BlockSpec((None, dff, d), lambda i, e: (e, 0, 0))],
        out_specs=pl.BlockSpec((tm, d), row),
        out_shape=jax.ShapeDtypeStruct((rows, d), F32),
        compiler_params=_cparams(("arbitrary", "arbitrary")),
        name="moe",
    )(xn, x2, gates, wg, wu, wd)


def _ssm_params(lam_re, lam_im, log_step, b_re, b_im, c_re, c_im):
    n_g, n_p = lam_re.shape
    step = jnp.exp(log_step.astype(F32))[:, None]
    mag = jnp.exp(lam_re * step)
    lb_re = mag * jnp.cos(lam_im * step)
    lb_im = mag * jnp.sin(lam_im * step)
    den = lam_re * lam_re + lam_im * lam_im
    f_re = ((lb_re - 1.0) * lam_re + lb_im * lam_im) / den
    f_im = (lb_im * lam_re - (lb_re - 1.0) * lam_im) / den
    bb_re = f_re[..., None] * b_re - f_im[..., None] * b_im
    bb_im = f_re[..., None] * b_im + f_im[..., None] * b_re
    eye = jnp.eye(n_g, dtype=F32)
    ns = n_g * n_p
    width = n_g * SSM_GROUP
    b_full = jnp.concatenate(
        [jnp.einsum("hg,gpc->hcgp", eye, bb_re).reshape(width, ns),
         jnp.einsum("hg,gpc->hcgp", eye, bb_im).reshape(width, ns)], axis=1)
    n_tiles = 2 * ns // MXU_N
    bc = jnp.stack([
        b_full[((j % (n_tiles // 2)) // 2) * LANES:((j % (n_tiles // 2)) // 2 + 1) * LANES,
               j * MXU_N:(j + 1) * MXU_N] for j in range(n_tiles)]).astype(BF16)
    c_full_re = jnp.einsum("gh,gcp->gphc", eye, c_re).reshape(ns, width)
    c_full_im = -jnp.einsum("gh,gcp->gphc", eye, c_im).reshape(ns, width)
    kt = ns * MXU_N // width
    tiles = range(width // MXU_N)
    cre = jnp.stack([c_full_re[n * kt:(n + 1) * kt, n * MXU_N:(n + 1) * MXU_N] for n in tiles]).astype(BF16)
    cim = jnp.stack([c_full_im[n * kt:(n + 1) * kt, n * MXU_N:(n + 1) * MXU_N] for n in tiles]).astype(BF16)
    return bc, cre, cim, lb_re.reshape(1, ns), lb_im.reshape(1, ns)


def _rel_bias(rel_bias, q_pos, k_pos):
    idx = jnp.clip(q_pos[:, None] - k_pos[None, :], -REL_CLIP, REL_CLIP) + REL_CLIP
    return rel_bias.astype(F32)[:, idx]


def _layer(xp, xs, mem_p, ck, cv, s_re, s_im, cmk, cmv, w, dims):
    batch, seq, dec_batch, dec_seq, d = dims
    width = w["w_in"].shape[1] // 4
    ns = w["lam_re"].shape[1]
    n_mem = w["n_mem"]
    tm_p = 512
    nt_p = seq // tm_p
    n_dec = dec_batch * dec_seq

    u, q, k, v, kk, vk = _in_proj(xp, w["norm_mix"], w["w_in"], w["qg"], w["kg"], w["head_mean"],
                                  nb=batch, nt=nt_p, tm=tm_p)
    zeros = jnp.zeros((batch, ns), F32)
    y_ssm, pre, pim = _ssm(u.reshape(seq * batch, width), w["bc"], w["cre"], w["cim"], w["lam_re"],
                           w["lam_im"], w["ssm_d"], zeros, zeros, nb=batch, tc=32)
    y_att = _band_prompt(q, k, v, w["bias_p"], nb=batch, t_len=seq)
    mk, mv, mkb, mvb = _mem_kv(mem_p, w["mem_in_norm"], w["w_mem_k"], w["w_mem_v"], w["mem_k_gain"], tm=512)
    tm_d = 256
    nt_d = seq // tm_d
    x2, xn, gates = _mix_mem(xp, y_ssm.reshape(seq, batch * width), y_att, mkb, mvb, w,
                             grid=(batch, nt_d), tm=tm_d,
                             row_map=lambda b, t: (b * nt_d + t, 0), ssm_map=lambda b, t: (t, b),
                             mem_map=lambda b, t: (b, 0))
    yp =_moe(xn, x2, gates, w["exp_w_gate"], w["exp_w_up"], w["exp_w_down"], tm=1024)

    us, qs, ks, vs, kks, vks = _in_proj(xs, w["norm_mix"], w["w_in"], w["qg"], w["kg"], w["head_mean"],
                                        nb=1, nt=1, tm=n_dec)
    ys_ssm, sre, sim = _ssm(us, w["bc"], w["cre"], w["cim"], w["lam_re"], w["lam_im"], w["ssm_d"],
                            s_re, s_im, nb=dec_batch, tc=dec_seq)
    tmv = lambda a: a.reshape(dec_seq, dec_batch * a.shape[1])
    ys_att = _band_sample(tmv(qs), tmv(ks), tmv(vs), ck, cv, w["bias_sc"], w["bias_sn"],
                          nb=dec_batch, s_len=dec_seq)
    x2s, xns, gates_s = _mix_mem(tmv(xs), tmv(ys_ssm), ys_att, cmk, cmv, w,
                                 grid=(dec_batch,), tm=dec_seq,
                                 row_map=lambda b: (0, b), ssm_map=lambda b: (0, b),
                                 mem_map=lambda b: (b, 0))
    ys = _moe(xns.reshape(n_dec, d), x2s.reshape(n_dec, d), gates_s.reshape(n_dec, ROUTER_LANES),
              w["exp_w_gate"], w["exp_w_up"], w["exp_w_down"], tm=n_dec)
    return yp, ys, (kk, vk, pre, pim, mk, mv), (kks, vks, sre, sim)


def kernel(x_prompt, x_sample, mem_prompt, cache_attn_k, cache_attn_v, state_ssm_re, state_ssm_im, cache_mem_k, cache_mem_v, norm_mix, w_in, ssm_lambda_re, ssm_lambda_im, ssm_log_step, ssm_b_re, ssm_b_im, ssm_c_re, ssm_c_im, ssm_d, w_glu, att_q_gain, att_k_gain, att_rel_bias, out_norm_ssm, out_norm_att, w_out, norm_mem, mem_in_norm, w_mem_q, w_mem_k, w_mem_v, w_mem_o, mem_q_gain, mem_k_gain, norm_ffn, router_g_w, router_g_b, router_e_w, router_e_b, exp_w_gate, exp_w_up, exp_w_down):
    depth = norm_mix.shape[0]
    batch, seq, d = x_prompt.shape
    dec_batch, dec_seq, _ = x_sample.shape
    n_mem = mem_prompt.shape[1]
    att_rows = cache_attn_k.shape[2]
    n_g, n_p = ssm_lambda_re.shape[1:]
    width = n_g * SSM_GROUP
    heads = width // ATT_HEAD_DIM
    ns = n_g * n_p
    assert seq % 512 == 0 and att_rows == PAD_ROWS and seq >= PAD_ROWS
    assert (dec_batch * dec_seq) % 8 == 0 and dec_seq % 16 == 0

    xp = x_prompt.reshape(batch * seq, d)
    xs = x_sample.transpose(1, 0, 2).reshape(dec_seq * dec_batch, d)
    mem_p = mem_prompt.reshape(batch * n_mem, d)
    row = lambda a: a.reshape(1, -1).astype(F32)
    head_mean = jnp.kron(jnp.eye(heads, dtype=F32),
                         jnp.full((ATT_HEAD_DIM, ATT_HEAD_DIM), 1.0 / ATT_HEAD_DIM, F32)).astype(BF16)
    q_pos = PAD_ROWS + jnp.arange(CHUNK)
    s_pos = att_rows + jnp.arange(dec_seq)

    p_out, s_out = [], []
    for l in range(depth):
        bc, cre, cim, lb_re, lb_im = _ssm_params(ssm_lambda_re[l], ssm_lambda_im[l], ssm_log_step[l],
                                                 ssm_b_re[l], ssm_b_im[l], ssm_c_re[l], ssm_c_im[l])
        w_router = jnp.concatenate(
            [router_g_w[l], router_e_w[l].transpose(1, 0, 2).reshape(d, N_GROUPS * EXPERTS_PER_GROUP),
             jnp.zeros((d, ROUTER_LANES - N_GROUPS * (1 + EXPERTS_PER_GROUP)), F32)], axis=1)
        wr_hi, wr_lo = _split_bf16(w_router)
        r_bias = jnp.concatenate(
            [router_g_b[l], router_e_b[l].reshape(-1),
             jnp.zeros((ROUTER_LANES - N_GROUPS * (1 + EXPERTS_PER_GROUP),), F32)]).reshape(1, ROUTER_LANES)
        bias_s = _rel_bias(att_rel_bias[l], s_pos, jnp.arange(att_rows + dec_seq))
        w = dict(
            n_mem=n_mem,
            norm_mix=row(norm_mix[l]), w_in=w_in[l].astype(BF16),
            qg=row(jnp.tile(att_q_gain[l], heads)), kg=row(jnp.tile(att_k_gain[l], heads)),
            head_mean=head_mean, bc=bc, cre=cre, cim=cim, lam_re=lb_re, lam_im=lb_im,
            ssm_d=row(ssm_d[l]), bias_p=_rel_bias(att_rel_bias[l], q_pos, jnp.arange(BAND)),
            bias_sc=bias_s[:, :, :att_rows], bias_sn=bias_s[:, :, att_rows:],
            mem_in_norm=row(mem_in_norm[l]), w_mem_k=w_mem_k[l].astype(BF16),
            w_mem_v=w_mem_v[l].astype(BF16), mem_k_gain=row(mem_k_gain[l]),
            w_glu=w_glu[l].astype(BF16), out_norm_ssm=row(out_norm_ssm[l]),
            out_norm_att=row(out_norm_att[l]), w_out=w_out[l].astype(BF16), norm_mem=row(norm_mem[l]),
            w_mem_q=w_mem_q[l].astype(BF16), mem_q_gain=row(mem_q_gain[l]),
            w_mem_o=w_mem_o[l].astype(BF16), norm_ffn=row(norm_ffn[l]),
            w_router_hi=wr_hi, w_router_lo=wr_lo, router_bias=r_bias,
            exp_w_gate=exp_w_gate[l].astype(BF16), exp_w_up=exp_w_up[l].astype(BF16),
            exp_w_down=exp_w_down[l].astype(BF16),
        )
        xp, xs, p_new, s_new = _layer(
            xp, xs, mem_p,
            cache_attn_k[l].reshape(dec_batch * att_rows, width),
            cache_attn_v[l].reshape(dec_batch * att_rows, width),
            state_ssm_re[l].reshape(dec_batch, ns), state_ssm_im[l].reshape(dec_batch, ns),
            cache_mem_k[l].reshape(dec_batch * n_mem, d), cache_mem_v[l].reshape(dec_batch * n_mem, d),
            w, (batch, seq, dec_batch, dec_seq, d))
        p_out.append(p_new)
        s_out.append(s_new)

    sdt = state_ssm_re.dtype
    keep = min(PAD_ROWS, seq)
    kv_p = lambda a: a.reshape(batch, keep, heads, ATT_HEAD_DIM)
    kv_s = lambda a: a.reshape(dec_seq, dec_batch, heads, ATT_HEAD_DIM).transpose(1, 0, 2, 3)
    st = lambda a: a.reshape(a.shape[0], n_g, n_p).astype(sdt)
    mkv = lambda a: a.reshape(batch, n_mem, MEM_HEADS, d // MEM_HEADS)
    stack = lambda f, outs, i: jnp.stack([f(o[i]) for o in outs])
    yp = xp.reshape(batch, seq, d)
    ys = xs.reshape(dec_seq, dec_batch, d).transpose(1, 0, 2)
    return (yp, ys,
            stack(kv_p, p_out, 0), stack(kv_p, p_out, 1), stack(st, p_out, 2), stack(st, p_out, 3),
            stack(mkv, p_out, 4), stack(mkv, p_out, 5),
            stack(kv_s, s_out, 0), stack(kv_s, s_out, 1), stack(st, s_out, 2), stack(st, s_out, 3))
```

```python
import functools
import math

import jax
import jax.numpy as jnp
from jax import lax
from jax.experimental import pallas as pl
from jax.experimental.pallas import tpu as pltpu

F32 = jnp.float32
BF16 = jnp.bfloat16

CHUNK = 64
N_PREV_CHUNKS = 8
BAND = (N_PREV_CHUNKS + 1) * CHUNK
PAD_ROWS = N_PREV_CHUNKS * CHUNK
PAIR_ROWS = 2 * CHUNK
PAIR_BAND = BAND + CHUNK
SOFTMAX_ROWS = 32
SSM_GROUP = 16
SSM_STATE = 64
ATT_HEAD_DIM = 64
REL_CLIP = 128
MEM_HEADS = 4
N_GROUPS = 4
EXPERTS_PER_GROUP = 8
EPS = 1e-6
NEG_INF = -1e30

LANES = 128
MXU_N = 256
VMEM_LIMIT = 56 * 1024 * 1024
ROUTER_LANES = LANES
ROUTER_E0 = N_GROUPS


def _cparams(sem):
    return pltpu.CompilerParams(dimension_semantics=sem, vmem_limit_bytes=VMEM_LIMIT)


def _rms(x, gain):
    ms = jnp.mean(x * x, axis=-1, keepdims=True)
    return x * lax.rsqrt(ms + EPS) * gain


def _sigmoid(x):
    return 1.0 / (1.0 + jnp.exp(-x))


def _gelu_tanh(x):
    c = math.sqrt(2.0 / math.pi)
    return 0.5 * x * (1.0 + jnp.tanh(c * (x + 0.044715 * (x * x * x))))


def _dot(a, b):
    return jnp.dot(a, b, preferred_element_type=F32)


def _dot_nt(a, b):
    return lax.dot_general(a, b, (((1,), (1,)), ((), ())), preferred_element_type=F32)


def _split_bf16(x):
    hi = x.astype(BF16)
    lo = (x - hi.astype(F32)).astype(BF16)
    return hi, lo


def _full(shape):
    n = len(shape)
    return pl.BlockSpec(shape, lambda *_: (0,) * n)


def _in_proj_kernel(x_ref, g_ref, w_ref, qg_ref, kg_ref, hm_ref,
                    u_ref, q_ref, k_ref, v_ref, kk_ref, vk_ref, *, width, nt):
    h = _rms(x_ref[...], g_ref[...]).astype(BF16)
    z = _dot(h, w_ref[...])
    u_ref[...] = z[:, :width].astype(BF16)

    def head_norm(a, gain):
        hi, lo = _split_bf16(a * a)
        ms = _dot(hi, hm_ref[...]) + _dot(lo, hm_ref[...])
        return a * lax.rsqrt(ms + EPS) * gain

    q = head_norm(z[:, width:2 * width], qg_ref[...])
    k = head_norm(z[:, 2 * width:3 * width], kg_ref[...])
    v = z[:, 3 * width:]
    q_ref[...] = q.astype(BF16)
    k_ref[...] = k.astype(BF16)
    v_ref[...] = v.astype(BF16)

    @pl.when(pl.program_id(1) == nt - 1)
    def _():
        kk_ref[...] = k
        vk_ref[...] = v


def _in_proj(x2d, gain, w_bf, qg, kg, hm, *, nb, nt, tm):
    rows, d = x2d.shape
    width = w_bf.shape[1] // 4
    tok = lambda b, t: (b * nt + t, 0)
    out_shape = (
        jax.ShapeDtypeStruct((nt * tm, nb * width), BF16),
        jax.ShapeDtypeStruct((rows, width), BF16),
        jax.ShapeDtypeStruct((rows, width), BF16),
        jax.ShapeDtypeStruct((rows, width), BF16),
        jax.ShapeDtypeStruct((nb * tm, width), F32),
        jax.ShapeDtypeStruct((nb * tm, width), F32),
    )
    return pl.pallas_call(
        functools.partial(_in_proj_kernel, width=width, nt=nt),
        grid=(nb, nt),
        in_specs=[pl.BlockSpec((tm, d), tok), _full((1, d)), _full(w_bf.shape),
                  _full((1, width)), _full((1, width)), _full((width, width))],
        out_specs=(pl.BlockSpec((tm, width), lambda b, t: (t, b)),
                   pl.BlockSpec((tm, width), tok), pl.BlockSpec((tm, width), tok),
                   pl.BlockSpec((tm, width), tok),
                   pl.BlockSpec((tm, width), lambda b, t: (b, 0)),
                   pl.BlockSpec((tm, width), lambda b, t: (b, 0))),
        out_shape=out_shape,
        compiler_params=_cparams(("arbitrary", "arbitrary")),
        name="in_proj",
    )(x2d, gain, w_bf, qg, kg, hm)


def _ssm_kernel(u_ref, bc_ref, cre_ref, cim_ref, lre_ref, lim_ref, d_ref, s0re_ref, s0im_ref,
                y_ref, sre_ref, sim_ref, bu_ref, *, nb, tc, ns):
    @pl.when(pl.program_id(0) == 0)
    def _():
        sre_ref[...] = s0re_ref[...]
        sim_ref[...] = s0im_ref[...]

    u = u_ref[...]
    n_tiles = 2 * ns // MXU_N
    for j in range(n_tiles):
        slab = (j % (n_tiles // 2)) // 2
        bu_ref[:, j * MXU_N:(j + 1) * MXU_N] = _dot(u[:, slab * LANES:(slab + 1) * LANES], bc_ref[j])

    cw = 8 * 1024 // nb
    for cb in range(ns // cw):
        c0 = cb * cw
        lre = jnp.broadcast_to(lre_ref[:, c0:c0 + cw], (nb, cw))
        lim = jnp.broadcast_to(lim_ref[:, c0:c0 + cw], (nb, cw))

        def step(t, carry, c0=c0, lre=lre, lim=lim):
            sr, si = carry
            r0 = pl.multiple_of(t * nb, nb)
            nr = lre * sr - lim * si + bu_ref[pl.ds(r0, nb), c0:c0 + cw]
            ni = lre * si + lim * sr + bu_ref[pl.ds(r0, nb), ns + c0:ns + c0 + cw]
            bu_ref[pl.ds(r0, nb), c0:c0 + cw] = nr
            bu_ref[pl.ds(r0, nb), ns + c0:ns + c0 + cw] = ni
            return nr, ni

        sr, si = lax.fori_loop(0, tc, step, (sre_ref[:, c0:c0 + cw], sim_ref[:, c0:c0 + cw]))
        sre_ref[:, c0:c0 + cw] = sr
        sim_ref[:, c0:c0 + cw] = si

    width = u.shape[1]
    kt = ns * MXU_N // width
    for n in range(width // MXU_N):
        s_re = bu_ref[:, n * kt:(n + 1) * kt].astype(BF16)
        s_im = bu_ref[:, ns + n * kt:ns + (n + 1) * kt].astype(BF16)
        y = _dot(s_re, cre_ref[n]) + _dot(s_im, cim_ref[n])
        cols = slice(n * MXU_N, (n + 1) * MXU_N)
        y_ref[:, cols] = y + d_ref[:, cols] * u[:, cols].astype(F32)


def _ssm(u_rows, bc, cre, cim, lre, lim, dskip, s0re, s0im, *, nb, tc):
    rows, width = u_rows.shape
    ns = lre.shape[1]
    r = tc * nb
    return pl.pallas_call(
        functools.partial(_ssm_kernel, nb=nb, tc=tc, ns=ns),
        grid=(rows // r,),
        in_specs=[pl.BlockSpec((r, width), lambda i: (i, 0)), _full(bc.shape), _full(cre.shape),
                  _full(cim.shape), _full((1, ns)), _full((1, ns)), _full((1, width)),
                  _full((nb, ns)), _full((nb, ns))],
        out_specs=(pl.BlockSpec((r, width), lambda i: (i, 0)), _full((nb, ns)), _full((nb, ns))),
        out_shape=(jax.ShapeDtypeStruct((rows, width), F32),
                   jax.ShapeDtypeStruct((nb, ns), F32), jax.ShapeDtypeStruct((nb, ns), F32)),
        scratch_shapes=[pltpu.VMEM((r, 2 * ns), F32)],
        compiler_params=_cparams(("arbitrary",)),
        name="ssm",
    )(u_rows, bc, cre, cim, lre, lim, dskip, s0re, s0im)


def _half_select(shape):
    lane = lax.broadcasted_iota(jnp.int32, shape, 1)
    return lane < ATT_HEAD_DIM


def _head_masks(first):
    m0 = jnp.where(first, 1.0, 0.0).astype(BF16)
    return m0, (1.0 - m0.astype(F32)).astype(BF16)


def _band_prompt_kernel(q_ref, k_ref, v_ref, bias_ref, o_ref, kp_ref, vp_ref, s_scr, p_scr, l_scr, *, t_len):
    width = q_ref.shape[1]
    n_hp = width // LANES
    kp_ref[0:PAD_ROWS, :] = jnp.zeros((PAD_ROWS, width), BF16)
    vp_ref[0:PAD_ROWS, :] = jnp.zeros((PAD_ROWS, width), BF16)
    kp_ref[PAD_ROWS:, :] = k_ref[...]
    vp_ref[PAD_ROWS:, :] = v_ref[...]
    first = _half_select((PAIR_ROWS, LANES))
    head_mask = _head_masks(first)
    col = lax.broadcasted_iota(jnp.int32, (SOFTMAX_ROWS, PAIR_BAND), 1)

    def pair(pi, carry, *, masked):
        r0 = pl.multiple_of(pi * PAIR_ROWS, PAIR_ROWS)
        for hp in range(n_hp):
            lanes = slice(hp * LANES, (hp + 1) * LANES)
            qp = q_ref[pl.ds(r0, PAIR_ROWS), lanes]
            qs = jnp.concatenate([qp * head_mask[0], qp * head_mask[1]], axis=0)
            s_scr[2 * hp * PAIR_ROWS:2 * (hp + 1) * PAIR_ROWS, :] = _dot_nt(
                qs, kp_ref[pl.ds(r0, PAIR_BAND), lanes])
        for h in range(2 * n_hp):
            for rb in range(0, PAIR_ROWS, SOFTMAX_ROWS):
                rows = slice(h * PAIR_ROWS + rb, h * PAIR_ROWS + rb + SOFTMAX_ROWS)
                s = s_scr[rows, :] + bias_ref[h, rb:rb + SOFTMAX_ROWS, :]
                if masked:
                    s = jnp.where(col >= PAD_ROWS - r0, s, NEG_INF)
                p = jnp.exp(s - jnp.max(s, axis=-1, keepdims=True))
                p_scr[rows, :] = p.astype(BF16)
                l_scr[rows, :] = jnp.broadcast_to(1.0 / jnp.sum(p, axis=-1, keepdims=True),
                                                  (SOFTMAX_ROWS, LANES))
        for hp in range(n_hp):
            lanes = slice(hp * LANES, (hp + 1) * LANES)
            rows = slice(2 * hp * PAIR_ROWS, 2 * (hp + 1) * PAIR_ROWS)
            o2 = _dot(p_scr[rows, :], vp_ref[pl.ds(r0, PAIR_BAND), lanes]) * l_scr[rows, :]
            o_ref[pl.ds(r0, PAIR_ROWS), lanes] = jnp.where(
                first, o2[:PAIR_ROWS], o2[PAIR_ROWS:]).astype(BF16)
        return carry

    n_masked = PAD_ROWS // PAIR_ROWS
    lax.fori_loop(0, n_masked, functools.partial(pair, masked=True), 0)
    lax.fori_loop(n_masked, t_len // PAIR_ROWS, functools.partial(pair, masked=False), 0)


def _band_prompt(q, k, v, bias, *, nb, t_len):
    width = q.shape[1]
    heads = width // ATT_HEAD_DIM
    blk = pl.BlockSpec((t_len, width), lambda b: (b, 0))
    return pl.pallas_call(
        functools.partial(_band_prompt_kernel, t_len=t_len),
        grid=(nb,),
        in_specs=[blk, blk, blk, _full(bias.shape)],
        out_specs=blk,
        out_shape=jax.ShapeDtypeStruct(q.shape, BF16),
        scratch_shapes=[pltpu.VMEM((t_len + PAD_ROWS, width), BF16),
                        pltpu.VMEM((t_len + PAD_ROWS, width), BF16),
                        pltpu.VMEM((heads * PAIR_ROWS, PAIR_BAND), F32),
                        pltpu.VMEM((heads * PAIR_ROWS, PAIR_BAND), BF16),
                        pltpu.VMEM((heads * PAIR_ROWS, LANES), F32)],
        compiler_params=_cparams(("arbitrary",)),
        name="band_prompt",
    )(q, k, v, bias)


def _band_sample_kernel(q_ref, k_ref, v_ref, ck_ref, cv_ref, bc_ref, bn_ref, o_ref):
    width = q_ref.shape[1]
    s_len = q_ref.shape[0]
    first = _half_select((s_len, LANES))
    head_mask = _head_masks(first)
    for hp in range(width // LANES):
        lanes = slice(hp * LANES, (hp + 1) * LANES)
        qp = q_ref[:, lanes]
        kc = ck_ref[:, lanes].astype(BF16)
        vc = cv_ref[:, lanes].astype(BF16)
        kn = k_ref[:, lanes]
        vn = v_ref[:, lanes]
        outs = []
        for hh in range(2):
            qm = qp * head_mask[hh]
            sc = _dot_nt(qm, kc) + bc_ref[2 * hp + hh]
            sn = _dot_nt(qm, kn) + bn_ref[2 * hp + hh]
            m = jnp.maximum(jnp.max(sc, axis=-1, keepdims=True), jnp.max(sn, axis=-1, keepdims=True))
            pc = jnp.exp(sc - m)
            pn = jnp.exp(sn - m)
            l = jnp.sum(pc, axis=-1, keepdims=True) + jnp.sum(pn, axis=-1, keepdims=True)
            outs.append((_dot(pc.astype(BF16), vc) + _dot(pn.astype(BF16), vn)) / l)
        o_ref[:, lanes] = jnp.where(first, outs[0], outs[1]).astype(BF16)


def _band_sample(q_tm, k_tm, v_tm, cache_k, cache_v, bias_c, bias_n, *, nb, s_len):
    width = q_tm.shape[1] // nb
    w_rows = cache_k.shape[0] // nb
    col = pl.BlockSpec((s_len, width), lambda b: (0, b))
    cache = pl.BlockSpec((w_rows, width), lambda b: (b, 0))
    return pl.pallas_call(
        _band_sample_kernel,
        grid=(nb,),
        in_specs=[col, col, col, cache, cache, _full(bias_c.shape), _full(bias_n.shape)],
        out_specs=col,
        out_shape=jax.ShapeDtypeStruct(q_tm.shape, BF16),
        compiler_params=_cparams(("arbitrary",)),
        name="band_sample",
    )(q_tm, k_tm, v_tm, cache_k, cache_v, bias_c, bias_n)


def _mem_kv_kernel(m_ref, g_ref, wk_ref, wv_ref, kg_ref, k_ref, v_ref, kb_ref, vb_ref):
    m = _rms(m_ref[...], g_ref[...]).astype(BF16)
    k = _dot(m, wk_ref[...])
    v = _dot(m, wv_ref[...])
    hd = kg_ref.shape[1]
    for h in range(k.shape[1] // hd):
        cols = slice(h * hd, (h + 1) * hd)
        kh = _rms(k[:, cols], kg_ref[...])
        k_ref[:, cols] = kh
        kb_ref[:, cols] = kh.astype(BF16)
    v_ref[...] = v
    vb_ref[...] = v.astype(BF16)


def _mem_kv(mem2d, gain, wk, wv, kgain, *, tm):
    rows, d = mem2d.shape
    blk = pl.BlockSpec((tm, d), lambda i: (i, 0))
    return pl.pallas_call(
        _mem_kv_kernel,
        grid=(rows // tm,),
        in_specs=[blk, _full((1, d)), _full(wk.shape), _full(wv.shape), _full(kgain.shape)],
        out_specs=(blk, blk, blk, blk),
        out_shape=(jax.ShapeDtypeStruct((rows, d), F32), jax.ShapeDtypeStruct((rows, d), F32),
                   jax.ShapeDtypeStruct((rows, d), BF16), jax.ShapeDtypeStruct((rows, d), BF16)),
        compiler_params=_cparams(("arbitrary",)),
        name="mem_kv",
    )(mem2d, gain, wk, wv, kgain)


def _route(logits):
    lane = lax.broadcasted_iota(jnp.int32, logits.shape, 1).astype(F32)
    big = float(ROUTER_LANES)
    is_g = lane < N_GROUPS
    lg = jnp.where(is_g, logits, NEG_INF)
    gmax = jnp.max(lg, axis=-1, keepdims=True)
    p1 = 1.0 / jnp.sum(jnp.where(is_g, jnp.exp(lg - gmax), 0.0), axis=-1, keepdims=True)
    g_idx = jnp.min(jnp.where(lg == gmax, lane, big), axis=-1, keepdims=True)
    lo = ROUTER_E0 + g_idx * EXPERTS_PER_GROUP
    le = jnp.where((lane >= lo) & (lane < lo + EXPERTS_PER_GROUP), logits, NEG_INF)
    v1 = jnp.max(le, axis=-1, keepdims=True)
    i1 = jnp.min(jnp.where(le == v1, lane, big), axis=-1, keepdims=True)
    le2 = jnp.where(lane == i1, NEG_INF, le)
    v2 = jnp.max(le2, axis=-1, keepdims=True)
    i2 = jnp.min(jnp.where(le2 == v2, lane, big), axis=-1, keepdims=True)
    e2 = jnp.exp(v2 - v1)
    w1 = p1 / (1.0 + e2)
    w2 = p1 * e2 / (1.0 + e2)
    return jnp.where(lane == i1, w1, 0.0) + jnp.where(lane == i2, w2, 0.0)


def _mix_mem_kernel(x_ref, ys_ref, ya_ref, mk_ref, mv_ref, wglu_ref, ons_ref, ona_ref, wout_ref,
                    nmem_ref, wq_ref, qg_ref, wo_ref, nffn_ref, wrh_ref, wrl_ref, rb_ref,
                    x2_ref, xn_ref, gate_ref, o_scr):
    width = ys_ref.shape[1]
    g = _dot(_gelu_tanh(ys_ref[...]).astype(BF16), wglu_ref[...])
    y_s = g[:, :width] * _sigmoid(g[:, width:])
    cat_s = _rms(y_s, ons_ref[...]).astype(BF16)
    cat_a = _rms(ya_ref[...].astype(F32), ona_ref[...]).astype(BF16)
    x1 = x_ref[...] + _dot(cat_s, wout_ref[0:width, :]) + _dot(cat_a, wout_ref[width:, :])

    q = _dot(_rms(x1, nmem_ref[...]).astype(BF16), wq_ref[...])
    hd = qg_ref.shape[1]
    scale = hd ** -0.5
    for h in range(q.shape[1] // hd):
        cols = slice(h * hd, (h + 1) * hd)
        qh = _rms(q[:, cols], qg_ref[...]).astype(BF16)
        s = _dot_nt(qh, mk_ref[:, cols].astype(BF16)) * scale
        p = jnp.exp(s - jnp.max(s, axis=-1, keepdims=True))
        l = jnp.sum(p, axis=-1, keepdims=True)
        o_scr[:, cols] = (_dot(p.astype(BF16), mv_ref[:, cols].astype(BF16)) / l).astype(BF16)
    acc = x1 + _dot(o_scr[...], wo_ref[...])
    x2_ref[...] = acc

    xn = _rms(acc, nffn_ref[...])
    xh, xl = _split_bf16(xn)
    xn_ref[...] = xh
    logits = _dot(xh, wrh_ref[...]) + _dot(xl, wrh_ref[...]) + _dot(xh, wrl_ref[...]) + rb_ref[...]
    gate_ref[...] = _route(logits)


def _mix_mem(x, ys, ya, mk, mv, weights, *, grid, tm, row_map, ssm_map, mem_map):
    d = weights["w_out"].shape[1]
    width = weights["w_glu"].shape[0]
    names = ("w_glu", "out_norm_ssm", "out_norm_att", "w_out", "norm_mem", "w_mem_q", "mem_q_gain",
             "w_mem_o", "norm_ffn", "w_router_hi", "w_router_lo", "router_bias")
    ws = [weights[n] for n in names]
    mem_rows = weights["n_mem"]
    xspec = pl.BlockSpec((tm, d), row_map)
    hspec = pl.BlockSpec((tm, width), row_map)
    sspec = pl.BlockSpec((tm, width), ssm_map)
    mspec = pl.BlockSpec((mem_rows, d), mem_map)
    return pl.pallas_call(
        _mix_mem_kernel,
        grid=grid,
        in_specs=[xspec, sspec, hspec, mspec, mspec] + [_full(w.shape) for w in ws],
        out_specs=(xspec, xspec, pl.BlockSpec((tm, ROUTER_LANES), row_map)),
        out_shape=(jax.ShapeDtypeStruct(x.shape, F32), jax.ShapeDtypeStruct(x.shape, BF16),
                   jax.ShapeDtypeStruct((x.shape[0], x.shape[1] // d * ROUTER_LANES), F32)),
        scratch_shapes=[pltpu.VMEM((tm, d), BF16)],
        compiler_params=_cparams(("arbitrary",) * len(grid)),
        name="mix_mem",
    )(x, ys, ya, mk, mv, *ws)


def _moe_kernel(xn_ref, x2_ref, gate_ref, wg_ref, wu_ref, wd_ref, o_ref):
    e = pl.program_id(1)

    @pl.when(e == 0)
    def _():
        o_ref[...] = x2_ref[...]

    gates = gate_ref[...]
    lane = lax.broadcasted_iota(jnp.int32, gates.shape, 1)
    ge = jnp.sum(jnp.where(lane == e + ROUTER_E0, gates, 0.0), axis=-1, keepdims=True)
    xn = xn_ref[...]
    a = _dot(xn, wg_ref[...])
    h = a * _sigmoid(a) * _dot(xn, wu_ref[...])
    o_ref[...] += ge * _dot(h.astype(BF16), wd_ref[...])


def _moe(xn, x2, gates, wg, wu, wd, *, tm):
    rows, d = xn.shape
    n_exp, _, dff = wg.shape
    row = lambda i, e: (i, 0)
    return pl.pallas_call(
        _moe_kernel,
        grid=(rows // tm, n_exp),
        in_specs=[pl.BlockSpec((tm, d), row), pl.BlockSpec((tm, d), row),
                  pl.BlockSpec((tm, ROUTER_LANES), row),
                  pl.BlockSpec((None, d, dff), lambda i, e: (e, 0, 0)),
                  pl.BlockSpec((None, d, dff), lambda i, e: (e, 0, 0)),
                  pl.BlockSpec((None, dff, d), lambda i, e: (e, 0, 0))],
        out_specs=pl.BlockSpec((tm, d), row),
        out_shape=jax.ShapeDtypeStruct((rows, d), F32),
        compiler_params=_cparams(("arbitrary", "arbitrary")),
        name="moe",
    )(xn, x2, gates, wg, wu, wd)


def _ssm_params(lam_re, lam_im, log_step, b_re, b_im, c_re, c_im):
    n_g, n_p = lam_re.shape
    step = jnp.exp(log_step.astype(F32))[:, None]
    mag = jnp.exp(lam_re * step)
    lb_re = mag * jnp.cos(lam_im * step)
    lb_im = mag * jnp.sin(lam_im * step)
    den = lam_re * lam_re + lam_im * lam_im
    f_re = ((lb_re - 1.0) * lam_re + lb_im * lam_im) / den
    f_im = (lb_im * lam_re - (lb_re - 1.0) * lam_im) / den
    bb_re = f_re[..., None] * b_re - f_im[..., None] * b_im
    bb_im = f_re[..., None] * b_im + f_im[..., None] * b_re
    eye = jnp.eye(n_g, dtype=F32)
    ns = n_g * n_p
    width = n_g * SSM_GROUP
    b_full = jnp.concatenate(
        [jnp.einsum("hg,gpc->hcgp", eye, bb_re).reshape(width, ns),
         jnp.einsum("hg,gpc->hcgp", eye, bb_im).reshape(width, ns)], axis=1)
    n_tiles = 2 * ns // MXU_N
    bc = jnp.stack([
        b_full[((j % (n_tiles // 2)) // 2) * LANES:((j % (n_tiles // 2)) // 2 + 1) * LANES,
               j * MXU_N:(j + 1) * MXU_N] for j in range(n_tiles)]).astype(BF16)
    c_full_re = jnp.einsum("gh,gcp->gphc", eye, c_re).reshape(ns, width)
    c_full_im = -jnp.einsum("gh,gcp->gphc", eye, c_im).reshape(ns, width)
    kt = ns * MXU_N // width
    tiles = range(width // MXU_N)
    cre = jnp.stack([c_full_re[n * kt:(n + 1) * kt, n * MXU_N:(n + 1) * MXU_N] for n in tiles]).astype(BF16)
    cim = jnp.stack([c_full_im[n * kt:(n + 1) * kt, n * MXU_N:(n + 1) * MXU_N] for n in tiles]).astype(BF16)
    return bc, cre, cim, lb_re.reshape(1, ns), lb_im.reshape(1, ns)


def _rel_bias(rel_bias, q0, n_q, n_k):
    n_r = n_q + n_k - 1
    dist = q0 + n_q - 1 - jnp.arange(n_r)
    r = rel_bias.astype(F32)[:, jnp.clip(dist, -REL_CLIP, REL_CLIP) + REL_CLIP]
    r = jnp.pad(r, ((0, 0), (0, 1)))
    rows = jnp.tile(r, (1, n_q))[:, :n_q * n_r].reshape(-1, n_q, n_r)
    return rows[:, :, n_q - 1:n_q - 1 + n_k]


def _layer(xp, xs, mem_p, ck, cv, s_re, s_im, cmk, cmv, w, dims):
    batch, seq, dec_batch, dec_seq, d = dims
    width = w["w_in"].shape[1] // 4
    ns = w["lam_re"].shape[1]
    n_mem = w["n_mem"]
    tm_p = 512
    nt_p = seq // tm_p
    n_dec = dec_batch * dec_seq

    u, q, k, v, kk, vk = _in_proj(xp, w["norm_mix"], w["w_in"], w["qg"], w["kg"], w["head_mean"],
                                  nb=batch, nt=nt_p, tm=tm_p)
    zeros = jnp.zeros((batch, ns), F32)
    y_ssm, pre, pim = _ssm(u.reshape(seq * batch, width), w["bc"], w["cre"], w["cim"], w["lam_re"],
                           w["lam_im"], w["ssm_d"], zeros, zeros, nb=batch, tc=32)
    y_att = _band_prompt(q, k, v, w["bias_p"], nb=batch, t_len=seq)
    mk, mv, mkb, mvb = _mem_kv(mem_p, w["mem_in_norm"], w["w_mem_k"], w["w_mem_v"], w["mem_k_gain"], tm=512)
    tm_d = 256
    nt_d = seq // tm_d
    x2, xn, gates = _mix_mem(xp, y_ssm.reshape(seq, batch * width), y_att, mkb, mvb, w,
                             grid=(batch, nt_d), tm=tm_d,
                             row_map=lambda b, t: (b * nt_d + t, 0), ssm_map=lambda b, t: (t, b),
                             mem_map=lambda b, t: (b, 0))
    yp =_moe(xn, x2, gates, w["exp_w_gate"], w["exp_w_up"], w["exp_w_down"], tm=1024)

    us, qs, ks, vs, kks, vks = _in_proj(xs, w["norm_mix"], w["w_in"], w["qg"], w["kg"], w["head_mean"],
                                        nb=1, nt=1, tm=n_dec)
    ys_ssm, sre, sim = _ssm(us, w["bc"], w["cre"], w["cim"], w["lam_re"], w["lam_im"], w["ssm_d"],
                            s_re, s_im, nb=dec_batch, tc=dec_seq)
    tmv = lambda a: a.reshape(dec_seq, dec_batch * a.shape[1])
    ys_att = _band_sample(tmv(qs), tmv(ks), tmv(vs), ck, cv, w["bias_sc"], w["bias_sn"],
                          nb=dec_batch, s_len=dec_seq)
    x2s, xns, gates_s = _mix_mem(tmv(xs), tmv(ys_ssm), ys_att, cmk, cmv, w,
                                 grid=(dec_batch,), tm=dec_seq,
                                 row_map=lambda b: (0, b), ssm_map=lambda b: (0, b),
                                 mem_map=lambda b: (b, 0))
    ys = _moe(xns.reshape(n_dec, d), x2s.reshape(n_dec, d), gates_s.reshape(n_dec, ROUTER_LANES),
              w["exp_w_gate"], w["exp_w_up"], w["exp_w_down"], tm=n_dec)
    return yp, ys, (kk, vk, pre, pim, mk, mv), (kks, vks, sre, sim)


def kernel(x_prompt, x_sample, mem_prompt, cache_attn_k, cache_attn_v, state_ssm_re, state_ssm_im, cache_mem_k, cache_mem_v, norm_mix, w_in, ssm_lambda_re, ssm_lambda_im, ssm_log_step, ssm_b_re, ssm_b_im, ssm_c_re, ssm_c_im, ssm_d, w_glu, att_q_gain, att_k_gain, att_rel_bias, out_norm_ssm, out_norm_att, w_out, norm_mem, mem_in_norm, w_mem_q, w_mem_k, w_mem_v, w_mem_o, mem_q_gain, mem_k_gain, norm_ffn, router_g_w, router_g_b, router_e_w, router_e_b, exp_w_gate, exp_w_up, exp_w_down):
    depth = norm_mix.shape[0]
    batch, seq, d = x_prompt.shape
    dec_batch, dec_seq, _ = x_sample.shape
    n_mem = mem_prompt.shape[1]
    att_rows = cache_attn_k.shape[2]
    n_g, n_p = ssm_lambda_re.shape[1:]
    width = n_g * SSM_GROUP
    heads = width // ATT_HEAD_DIM
    ns = n_g * n_p
    assert seq % 512 == 0 and att_rows == PAD_ROWS and seq >= PAD_ROWS
    assert (dec_batch * dec_seq) % 8 == 0 and dec_seq % 16 == 0

    xp = x_prompt.reshape(batch * seq, d)
    xs = x_sample.transpose(1, 0, 2).reshape(dec_seq * dec_batch, d)
    mem_p = mem_prompt.reshape(batch * n_mem, d)
    row = lambda a: a.reshape(1, -1).astype(F32)
    head_mean = jnp.kron(jnp.eye(heads, dtype=F32),
                         jnp.full((ATT_HEAD_DIM, ATT_HEAD_DIM), 1.0 / ATT_HEAD_DIM, F32)).astype(BF16)
    pr = jnp.arange(PAIR_ROWS)[:, None]
    pc = jnp.arange(PAIR_BAND)[None, :]
    pair_ok = jnp.where(pr < CHUNK, pc < BAND, pc >= CHUNK)
    att_scale = ATT_HEAD_DIM ** -0.5

    p_out, s_out = [], []
    for l in range(depth):
        bc, cre, cim, lb_re, lb_im = _ssm_params(ssm_lambda_re[l], ssm_lambda_im[l], ssm_log_step[l],
                                                 ssm_b_re[l], ssm_b_im[l], ssm_c_re[l], ssm_c_im[l])
        w_router = jnp.concatenate(
            [router_g_w[l], router_e_w[l].transpose(1, 0, 2).reshape(d, N_GROUPS * EXPERTS_PER_GROUP),
             jnp.zeros((d, ROUTER_LANES - N_GROUPS * (1 + EXPERTS_PER_GROUP)), F32)], axis=1)
        wr_hi, wr_lo = _split_bf16(w_router)
        r_bias = jnp.concatenate(
            [router_g_b[l], router_e_b[l].reshape(-1),
             jnp.zeros((ROUTER_LANES - N_GROUPS * (1 + EXPERTS_PER_GROUP),), F32)]).reshape(1, ROUTER_LANES)
        bias_s = _rel_bias(att_rel_bias[l], att_rows, dec_seq, att_rows + dec_seq)
        bias_p = jnp.where(pair_ok, _rel_bias(att_rel_bias[l], PAD_ROWS, PAIR_ROWS, PAIR_BAND), NEG_INF)
        w = dict(
            n_mem=n_mem,
            norm_mix=row(norm_mix[l]), w_in=w_in[l].astype(BF16),
            qg=row(jnp.tile(att_q_gain[l], heads) * att_scale), kg=row(jnp.tile(att_k_gain[l], heads)),
            head_mean=head_mean, bc=bc, cre=cre, cim=cim, lam_re=lb_re, lam_im=lb_im,
            ssm_d=row(ssm_d[l]), bias_p=bias_p,
            bias_sc=bias_s[:, :, :att_rows], bias_sn=bias_s[:, :, att_rows:],
            mem_in_norm=row(mem_in_norm[l]), w_mem_k=w_mem_k[l].astype(BF16),
            w_mem_v=w_mem_v[l].astype(BF16), mem_k_gain=row(mem_k_gain[l]),
            w_glu=w_glu[l].astype(BF16), out_norm_ssm=row(out_norm_ssm[l]),
            out_norm_att=row(out_norm_att[l]), w_out=w_out[l].astype(BF16), norm_mem=row(norm_mem[l]),
            w_mem_q=w_mem_q[l].astype(BF16), mem_q_gain=row(mem_q_gain[l]),
            w_mem_o=w_mem_o[l].astype(BF16), norm_ffn=row(norm_ffn[l]),
            w_router_hi=wr_hi, w_router_lo=wr_lo, router_bias=r_bias,
            exp_w_gate=exp_w_gate[l].astype(BF16), exp_w_up=exp_w_up[l].astype(BF16),
            exp_w_down=exp_w_down[l].astype(BF16),
        )
        xp, xs, p_new, s_new = _layer(
            xp, xs, mem_p,
            cache_attn_k[l].reshape(dec_batch * att_rows, width),
            cache_attn_v[l].reshape(dec_batch * att_rows, width),
            state_ssm_re[l].reshape(dec_batch, ns), state_ssm_im[l].reshape(dec_batch, ns),
            cache_mem_k[l].reshape(dec_batch * n_mem, d), cache_mem_v[l].reshape(dec_batch * n_mem, d),
            w, (batch, seq, dec_batch, dec_seq, d))
        p_out.append(p_new)
        s_out.append(s_new)

    sdt = state_ssm_re.dtype
    keep = min(PAD_ROWS, seq)
    kv_p = lambda a: a.reshape(batch, keep, heads, ATT_HEAD_DIM)
    kv_s = lambda a: a.reshape(dec_seq, dec_batch, heads, ATT_HEAD_DIM).transpose(1, 0, 2, 3)
    st = lambda a: a.reshape(a.shape[0], n_g, n_p).astype(sdt)
    mkv = lambda a: a.reshape(batch, n_mem, MEM_HEADS, d // MEM_HEADS)
    stack = lambda f, outs, i: jnp.stack([f(o[i]) for o in outs])
    yp = xp.reshape(batch, seq, d)
    ys = xs.reshape(dec_seq, dec_batch, d).transpose(1, 0, 2)
    return (yp, ys,
            stack(kv_p, p_out, 0), stack(kv_p, p_out, 1), stack(st, p_out, 2), stack(st, p_out, 3),
            stack(mkv, p_out, 4), stack(mkv, p_out, 5),
            stack(kv_s, s_out, 0), stack(kv_s, s_out, 1), stack(st, s_out, 2), stack(st, s_out, 3))
```

```python
import functools
import math

import jax
import jax.numpy as jnp
from jax import lax
from jax.experimental import pallas as pl
from jax.experimental.pallas import tpu as pltpu

F32 = jnp.float32
BF16 = jnp.bfloat16

CHUNK = 64
N_PREV_CHUNKS = 8
BAND = (N_PREV_CHUNKS + 1) * CHUNK
PAD_ROWS = N_PREV_CHUNKS * CHUNK
PAIR_ROWS = 2 * CHUNK
PAIR_BAND = BAND + CHUNK
SOFTMAX_ROWS = 32
SSM_GROUP = 16
SSM_STATE = 64
ATT_HEAD_DIM = 64
REL_CLIP = 128
MEM_HEADS = 4
N_GROUPS = 4
EXPERTS_PER_GROUP = 8
EPS = 1e-6
NEG_INF = -1e30

LANES = 128
MXU_N = 256
VMEM_LIMIT = 56 * 1024 * 1024
ROUTER_LANES = LANES
MOE_TILE = 128
ROUTER_E0 = N_GROUPS
PAIRS_PER_GROUP = EXPERTS_PER_GROUP * (EXPERTS_PER_GROUP - 1) // 2
N_CLASSES = N_GROUPS * PAIRS_PER_GROUP
ROUTE_CLS, ROUTE_WA, ROUTE_WB = 64, 65, 66


def _cparams(sem):
    return pltpu.CompilerParams(dimension_semantics=sem, vmem_limit_bytes=VMEM_LIMIT)


def _rms(x, gain):
    ms = jnp.mean(x * x, axis=-1, keepdims=True)
    return x * lax.rsqrt(ms + EPS) * gain


def _sigmoid(x):
    return 1.0 / (1.0 + jnp.exp(-x))


def _gelu_tanh(x):
    c = math.sqrt(2.0 / math.pi)
    return 0.5 * x * (1.0 + jnp.tanh(c * (x + 0.044715 * (x * x * x))))


def _dot(a, b):
    return jnp.dot(a, b, preferred_element_type=F32)


def _dot_nt(a, b):
    return lax.dot_general(a, b, (((1,), (1,)), ((), ())), preferred_element_type=F32)


def _split_bf16(x):
    hi = x.astype(BF16)
    lo = (x - hi.astype(F32)).astype(BF16)
    return hi, lo


def _full(shape):
    n = len(shape)
    return pl.BlockSpec(shape, lambda *_: (0,) * n)


def _in_proj_kernel(x_ref, g_ref, w_ref, qg_ref, kg_ref, hm_ref,
                    u_ref, q_ref, k_ref, v_ref, kk_ref, vk_ref, *, width, nt):
    h = _rms(x_ref[...], g_ref[...]).astype(BF16)
    z = _dot(h, w_ref[...])
    u_ref[...] = z[:, :width].astype(BF16)

    def head_norm(a, gain):
        hi, lo = _split_bf16(a * a)
        ms = _dot(hi, hm_ref[...]) + _dot(lo, hm_ref[...])
        return a * lax.rsqrt(ms + EPS) * gain

    q = head_norm(z[:, width:2 * width], qg_ref[...])
    k = head_norm(z[:, 2 * width:3 * width], kg_ref[...])
    v = z[:, 3 * width:]
    q_ref[...] = q.astype(BF16)
    k_ref[...] = k.astype(BF16)
    v_ref[...] = v.astype(BF16)

    @pl.when(pl.program_id(1) == nt - 1)
    def _():
        kk_ref[...] = k
        vk_ref[...] = v


def _in_proj(x2d, gain, w_bf, qg, kg, hm, *, nb, nt, tm):
    rows, d = x2d.shape
    width = w_bf.shape[1] // 4
    tok = lambda b, t: (b * nt + t, 0)
    out_shape = (
        jax.ShapeDtypeStruct((nt * tm, nb * width), BF16),
        jax.ShapeDtypeStruct((rows, width), BF16),
        jax.ShapeDtypeStruct((rows, width), BF16),
        jax.ShapeDtypeStruct((rows, width), BF16),
        jax.ShapeDtypeStruct((nb * tm, width), F32),
        jax.ShapeDtypeStruct((nb * tm, width), F32),
    )
    return pl.pallas_call(
        functools.partial(_in_proj_kernel, width=width, nt=nt),
        grid=(nb, nt),
        in_specs=[pl.BlockSpec((tm, d), tok), _full((1, d)), _full(w_bf.shape),
                  _full((1, width)), _full((1, width)), _full((width, width))],
        out_specs=(pl.BlockSpec((tm, width), lambda b, t: (t, b)),
                   pl.BlockSpec((tm, width), tok), pl.BlockSpec((tm, width), tok),
                   pl.BlockSpec((tm, width), tok),
                   pl.BlockSpec((tm, width), lambda b, t: (b, 0)),
                   pl.BlockSpec((tm, width), lambda b, t: (b, 0))),
        out_shape=out_shape,
        compiler_params=_cparams(("arbitrary", "arbitrary")),
        name="in_proj",
    )(x2d, gain, w_bf, qg, kg, hm)


def _ssm_kernel(u_ref, bc_ref, cre_ref, cim_ref, lre_ref, lim_ref, d_ref, s0re_ref, s0im_ref,
                y_ref, sre_ref, sim_ref, bu_ref, *, nb, tc, ns):
    @pl.when(pl.program_id(0) == 0)
    def _():
        sre_ref[...] = s0re_ref[...]
        sim_ref[...] = s0im_ref[...]

    u = u_ref[...]
    n_tiles = 2 * ns // MXU_N
    for j in range(n_tiles):
        slab = (j % (n_tiles // 2)) // 2
        bu_ref[:, j * MXU_N:(j + 1) * MXU_N] = _dot(u[:, slab * LANES:(slab + 1) * LANES], bc_ref[j])

    cw = 8 * 1024 // nb
    for cb in range(ns // cw):
        c0 = cb * cw
        lre = jnp.broadcast_to(lre_ref[:, c0:c0 + cw], (nb, cw))
        lim = jnp.broadcast_to(lim_ref[:, c0:c0 + cw], (nb, cw))

        def step(t, carry, c0=c0, lre=lre, lim=lim):
            sr, si = carry
            r0 = pl.multiple_of(t * nb, nb)
            nr = lre * sr - lim * si + bu_ref[pl.ds(r0, nb), c0:c0 + cw]
            ni = lre * si + lim * sr + bu_ref[pl.ds(r0, nb), ns + c0:ns + c0 + cw]
            bu_ref[pl.ds(r0, nb), c0:c0 + cw] = nr
            bu_ref[pl.ds(r0, nb), ns + c0:ns + c0 + cw] = ni
            return nr, ni

        sr, si = lax.fori_loop(0, tc, step, (sre_ref[:, c0:c0 + cw], sim_ref[:, c0:c0 + cw]))
        sre_ref[:, c0:c0 + cw] = sr
        sim_ref[:, c0:c0 + cw] = si

    width = u.shape[1]
    kt = ns * MXU_N // width
    for n in range(width // MXU_N):
        s_re = bu_ref[:, n * kt:(n + 1) * kt].astype(BF16)
        s_im = bu_ref[:, ns + n * kt:ns + (n + 1) * kt].astype(BF16)
        y = _dot(s_re, cre_ref[n]) + _dot(s_im, cim_ref[n])
        cols = slice(n * MXU_N, (n + 1) * MXU_N)
        y_ref[:, cols] = y + d_ref[:, cols] * u[:, cols].astype(F32)


def _ssm(u_rows, bc, cre, cim, lre, lim, dskip, s0re, s0im, *, nb, tc):
    rows, width = u_rows.shape
    ns = lre.shape[1]
    r = tc * nb
    return pl.pallas_call(
        functools.partial(_ssm_kernel, nb=nb, tc=tc, ns=ns),
        grid=(rows // r,),
        in_specs=[pl.BlockSpec((r, width), lambda i: (i, 0)), _full(bc.shape), _full(cre.shape),
                  _full(cim.shape), _full((1, ns)), _full((1, ns)), _full((1, width)),
                  _full((nb, ns)), _full((nb, ns))],
        out_specs=(pl.BlockSpec((r, width), lambda i: (i, 0)), _full((nb, ns)), _full((nb, ns))),
        out_shape=(jax.ShapeDtypeStruct((rows, width), F32),
                   jax.ShapeDtypeStruct((nb, ns), F32), jax.ShapeDtypeStruct((nb, ns), F32)),
        scratch_shapes=[pltpu.VMEM((r, 2 * ns), F32)],
        compiler_params=_cparams(("arbitrary",)),
        name="ssm",
    )(u_rows, bc, cre, cim, lre, lim, dskip, s0re, s0im)


def _half_select(shape):
    lane = lax.broadcasted_iota(jnp.int32, shape, 1)
    return lane < ATT_HEAD_DIM


def _head_masks(first):
    m0 = jnp.where(first, 1.0, 0.0).astype(BF16)
    return m0, (1.0 - m0.astype(F32)).astype(BF16)


def _band_prompt_kernel(q_ref, k_ref, v_ref, bias_ref, o_ref, kp_ref, vp_ref, s_scr, p_scr, l_scr, *, t_len):
    width = q_ref.shape[1]
    n_hp = width // LANES
    kp_ref[0:PAD_ROWS, :] = jnp.zeros((PAD_ROWS, width), BF16)
    vp_ref[0:PAD_ROWS, :] = jnp.zeros((PAD_ROWS, width), BF16)
    kp_ref[PAD_ROWS:, :] = k_ref[...]
    vp_ref[PAD_ROWS:, :] = v_ref[...]
    first = _half_select((PAIR_ROWS, LANES))
    head_mask = _head_masks(first)
    col = lax.broadcasted_iota(jnp.int32, (SOFTMAX_ROWS, PAIR_BAND), 1)

    def pair(pi, carry, *, masked):
        r0 = pl.multiple_of(pi * PAIR_ROWS, PAIR_ROWS)
        for hp in range(n_hp):
            lanes = slice(hp * LANES, (hp + 1) * LANES)
            qp = q_ref[pl.ds(r0, PAIR_ROWS), lanes]
            qs = jnp.concatenate([qp * head_mask[0], qp * head_mask[1]], axis=0)
            s_scr[2 * hp * PAIR_ROWS:2 * (hp + 1) * PAIR_ROWS, :] = _dot_nt(
                qs, kp_ref[pl.ds(r0, PAIR_BAND), lanes])
        for h in range(2 * n_hp):
            for rb in range(0, PAIR_ROWS, SOFTMAX_ROWS):
                rows = slice(h * PAIR_ROWS + rb, h * PAIR_ROWS + rb + SOFTMAX_ROWS)
                s = s_scr[rows, :] + bias_ref[h, rb:rb + SOFTMAX_ROWS, :]
                if masked:
                    s = jnp.where(col >= PAD_ROWS - r0, s, NEG_INF)
                p = jnp.exp(s - jnp.max(s, axis=-1, keepdims=True))
                p_scr[rows, :] = p.astype(BF16)
                l_scr[rows, :] = jnp.broadcast_to(1.0 / jnp.sum(p, axis=-1, keepdims=True),
                                                  (SOFTMAX_ROWS, LANES))
        for hp in range(n_hp):
            lanes = slice(hp * LANES, (hp + 1) * LANES)
            rows = slice(2 * hp * PAIR_ROWS, 2 * (hp + 1) * PAIR_ROWS)
            o2 = _dot(p_scr[rows, :], vp_ref[pl.ds(r0, PAIR_BAND), lanes]) * l_scr[rows, :]
            o_ref[pl.ds(r0, PAIR_ROWS), lanes] = jnp.where(
                first, o2[:PAIR_ROWS], o2[PAIR_ROWS:]).astype(BF16)
        return carry

    n_masked = PAD_ROWS // PAIR_ROWS
    lax.fori_loop(0, n_masked, functools.partial(pair, masked=True), 0)
    lax.fori_loop(n_masked, t_len // PAIR_ROWS, functools.partial(pair, masked=False), 0)


def _band_prompt(q, k, v, bias, *, nb, t_len):
    width = q.shape[1]
    heads = width // ATT_HEAD_DIM
    blk = pl.BlockSpec((t_len, width), lambda b: (b, 0))
    return pl.pallas_call(
        functools.partial(_band_prompt_kernel, t_len=t_len),
        grid=(nb,),
        in_specs=[blk, blk, blk, _full(bias.shape)],
        out_specs=blk,
        out_shape=jax.ShapeDtypeStruct(q.shape, BF16),
        scratch_shapes=[pltpu.VMEM((t_len + PAD_ROWS, width), BF16),
                        pltpu.VMEM((t_len + PAD_ROWS, width), BF16),
                        pltpu.VMEM((heads * PAIR_ROWS, PAIR_BAND), F32),
                        pltpu.VMEM((heads * PAIR_ROWS, PAIR_BAND), BF16),
                        pltpu.VMEM((heads * PAIR_ROWS, LANES), F32)],
        compiler_params=_cparams(("arbitrary",)),
        name="band_prompt",
    )(q, k, v, bias)


def _band_sample_kernel(q_ref, k_ref, v_ref, ck_ref, cv_ref, bc_ref, bn_ref, o_ref):
    width = q_ref.shape[1]
    s_len = q_ref.shape[0]
    first = _half_select((s_len, LANES))
    head_mask = _head_masks(first)
    for hp in range(width // LANES):
        lanes = slice(hp * LANES, (hp + 1) * LANES)
        qp = q_ref[:, lanes]
        kc = ck_ref[:, lanes].astype(BF16)
        vc = cv_ref[:, lanes].astype(BF16)
        kn = k_ref[:, lanes]
        vn = v_ref[:, lanes]
        outs = []
        for hh in range(2):
            qm = qp * head_mask[hh]
            sc = _dot_nt(qm, kc) + bc_ref[2 * hp + hh]
            sn = _dot_nt(qm, kn) + bn_ref[2 * hp + hh]
            m = jnp.maximum(jnp.max(sc, axis=-1, keepdims=True), jnp.max(sn, axis=-1, keepdims=True))
            pc = jnp.exp(sc - m)
            pn = jnp.exp(sn - m)
            l = jnp.sum(pc, axis=-1, keepdims=True) + jnp.sum(pn, axis=-1, keepdims=True)
            outs.append((_dot(pc.astype(BF16), vc) + _dot(pn.astype(BF16), vn)) / l)
        o_ref[:, lanes] = jnp.where(first, outs[0], outs[1]).astype(BF16)


def _band_sample(q_tm, k_tm, v_tm, cache_k, cache_v, bias_c, bias_n, *, nb, s_len):
    width = q_tm.shape[1] // nb
    w_rows = cache_k.shape[0] // nb
    col = pl.BlockSpec((s_len, width), lambda b: (0, b))
    cache = pl.BlockSpec((w_rows, width), lambda b: (b, 0))
    return pl.pallas_call(
        _band_sample_kernel,
        grid=(nb,),
        in_specs=[col, col, col, cache, cache, _full(bias_c.shape), _full(bias_n.shape)],
        out_specs=col,
        out_shape=jax.ShapeDtypeStruct(q_tm.shape, BF16),
        compiler_params=_cparams(("arbitrary",)),
        name="band_sample",
    )(q_tm, k_tm, v_tm, cache_k, cache_v, bias_c, bias_n)


def _mem_kv_kernel(m_ref, g_ref, wk_ref, wv_ref, kg_ref, k_ref, v_ref, kb_ref, vb_ref):
    m = _rms(m_ref[...], g_ref[...]).astype(BF16)
    k = _dot(m, wk_ref[...])
    v = _dot(m, wv_ref[...])
    hd = kg_ref.shape[1]
    for h in range(k.shape[1] // hd):
        cols = slice(h * hd, (h + 1) * hd)
        kh = _rms(k[:, cols], kg_ref[...])
        k_ref[:, cols] = kh
        kb_ref[:, cols] = kh.astype(BF16)
    v_ref[...] = v
    vb_ref[...] = v.astype(BF16)


def _mem_kv(mem2d, gain, wk, wv, kgain, *, tm):
    rows, d = mem2d.shape
    blk = pl.BlockSpec((tm, d), lambda i: (i, 0))
    return pl.pallas_call(
        _mem_kv_kernel,
        grid=(rows // tm,),
        in_specs=[blk, _full((1, d)), _full(wk.shape), _full(wv.shape), _full(kgain.shape)],
        out_specs=(blk, blk, blk, blk),
        out_shape=(jax.ShapeDtypeStruct((rows, d), F32), jax.ShapeDtypeStruct((rows, d), F32),
                   jax.ShapeDtypeStruct((rows, d), BF16), jax.ShapeDtypeStruct((rows, d), BF16)),
        compiler_params=_cparams(("arbitrary",)),
        name="mem_kv",
    )(mem2d, gain, wk, wv, kgain)


def _route(logits):
    lane = lax.broadcasted_iota(jnp.int32, logits.shape, 1).astype(F32)
    big = float(ROUTER_LANES)
    is_g = lane < N_GROUPS
    lg = jnp.where(is_g, logits, NEG_INF)
    gmax = jnp.max(lg, axis=-1, keepdims=True)
    p1 = 1.0 / jnp.sum(jnp.where(is_g, jnp.exp(lg - gmax), 0.0), axis=-1, keepdims=True)
    g_idx = jnp.min(jnp.where(lg == gmax, lane, big), axis=-1, keepdims=True)
    lo = ROUTER_E0 + g_idx * EXPERTS_PER_GROUP
    le = jnp.where((lane >= lo) & (lane < lo + EXPERTS_PER_GROUP), logits, NEG_INF)
    v1 = jnp.max(le, axis=-1, keepdims=True)
    i1 = jnp.min(jnp.where(le == v1, lane, big), axis=-1, keepdims=True)
    le2 = jnp.where(lane == i1, NEG_INF, le)
    v2 = jnp.max(le2, axis=-1, keepdims=True)
    i2 = jnp.min(jnp.where(le2 == v2, lane, big), axis=-1, keepdims=True)
    e2 = jnp.exp(v2 - v1)
    w1 = p1 / (1.0 + e2)
    w2 = p1 * e2 / (1.0 + e2)
    gates = jnp.where(lane == i1, w1, 0.0) + jnp.where(lane == i2, w2, 0.0)
    a = jnp.minimum(i1, i2) - lo
    b = jnp.maximum(i1, i2) - lo
    pair = a * (2 * EXPERTS_PER_GROUP - 1 - a) * 0.5 + (b - a - 1.0)
    cls = g_idx * PAIRS_PER_GROUP + pair
    w_a = jnp.where(i1 < i2, w1, w2)
    w_b = jnp.where(i1 < i2, w2, w1)
    return (gates + jnp.where(lane == ROUTE_CLS, cls, 0.0) + jnp.where(lane == ROUTE_WA, w_a, 0.0)
            + jnp.where(lane == ROUTE_WB, w_b, 0.0))


def _mix_mem_kernel(x_ref, ys_ref, ya_ref, mk_ref, mv_ref, wglu_ref, ons_ref, ona_ref, wout_ref,
                    nmem_ref, wq_ref, qg_ref, wo_ref, nffn_ref, wrh_ref, wrl_ref, rb_ref,
                    *outs, routed):
    if routed:
        x2_ref, o_scr = outs
    else:
        x2_ref, xn_ref, gate_ref, o_scr = outs
    width = ys_ref.shape[1]
    g = _dot(_gelu_tanh(ys_ref[...]).astype(BF16), wglu_ref[...])
    y_s = g[:, :width] * _sigmoid(g[:, width:])
    cat_s = _rms(y_s, ons_ref[...]).astype(BF16)
    cat_a = _rms(ya_ref[...].astype(F32), ona_ref[...]).astype(BF16)
    x1 = x_ref[...] + _dot(cat_s, wout_ref[0:width, :]) + _dot(cat_a, wout_ref[width:, :])

    q = _dot(_rms(x1, nmem_ref[...]).astype(BF16), wq_ref[...])
    hd = qg_ref.shape[1]
    scale = hd ** -0.5
    for h in range(q.shape[1] // hd):
        cols = slice(h * hd, (h + 1) * hd)
        qh = _rms(q[:, cols], qg_ref[...]).astype(BF16)
        s = _dot_nt(qh, mk_ref[:, cols].astype(BF16)) * scale
        p = jnp.exp(s - jnp.max(s, axis=-1, keepdims=True))
        l = jnp.sum(p, axis=-1, keepdims=True)
        o_scr[:, cols] = (_dot(p.astype(BF16), mv_ref[:, cols].astype(BF16)) / l).astype(BF16)
    acc = x1 + _dot(o_scr[...], wo_ref[...])
    d = acc.shape[1]
    x2_ref[:, :d] = acc

    xn = _rms(acc, nffn_ref[...])
    xh, xl = _split_bf16(xn)
    logits = _dot(xh, wrh_ref[...]) + _dot(xl, wrh_ref[...]) + _dot(xh, wrl_ref[...]) + rb_ref[...]
    if routed:
        x2_ref[:, d:] = _route(logits)
    else:
        xn_ref[...] = xh
        gate_ref[...] = _route(logits)


def _mix_mem(x, ys, ya, mk, mv, weights, *, grid, tm, row_map, ssm_map, mem_map, routed):
    d = weights["w_out"].shape[1]
    width = weights["w_glu"].shape[0]
    names = ("w_glu", "out_norm_ssm", "out_norm_att", "w_out", "norm_mem", "w_mem_q", "mem_q_gain",
             "w_mem_o", "norm_ffn", "w_router_hi", "w_router_lo", "router_bias")
    ws = [weights[n] for n in names]
    mem_rows = weights["n_mem"]
    xspec = pl.BlockSpec((tm, d), row_map)
    hspec = pl.BlockSpec((tm, width), row_map)
    sspec = pl.BlockSpec((tm, width), ssm_map)
    mspec = pl.BlockSpec((mem_rows, d), mem_map)
    n_col = x.shape[1] // d
    if routed:
        out_specs = pl.BlockSpec((tm, d + ROUTER_LANES), row_map)
        out_shape = jax.ShapeDtypeStruct((x.shape[0], n_col * (d + ROUTER_LANES)), F32)
    else:
        out_specs = (xspec, xspec, pl.BlockSpec((tm, ROUTER_LANES), row_map))
        out_shape = (jax.ShapeDtypeStruct(x.shape, F32), jax.ShapeDtypeStruct(x.shape, BF16),
                     jax.ShapeDtypeStruct((x.shape[0], n_col * ROUTER_LANES), F32))
    return pl.pallas_call(
        functools.partial(_mix_mem_kernel, routed=routed),
        grid=grid,
        in_specs=[xspec, sspec, hspec, mspec, mspec] + [_full(w.shape) for w in ws],
        out_specs=out_specs,
        out_shape=out_shape,
        scratch_shapes=[pltpu.VMEM((tm, d), BF16)],
        compiler_params=_cparams(("arbitrary",) * len(grid)),
        name="mix_mem",
    )(x, ys, ya, mk, mv, *ws)


def _moe_kernel(xn_ref, x2_ref, gate_ref, wg_ref, wu_ref, wd_ref, o_ref):
    e = pl.program_id(1)

    @pl.when(e == 0)
    def _():
        o_ref[...] = x2_ref[...]

    gates = gate_ref[...]
    lane = lax.broadcasted_iota(jnp.int32, gates.shape, 1)
    ge = jnp.sum(jnp.where(lane == e + ROUTER_E0, gates, 0.0), axis=-1, keepdims=True)
    xn = xn_ref[...]
    a = _dot(xn, wg_ref[...])
    h = a * _sigmoid(a) * _dot(xn, wu_ref[...])
    o_ref[...] += ge * _dot(h.astype(BF16), wd_ref[...])


def _moe(xn, x2, gates, wg, wu, wd, *, tm):
    rows, d = xn.shape
    n_exp, _, dff = wg.shape
    row = lambda i, e: (i, 0)
    return pl.pallas_call(
        _moe_kernel,
        grid=(rows // tm, n_exp),
        in_specs=[pl.BlockSpec((tm, d), row), pl.BlockSpec((tm, d), row),
                  pl.BlockSpec((tm, ROUTER_LANES), row),
                  pl.BlockSpec((None, d, dff), lambda i, e: (e, 0, 0)),
                  pl.BlockSpec((None, d, dff), lambda i, e: (e, 0, 0)),
                  pl.BlockSpec((None, dff, d), lambda i, e: (e, 0, 0))],
        out_specs=pl.BlockSpec((tm, d), row),
        out_shape=jax.ShapeDtypeStruct((rows, d), F32),
        compiler_params=_cparams(("arbitrary", "arbitrary")),
        name="moe",
    )(xn, x2, gates, wg, wu, wd)


def _moe_routed_kernel(tile_ref, ea_ref, eb_ref, lo_ref, hi_ref, valid_ref, tok_ref,
                       x_hbm, nffn_ref, wga_ref, wua_ref, wda_ref, wgb_ref, wub_ref, wdb_ref,
                       y_hbm, xbuf, obuf, acc_ref, gsem, ssem, *, tm, n_tiles):
    w = pl.program_id(0)
    t = tile_ref[w]
    lo = lo_ref[w]
    hi = hi_ref[w]
    valid = valid_ref[w] == 1
    slot = t % 2
    d = y_hbm.shape[1]

    def row_copy_in(tile, s, r):
        tok = tok_ref[tile * tm + r]
        return pltpu.make_async_copy(x_hbm.at[pl.ds(tok, 1)], xbuf.at[s, pl.ds(r, 1)], gsem.at[s])

    def row_copy_out(tile, s, r):
        tok = tok_ref[tile * tm + r]
        return pltpu.make_async_copy(obuf.at[s, pl.ds(r, 1)], y_hbm.at[pl.ds(tok, 1)], ssem.at[s])

    def start_gather(tile, s):
        for r in range(tm):
            row_copy_in(tile, s, r).start()

    def wait_gather(s):
        pltpu.make_async_copy(x_hbm.at[pl.ds(0, tm)], xbuf.at[s], gsem.at[s]).wait()

    def wait_scatter(s):
        pltpu.make_async_copy(obuf.at[s], y_hbm.at[pl.ds(0, tm)], ssem.at[s]).wait()

    @pl.when(w == 0)
    def _():
        start_gather(0, 0)

    @pl.when(jnp.logical_and(valid, lo == 0))
    def _():
        wait_gather(slot)

        @pl.when(t + 1 < n_tiles)
        def _():
            start_gather(t + 1, 1 - slot)

        acc_ref[...] = jnp.zeros_like(acc_ref)

    @pl.when(valid)
    def _():
        xe = xbuf[slot]
        xn = _rms(xe[:, :d], nffn_ref[...]).astype(BF16)
        row = lax.broadcasted_iota(jnp.int32, (tm, 1), 0)
        inseg = jnp.logical_and(row >= lo, row < hi)
        w_a = jnp.where(inseg, xe[:, d + ROUTE_WA:d + ROUTE_WA + 1], 0.0)
        w_b = jnp.where(inseg, xe[:, d + ROUTE_WB:d + ROUTE_WB + 1], 0.0)
        a = _dot(xn, wga_ref[...])
        h_a = (a * _sigmoid(a) * _dot(xn, wua_ref[...])).astype(BF16)
        b = _dot(xn, wgb_ref[...])
        h_b = (b * _sigmoid(b) * _dot(xn, wub_ref[...])).astype(BF16)
        acc_ref[...] += w_a * _dot(h_a, wda_ref[...]) + w_b * _dot(h_b, wdb_ref[...])

    @pl.when(jnp.logical_and(valid, hi == tm))
    def _():
        @pl.when(t >= 2)
        def _():
            wait_scatter(slot)

        obuf[slot] = xbuf[slot][:, :d] + acc_ref[...]
        for r in range(tm):
            row_copy_out(t, slot, r).start()

        @pl.when(t == n_tiles - 1)
        def _():
            wait_scatter(slot)
            if n_tiles > 1:
                wait_scatter(1 - slot)


def _moe_routed(x2ext, plan, nffn, wg, wu, wd, *, tm):
    rows, de = x2ext.shape
    d = de - ROUTER_LANES
    n_exp, _, dff = wg.shape
    n_tiles = rows // tm
    n_items = plan[0].shape[0]
    ea = lambda w, tile, ea_, eb_, *_: (ea_[w], 0, 0)
    eb = lambda w, tile, ea_, eb_, *_: (eb_[w], 0, 0)
    grid_spec = pltpu.PrefetchScalarGridSpec(
        num_scalar_prefetch=len(plan),
        grid=(n_items,),
        in_specs=[pl.BlockSpec(memory_space=pl.ANY),
                  pl.BlockSpec((1, d), lambda w, *_: (0, 0)),
                  pl.BlockSpec((None, d, dff), ea), pl.BlockSpec((None, d, dff), ea),
                  pl.BlockSpec((None, dff, d), ea),
                  pl.BlockSpec((None, d, dff), eb), pl.BlockSpec((None, d, dff), eb),
                  pl.BlockSpec((None, dff, d), eb)],
        out_specs=pl.BlockSpec(memory_space=pl.ANY),
        scratch_shapes=[pltpu.VMEM((2, tm, de), F32), pltpu.VMEM((2, tm, d), F32),
                        pltpu.VMEM((tm, d), F32),
                        pltpu.SemaphoreType.DMA((2,)), pltpu.SemaphoreType.DMA((2,))],
    )
    return pl.pallas_call(
        functools.partial(_moe_routed_kernel, tm=tm, n_tiles=n_tiles),
        grid_spec=grid_spec,
        out_shape=jax.ShapeDtypeStruct((rows, d), F32),
        compiler_params=_cparams(("arbitrary",)),
        name="moe_routed",
    )(*plan, x2ext, nffn, wg, wu, wd, wg, wu, wd)


def _route_plan(cls, *, tm):
    n = cls.shape[0]
    n_tiles = n // tm
    n_items = n_tiles + N_CLASSES - 1
    order = jnp.argsort(cls).astype(jnp.int32)
    cs = cls[order]
    pos = jnp.arange(n, dtype=jnp.int32)
    is_start = jnp.logical_or(pos % tm == 0, cs != jnp.roll(cs, 1))
    item_of_pos = jnp.cumsum(is_start.astype(jnp.int32))
    wid = jnp.arange(n_items + 1, dtype=jnp.int32)
    start = jnp.searchsorted(item_of_pos, wid + 1, side="left").astype(jnp.int32)
    valid = start[:-1] < n
    last = jnp.maximum(item_of_pos[-1] - 1, 0)
    s0 = jnp.where(valid, start[:-1], start[last])
    tile = s0 // tm
    lo = s0 - tile * tm
    hi = jnp.where(valid, start[1:], n) - tile * tm
    c = cs[s0]
    g = c // PAIRS_PER_GROUP
    pair_a = [a for a in range(EXPERTS_PER_GROUP) for _ in range(a + 1, EXPERTS_PER_GROUP)]
    pair_b = [b for a in range(EXPERTS_PER_GROUP) for b in range(a + 1, EXPERTS_PER_GROUP)]
    pair = c % PAIRS_PER_GROUP
    e_a = g * EXPERTS_PER_GROUP + jnp.asarray(pair_a, jnp.int32)[pair]
    e_b = g * EXPERTS_PER_GROUP + jnp.asarray(pair_b, jnp.int32)[pair]
    i32 = lambda a: a.astype(jnp.int32)
    return (i32(tile), i32(e_a), i32(e_b), i32(lo), i32(hi), i32(valid), order)


def _ssm_params(lam_re, lam_im, log_step, b_re, b_im, c_re, c_im):
    n_g, n_p = lam_re.shape
    step = jnp.exp(log_step.astype(F32))[:, None]
    mag = jnp.exp(lam_re * step)
    lb_re = mag * jnp.cos(lam_im * step)
    lb_im = mag * jnp.sin(lam_im * step)
    den = lam_re * lam_re + lam_im * lam_im
    f_re = ((lb_re - 1.0) * lam_re + lb_im * lam_im) / den
    f_im = (lb_im * lam_re - (lb_re - 1.0) * lam_im) / den
    bb_re = f_re[..., None] * b_re - f_im[..., None] * b_im
    bb_im = f_re[..., None] * b_im + f_im[..., None] * b_re
    eye = jnp.eye(n_g, dtype=F32)
    ns = n_g * n_p
    width = n_g * SSM_GROUP
    b_full = jnp.concatenate(
        [jnp.einsum("hg,gpc->hcgp", eye, bb_re).reshape(width, ns),
         jnp.einsum("hg,gpc->hcgp", eye, bb_im).reshape(width, ns)], axis=1)
    n_tiles = 2 * ns // MXU_N
    bc = jnp.stack([
        b_full[((j % (n_tiles // 2)) // 2) * LANES:((j % (n_tiles // 2)) // 2 + 1) * LANES,
               j * MXU_N:(j + 1) * MXU_N] for j in range(n_tiles)]).astype(BF16)
    c_full_re = jnp.einsum("gh,gcp->gphc", eye, c_re).reshape(ns, width)
    c_full_im = -jnp.einsum("gh,gcp->gphc", eye, c_im).reshape(ns, width)
    kt = ns * MXU_N // width
    tiles = range(width // MXU_N)
    cre = jnp.stack([c_full_re[n * kt:(n + 1) * kt, n * MXU_N:(n + 1) * MXU_N] for n in tiles]).astype(BF16)
    cim = jnp.stack([c_full_im[n * kt:(n + 1) * kt, n * MXU_N:(n + 1) * MXU_N] for n in tiles]).astype(BF16)
    return bc, cre, cim, lb_re.reshape(1, ns), lb_im.reshape(1, ns)


def _rel_bias(rel_bias, q0, n_q, n_k):
    n_r = n_q + n_k - 1
    dist = q0 + n_q - 1 - jnp.arange(n_r)
    r = rel_bias.astype(F32)[:, jnp.clip(dist, -REL_CLIP, REL_CLIP) + REL_CLIP]
    r = jnp.pad(r, ((0, 0), (0, 1)))
    rows = jnp.tile(r, (1, n_q))[:, :n_q * n_r].reshape(-1, n_q, n_r)
    return rows[:, :, n_q - 1:n_q - 1 + n_k]


def _layer(xp, xs, mem_p, ck, cv, s_re, s_im, cmk, cmv, w, dims):
    batch, seq, dec_batch, dec_seq, d = dims
    width = w["w_in"].shape[1] // 4
    ns = w["lam_re"].shape[1]
    n_mem = w["n_mem"]
    tm_p = 512
    nt_p = seq // tm_p
    n_dec = dec_batch * dec_seq

    u, q, k, v, kk, vk = _in_proj(xp, w["norm_mix"], w["w_in"], w["qg"], w["kg"], w["head_mean"],
                                  nb=batch, nt=nt_p, tm=tm_p)
    zeros = jnp.zeros((batch, ns), F32)
    y_ssm, pre, pim = _ssm(u.reshape(seq * batch, width), w["bc"], w["cre"], w["cim"], w["lam_re"],
                           w["lam_im"], w["ssm_d"], zeros, zeros, nb=batch, tc=32)
    y_att = _band_prompt(q, k, v, w["bias_p"], nb=batch, t_len=seq)
    mk, mv, mkb, mvb = _mem_kv(mem_p, w["mem_in_norm"], w["w_mem_k"], w["w_mem_v"], w["mem_k_gain"], tm=512)
    tm_d = 256
    nt_d = seq // tm_d
    x2ext = _mix_mem(xp, y_ssm.reshape(seq, batch * width), y_att, mkb, mvb, w,
                     grid=(batch, nt_d), tm=tm_d,
                     row_map=lambda b, t: (b * nt_d + t, 0), ssm_map=lambda b, t: (t, b),
                     mem_map=lambda b, t: (b, 0), routed=True)
    plan = _route_plan(x2ext[:, d + ROUTE_CLS].astype(jnp.int32), tm=MOE_TILE)
    yp = _moe_routed(x2ext, plan, w["norm_ffn"], w["exp_w_gate"], w["exp_w_up"], w["exp_w_down"],
                     tm=MOE_TILE)

    us, qs, ks, vs, kks, vks = _in_proj(xs, w["norm_mix"], w["w_in"], w["qg"], w["kg"], w["head_mean"],
                                        nb=1, nt=1, tm=n_dec)
    ys_ssm, sre, sim = _ssm(us, w["bc"], w["cre"], w["cim"], w["lam_re"], w["lam_im"], w["ssm_d"],
                            s_re, s_im, nb=dec_batch, tc=dec_seq)
    tmv = lambda a: a.reshape(dec_seq, dec_batch * a.shape[1])
    ys_att = _band_sample(tmv(qs), tmv(ks), tmv(vs), ck, cv, w["bias_sc"], w["bias_sn"],
                          nb=dec_batch, s_len=dec_seq)
    x2s, xns, gates_s = _mix_mem(tmv(xs), tmv(ys_ssm), ys_att, cmk, cmv, w,
                                 grid=(dec_batch,), tm=dec_seq,
                                 row_map=lambda b: (0, b), ssm_map=lambda b: (0, b),
                                 mem_map=lambda b: (b, 0), routed=False)
    ys = _moe(xns.reshape(n_dec, d), x2s.reshape(n_dec, d), gates_s.reshape(n_dec, ROUTER_LANES),
              w["exp_w_gate"], w["exp_w_up"], w["exp_w_down"], tm=n_dec)
    return yp, ys, (kk, vk, pre, pim, mk, mv), (kks, vks, sre, sim)


def kernel(x_prompt, x_sample, mem_prompt, cache_attn_k, cache_attn_v, state_ssm_re, state_ssm_im, cache_mem_k, cache_mem_v, norm_mix, w_in, ssm_lambda_re, ssm_lambda_im, ssm_log_step, ssm_b_re, ssm_b_im, ssm_c_re, ssm_c_im, ssm_d, w_glu, att_q_gain, att_k_gain, att_rel_bias, out_norm_ssm, out_norm_att, w_out, norm_mem, mem_in_norm, w_mem_q, w_mem_k, w_mem_v, w_mem_o, mem_q_gain, mem_k_gain, norm_ffn, router_g_w, router_g_b, router_e_w, router_e_b, exp_w_gate, exp_w_up, exp_w_down):
    depth = norm_mix.shape[0]
    batch, seq, d = x_prompt.shape
    dec_batch, dec_seq, _ = x_sample.shape
    n_mem = mem_prompt.shape[1]
    att_rows = cache_attn_k.shape[2]
    n_g, n_p = ssm_lambda_re.shape[1:]
    width = n_g * SSM_GROUP
    heads = width // ATT_HEAD_DIM
    ns = n_g * n_p
    assert seq % 512 == 0 and att_rows == PAD_ROWS and seq >= PAD_ROWS
    assert (dec_batch * dec_seq) % 8 == 0 and dec_seq % 16 == 0

    xp = x_prompt.reshape(batch * seq, d)
    xs = x_sample.transpose(1, 0, 2).reshape(dec_seq * dec_batch, d)
    mem_p = mem_prompt.reshape(batch * n_mem, d)
    row = lambda a: a.reshape(1, -1).astype(F32)
    head_mean = jnp.kron(jnp.eye(heads, dtype=F32),
                         jnp.full((ATT_HEAD_DIM, ATT_HEAD_DIM), 1.0 / ATT_HEAD_DIM, F32)).astype(BF16)
    pr = jnp.arange(PAIR_ROWS)[:, None]
    pc = jnp.arange(PAIR_BAND)[None, :]
    pair_ok = jnp.where(pr < CHUNK, pc < BAND, pc >= CHUNK)
    att_scale = ATT_HEAD_DIM ** -0.5

    p_out, s_out = [], []
    for l in range(depth):
        bc, cre, cim, lb_re, lb_im = _ssm_params(ssm_lambda_re[l], ssm_lambda_im[l], ssm_log_step[l],
                                                 ssm_b_re[l], ssm_b_im[l], ssm_c_re[l], ssm_c_im[l])
        w_router = jnp.concatenate(
            [router_g_w[l], router_e_w[l].transpose(1, 0, 2).reshape(d, N_GROUPS * EXPERTS_PER_GROUP),
             jnp.zeros((d, ROUTER_LANES - N_GROUPS * (1 + EXPERTS_PER_GROUP)), F32)], axis=1)
        wr_hi, wr_lo = _split_bf16(w_router)
        r_bias = jnp.concatenate(
            [router_g_b[l], router_e_b[l].reshape(-1),
             jnp.zeros((ROUTER_LANES - N_GROUPS * (1 + EXPERTS_PER_GROUP),), F32)]).reshape(1, ROUTER_LANES)
        bias_s = _rel_bias(att_rel_bias[l], att_rows, dec_seq, att_rows + dec_seq)
        bias_p = jnp.where(pair_ok, _rel_bias(att_rel_bias[l], PAD_ROWS, PAIR_ROWS, PAIR_BAND), NEG_INF)
        w = dict(
            n_mem=n_mem,
            norm_mix=row(norm_mix[l]), w_in=w_in[l].astype(BF16),
            qg=row(jnp.tile(att_q_gain[l], heads) * att_scale), kg=row(jnp.tile(att_k_gain[l], heads)),
            head_mean=head_mean, bc=bc, cre=cre, cim=cim, lam_re=lb_re, lam_im=lb_im,
            ssm_d=row(ssm_d[l]), bias_p=bias_p,
            bias_sc=bias_s[:, :, :att_rows], bias_sn=bias_s[:, :, att_rows:],
            mem_in_norm=row(mem_in_norm[l]), w_mem_k=w_mem_k[l].astype(BF16),
            w_mem_v=w_mem_v[l].astype(BF16), mem_k_gain=row(mem_k_gain[l]),
            w_glu=w_glu[l].astype(BF16), out_norm_ssm=row(out_norm_ssm[l]),
            out_norm_att=row(out_norm_att[l]), w_out=w_out[l].astype(BF16), norm_mem=row(norm_mem[l]),
            w_mem_q=w_mem_q[l].astype(BF16), mem_q_gain=row(mem_q_gain[l]),
            w_mem_o=w_mem_o[l].astype(BF16), norm_ffn=row(norm_ffn[l]),
            w_router_hi=wr_hi, w_router_lo=wr_lo, router_bias=r_bias,
            exp_w_gate=exp_w_gate[l].astype(BF16), exp_w_up=exp_w_up[l].astype(BF16),
            exp_w_down=exp_w_down[l].astype(BF16),
        )
        xp, xs, p_new, s_new = _layer(
            xp, xs, mem_p,
            cache_attn_k[l].reshape(dec_batch * att_rows, width),
            cache_attn_v[l].reshape(dec_batch * att_rows, width),
            state_ssm_re[l].reshape(dec_batch, ns), state_ssm_im[l].reshape(dec_batch, ns),
            cache_mem_k[l].reshape(dec_batch * n_mem, d), cache_mem_v[l].reshape(dec_batch * n_mem, d),
            w, (batch, seq, dec_batch, dec_seq, d))
        p_out.append(p_new)
        s_out.append(s_new)

    sdt = state_ssm_re.dtype
    keep = min(PAD_ROWS, seq)
    kv_p = lambda a: a.reshape(batch, keep, heads, ATT_HEAD_DIM)
    kv_s = lambda a: a.reshape(dec_seq, dec_batch, heads, ATT_HEAD_DIM).transpose(1, 0, 2, 3)
    st = lambda a: a.reshape(a.shape[0], n_g, n_p).astype(sdt)
    mkv = lambda a: a.reshape(batch, n_mem, MEM_HEADS, d // MEM_HEADS)
    stack = lambda f, outs, i: jnp.stack([f(o[i]) for o in outs])
    yp = xp.reshape(batch, seq, d)
    ys = xs.reshape(dec_seq, dec_batch, d).transpose(1, 0, 2)
    return (yp, ys,
            stack(kv_p, p_out, 0), stack(kv_p, p_out, 1), stack(st, p_out, 2), stack(st, p_out, 3),
            stack(mkv, p_out, 4), stack(mkv, p_out, 5),
            stack(kv_s, s_out, 0), stack(kv_s, s_out, 1), stack(st, s_out, 2), stack(st, s_out, 3))
```

```python
import functools
import math

import jax
import jax.numpy as jnp
from jax import lax
from jax.experimental import pallas as pl
from jax.experimental.pallas import tpu as pltpu

F32 = jnp.float32
BF16 = jnp.bfloat16

CHUNK = 64
N_PREV_CHUNKS = 8
BAND = (N_PREV_CHUNKS + 1) * CHUNK
PAD_ROWS = N_PREV_CHUNKS * CHUNK
PAIR_ROWS = 2 * CHUNK
PAIR_BAND = BAND + CHUNK
SOFTMAX_ROWS = 32
SSM_GROUP = 16
SSM_STATE = 64
ATT_HEAD_DIM = 64
REL_CLIP = 128
MEM_HEADS = 4
N_GROUPS = 4
EXPERTS_PER_GROUP = 8
EPS = 1e-6
NEG_INF = -1e30

LANES = 128
MXU_N = 256
VMEM_LIMIT = 56 * 1024 * 1024
ROUTER_LANES = LANES
MOE_TILE = 128
ROUTER_E0 = N_GROUPS
PAIRS_PER_GROUP = EXPERTS_PER_GROUP * (EXPERTS_PER_GROUP - 1) // 2
N_CLASSES = N_GROUPS * PAIRS_PER_GROUP
ROUTE_CLS, ROUTE_WA, ROUTE_WB = 64, 65, 66


def _cparams(sem):
    return pltpu.CompilerParams(dimension_semantics=sem, vmem_limit_bytes=VMEM_LIMIT)


def _rms(x, gain):
    ms = jnp.mean(x * x, axis=-1, keepdims=True)
    return x * lax.rsqrt(ms + EPS) * gain


def _sigmoid(x):
    return 1.0 / (1.0 + jnp.exp(-x))


def _gelu_tanh(x):
    c = math.sqrt(2.0 / math.pi)
    return 0.5 * x * (1.0 + jnp.tanh(c * (x + 0.044715 * (x * x * x))))


def _dot(a, b):
    return jnp.dot(a, b, preferred_element_type=F32)


def _dot_nt(a, b):
    return lax.dot_general(a, b, (((1,), (1,)), ((), ())), preferred_element_type=F32)


def _split_bf16(x):
    hi = x.astype(BF16)
    lo = (x - hi.astype(F32)).astype(BF16)
    return hi, lo


def _full(shape):
    n = len(shape)
    return pl.BlockSpec(shape, lambda *_: (0,) * n)


def _in_proj_kernel(x_ref, g_ref, w_ref, qg_ref, kg_ref, hm_ref,
                    u_ref, q_ref, k_ref, v_ref, kk_ref, vk_ref, *, width, nt):
    h = _rms(x_ref[...], g_ref[...]).astype(BF16)
    z = _dot(h, w_ref[...])
    u_ref[...] = z[:, :width].astype(BF16)

    def head_norm(a, gain):
        hi, lo = _split_bf16(a * a)
        ms = _dot(hi, hm_ref[...]) + _dot(lo, hm_ref[...])
        return a * lax.rsqrt(ms + EPS) * gain

    q = head_norm(z[:, width:2 * width], qg_ref[...])
    k = head_norm(z[:, 2 * width:3 * width], kg_ref[...])
    v = z[:, 3 * width:]
    q_ref[...] = q.astype(BF16)
    k_ref[...] = k.astype(BF16)
    v_ref[...] = v.astype(BF16)

    @pl.when(pl.program_id(1) == nt - 1)
    def _():
        kk_ref[...] = k
        vk_ref[...] = v


def _in_proj(x2d, gain, w_bf, qg, kg, hm, *, nb, nt, tm):
    rows, d = x2d.shape
    width = w_bf.shape[1] // 4
    tok = lambda b, t: (b * nt + t, 0)
    out_shape = (
        jax.ShapeDtypeStruct((nt * tm, nb * width), BF16),
        jax.ShapeDtypeStruct((rows, width), BF16),
        jax.ShapeDtypeStruct((rows, width), BF16),
        jax.ShapeDtypeStruct((rows, width), BF16),
        jax.ShapeDtypeStruct((nb * tm, width), F32),
        jax.ShapeDtypeStruct((nb * tm, width), F32),
    )
    return pl.pallas_call(
        functools.partial(_in_proj_kernel, width=width, nt=nt),
        grid=(nb, nt),
        in_specs=[pl.BlockSpec((tm, d), tok), _full((1, d)), _full(w_bf.shape),
                  _full((1, width)), _full((1, width)), _full((width, width))],
        out_specs=(pl.BlockSpec((tm, width), lambda b, t: (t, b)),
                   pl.BlockSpec((tm, width), tok), pl.BlockSpec((tm, width), tok),
                   pl.BlockSpec((tm, width), tok),
                   pl.BlockSpec((tm, width), lambda b, t: (b, 0)),
                   pl.BlockSpec((tm, width), lambda b, t: (b, 0))),
        out_shape=out_shape,
        compiler_params=_cparams(("arbitrary", "arbitrary")),
        name="in_proj",
    )(x2d, gain, w_bf, qg, kg, hm)


def _ssm_kernel(u_ref, bc_ref, cre_ref, cim_ref, lre_ref, lim_ref, d_ref, s0re_ref, s0im_ref,
                y_ref, sre_ref, sim_ref, bu_ref, *, nb, tc, ns):
    @pl.when(pl.program_id(0) == 0)
    def _():
        sre_ref[...] = s0re_ref[...]
        sim_ref[...] = s0im_ref[...]

    u = u_ref[...]
    n_tiles = 2 * ns // MXU_N
    for j in range(n_tiles):
        slab = (j % (n_tiles // 2)) // 2
        bu_ref[:, j * MXU_N:(j + 1) * MXU_N] = _dot(u[:, slab * LANES:(slab + 1) * LANES], bc_ref[j])

    cw = 8 * 1024 // nb
    for cb in range(ns // cw):
        c0 = cb * cw
        lre = jnp.broadcast_to(lre_ref[:, c0:c0 + cw], (nb, cw))
        lim = jnp.broadcast_to(lim_ref[:, c0:c0 + cw], (nb, cw))

        def step(t, carry, c0=c0, lre=lre, lim=lim):
            sr, si = carry
            r0 = pl.multiple_of(t * nb, nb)
            nr = lre * sr - lim * si + bu_ref[pl.ds(r0, nb), c0:c0 + cw]
            ni = lre * si + lim * sr + bu_ref[pl.ds(r0, nb), ns + c0:ns + c0 + cw]
            bu_ref[pl.ds(r0, nb), c0:c0 + cw] = nr
            bu_ref[pl.ds(r0, nb), ns + c0:ns + c0 + cw] = ni
            return nr, ni

        sr, si = lax.fori_loop(0, tc, step, (sre_ref[:, c0:c0 + cw], sim_ref[:, c0:c0 + cw]))
        sre_ref[:, c0:c0 + cw] = sr
        sim_ref[:, c0:c0 + cw] = si

    width = u.shape[1]
    kt = ns * MXU_N // width
    for n in range(width // MXU_N):
        s_re = bu_ref[:, n * kt:(n + 1) * kt].astype(BF16)
        s_im = bu_ref[:, ns + n * kt:ns + (n + 1) * kt].astype(BF16)
        y = _dot(s_re, cre_ref[n]) + _dot(s_im, cim_ref[n])
        cols = slice(n * MXU_N, (n + 1) * MXU_N)
        y_ref[:, cols] = y + d_ref[:, cols] * u[:, cols].astype(F32)


def _ssm(u_rows, bc, cre, cim, lre, lim, dskip, s0re, s0im, *, nb, tc):
    rows, width = u_rows.shape
    ns = lre.shape[1]
    r = tc * nb
    return pl.pallas_call(
        functools.partial(_ssm_kernel, nb=nb, tc=tc, ns=ns),
        grid=(rows // r,),
        in_specs=[pl.BlockSpec((r, width), lambda i: (i, 0)), _full(bc.shape), _full(cre.shape),
                  _full(cim.shape), _full((1, ns)), _full((1, ns)), _full((1, width)),
                  _full((nb, ns)), _full((nb, ns))],
        out_specs=(pl.BlockSpec((r, width), lambda i: (i, 0)), _full((nb, ns)), _full((nb, ns))),
        out_shape=(jax.ShapeDtypeStruct((rows, width), F32),
                   jax.ShapeDtypeStruct((nb, ns), F32), jax.ShapeDtypeStruct((nb, ns), F32)),
        scratch_shapes=[pltpu.VMEM((r, 2 * ns), F32)],
        compiler_params=_cparams(("arbitrary",)),
        name="ssm",
    )(u_rows, bc, cre, cim, lre, lim, dskip, s0re, s0im)


def _half_select(shape):
    lane = lax.broadcasted_iota(jnp.int32, shape, 1)
    return lane < ATT_HEAD_DIM


def _head_masks(first):
    m0 = jnp.where(first, 1.0, 0.0).astype(BF16)
    return m0, (1.0 - m0.astype(F32)).astype(BF16)


def _band_prompt_kernel(q_ref, k_ref, v_ref, bias_ref, o_ref, kp_ref, vp_ref, s_scr, p_scr, l_scr, *, t_len):
    width = q_ref.shape[1]
    n_hp = width // LANES
    kp_ref[0:PAD_ROWS, :] = jnp.zeros((PAD_ROWS, width), BF16)
    vp_ref[0:PAD_ROWS, :] = jnp.zeros((PAD_ROWS, width), BF16)
    kp_ref[PAD_ROWS:, :] = k_ref[...]
    vp_ref[PAD_ROWS:, :] = v_ref[...]
    first = _half_select((PAIR_ROWS, LANES))
    head_mask = _head_masks(first)
    col = lax.broadcasted_iota(jnp.int32, (SOFTMAX_ROWS, PAIR_BAND), 1)

    def pair(pi, carry, *, masked):
        r0 = pl.multiple_of(pi * PAIR_ROWS, PAIR_ROWS)
        for hp in range(n_hp):
            lanes = slice(hp * LANES, (hp + 1) * LANES)
            qp = q_ref[pl.ds(r0, PAIR_ROWS), lanes]
            qs = jnp.concatenate([qp * head_mask[0], qp * head_mask[1]], axis=0)
            s_scr[2 * hp * PAIR_ROWS:2 * (hp + 1) * PAIR_ROWS, :] = _dot_nt(
                qs, kp_ref[pl.ds(r0, PAIR_BAND), lanes])
        for h in range(2 * n_hp):
            for rb in range(0, PAIR_ROWS, SOFTMAX_ROWS):
                rows = slice(h * PAIR_ROWS + rb, h * PAIR_ROWS + rb + SOFTMAX_ROWS)
                s = s_scr[rows, :] + bias_ref[h, rb:rb + SOFTMAX_ROWS, :]
                if masked:
                    s = jnp.where(col >= PAD_ROWS - r0, s, NEG_INF)
                p = jnp.exp(s - jnp.max(s, axis=-1, keepdims=True))
                p_scr[rows, :] = p.astype(BF16)
                l_scr[rows, :] = jnp.broadcast_to(1.0 / jnp.sum(p, axis=-1, keepdims=True),
                                                  (SOFTMAX_ROWS, LANES))
        for hp in range(n_hp):
            lanes = slice(hp * LANES, (hp + 1) * LANES)
            rows = slice(2 * hp * PAIR_ROWS, 2 * (hp + 1) * PAIR_ROWS)
            o2 = _dot(p_scr[rows, :], vp_ref[pl.ds(r0, PAIR_BAND), lanes]) * l_scr[rows, :]
            o_ref[pl.ds(r0, PAIR_ROWS), lanes] = jnp.where(
                first, o2[:PAIR_ROWS], o2[PAIR_ROWS:]).astype(BF16)
        return carry

    n_masked = PAD_ROWS // PAIR_ROWS
    lax.fori_loop(0, n_masked, functools.partial(pair, masked=True), 0)
    lax.fori_loop(n_masked, t_len // PAIR_ROWS, functools.partial(pair, masked=False), 0)


def _band_prompt(q, k, v, bias, *, nb, t_len):
    width = q.shape[1]
    heads = width // ATT_HEAD_DIM
    blk = pl.BlockSpec((t_len, width), lambda b: (b, 0))
    return pl.pallas_call(
        functools.partial(_band_prompt_kernel, t_len=t_len),
        grid=(nb,),
        in_specs=[blk, blk, blk, _full(bias.shape)],
        out_specs=blk,
        out_shape=jax.ShapeDtypeStruct(q.shape, BF16),
        scratch_shapes=[pltpu.VMEM((t_len + PAD_ROWS, width), BF16),
                        pltpu.VMEM((t_len + PAD_ROWS, width), BF16),
                        pltpu.VMEM((heads * PAIR_ROWS, PAIR_BAND), F32),
                        pltpu.VMEM((heads * PAIR_ROWS, PAIR_BAND), BF16),
                        pltpu.VMEM((heads * PAIR_ROWS, LANES), F32)],
        compiler_params=_cparams(("arbitrary",)),
        name="band_prompt",
    )(q, k, v, bias)


def _band_sample_kernel(q_ref, k_ref, v_ref, ck_ref, cv_ref, bc_ref, bn_ref, o_ref):
    width = q_ref.shape[1]
    s_len = q_ref.shape[0]
    first = _half_select((s_len, LANES))
    head_mask = _head_masks(first)
    for hp in range(width // LANES):
        lanes = slice(hp * LANES, (hp + 1) * LANES)
        qp = q_ref[:, lanes]
        kc = ck_ref[:, lanes].astype(BF16)
        vc = cv_ref[:, lanes].astype(BF16)
        kn = k_ref[:, lanes]
        vn = v_ref[:, lanes]
        outs = []
        for hh in range(2):
            qm = qp * head_mask[hh]
            sc = _dot_nt(qm, kc) + bc_ref[2 * hp + hh]
            sn = _dot_nt(qm, kn) + bn_ref[2 * hp + hh]
            m = jnp.maximum(jnp.max(sc, axis=-1, keepdims=True), jnp.max(sn, axis=-1, keepdims=True))
            pc = jnp.exp(sc - m)
            pn = jnp.exp(sn - m)
            l = jnp.sum(pc, axis=-1, keepdims=True) + jnp.sum(pn, axis=-1, keepdims=True)
            outs.append((_dot(pc.astype(BF16), vc) + _dot(pn.astype(BF16), vn)) / l)
        o_ref[:, lanes] = jnp.where(first, outs[0], outs[1]).astype(BF16)


def _band_sample(q_tm, k_tm, v_tm, cache_k, cache_v, bias_c, bias_n, *, nb, s_len):
    width = q_tm.shape[1] // nb
    w_rows = cache_k.shape[0] // nb
    col = pl.BlockSpec((s_len, width), lambda b: (0, b))
    cache = pl.BlockSpec((w_rows, width), lambda b: (b, 0))
    return pl.pallas_call(
        _band_sample_kernel,
        grid=(nb,),
        in_specs=[col, col, col, cache, cache, _full(bias_c.shape), _full(bias_n.shape)],
        out_specs=col,
        out_shape=jax.ShapeDtypeStruct(q_tm.shape, BF16),
        compiler_params=_cparams(("arbitrary",)),
        name="band_sample",
    )(q_tm, k_tm, v_tm, cache_k, cache_v, bias_c, bias_n)


def _mem_kv_kernel(m_ref, g_ref, wk_ref, wv_ref, kg_ref, k_ref, v_ref, kb_ref, vb_ref):
    m = _rms(m_ref[...], g_ref[...]).astype(BF16)
    k = _dot(m, wk_ref[...])
    v = _dot(m, wv_ref[...])
    hd = kg_ref.shape[1]
    for h in range(k.shape[1] // hd):
        cols = slice(h * hd, (h + 1) * hd)
        kh = _rms(k[:, cols], kg_ref[...])
        k_ref[:, cols] = kh
        kb_ref[:, cols] = kh.astype(BF16)
    v_ref[...] = v
    vb_ref[...] = v.astype(BF16)


def _mem_kv(mem2d, gain, wk, wv, kgain, *, tm):
    rows, d = mem2d.shape
    blk = pl.BlockSpec((tm, d), lambda i: (i, 0))
    return pl.pallas_call(
        _mem_kv_kernel,
        grid=(rows // tm,),
        in_specs=[blk, _full((1, d)), _full(wk.shape), _full(wv.shape), _full(kgain.shape)],
        out_specs=(blk, blk, blk, blk),
        out_shape=(jax.ShapeDtypeStruct((rows, d), F32), jax.ShapeDtypeStruct((rows, d), F32),
                   jax.ShapeDtypeStruct((rows, d), BF16), jax.ShapeDtypeStruct((rows, d), BF16)),
        compiler_params=_cparams(("arbitrary",)),
        name="mem_kv",
    )(mem2d, gain, wk, wv, kgain)


def _route(logits):
    lane = lax.broadcasted_iota(jnp.int32, logits.shape, 1).astype(F32)
    big = float(ROUTER_LANES)
    is_g = lane < N_GROUPS
    lg = jnp.where(is_g, logits, NEG_INF)
    gmax = jnp.max(lg, axis=-1, keepdims=True)
    p1 = 1.0 / jnp.sum(jnp.where(is_g, jnp.exp(lg - gmax), 0.0), axis=-1, keepdims=True)
    g_idx = jnp.min(jnp.where(lg == gmax, lane, big), axis=-1, keepdims=True)
    lo = ROUTER_E0 + g_idx * EXPERTS_PER_GROUP
    le = jnp.where((lane >= lo) & (lane < lo + EXPERTS_PER_GROUP), logits, NEG_INF)
    v1 = jnp.max(le, axis=-1, keepdims=True)
    i1 = jnp.min(jnp.where(le == v1, lane, big), axis=-1, keepdims=True)
    le2 = jnp.where(lane == i1, NEG_INF, le)
    v2 = jnp.max(le2, axis=-1, keepdims=True)
    i2 = jnp.min(jnp.where(le2 == v2, lane, big), axis=-1, keepdims=True)
    e2 = jnp.exp(v2 - v1)
    w1 = p1 / (1.0 + e2)
    w2 = p1 * e2 / (1.0 + e2)
    gates = jnp.where(lane == i1, w1, 0.0) + jnp.where(lane == i2, w2, 0.0)
    a = jnp.minimum(i1, i2) - lo
    b = jnp.maximum(i1, i2) - lo
    pair = a * (2 * EXPERTS_PER_GROUP - 1 - a) * 0.5 + (b - a - 1.0)
    cls = g_idx * PAIRS_PER_GROUP + pair
    w_a = jnp.where(i1 < i2, w1, w2)
    w_b = jnp.where(i1 < i2, w2, w1)
    return (gates + jnp.where(lane == ROUTE_CLS, cls, 0.0) + jnp.where(lane == ROUTE_WA, w_a, 0.0)
            + jnp.where(lane == ROUTE_WB, w_b, 0.0))


def _mix_mem_kernel(x_ref, ys_ref, ya_ref, mk_ref, mv_ref, wglu_ref, ons_ref, ona_ref, wout_ref,
                    nmem_ref, wq_ref, qg_ref, wo_ref, nffn_ref, wrh_ref, wrl_ref, rb_ref,
                    *outs, routed):
    if routed:
        x2_ref, o_scr = outs
    else:
        x2_ref, xn_ref, gate_ref, o_scr = outs
    width = ys_ref.shape[1]
    g = _dot(_gelu_tanh(ys_ref[...]).astype(BF16), wglu_ref[...])
    y_s = g[:, :width] * _sigmoid(g[:, width:])
    cat_s = _rms(y_s, ons_ref[...]).astype(BF16)
    cat_a = _rms(ya_ref[...].astype(F32), ona_ref[...]).astype(BF16)
    x1 = x_ref[...] + _dot(cat_s, wout_ref[0:width, :]) + _dot(cat_a, wout_ref[width:, :])

    q = _dot(_rms(x1, nmem_ref[...]).astype(BF16), wq_ref[...])
    hd = qg_ref.shape[1]
    scale = hd ** -0.5
    for h in range(q.shape[1] // hd):
        cols = slice(h * hd, (h + 1) * hd)
        qh = _rms(q[:, cols], qg_ref[...]).astype(BF16)
        s = _dot_nt(qh, mk_ref[:, cols].astype(BF16)) * scale
        p = jnp.exp(s - jnp.max(s, axis=-1, keepdims=True))
        l = jnp.sum(p, axis=-1, keepdims=True)
        o_scr[:, cols] = (_dot(p.astype(BF16), mv_ref[:, cols].astype(BF16)) / l).astype(BF16)
    acc = x1 + _dot(o_scr[...], wo_ref[...])
    d = acc.shape[1]
    x2_ref[:, :d] = acc

    xn = _rms(acc, nffn_ref[...])
    xh, xl = _split_bf16(xn)
    logits = _dot(xh, wrh_ref[...]) + _dot(xl, wrh_ref[...]) + _dot(xh, wrl_ref[...]) + rb_ref[...]
    if routed:
        x2_ref[:, d:] = _route(logits)
    else:
        xn_ref[...] = xh
        gate_ref[...] = _route(logits)


def _mix_mem(x, ys, ya, mk, mv, weights, *, grid, tm, row_map, ssm_map, mem_map, routed):
    d = weights["w_out"].shape[1]
    width = weights["w_glu"].shape[0]
    names = ("w_glu", "out_norm_ssm", "out_norm_att", "w_out", "norm_mem", "w_mem_q", "mem_q_gain",
             "w_mem_o", "norm_ffn", "w_router_hi", "w_router_lo", "router_bias")
    ws = [weights[n] for n in names]
    mem_rows = weights["n_mem"]
    xspec = pl.BlockSpec((tm, d), row_map)
    hspec = pl.BlockSpec((tm, width), row_map)
    sspec = pl.BlockSpec((tm, width), ssm_map)
    mspec = pl.BlockSpec((mem_rows, d), mem_map)
    n_col = x.shape[1] // d
    if routed:
        out_specs = pl.BlockSpec((tm, d + ROUTER_LANES), row_map)
        out_shape = jax.ShapeDtypeStruct((x.shape[0], n_col * (d + ROUTER_LANES)), F32)
    else:
        out_specs = (xspec, xspec, pl.BlockSpec((tm, ROUTER_LANES), row_map))
        out_shape = (jax.ShapeDtypeStruct(x.shape, F32), jax.ShapeDtypeStruct(x.shape, BF16),
                     jax.ShapeDtypeStruct((x.shape[0], n_col * ROUTER_LANES), F32))
    return pl.pallas_call(
        functools.partial(_mix_mem_kernel, routed=routed),
        grid=grid,
        in_specs=[xspec, sspec, hspec, mspec, mspec] + [_full(w.shape) for w in ws],
        out_specs=out_specs,
        out_shape=out_shape,
        scratch_shapes=[pltpu.VMEM((tm, d), BF16)],
        compiler_params=_cparams(("arbitrary",) * len(grid)),
        name="mix_mem",
    )(x, ys, ya, mk, mv, *ws)


def _moe_kernel(xn_ref, x2_ref, gate_ref, wg_ref, wu_ref, wd_ref, o_ref):
    e = pl.program_id(1)

    @pl.when(e == 0)
    def _():
        o_ref[...] = x2_ref[...]

    gates = gate_ref[...]
    lane = lax.broadcasted_iota(jnp.int32, gates.shape, 1)
    ge = jnp.sum(jnp.where(lane == e + ROUTER_E0, gates, 0.0), axis=-1, keepdims=True)
    xn = xn_ref[...]
    a = _dot(xn, wg_ref[...])
    h = a * _sigmoid(a) * _dot(xn, wu_ref[...])
    o_ref[...] += ge * _dot(h.astype(BF16), wd_ref[...])


def _moe(xn, x2, gates, wg, wu, wd, *, tm):
    rows, d = xn.shape
    n_exp, _, dff = wg.shape
    row = lambda i, e: (i, 0)
    return pl.pallas_call(
        _moe_kernel,
        grid=(rows // tm, n_exp),
        in_specs=[pl.BlockSpec((tm, d), row), pl.BlockSpec((tm, d), row),
                  pl.BlockSpec((tm, ROUTER_LANES), row),
                  pl.BlockSpec((None, d, dff), lambda i, e: (e, 0, 0)),
                  pl.BlockSpec((None, d, dff), lambda i, e: (e, 0, 0)),
                  pl.BlockSpec((None, dff, d), lambda i, e: (e, 0, 0))],
        out_specs=pl.BlockSpec((tm, d), row),
        out_shape=jax.ShapeDtypeStruct((rows, d), F32),
        compiler_params=_cparams(("arbitrary", "arbitrary")),
        name="moe",
    )(xn, x2, gates, wg, wu, wd)


def _moe_routed_kernel(tile_ref, ea_ref, eb_ref, lo_ref, hi_ref, valid_ref, tok_ref,
                       x_hbm, nffn_ref, wga_ref, wua_ref, wda_ref, wgb_ref, wub_ref, wdb_ref,
                       y_hbm, xbuf0, xbuf1, obuf0, obuf1, acc_ref, gsem, ssem, *, tm, n_tiles):
    w = pl.program_id(0)
    t = tile_ref[w]
    lo = lo_ref[w]
    hi = hi_ref[w]
    valid = valid_ref[w] == 1
    d = y_hbm.shape[1]
    xbuf = (xbuf0, xbuf1)
    obuf = (obuf0, obuf1)

    def row_copy_in(tile, s, r):
        tok = tok_ref[tile * tm + r]
        return pltpu.make_async_copy(x_hbm.at[pl.ds(tok, 1)], xbuf[s].at[pl.ds(r, 1)], gsem.at[s])

    def row_copy_out(tile, s, r):
        tok = tok_ref[tile * tm + r]
        return pltpu.make_async_copy(obuf[s].at[pl.ds(r, 1)], y_hbm.at[pl.ds(tok, 1)], ssem.at[s])

    def start_gather(tile, s):
        for r in range(tm):
            row_copy_in(tile, s, r).start()

    def start_scatter(tile, s):
        for r in range(tm):
            row_copy_out(tile, s, r).start()

    def wait_gather(s):
        pltpu.make_async_copy(x_hbm.at[pl.ds(0, tm)], xbuf[s], gsem.at[s]).wait()

    def wait_scatter(s):
        pltpu.make_async_copy(obuf[s], y_hbm.at[pl.ds(0, tm)], ssem.at[s]).wait()

    def compute(s, first):
        xe = xbuf[s][...]
        xn = _rms(xe[:, :d], nffn_ref[...]).astype(BF16)
        row = lax.broadcasted_iota(jnp.int32, (tm, 1), 0)
        inseg = jnp.logical_and(row >= lo, row < hi)
        w_a = jnp.where(inseg, xe[:, d + ROUTE_WA:d + ROUTE_WA + 1], 0.0)
        w_b = jnp.where(inseg, xe[:, d + ROUTE_WB:d + ROUTE_WB + 1], 0.0)
        a = _dot(xn, wga_ref[...])
        h_a = (a * _sigmoid(a) * _dot(xn, wua_ref[...])).astype(BF16)
        b = _dot(xn, wgb_ref[...])
        h_b = (b * _sigmoid(b) * _dot(xn, wub_ref[...])).astype(BF16)
        upd = w_a * _dot(h_a, wda_ref[...]) + w_b * _dot(h_b, wdb_ref[...])
        if first:
            acc_ref[...] = upd
        else:
            acc_ref[...] += upd

    def when(*conds):
        c = conds[0]
        for extra in conds[1:]:
            c = jnp.logical_and(c, extra)
        return pl.when(c)

    @pl.when(w == 0)
    def _():
        start_gather(0, 0)

    is_first = jnp.logical_and(valid, lo == 0)
    is_last = jnp.logical_and(valid, hi == tm)
    last_tile = n_tiles - 1
    for s in (0, 1):
        mine = t % 2 == s

        if s == 0:
            @when(is_first, t == 0)
            def _():
                wait_gather(0)
                start_gather(1, 1)
                compute(0, True)

        @when(is_first, mine, t > 0, t < last_tile)
        def _(s=s):
            wait_gather(s)
            start_gather(t + 1, 1 - s)
            start_scatter(t - 1, 1 - s)
            compute(s, True)

        if s == last_tile % 2:
            @when(is_first, t == last_tile)
            def _(s=s):
                wait_gather(s)
                start_scatter(t - 1, 1 - s)
                compute(s, True)

        @when(valid, mine, lo > 0)
        def _(s=s):
            compute(s, False)

        @when(is_last, mine)
        def _(s=s):
            @pl.when(t >= 2)
            def _():
                wait_scatter(s)

            obuf[s][...] = xbuf[s][:, :d] + acc_ref[...]

            if s == last_tile % 2:
                @pl.when(t == last_tile)
                def _():
                    start_scatter(t, s)
                    wait_scatter(s)
                    wait_scatter(1 - s)


def _moe_routed(x2ext, plan, nffn, wg, wu, wd, *, tm):
    rows, de = x2ext.shape
    d = de - ROUTER_LANES
    n_exp, _, dff = wg.shape
    n_tiles = rows // tm
    assert rows % tm == 0 and n_tiles >= 3
    n_items = plan[0].shape[0]
    ea = lambda w, tile, ea_, eb_, *_: (ea_[w], 0, 0)
    eb = lambda w, tile, ea_, eb_, *_: (eb_[w], 0, 0)
    grid_spec = pltpu.PrefetchScalarGridSpec(
        num_scalar_prefetch=len(plan),
        grid=(n_items,),
        in_specs=[pl.BlockSpec(memory_space=pl.ANY),
                  pl.BlockSpec((1, d), lambda w, *_: (0, 0)),
                  pl.BlockSpec((None, d, dff), ea), pl.BlockSpec((None, d, dff), ea),
                  pl.BlockSpec((None, dff, d), ea),
                  pl.BlockSpec((None, d, dff), eb), pl.BlockSpec((None, d, dff), eb),
                  pl.BlockSpec((None, dff, d), eb)],
        out_specs=pl.BlockSpec(memory_space=pl.ANY),
        scratch_shapes=[pltpu.VMEM((tm, de), F32), pltpu.VMEM((tm, de), F32),
                        pltpu.VMEM((tm, d), F32), pltpu.VMEM((tm, d), F32),
                        pltpu.VMEM((tm, d), F32),
                        pltpu.SemaphoreType.DMA((2,)), pltpu.SemaphoreType.DMA((2,))],
    )
    return pl.pallas_call(
        functools.partial(_moe_routed_kernel, tm=tm, n_tiles=n_tiles),
        grid_spec=grid_spec,
        out_shape=jax.ShapeDtypeStruct((rows, d), F32),
        compiler_params=_cparams(("arbitrary",)),
        name="moe_routed",
    )(*plan, x2ext, nffn, wg, wu, wd, wg, wu, wd)


def _route_plan(cls, *, tm):
    n = cls.shape[0]
    n_tiles = n // tm
    order = jnp.argsort(cls).astype(jnp.int32)
    classes = jnp.arange(N_CLASSES, dtype=jnp.int32)
    class_end = jnp.sum((cls[:, None] <= classes[None, :]).astype(jnp.int32), axis=0)
    class_start = jnp.concatenate([jnp.zeros((1,), jnp.int32), class_end[:-1]])
    bounds = jnp.concatenate([jnp.arange(n_tiles, dtype=jnp.int32) * tm,
                              jnp.where(class_end > class_start, class_start, n)])
    n_items = bounds.shape[0]
    idx = jnp.arange(n_items, dtype=jnp.int32)
    before = jnp.logical_or(bounds[None, :] < bounds[:, None],
                            jnp.logical_and(bounds[None, :] == bounds[:, None], idx[None, :] < idx[:, None]))
    rank = jnp.sum(before.astype(jnp.int32), axis=1)
    start = jnp.sum(jnp.where(rank[:, None] == idx[None, :], bounds[:, None], 0), axis=0)
    stop = jnp.concatenate([start[1:], jnp.full((1,), n, jnp.int32)])
    valid = stop > start
    tile = jnp.minimum(start // tm, n_tiles - 1)
    lo = start - tile * tm
    hi = stop - tile * tm
    c = jnp.minimum(jnp.sum((class_end[None, :] <= start[:, None]).astype(jnp.int32), axis=1),
                    N_CLASSES - 1)
    g = c // PAIRS_PER_GROUP
    pair = c % PAIRS_PER_GROUP
    a = jnp.zeros_like(pair)
    for k in range(1, EXPERTS_PER_GROUP - 1):
        a = a + (pair >= k * (2 * EXPERTS_PER_GROUP - 1 - k) // 2).astype(jnp.int32)
    b = pair - a * (2 * EXPERTS_PER_GROUP - 1 - a) // 2 + a + 1
    e_a = g * EXPERTS_PER_GROUP + a
    e_b = g * EXPERTS_PER_GROUP + b
    i32 = lambda v: v.astype(jnp.int32)
    return (i32(tile), i32(e_a), i32(e_b), i32(lo), i32(hi), i32(valid), order)


def _ssm_params(lam_re, lam_im, log_step, b_re, b_im, c_re, c_im):
    n_g, n_p = lam_re.shape
    step = jnp.exp(log_step.astype(F32))[:, None]
    mag = jnp.exp(lam_re * step)
    lb_re = mag * jnp.cos(lam_im * step)
    lb_im = mag * jnp.sin(lam_im * step)
    den = lam_re * lam_re + lam_im * lam_im
    f_re = ((lb_re - 1.0) * lam_re + lb_im * lam_im) / den
    f_im = (lb_im * lam_re - (lb_re - 1.0) * lam_im) / den
    bb_re = f_re[..., None] * b_re - f_im[..., None] * b_im
    bb_im = f_re[..., None] * b_im + f_im[..., None] * b_re
    eye = jnp.eye(n_g, dtype=F32)
    ns = n_g * n_p
    width = n_g * SSM_GROUP
    b_full = jnp.concatenate(
        [jnp.einsum("hg,gpc->hcgp", eye, bb_re).reshape(width, ns),
         jnp.einsum("hg,gpc->hcgp", eye, bb_im).reshape(width, ns)], axis=1)
    n_tiles = 2 * ns // MXU_N
    bc = jnp.stack([
        b_full[((j % (n_tiles // 2)) // 2) * LANES:((j % (n_tiles // 2)) // 2 + 1) * LANES,
               j * MXU_N:(j + 1) * MXU_N] for j in range(n_tiles)]).astype(BF16)
    c_full_re = jnp.einsum("gh,gcp->gphc", eye, c_re).reshape(ns, width)
    c_full_im = -jnp.einsum("gh,gcp->gphc", eye, c_im).reshape(ns, width)
    kt = ns * MXU_N // width
    tiles = range(width // MXU_N)
    cre = jnp.stack([c_full_re[n * kt:(n + 1) * kt, n * MXU_N:(n + 1) * MXU_N] for n in tiles]).astype(BF16)
    cim = jnp.stack([c_full_im[n * kt:(n + 1) * kt, n * MXU_N:(n + 1) * MXU_N] for n in tiles]).astype(BF16)
    return bc, cre, cim, lb_re.reshape(1, ns), lb_im.reshape(1, ns)


def _rel_bias(rel_bias, q0, n_q, n_k):
    n_r = n_q + n_k - 1
    dist = q0 + n_q - 1 - jnp.arange(n_r)
    r = rel_bias.astype(F32)[:, jnp.clip(dist, -REL_CLIP, REL_CLIP) + REL_CLIP]
    r = jnp.pad(r, ((0, 0), (0, 1)))
    rows = jnp.tile(r, (1, n_q))[:, :n_q * n_r].reshape(-1, n_q, n_r)
    return rows[:, :, n_q - 1:n_q - 1 + n_k]


def _layer(xp, xs, mem_p, ck, cv, s_re, s_im, cmk, cmv, w, dims):
    batch, seq, dec_batch, dec_seq, d = dims
    width = w["w_in"].shape[1] // 4
    ns = w["lam_re"].shape[1]
    n_mem = w["n_mem"]
    tm_p = 512
    nt_p = seq // tm_p
    n_dec = dec_batch * dec_seq

    u, q, k, v, kk, vk = _in_proj(xp, w["norm_mix"], w["w_in"], w["qg"], w["kg"], w["head_mean"],
                                  nb=batch, nt=nt_p, tm=tm_p)
    zeros = jnp.zeros((batch, ns), F32)
    y_ssm, pre, pim = _ssm(u.reshape(seq * batch, width), w["bc"], w["cre"], w["cim"], w["lam_re"],
                           w["lam_im"], w["ssm_d"], zeros, zeros, nb=batch, tc=32)
    y_att = _band_prompt(q, k, v, w["bias_p"], nb=batch, t_len=seq)
    mk, mv, mkb, mvb = _mem_kv(mem_p, w["mem_in_norm"], w["w_mem_k"], w["w_mem_v"], w["mem_k_gain"], tm=512)
    tm_d = 256
    nt_d = seq // tm_d
    x2ext = _mix_mem(xp, y_ssm.reshape(seq, batch * width), y_att, mkb, mvb, w,
                     grid=(batch, nt_d), tm=tm_d,
                     row_map=lambda b, t: (b * nt_d + t, 0), ssm_map=lambda b, t: (t, b),
                     mem_map=lambda b, t: (b, 0), routed=True)
    plan = _route_plan(x2ext[:, d + ROUTE_CLS].astype(jnp.int32), tm=MOE_TILE)
    yp = _moe_routed(x2ext, plan, w["norm_ffn"], w["exp_w_gate"], w["exp_w_up"], w["exp_w_down"],
                     tm=MOE_TILE)

    us, qs, ks, vs, kks, vks = _in_proj(xs, w["norm_mix"], w["w_in"], w["qg"], w["kg"], w["head_mean"],
                                        nb=1, nt=1, tm=n_dec)
    ys_ssm, sre, sim = _ssm(us, w["bc"], w["cre"], w["cim"], w["lam_re"], w["lam_im"], w["ssm_d"],
                            s_re, s_im, nb=dec_batch, tc=dec_seq)
    tmv = lambda a: a.reshape(dec_seq, dec_batch * a.shape[1])
    ys_att = _band_sample(tmv(qs), tmv(ks), tmv(vs), ck, cv, w["bias_sc"], w["bias_sn"],
                          nb=dec_batch, s_len=dec_seq)
    x2s, xns, gates_s = _mix_mem(tmv(xs), tmv(ys_ssm), ys_att, cmk, cmv, w,
                                 grid=(dec_batch,), tm=dec_seq,
                                 row_map=lambda b: (0, b), ssm_map=lambda b: (0, b),
                                 mem_map=lambda b: (b, 0), routed=False)
    ys = _moe(xns.reshape(n_dec, d), x2s.reshape(n_dec, d), gates_s.reshape(n_dec, ROUTER_LANES),
              w["exp_w_gate"], w["exp_w_up"], w["exp_w_down"], tm=n_dec)
    return yp, ys, (kk, vk, pre, pim, mk, mv), (kks, vks, sre, sim)


def kernel(x_prompt, x_sample, mem_prompt, cache_attn_k, cache_attn_v, state_ssm_re, state_ssm_im, cache_mem_k, cache_mem_v, norm_mix, w_in, ssm_lambda_re, ssm_lambda_im, ssm_log_step, ssm_b_re, ssm_b_im, ssm_c_re, ssm_c_im, ssm_d, w_glu, att_q_gain, att_k_gain, att_rel_bias, out_norm_ssm, out_norm_att, w_out, norm_mem, mem_in_norm, w_mem_q, w_mem_k, w_mem_v, w_mem_o, mem_q_gain, mem_k_gain, norm_ffn, router_g_w, router_g_b, router_e_w, router_e_b, exp_w_gate, exp_w_up, exp_w_down):
    depth = norm_mix.shape[0]
    batch, seq, d = x_prompt.shape
    dec_batch, dec_seq, _ = x_sample.shape
    n_mem = mem_prompt.shape[1]
    att_rows = cache_attn_k.shape[2]
    n_g, n_p = ssm_lambda_re.shape[1:]
    width = n_g * SSM_GROUP
    heads = width // ATT_HEAD_DIM
    ns = n_g * n_p
    assert seq % 512 == 0 and att_rows == PAD_ROWS and seq >= PAD_ROWS
    assert (dec_batch * dec_seq) % 8 == 0 and dec_seq % 16 == 0

    xp = x_prompt.reshape(batch * seq, d)
    xs = x_sample.transpose(1, 0, 2).reshape(dec_seq * dec_batch, d)
    mem_p = mem_prompt.reshape(batch * n_mem, d)
    row = lambda a: a.reshape(1, -1).astype(F32)
    head_mean = jnp.kron(jnp.eye(heads, dtype=F32),
                         jnp.full((ATT_HEAD_DIM, ATT_HEAD_DIM), 1.0 / ATT_HEAD_DIM, F32)).astype(BF16)
    pr = jnp.arange(PAIR_ROWS)[:, None]
    pc = jnp.arange(PAIR_BAND)[None, :]
    pair_ok = jnp.where(pr < CHUNK, pc < BAND, pc >= CHUNK)
    att_scale = ATT_HEAD_DIM ** -0.5

    p_out, s_out = [], []
    for l in range(depth):
        bc, cre, cim, lb_re, lb_im = _ssm_params(ssm_lambda_re[l], ssm_lambda_im[l], ssm_log_step[l],
                                                 ssm_b_re[l], ssm_b_im[l], ssm_c_re[l], ssm_c_im[l])
        w_router = jnp.concatenate(
            [router_g_w[l], router_e_w[l].transpose(1, 0, 2).reshape(d, N_GROUPS * EXPERTS_PER_GROUP),
             jnp.zeros((d, ROUTER_LANES - N_GROUPS * (1 + EXPERTS_PER_GROUP)), F32)], axis=1)
        wr_hi, wr_lo = _split_bf16(w_router)
        r_bias = jnp.concatenate(
            [router_g_b[l], router_e_b[l].reshape(-1),
             jnp.zeros((ROUTER_LANES - N_GROUPS * (1 + EXPERTS_PER_GROUP),), F32)]).reshape(1, ROUTER_LANES)
        bias_s = _rel_bias(att_rel_bias[l], att_rows, dec_seq, att_rows + dec_seq)
        bias_p = jnp.where(pair_ok, _rel_bias(att_rel_bias[l], PAD_ROWS, PAIR_ROWS, PAIR_BAND), NEG_INF)
        w = dict(
            n_mem=n_mem,
            norm_mix=row(norm_mix[l]), w_in=w_in[l].astype(BF16),
            qg=row(jnp.tile(att_q_gain[l], heads) * att_scale), kg=row(jnp.tile(att_k_gain[l], heads)),
            head_mean=head_mean, bc=bc, cre=cre, cim=cim, lam_re=lb_re, lam_im=lb_im,
            ssm_d=row(ssm_d[l]), bias_p=bias_p,
            bias_sc=bias_s[:, :, :att_rows], bias_sn=bias_s[:, :, att_rows:],
            mem_in_norm=row(mem_in_norm[l]), w_mem_k=w_mem_k[l].astype(BF16),
            w_mem_v=w_mem_v[l].astype(BF16), mem_k_gain=row(mem_k_gain[l]),
            w_glu=w_glu[l].astype(BF16), out_norm_ssm=row(out_norm_ssm[l]),
            out_norm_att=row(out_norm_att[l]), w_out=w_out[l].astype(BF16), norm_mem=row(norm_mem[l]),
            w_mem_q=w_mem_q[l].astype(BF16), mem_q_gain=row(mem_q_gain[l]),
            w_mem_o=w_mem_o[l].astype(BF16), norm_ffn=row(norm_ffn[l]),
            w_router_hi=wr_hi, w_router_lo=wr_lo, router_bias=r_bias,
            exp_w_gate=exp_w_gate[l].astype(BF16), exp_w_up=exp_w_up[l].astype(BF16),
            exp_w_down=exp_w_down[l].astype(BF16),
        )
        xp, xs, p_new, s_new = _layer(
            xp, xs, mem_p,
            cache_attn_k[l].reshape(dec_batch * att_rows, width),
            cache_attn_v[l].reshape(dec_batch * att_rows, width),
            state_ssm_re[l].reshape(dec_batch, ns), state_ssm_im[l].reshape(dec_batch, ns),
            cache_mem_k[l].reshape(dec_batch * n_mem, d), cache_mem_v[l].reshape(dec_batch * n_mem, d),
            w, (batch, seq, dec_batch, dec_seq, d))
        p_out.append(p_new)
        s_out.append(s_new)

    sdt = state_ssm_re.dtype
    keep = min(PAD_ROWS, seq)
    kv_p = lambda a: a.reshape(batch, keep, heads, ATT_HEAD_DIM)
    kv_s = lambda a: a.reshape(dec_seq, dec_batch, heads, ATT_HEAD_DIM).transpose(1, 0, 2, 3)
    st = lambda a: a.reshape(a.shape[0], n_g, n_p).astype(sdt)
    mkv = lambda a: a.reshape(batch, n_mem, MEM_HEADS, d // MEM_HEADS)
    stack = lambda f, outs, i: jnp.stack([f(o[i]) for o in outs])
    yp = xp.reshape(batch, seq, d)
    ys = xs.reshape(dec_seq, dec_batch, d).transpose(1, 0, 2)
    return (yp, ys,
            stack(kv_p, p_out, 0), stack(kv_p, p_out, 1), stack(st, p_out, 2), stack(st, p_out, 3),
            stack(mkv, p_out, 4), stack(mkv, p_out, 5),
            stack(kv_s, s_out, 0), stack(kv_s, s_out, 1), stack(st, s_out, 2), stack(st, s_out, 3))
```

```python
import functools
import math

import jax
import jax.numpy as jnp
from jax import lax
from jax.experimental import pallas as pl
from jax.experimental.pallas import tpu as pltpu

F32 = jnp.float32
BF16 = jnp.bfloat16

CHUNK = 64
N_PREV_CHUNKS = 8
BAND = (N_PREV_CHUNKS + 1) * CHUNK
PAD_ROWS = N_PREV_CHUNKS * CHUNK
PAIR_ROWS = 2 * CHUNK
PAIR_BAND = BAND + CHUNK
SOFTMAX_ROWS = 32
SSM_GROUP = 16
SSM_STATE = 64
ATT_HEAD_DIM = 64
REL_CLIP = 128
MEM_HEADS = 4
N_GROUPS = 4
EXPERTS_PER_GROUP = 8
EPS = 1e-6
NEG_INF = -1e30

LANES = 128
MXU_N = 256
VMEM_LIMIT = 56 * 1024 * 1024
ROUTER_LANES = LANES
MOE_TILE = 128
ROUTER_E0 = N_GROUPS
PAIRS_PER_GROUP = EXPERTS_PER_GROUP * (EXPERTS_PER_GROUP - 1) // 2
N_CLASSES = N_GROUPS * PAIRS_PER_GROUP
ROUTE_CLS, ROUTE_WA, ROUTE_WB = 64, 65, 66
SLAB_IN = 9
SLAB_OUT = 8


def _cparams(sem):
    return pltpu.CompilerParams(dimension_semantics=sem, vmem_limit_bytes=VMEM_LIMIT)


def _rms(x, gain):
    ms = jnp.mean(x * x, axis=-1, keepdims=True)
    return x * lax.rsqrt(ms + EPS) * gain


def _sigmoid(x):
    return 1.0 / (1.0 + jnp.exp(-x))


def _gelu_tanh(x):
    c = math.sqrt(2.0 / math.pi)
    return 0.5 * x * (1.0 + jnp.tanh(c * (x + 0.044715 * (x * x * x))))


def _dot(a, b):
    return jnp.dot(a, b, preferred_element_type=F32)


def _dot_nt(a, b):
    return lax.dot_general(a, b, (((1,), (1,)), ((), ())), preferred_element_type=F32)


def _split_bf16(x):
    hi = x.astype(BF16)
    lo = (x - hi.astype(F32)).astype(BF16)
    return hi, lo


def _full(shape):
    n = len(shape)
    return pl.BlockSpec(shape, lambda *_: (0,) * n)


def _in_proj_kernel(x_ref, g_ref, w_ref, qg_ref, kg_ref, hm_ref,
                    u_ref, q_ref, k_ref, v_ref, kk_ref, vk_ref, *, width, nt):
    h = _rms(x_ref[...], g_ref[...]).astype(BF16)
    z = _dot(h, w_ref[...])
    u_ref[...] = z[:, :width].astype(BF16)

    def head_norm(a, gain):
        hi, lo = _split_bf16(a * a)
        ms = _dot(hi, hm_ref[...]) + _dot(lo, hm_ref[...])
        return a * lax.rsqrt(ms + EPS) * gain

    q = head_norm(z[:, width:2 * width], qg_ref[...])
    k = head_norm(z[:, 2 * width:3 * width], kg_ref[...])
    v = z[:, 3 * width:]
    q_ref[...] = q.astype(BF16)
    k_ref[...] = k.astype(BF16)
    v_ref[...] = v.astype(BF16)

    @pl.when(pl.program_id(1) == nt - 1)
    def _():
        kk_ref[...] = k
        vk_ref[...] = v


def _in_proj(x2d, gain, w_bf, qg, kg, hm, *, nb, nt, tm):
    rows, d = x2d.shape
    width = w_bf.shape[1] // 4
    tok = lambda b, t: (b * nt + t, 0)
    out_shape = (
        jax.ShapeDtypeStruct((nt * tm, nb * width), BF16),
        jax.ShapeDtypeStruct((rows, width), BF16),
        jax.ShapeDtypeStruct((rows, width), BF16),
        jax.ShapeDtypeStruct((rows, width), BF16),
        jax.ShapeDtypeStruct((nb * tm, width), F32),
        jax.ShapeDtypeStruct((nb * tm, width), F32),
    )
    return pl.pallas_call(
        functools.partial(_in_proj_kernel, width=width, nt=nt),
        grid=(nb, nt),
        in_specs=[pl.BlockSpec((tm, d), tok), _full((1, d)), _full(w_bf.shape),
                  _full((1, width)), _full((1, width)), _full((width, width))],
        out_specs=(pl.BlockSpec((tm, width), lambda b, t: (t, b)),
                   pl.BlockSpec((tm, width), tok), pl.BlockSpec((tm, width), tok),
                   pl.BlockSpec((tm, width), tok),
                   pl.BlockSpec((tm, width), lambda b, t: (b, 0)),
                   pl.BlockSpec((tm, width), lambda b, t: (b, 0))),
        out_shape=out_shape,
        compiler_params=_cparams(("arbitrary", "arbitrary")),
        name="in_proj",
    )(x2d, gain, w_bf, qg, kg, hm)


def _ssm_kernel(u_ref, bc_ref, cre_ref, cim_ref, lre_ref, lim_ref, d_ref, s0re_ref, s0im_ref,
                y_ref, sre_ref, sim_ref, bu_ref, *, nb, tc, ns):
    @pl.when(pl.program_id(0) == 0)
    def _():
        sre_ref[...] = s0re_ref[...]
        sim_ref[...] = s0im_ref[...]

    u = u_ref[...]
    n_tiles = 2 * ns // MXU_N
    for j in range(n_tiles):
        slab = (j % (n_tiles // 2)) // 2
        bu_ref[:, j * MXU_N:(j + 1) * MXU_N] = _dot(u[:, slab * LANES:(slab + 1) * LANES], bc_ref[j])

    cw = 8 * 1024 // nb
    for cb in range(ns // cw):
        c0 = cb * cw
        lre = jnp.broadcast_to(lre_ref[:, c0:c0 + cw], (nb, cw))
        lim = jnp.broadcast_to(lim_ref[:, c0:c0 + cw], (nb, cw))

        def step(t, carry, c0=c0, lre=lre, lim=lim):
            sr, si = carry
            r0 = pl.multiple_of(t * nb, nb)
            nr = lre * sr - lim * si + bu_ref[pl.ds(r0, nb), c0:c0 + cw]
            ni = lre * si + lim * sr + bu_ref[pl.ds(r0, nb), ns + c0:ns + c0 + cw]
            bu_ref[pl.ds(r0, nb), c0:c0 + cw] = nr
            bu_ref[pl.ds(r0, nb), ns + c0:ns + c0 + cw] = ni
            return nr, ni

        sr, si = lax.fori_loop(0, tc, step, (sre_ref[:, c0:c0 + cw], sim_ref[:, c0:c0 + cw]))
        sre_ref[:, c0:c0 + cw] = sr
        sim_ref[:, c0:c0 + cw] = si

    width = u.shape[1]
    kt = ns * MXU_N // width
    for n in range(width // MXU_N):
        s_re = bu_ref[:, n * kt:(n + 1) * kt].astype(BF16)
        s_im = bu_ref[:, ns + n * kt:ns + (n + 1) * kt].astype(BF16)
        y = _dot(s_re, cre_ref[n]) + _dot(s_im, cim_ref[n])
        cols = slice(n * MXU_N, (n + 1) * MXU_N)
        y_ref[:, cols] = y + d_ref[:, cols] * u[:, cols].astype(F32)


def _ssm(u_rows, bc, cre, cim, lre, lim, dskip, s0re, s0im, *, nb, tc):
    rows, width = u_rows.shape
    ns = lre.shape[1]
    r = tc * nb
    return pl.pallas_call(
        functools.partial(_ssm_kernel, nb=nb, tc=tc, ns=ns),
        grid=(rows // r,),
        in_specs=[pl.BlockSpec((r, width), lambda i: (i, 0)), _full(bc.shape), _full(cre.shape),
                  _full(cim.shape), _full((1, ns)), _full((1, ns)), _full((1, width)),
                  _full((nb, ns)), _full((nb, ns))],
        out_specs=(pl.BlockSpec((r, width), lambda i: (i, 0)), _full((nb, ns)), _full((nb, ns))),
        out_shape=(jax.ShapeDtypeStruct((rows, width), F32),
                   jax.ShapeDtypeStruct((nb, ns), F32), jax.ShapeDtypeStruct((nb, ns), F32)),
        scratch_shapes=[pltpu.VMEM((r, 2 * ns), F32)],
        compiler_params=_cparams(("arbitrary",)),
        name="ssm",
    )(u_rows, bc, cre, cim, lre, lim, dskip, s0re, s0im)


def _half_select(shape):
    lane = lax.broadcasted_iota(jnp.int32, shape, 1)
    return lane < ATT_HEAD_DIM


def _head_masks(first):
    m0 = jnp.where(first, 1.0, 0.0).astype(BF16)
    return m0, (1.0 - m0.astype(F32)).astype(BF16)


def _band_prompt_kernel(q_ref, k_ref, v_ref, bias_ref, o_ref, kp_ref, vp_ref, s_scr, p_scr, l_scr, *, t_len):
    width = q_ref.shape[1]
    n_hp = width // LANES
    kp_ref[0:PAD_ROWS, :] = jnp.zeros((PAD_ROWS, width), BF16)
    vp_ref[0:PAD_ROWS, :] = jnp.zeros((PAD_ROWS, width), BF16)
    kp_ref[PAD_ROWS:, :] = k_ref[...]
    vp_ref[PAD_ROWS:, :] = v_ref[...]
    first = _half_select((PAIR_ROWS, LANES))
    head_mask = _head_masks(first)
    col = lax.broadcasted_iota(jnp.int32, (SOFTMAX_ROWS, PAIR_BAND), 1)

    def pair(pi, carry, *, masked):
        r0 = pl.multiple_of(pi * PAIR_ROWS, PAIR_ROWS)
        for hp in range(n_hp):
            lanes = slice(hp * LANES, (hp + 1) * LANES)
            qp = q_ref[pl.ds(r0, PAIR_ROWS), lanes]
            qs = jnp.concatenate([qp * head_mask[0], qp * head_mask[1]], axis=0)
            s_scr[2 * hp * PAIR_ROWS:2 * (hp + 1) * PAIR_ROWS, :] = _dot_nt(
                qs, kp_ref[pl.ds(r0, PAIR_BAND), lanes])
        for h in range(2 * n_hp):
            for rb in range(0, PAIR_ROWS, SOFTMAX_ROWS):
                rows = slice(h * PAIR_ROWS + rb, h * PAIR_ROWS + rb + SOFTMAX_ROWS)
                s = s_scr[rows, :] + bias_ref[h, rb:rb + SOFTMAX_ROWS, :]
                if masked:
                    s = jnp.where(col >= PAD_ROWS - r0, s, NEG_INF)
                p = jnp.exp(s - jnp.max(s, axis=-1, keepdims=True))
                p_scr[rows, :] = p.astype(BF16)
                l_scr[rows, :] = jnp.broadcast_to(1.0 / jnp.sum(p, axis=-1, keepdims=True),
                                                  (SOFTMAX_ROWS, LANES))
        for hp in range(n_hp):
            lanes = slice(hp * LANES, (hp + 1) * LANES)
            rows = slice(2 * hp * PAIR_ROWS, 2 * (hp + 1) * PAIR_ROWS)
            o2 = _dot(p_scr[rows, :], vp_ref[pl.ds(r0, PAIR_BAND), lanes]) * l_scr[rows, :]
            o_ref[pl.ds(r0, PAIR_ROWS), lanes] = jnp.where(
                first, o2[:PAIR_ROWS], o2[PAIR_ROWS:]).astype(BF16)
        return carry

    n_masked = PAD_ROWS // PAIR_ROWS
    lax.fori_loop(0, n_masked, functools.partial(pair, masked=True), 0)
    lax.fori_loop(n_masked, t_len // PAIR_ROWS, functools.partial(pair, masked=False), 0)


def _band_prompt(q, k, v, bias, *, nb, t_len):
    width = q.shape[1]
    heads = width // ATT_HEAD_DIM
    blk = pl.BlockSpec((t_len, width), lambda b: (b, 0))
    return pl.pallas_call(
        functools.partial(_band_prompt_kernel, t_len=t_len),
        grid=(nb,),
        in_specs=[blk, blk, blk, _full(bias.shape)],
        out_specs=blk,
        out_shape=jax.ShapeDtypeStruct(q.shape, BF16),
        scratch_shapes=[pltpu.VMEM((t_len + PAD_ROWS, width), BF16),
                        pltpu.VMEM((t_len + PAD_ROWS, width), BF16),
                        pltpu.VMEM((heads * PAIR_ROWS, PAIR_BAND), F32),
                        pltpu.VMEM((heads * PAIR_ROWS, PAIR_BAND), BF16),
                        pltpu.VMEM((heads * PAIR_ROWS, LANES), F32)],
        compiler_params=_cparams(("arbitrary",)),
        name="band_prompt",
    )(q, k, v, bias)


def _band_sample_kernel(q_ref, k_ref, v_ref, ck_ref, cv_ref, bc_ref, bn_ref, o_ref):
    width = q_ref.shape[1]
    s_len = q_ref.shape[0]
    first = _half_select((s_len, LANES))
    head_mask = _head_masks(first)
    for hp in range(width // LANES):
        lanes = slice(hp * LANES, (hp + 1) * LANES)
        qp = q_ref[:, lanes]
        kc = ck_ref[:, lanes].astype(BF16)
        vc = cv_ref[:, lanes].astype(BF16)
        kn = k_ref[:, lanes]
        vn = v_ref[:, lanes]
        outs = []
        for hh in range(2):
            qm = qp * head_mask[hh]
            sc = _dot_nt(qm, kc) + bc_ref[2 * hp + hh]
            sn = _dot_nt(qm, kn) + bn_ref[2 * hp + hh]
            m = jnp.maximum(jnp.max(sc, axis=-1, keepdims=True), jnp.max(sn, axis=-1, keepdims=True))
            pc = jnp.exp(sc - m)
            pn = jnp.exp(sn - m)
            l = jnp.sum(pc, axis=-1, keepdims=True) + jnp.sum(pn, axis=-1, keepdims=True)
            outs.append((_dot(pc.astype(BF16), vc) + _dot(pn.astype(BF16), vn)) / l)
        o_ref[:, lanes] = jnp.where(first, outs[0], outs[1]).astype(BF16)


def _band_sample(q_tm, k_tm, v_tm, cache_k, cache_v, bias_c, bias_n, *, nb, s_len):
    width = q_tm.shape[1] // nb
    w_rows = cache_k.shape[0] // nb
    col = pl.BlockSpec((s_len, width), lambda b: (0, b))
    cache = pl.BlockSpec((w_rows, width), lambda b: (b, 0))
    return pl.pallas_call(
        _band_sample_kernel,
        grid=(nb,),
        in_specs=[col, col, col, cache, cache, _full(bias_c.shape), _full(bias_n.shape)],
        out_specs=col,
        out_shape=jax.ShapeDtypeStruct(q_tm.shape, BF16),
        compiler_params=_cparams(("arbitrary",)),
        name="band_sample",
    )(q_tm, k_tm, v_tm, cache_k, cache_v, bias_c, bias_n)


def _mem_kv_kernel(m_ref, g_ref, wk_ref, wv_ref, kg_ref, k_ref, v_ref, kb_ref, vb_ref):
    m = _rms(m_ref[...], g_ref[...]).astype(BF16)
    k = _dot(m, wk_ref[...])
    v = _dot(m, wv_ref[...])
    hd = kg_ref.shape[1]
    for h in range(k.shape[1] // hd):
        cols = slice(h * hd, (h + 1) * hd)
        kh = _rms(k[:, cols], kg_ref[...])
        k_ref[:, cols] = kh
        kb_ref[:, cols] = kh.astype(BF16)
    v_ref[...] = v
    vb_ref[...] = v.astype(BF16)


def _mem_kv(mem2d, gain, wk, wv, kgain, *, tm):
    rows, d = mem2d.shape
    blk = pl.BlockSpec((tm, d), lambda i: (i, 0))
    return pl.pallas_call(
        _mem_kv_kernel,
        grid=(rows // tm,),
        in_specs=[blk, _full((1, d)), _full(wk.shape), _full(wv.shape), _full(kgain.shape)],
        out_specs=(blk, blk, blk, blk),
        out_shape=(jax.ShapeDtypeStruct((rows, d), F32), jax.ShapeDtypeStruct((rows, d), F32),
                   jax.ShapeDtypeStruct((rows, d), BF16), jax.ShapeDtypeStruct((rows, d), BF16)),
        compiler_params=_cparams(("arbitrary",)),
        name="mem_kv",
    )(mem2d, gain, wk, wv, kgain)


def _route(logits):
    lane = lax.broadcasted_iota(jnp.int32, logits.shape, 1).astype(F32)
    big = float(ROUTER_LANES)
    is_g = lane < N_GROUPS
    lg = jnp.where(is_g, logits, NEG_INF)
    gmax = jnp.max(lg, axis=-1, keepdims=True)
    p1 = 1.0 / jnp.sum(jnp.where(is_g, jnp.exp(lg - gmax), 0.0), axis=-1, keepdims=True)
    g_idx = jnp.min(jnp.where(lg == gmax, lane, big), axis=-1, keepdims=True)
    lo = ROUTER_E0 + g_idx * EXPERTS_PER_GROUP
    le = jnp.where((lane >= lo) & (lane < lo + EXPERTS_PER_GROUP), logits, NEG_INF)
    v1 = jnp.max(le, axis=-1, keepdims=True)
    i1 = jnp.min(jnp.where(le == v1, lane, big), axis=-1, keepdims=True)
    le2 = jnp.where(lane == i1, NEG_INF, le)
    v2 = jnp.max(le2, axis=-1, keepdims=True)
    i2 = jnp.min(jnp.where(le2 == v2, lane, big), axis=-1, keepdims=True)
    e2 = jnp.exp(v2 - v1)
    w1 = p1 / (1.0 + e2)
    w2 = p1 * e2 / (1.0 + e2)
    gates = jnp.where(lane == i1, w1, 0.0) + jnp.where(lane == i2, w2, 0.0)
    a = jnp.minimum(i1, i2) - lo
    b = jnp.maximum(i1, i2) - lo
    pair = a * (2 * EXPERTS_PER_GROUP - 1 - a) * 0.5 + (b - a - 1.0)
    cls = g_idx * PAIRS_PER_GROUP + pair
    w_a = jnp.where(i1 < i2, w1, w2)
    w_b = jnp.where(i1 < i2, w2, w1)
    return (gates + jnp.where(lane == ROUTE_CLS, cls, 0.0) + jnp.where(lane == ROUTE_WA, w_a, 0.0)
            + jnp.where(lane == ROUTE_WB, w_b, 0.0))


def _mix_mem_kernel(x_ref, ys_ref, ya_ref, mk_ref, mv_ref, wglu_ref, ons_ref, ona_ref, wout_ref,
                    nmem_ref, wq_ref, qg_ref, wo_ref, nffn_ref, wrh_ref, wrl_ref, rb_ref,
                    *outs, routed):
    if routed:
        x2_ref, o_scr = outs
    else:
        x2_ref, xn_ref, gate_ref, o_scr = outs
    width = ys_ref.shape[1]
    g = _dot(_gelu_tanh(ys_ref[...]).astype(BF16), wglu_ref[...])
    y_s = g[:, :width] * _sigmoid(g[:, width:])
    cat_s = _rms(y_s, ons_ref[...]).astype(BF16)
    cat_a = _rms(ya_ref[...].astype(F32), ona_ref[...]).astype(BF16)
    x1 = x_ref[...] + _dot(cat_s, wout_ref[0:width, :]) + _dot(cat_a, wout_ref[width:, :])

    q = _dot(_rms(x1, nmem_ref[...]).astype(BF16), wq_ref[...])
    hd = qg_ref.shape[1]
    scale = hd ** -0.5
    for h in range(q.shape[1] // hd):
        cols = slice(h * hd, (h + 1) * hd)
        qh = _rms(q[:, cols], qg_ref[...]).astype(BF16)
        s = _dot_nt(qh, mk_ref[:, cols].astype(BF16)) * scale
        p = jnp.exp(s - jnp.max(s, axis=-1, keepdims=True))
        l = jnp.sum(p, axis=-1, keepdims=True)
        o_scr[:, cols] = (_dot(p.astype(BF16), mv_ref[:, cols].astype(BF16)) / l).astype(BF16)
    acc = x1 + _dot(o_scr[...], wo_ref[...])
    tm, d = acc.shape
    xn = _rms(acc, nffn_ref[...])
    xh, xl = _split_bf16(xn)
    logits = _dot(xh, wrh_ref[...]) + _dot(xl, wrh_ref[...]) + _dot(xh, wrl_ref[...]) + rb_ref[...]
    if routed:
        for j in range(d // LANES):
            x2_ref[pl.ds(j, tm, stride=SLAB_IN), :] = acc[:, j * LANES:(j + 1) * LANES]
        x2_ref[pl.ds(d // LANES, tm, stride=SLAB_IN), :] = _route(logits)
    else:
        x2_ref[...] = acc
        xn_ref[...] = xh
        gate_ref[...] = _route(logits)


def _mix_mem(x, ys, ya, mk, mv, weights, *, grid, tm, row_map, ssm_map, mem_map, routed):
    d = weights["w_out"].shape[1]
    width = weights["w_glu"].shape[0]
    names = ("w_glu", "out_norm_ssm", "out_norm_att", "w_out", "norm_mem", "w_mem_q", "mem_q_gain",
             "w_mem_o", "norm_ffn", "w_router_hi", "w_router_lo", "router_bias")
    ws = [weights[n] for n in names]
    mem_rows = weights["n_mem"]
    xspec = pl.BlockSpec((tm, d), row_map)
    hspec = pl.BlockSpec((tm, width), row_map)
    sspec = pl.BlockSpec((tm, width), ssm_map)
    mspec = pl.BlockSpec((mem_rows, d), mem_map)
    n_col = x.shape[1] // d
    if routed:
        assert n_col == 1 and d // LANES + 1 == SLAB_IN
        out_specs = pl.BlockSpec((tm * SLAB_IN, LANES), row_map)
        out_shape = jax.ShapeDtypeStruct((x.shape[0] * SLAB_IN, LANES), F32)
    else:
        out_specs = (xspec, xspec, pl.BlockSpec((tm, ROUTER_LANES), row_map))
        out_shape = (jax.ShapeDtypeStruct(x.shape, F32), jax.ShapeDtypeStruct(x.shape, BF16),
                     jax.ShapeDtypeStruct((x.shape[0], n_col * ROUTER_LANES), F32))
    return pl.pallas_call(
        functools.partial(_mix_mem_kernel, routed=routed),
        grid=grid,
        in_specs=[xspec, sspec, hspec, mspec, mspec] + [_full(w.shape) for w in ws],
        out_specs=out_specs,
        out_shape=out_shape,
        scratch_shapes=[pltpu.VMEM((tm, d), BF16)],
        compiler_params=_cparams(("arbitrary",) * len(grid)),
        name="mix_mem",
    )(x, ys, ya, mk, mv, *ws)


def _moe_kernel(xn_ref, x2_ref, gate_ref, wg_ref, wu_ref, wd_ref, o_ref):
    e = pl.program_id(1)

    @pl.when(e == 0)
    def _():
        o_ref[...] = x2_ref[...]

    gates = gate_ref[...]
    lane = lax.broadcasted_iota(jnp.int32, gates.shape, 1)
    ge = jnp.sum(jnp.where(lane == e + ROUTER_E0, gates, 0.0), axis=-1, keepdims=True)
    xn = xn_ref[...]
    a = _dot(xn, wg_ref[...])
    h = a * _sigmoid(a) * _dot(xn, wu_ref[...])
    o_ref[...] += ge * _dot(h.astype(BF16), wd_ref[...])


def _moe(xn, x2, gates, wg, wu, wd, *, tm):
    rows, d = xn.shape
    n_exp, _, dff = wg.shape
    row = lambda i, e: (i, 0)
    return pl.pallas_call(
        _moe_kernel,
        grid=(rows // tm, n_exp),
        in_specs=[pl.BlockSpec((tm, d), row), pl.BlockSpec((tm, d), row),
                  pl.BlockSpec((tm, ROUTER_LANES), row),
                  pl.BlockSpec((None, d, dff), lambda i, e: (e, 0, 0)),
                  pl.BlockSpec((None, d, dff), lambda i, e: (e, 0, 0)),
                  pl.BlockSpec((None, dff, d), lambda i, e: (e, 0, 0))],
        out_specs=pl.BlockSpec((tm, d), row),
        out_shape=jax.ShapeDtypeStruct((rows, d), F32),
        compiler_params=_cparams(("arbitrary", "arbitrary")),
        name="moe",
    )(xn, x2, gates, wg, wu, wd)


def _moe_routed_kernel(tile_ref, ea_ref, eb_ref, lo_ref, hi_ref, valid_ref, tok_ref,
                       x_hbm, nffn_ref, wga_ref, wua_ref, wda_ref, wgb_ref, wub_ref, wdb_ref,
                       y_hbm, xbuf0, xbuf1, obuf0, obuf1, acc_ref, gsem, ssem, *, tm, n_tiles):
    w = pl.program_id(0)
    t = tile_ref[w]
    lo = lo_ref[w]
    hi = hi_ref[w]
    valid = valid_ref[w] == 1
    d = SLAB_OUT * LANES
    xbuf = (xbuf0, xbuf1)
    obuf = (obuf0, obuf1)

    def row_copy_in(tile, s, r):
        tok = tok_ref[tile * tm + r]
        return pltpu.make_async_copy(x_hbm.at[pl.ds(tok * SLAB_IN, SLAB_IN)],
                                     xbuf[s].at[pl.ds(r * SLAB_IN, SLAB_IN)], gsem.at[s])

    def row_copy_out(tile, s, r):
        tok = tok_ref[tile * tm + r]
        return pltpu.make_async_copy(obuf[s].at[pl.ds(r * SLAB_OUT, SLAB_OUT)],
                                     y_hbm.at[pl.ds(tok * SLAB_OUT, SLAB_OUT)], ssem.at[s])

    def start_gather(tile, s):
        for r in range(tm):
            row_copy_in(tile, s, r).start()

    def start_scatter(tile, s):
        for r in range(tm):
            row_copy_out(tile, s, r).start()

    def wait_gather(s):
        pltpu.make_async_copy(x_hbm.at[pl.ds(0, tm * SLAB_IN)], xbuf[s], gsem.at[s]).wait()

    def wait_scatter(s):
        pltpu.make_async_copy(obuf[s], y_hbm.at[pl.ds(0, tm * SLAB_OUT)], ssem.at[s]).wait()

    def load_x2(s):
        return jnp.concatenate([xbuf[s][pl.ds(j, tm, stride=SLAB_IN), :] for j in range(SLAB_OUT)],
                               axis=1)

    def compute(s, first):
        rec = xbuf[s][pl.ds(SLAB_OUT, tm, stride=SLAB_IN), :]
        xn = _rms(load_x2(s), nffn_ref[...]).astype(BF16)
        row = lax.broadcasted_iota(jnp.int32, (tm, 1), 0)
        inseg = jnp.logical_and(row >= lo, row < hi)
        w_a = jnp.where(inseg, rec[:, ROUTE_WA:ROUTE_WA + 1], 0.0)
        w_b = jnp.where(inseg, rec[:, ROUTE_WB:ROUTE_WB + 1], 0.0)
        a = _dot(xn, wga_ref[...])
        h_a = (a * _sigmoid(a) * _dot(xn, wua_ref[...])).astype(BF16)
        b = _dot(xn, wgb_ref[...])
        h_b = (b * _sigmoid(b) * _dot(xn, wub_ref[...])).astype(BF16)
        upd = w_a * _dot(h_a, wda_ref[...]) + w_b * _dot(h_b, wdb_ref[...])
        if first:
            acc_ref[...] = upd
        else:
            acc_ref[...] += upd

    def when(*conds):
        c = conds[0]
        for extra in conds[1:]:
            c = jnp.logical_and(c, extra)
        return pl.when(c)

    @pl.when(w == 0)
    def _():
        start_gather(0, 0)

    is_first = jnp.logical_and(valid, lo == 0)
    is_last = jnp.logical_and(valid, hi == tm)
    last_tile = n_tiles - 1
    for s in (0, 1):
        mine = t % 2 == s

        if s == 0:
            @when(is_first, t == 0)
            def _():
                wait_gather(0)
                start_gather(1, 1)
                compute(0, True)

        @when(is_first, mine, t > 0, t < last_tile)
        def _(s=s):
            wait_gather(s)
            start_gather(t + 1, 1 - s)
            start_scatter(t - 1, 1 - s)
            compute(s, True)

        if s == last_tile % 2:
            @when(is_first, t == last_tile)
            def _(s=s):
                wait_gather(s)
                start_scatter(t - 1, 1 - s)
                compute(s, True)

        @when(valid, mine, lo > 0)
        def _(s=s):
            compute(s, False)

        @when(is_last, mine)
        def _(s=s):
            @pl.when(t >= 2)
            def _():
                wait_scatter(s)

            y = load_x2(s) + acc_ref[...]
            for j in range(SLAB_OUT):
                obuf[s][pl.ds(j, tm, stride=SLAB_OUT), :] = y[:, j * LANES:(j + 1) * LANES]

            if s == last_tile % 2:
                @pl.when(t == last_tile)
                def _():
                    start_scatter(t, s)
                    wait_scatter(s)
                    wait_scatter(1 - s)


def _moe_routed(x_slab, plan, nffn, wg, wu, wd, *, tm):
    rows = x_slab.shape[0] // SLAB_IN
    d = SLAB_OUT * LANES
    n_exp, _, dff = wg.shape
    n_tiles = rows // tm
    assert rows % tm == 0 and n_tiles >= 3 and wg.shape[1] == d
    n_items = plan[0].shape[0]
    ea = lambda w, tile, ea_, eb_, *_: (ea_[w], 0, 0)
    eb = lambda w, tile, ea_, eb_, *_: (eb_[w], 0, 0)
    grid_spec = pltpu.PrefetchScalarGridSpec(
        num_scalar_prefetch=len(plan),
        grid=(n_items,),
        in_specs=[pl.BlockSpec(memory_space=pl.ANY),
                  pl.BlockSpec((1, d), lambda w, *_: (0, 0)),
                  pl.BlockSpec((None, d, dff), ea), pl.BlockSpec((None, d, dff), ea),
                  pl.BlockSpec((None, dff, d), ea),
                  pl.BlockSpec((None, d, dff), eb), pl.BlockSpec((None, d, dff), eb),
                  pl.BlockSpec((None, dff, d), eb)],
        out_specs=pl.BlockSpec(memory_space=pl.ANY),
        scratch_shapes=[pltpu.VMEM((tm * SLAB_IN, LANES), F32), pltpu.VMEM((tm * SLAB_IN, LANES), F32),
                        pltpu.VMEM((tm * SLAB_OUT, LANES), F32), pltpu.VMEM((tm * SLAB_OUT, LANES), F32),
                        pltpu.VMEM((tm, d), F32),
                        pltpu.SemaphoreType.DMA((2,)), pltpu.SemaphoreType.DMA((2,))],
    )
    return pl.pallas_call(
        functools.partial(_moe_routed_kernel, tm=tm, n_tiles=n_tiles),
        grid_spec=grid_spec,
        out_shape=jax.ShapeDtypeStruct((rows * SLAB_OUT, LANES), F32),
        compiler_params=_cparams(("arbitrary",)),
        name="moe_routed",
    )(*plan, x_slab, nffn, wg, wu, wd, wg, wu, wd)


def _unslab_kernel(s_ref, o_ref):
    tm = o_ref.shape[0]
    for j in range(SLAB_OUT):
        o_ref[:, j * LANES:(j + 1) * LANES] = s_ref[pl.ds(j, tm, stride=SLAB_OUT), :]


def _unslab(y_slab, *, tm):
    rows = y_slab.shape[0] // SLAB_OUT
    return pl.pallas_call(
        _unslab_kernel,
        grid=(rows // tm,),
        in_specs=[pl.BlockSpec((tm * SLAB_OUT, LANES), lambda i: (i, 0))],
        out_specs=pl.BlockSpec((tm, SLAB_OUT * LANES), lambda i: (i, 0)),
        out_shape=jax.ShapeDtypeStruct((rows, SLAB_OUT * LANES), F32),
        compiler_params=_cparams(("arbitrary",)),
        name="unslab",
    )(y_slab)


def _route_plan(cls, *, tm):
    n = cls.shape[0]
    n_tiles = n // tm
    order = jnp.argsort(cls).astype(jnp.int32)
    classes = jnp.arange(N_CLASSES, dtype=jnp.int32)
    class_end = jnp.sum((cls[:, None] <= classes[None, :]).astype(jnp.int32), axis=0)
    class_start = jnp.concatenate([jnp.zeros((1,), jnp.int32), class_end[:-1]])
    bounds = jnp.concatenate([jnp.arange(n_tiles, dtype=jnp.int32) * tm,
                              jnp.where(class_end > class_start, class_start, n)])
    n_items = bounds.shape[0]
    idx = jnp.arange(n_items, dtype=jnp.int32)
    before = jnp.logical_or(bounds[None, :] < bounds[:, None],
                            jnp.logical_and(bounds[None, :] == bounds[:, None], idx[None, :] < idx[:, None]))
    rank = jnp.sum(before.astype(jnp.int32), axis=1)
    start = jnp.sum(jnp.where(rank[:, None] == idx[None, :], bounds[:, None], 0), axis=0)
    stop = jnp.concatenate([start[1:], jnp.full((1,), n, jnp.int32)])
    valid = stop > start
    tile = jnp.minimum(start // tm, n_tiles - 1)
    lo = start - tile * tm
    hi = stop - tile * tm
    c = jnp.minimum(jnp.sum((class_end[None, :] <= start[:, None]).astype(jnp.int32), axis=1),
                    N_CLASSES - 1)
    g = c // PAIRS_PER_GROUP
    pair = c % PAIRS_PER_GROUP
    a = jnp.zeros_like(pair)
    for k in range(1, EXPERTS_PER_GROUP - 1):
        a = a + (pair >= k * (2 * EXPERTS_PER_GROUP - 1 - k) // 2).astype(jnp.int32)
    b = pair - a * (2 * EXPERTS_PER_GROUP - 1 - a) // 2 + a + 1
    e_a = g * EXPERTS_PER_GROUP + a
    e_b = g * EXPERTS_PER_GROUP + b
    i32 = lambda v: v.astype(jnp.int32)
    return (i32(tile), i32(e_a), i32(e_b), i32(lo), i32(hi), i32(valid), order)


def _ssm_params(lam_re, lam_im, log_step, b_re, b_im, c_re, c_im):
    n_g, n_p = lam_re.shape
    step = jnp.exp(log_step.astype(F32))[:, None]
    mag = jnp.exp(lam_re * step)
    lb_re = mag * jnp.cos(lam_im * step)
    lb_im = mag * jnp.sin(lam_im * step)
    den = lam_re * lam_re + lam_im * lam_im
    f_re = ((lb_re - 1.0) * lam_re + lb_im * lam_im) / den
    f_im = (lb_im * lam_re - (lb_re - 1.0) * lam_im) / den
    bb_re = f_re[..., None] * b_re - f_im[..., None] * b_im
    bb_im = f_re[..., None] * b_im + f_im[..., None] * b_re
    eye = jnp.eye(n_g, dtype=F32)
    ns = n_g * n_p
    width = n_g * SSM_GROUP
    b_full = jnp.concatenate(
        [jnp.einsum("hg,gpc->hcgp", eye, bb_re).reshape(width, ns),
         jnp.einsum("hg,gpc->hcgp", eye, bb_im).reshape(width, ns)], axis=1)
    n_tiles = 2 * ns // MXU_N
    bc = jnp.stack([
        b_full[((j % (n_tiles // 2)) // 2) * LANES:((j % (n_tiles // 2)) // 2 + 1) * LANES,
               j * MXU_N:(j + 1) * MXU_N] for j in range(n_tiles)]).astype(BF16)
    c_full_re = jnp.einsum("gh,gcp->gphc", eye, c_re).reshape(ns, width)
    c_full_im = -jnp.einsum("gh,gcp->gphc", eye, c_im).reshape(ns, width)
    kt = ns * MXU_N // width
    tiles = range(width // MXU_N)
    cre = jnp.stack([c_full_re[n * kt:(n + 1) * kt, n * MXU_N:(n + 1) * MXU_N] for n in tiles]).astype(BF16)
    cim = jnp.stack([c_full_im[n * kt:(n + 1) * kt, n * MXU_N:(n + 1) * MXU_N] for n in tiles]).astype(BF16)
    return bc, cre, cim, lb_re.reshape(1, ns), lb_im.reshape(1, ns)


def _rel_bias(rel_bias, q0, n_q, n_k):
    n_r = n_q + n_k - 1
    dist = q0 + n_q - 1 - jnp.arange(n_r)
    r = rel_bias.astype(F32)[:, jnp.clip(dist, -REL_CLIP, REL_CLIP) + REL_CLIP]
    r = jnp.pad(r, ((0, 0), (0, 1)))
    rows = jnp.tile(r, (1, n_q))[:, :n_q * n_r].reshape(-1, n_q, n_r)
    return rows[:, :, n_q - 1:n_q - 1 + n_k]


def _layer(xp, xs, mem_p, ck, cv, s_re, s_im, cmk, cmv, w, dims):
    batch, seq, dec_batch, dec_seq, d = dims
    width = w["w_in"].shape[1] // 4
    ns = w["lam_re"].shape[1]
    n_mem = w["n_mem"]
    tm_p = 512
    nt_p = seq // tm_p
    n_dec = dec_batch * dec_seq

    u, q, k, v, kk, vk = _in_proj(xp, w["norm_mix"], w["w_in"], w["qg"], w["kg"], w["head_mean"],
                                  nb=batch, nt=nt_p, tm=tm_p)
    zeros = jnp.zeros((batch, ns), F32)
    y_ssm, pre, pim = _ssm(u.reshape(seq * batch, width), w["bc"], w["cre"], w["cim"], w["lam_re"],
                           w["lam_im"], w["ssm_d"], zeros, zeros, nb=batch, tc=32)
    y_att = _band_prompt(q, k, v, w["bias_p"], nb=batch, t_len=seq)
    mk, mv, mkb, mvb = _mem_kv(mem_p, w["mem_in_norm"], w["w_mem_k"], w["w_mem_v"], w["mem_k_gain"], tm=512)
    tm_d = 256
    nt_d = seq // tm_d
    x2ext = _mix_mem(xp, y_ssm.reshape(seq, batch * width), y_att, mkb, mvb, w,
                     grid=(batch, nt_d), tm=tm_d,
                     row_map=lambda b, t: (b * nt_d + t, 0), ssm_map=lambda b, t: (t, b),
                     mem_map=lambda b, t: (b, 0), routed=True)
    cls = x2ext[SLAB_IN - 1::SLAB_IN, ROUTE_CLS].astype(jnp.int32)
    plan = _route_plan(cls, tm=MOE_TILE)
    yp = _unslab(_moe_routed(x2ext, plan, w["norm_ffn"], w["exp_w_gate"], w["exp_w_up"],
                             w["exp_w_down"], tm=MOE_TILE), tm=512)

    us, qs, ks, vs, kks, vks = _in_proj(xs, w["norm_mix"], w["w_in"], w["qg"], w["kg"], w["head_mean"],
                                        nb=1, nt=1, tm=n_dec)
    ys_ssm, sre, sim = _ssm(us, w["bc"], w["cre"], w["cim"], w["lam_re"], w["lam_im"], w["ssm_d"],
                            s_re, s_im, nb=dec_batch, tc=dec_seq)
    tmv = lambda a: a.reshape(dec_seq, dec_batch * a.shape[1])
    ys_att = _band_sample(tmv(qs), tmv(ks), tmv(vs), ck, cv, w["bias_sc"], w["bias_sn"],
                          nb=dec_batch, s_len=dec_seq)
    x2s, xns, gates_s = _mix_mem(tmv(xs), tmv(ys_ssm), ys_att, cmk, cmv, w,
                                 grid=(dec_batch,), tm=dec_seq,
                                 row_map=lambda b: (0, b), ssm_map=lambda b: (0, b),
                                 mem_map=lambda b: (b, 0), routed=False)
    ys = _moe(xns.reshape(n_dec, d), x2s.reshape(n_dec, d), gates_s.reshape(n_dec, ROUTER_LANES),
              w["exp_w_gate"], w["exp_w_up"], w["exp_w_down"], tm=n_dec)
    return yp, ys, (kk, vk, pre, pim, mk, mv), (kks, vks, sre, sim)


def kernel(x_prompt, x_sample, mem_prompt, cache_attn_k, cache_attn_v, state_ssm_re, state_ssm_im, cache_mem_k, cache_mem_v, norm_mix, w_in, ssm_lambda_re, ssm_lambda_im, ssm_log_step, ssm_b_re, ssm_b_im, ssm_c_re, ssm_c_im, ssm_d, w_glu, att_q_gain, att_k_gain, att_rel_bias, out_norm_ssm, out_norm_att, w_out, norm_mem, mem_in_norm, w_mem_q, w_mem_k, w_mem_v, w_mem_o, mem_q_gain, mem_k_gain, norm_ffn, router_g_w, router_g_b, router_e_w, router_e_b, exp_w_gate, exp_w_up, exp_w_down):
    depth = norm_mix.shape[0]
    batch, seq, d = x_prompt.shape
    dec_batch, dec_seq, _ = x_sample.shape
    n_mem = mem_prompt.shape[1]
    att_rows = cache_attn_k.shape[2]
    n_g, n_p = ssm_lambda_re.shape[1:]
    width = n_g * SSM_GROUP
    heads = width // ATT_HEAD_DIM
    ns = n_g * n_p
    assert seq % 512 == 0 and att_rows == PAD_ROWS and seq >= PAD_ROWS
    assert (dec_batch * dec_seq) % 8 == 0 and dec_seq % 16 == 0

    xp = x_prompt.reshape(batch * seq, d)
    xs = x_sample.transpose(1, 0, 2).reshape(dec_seq * dec_batch, d)
    mem_p = mem_prompt.reshape(batch * n_mem, d)
    row = lambda a: a.reshape(1, -1).astype(F32)
    head_mean = jnp.kron(jnp.eye(heads, dtype=F32),
                         jnp.full((ATT_HEAD_DIM, ATT_HEAD_DIM), 1.0 / ATT_HEAD_DIM, F32)).astype(BF16)
    pr = jnp.arange(PAIR_ROWS)[:, None]
    pc = jnp.arange(PAIR_BAND)[None, :]
    pair_ok = jnp.where(pr < CHUNK, pc < BAND, pc >= CHUNK)
    att_scale = ATT_HEAD_DIM ** -0.5

    p_out, s_out = [], []
    for l in range(depth):
        bc, cre, cim, lb_re, lb_im = _ssm_params(ssm_lambda_re[l], ssm_lambda_im[l], ssm_log_step[l],
                                                 ssm_b_re[l], ssm_b_im[l], ssm_c_re[l], ssm_c_im[l])
        w_router = jnp.concatenate(
            [router_g_w[l], router_e_w[l].transpose(1, 0, 2).reshape(d, N_GROUPS * EXPERTS_PER_GROUP),
             jnp.zeros((d, ROUTER_LANES - N_GROUPS * (1 + EXPERTS_PER_GROUP)), F32)], axis=1)
        wr_hi, wr_lo = _split_bf16(w_router)
        r_bias = jnp.concatenate(
            [router_g_b[l], router_e_b[l].reshape(-1),
             jnp.zeros((ROUTER_LANES - N_GROUPS * (1 + EXPERTS_PER_GROUP),), F32)]).reshape(1, ROUTER_LANES)
        bias_s = _rel_bias(att_rel_bias[l], att_rows, dec_seq, att_rows + dec_seq)
        bias_p = jnp.where(pair_ok, _rel_bias(att_rel_bias[l], PAD_ROWS, PAIR_ROWS, PAIR_BAND), NEG_INF)
        w = dict(
            n_mem=n_mem,
            norm_mix=row(norm_mix[l]), w_in=w_in[l].astype(BF16),
            qg=row(jnp.tile(att_q_gain[l], heads) * att_scale), kg=row(jnp.tile(att_k_gain[l], heads)),
            head_mean=head_mean, bc=bc, cre=cre, cim=cim, lam_re=lb_re, lam_im=lb_im,
            ssm_d=row(ssm_d[l]), bias_p=bias_p,
            bias_sc=bias_s[:, :, :att_rows], bias_sn=bias_s[:, :, att_rows:],
            mem_in_norm=row(mem_in_norm[l]), w_mem_k=w_mem_k[l].astype(BF16),
            w_mem_v=w_mem_v[l].astype(BF16), mem_k_gain=row(mem_k_gain[l]),
            w_glu=w_glu[l].astype(BF16), out_norm_ssm=row(out_norm_ssm[l]),
            out_norm_att=row(out_norm_att[l]), w_out=w_out[l].astype(BF16), norm_mem=row(norm_mem[l]),
            w_mem_q=w_mem_q[l].astype(BF16), mem_q_gain=row(mem_q_gain[l]),
            w_mem_o=w_mem_o[l].astype(BF16), norm_ffn=row(norm_ffn[l]),
            w_router_hi=wr_hi, w_router_lo=wr_lo, router_bias=r_bias,
            exp_w_gate=exp_w_gate[l].astype(BF16), exp_w_up=exp_w_up[l].astype(BF16),
            exp_w_down=exp_w_down[l].astype(BF16),
        )
        xp, xs, p_new, s_new = _layer(
            xp, xs, mem_p,
            cache_attn_k[l].reshape(dec_batch * att_rows, width),
            cache_attn_v[l].reshape(dec_batch * att_rows, width),
            state_ssm_re[l].reshape(dec_batch, ns), state_ssm_im[l].reshape(dec_batch, ns),
            cache_mem_k[l].reshape(dec_batch * n_mem, d), cache_mem_v[l].reshape(dec_batch * n_mem, d),
            w, (batch, seq, dec_batch, dec_seq, d))
        p_out.append(p_new)
        s_out.append(s_new)

    sdt = state_ssm_re.dtype
    keep = min(PAD_ROWS, seq)
    kv_p = lambda a: a.reshape(batch, keep, heads, ATT_HEAD_DIM)
    kv_s = lambda a: a.reshape(dec_seq, dec_batch, heads, ATT_HEAD_DIM).transpose(1, 0, 2, 3)
    st = lambda a: a.reshape(a.shape[0], n_g, n_p).astype(sdt)
    mkv = lambda a: a.reshape(batch, n_mem, MEM_HEADS, d // MEM_HEADS)
    stack = lambda f, outs, i: jnp.stack([f(o[i]) for o in outs])
    yp = xp.reshape(batch, seq, d)
    ys = xs.reshape(dec_seq, dec_batch, d).transpose(1, 0, 2)
    return (yp, ys,
            stack(kv_p, p_out, 0), stack(kv_p, p_out, 1), stack(st, p_out, 2), stack(st, p_out, 3),
            stack(mkv, p_out, 4), stack(mkv, p_out, 5),
            stack(kv_s, s_out, 0), stack(kv_s, s_out, 1), stack(st, s_out, 2), stack(st, s_out, 3))
```

```python
import functools
import math

import jax
import jax.numpy as jnp
from jax import lax
from jax.experimental import pallas as pl
from jax.experimental.pallas import tpu as pltpu

F32 = jnp.float32
BF16 = jnp.bfloat16

CHUNK = 64
N_PREV_CHUNKS = 8
BAND = (N_PREV_CHUNKS + 1) * CHUNK
PAD_ROWS = N_PREV_CHUNKS * CHUNK
PAIR_ROWS = 2 * CHUNK
PAIR_BAND = BAND + CHUNK
SOFTMAX_ROWS = 32
SSM_GROUP = 16
SSM_STATE = 64
ATT_HEAD_DIM = 64
REL_CLIP = 128
MEM_HEADS = 4
N_GROUPS = 4
EXPERTS_PER_GROUP = 8
EPS = 1e-6
NEG_INF = -1e30

LANES = 128
MXU_N = 256
VMEM_LIMIT = 56 * 1024 * 1024
ROUTER_LANES = LANES
MOE_TILE = 128
MOE_BUFS = 3
ROUTER_E0 = N_GROUPS
PAIRS_PER_GROUP = EXPERTS_PER_GROUP * (EXPERTS_PER_GROUP - 1) // 2
N_CLASSES = N_GROUPS * PAIRS_PER_GROUP
ROUTE_CLS, ROUTE_WA, ROUTE_WB = 64, 65, 66


def _cparams(sem):
    return pltpu.CompilerParams(dimension_semantics=sem, vmem_limit_bytes=VMEM_LIMIT)


def _rms(x, gain):
    ms = jnp.mean(x * x, axis=-1, keepdims=True)
    return x * lax.rsqrt(ms + EPS) * gain


def _sigmoid(x):
    return 1.0 / (1.0 + jnp.exp(-x))


def _gelu_tanh(x):
    c = math.sqrt(2.0 / math.pi)
    return 0.5 * x * (1.0 + jnp.tanh(c * (x + 0.044715 * (x * x * x))))


def _dot(a, b):
    return jnp.dot(a, b, preferred_element_type=F32)


def _dot_nt(a, b):
    return lax.dot_general(a, b, (((1,), (1,)), ((), ())), preferred_element_type=F32)


def _split_bf16(x):
    hi = x.astype(BF16)
    lo = (x - hi.astype(F32)).astype(BF16)
    return hi, lo


def _full(shape):
    n = len(shape)
    return pl.BlockSpec(shape, lambda *_: (0,) * n)


def _in_proj_kernel(x_ref, g_ref, w_ref, qg_ref, kg_ref, hm_ref,
                    u_ref, q_ref, k_ref, v_ref, kk_ref, vk_ref, *, width, nt):
    h = _rms(x_ref[...], g_ref[...]).astype(BF16)
    z = _dot(h, w_ref[...])
    u_ref[...] = z[:, :width].astype(BF16)

    def head_norm(a, gain):
        hi, lo = _split_bf16(a * a)
        ms = _dot(hi, hm_ref[...]) + _dot(lo, hm_ref[...])
        return a * lax.rsqrt(ms + EPS) * gain

    q = head_norm(z[:, width:2 * width], qg_ref[...])
    k = head_norm(z[:, 2 * width:3 * width], kg_ref[...])
    v = z[:, 3 * width:]
    q_ref[...] = q.astype(BF16)
    k_ref[...] = k.astype(BF16)
    v_ref[...] = v.astype(BF16)

    @pl.when(pl.program_id(1) == nt - 1)
    def _():
        for hd in range(width // ATT_HEAD_DIM):
            cols = slice(hd * ATT_HEAD_DIM, (hd + 1) * ATT_HEAD_DIM)
            kk_ref[:, hd, :] = k[:, cols]
            vk_ref[:, hd, :] = v[:, cols]


def _in_proj(x2d, gain, w_bf, qg, kg, hm, *, nb, nt, tm):
    rows, d = x2d.shape
    width = w_bf.shape[1] // 4
    heads = width // ATT_HEAD_DIM
    tok = lambda b, t: (b * nt + t, 0)
    out_shape = (
        jax.ShapeDtypeStruct((nt * tm, nb * width), BF16),
        jax.ShapeDtypeStruct((rows, width), BF16),
        jax.ShapeDtypeStruct((rows, width), BF16),
        jax.ShapeDtypeStruct((rows, width), BF16),
        jax.ShapeDtypeStruct((nb * tm, heads, ATT_HEAD_DIM), F32),
        jax.ShapeDtypeStruct((nb * tm, heads, ATT_HEAD_DIM), F32),
    )
    keep = pl.BlockSpec((tm, heads, ATT_HEAD_DIM), lambda b, t: (b, 0, 0))
    return pl.pallas_call(
        functools.partial(_in_proj_kernel, width=width, nt=nt),
        grid=(nb, nt),
        in_specs=[pl.BlockSpec((tm, d), tok), _full((1, d)), _full(w_bf.shape),
                  _full((1, width)), _full((1, width)), _full((width, width))],
        out_specs=(pl.BlockSpec((tm, width), lambda b, t: (t, b)),
                   pl.BlockSpec((tm, width), tok), pl.BlockSpec((tm, width), tok),
                   pl.BlockSpec((tm, width), tok),
                   keep, keep),
        out_shape=out_shape,
        compiler_params=_cparams(("arbitrary", "arbitrary")),
        name="in_proj",
    )(x2d, gain, w_bf, qg, kg, hm)


def _ssm_kernel(u_ref, bc_ref, cre_ref, cim_ref, lre_ref, lim_ref, d_ref, s0re_ref, s0im_ref,
                y_ref, sre_ref, sim_ref, bu_ref, *, nb, tc, ns):
    @pl.when(pl.program_id(0) == 0)
    def _():
        sre_ref[...] = s0re_ref[...]
        sim_ref[...] = s0im_ref[...]

    u = u_ref[...]
    n_tiles = 2 * ns // MXU_N
    for j in range(n_tiles):
        slab = (j % (n_tiles // 2)) // 2
        bu_ref[:, j * MXU_N:(j + 1) * MXU_N] = _dot(u[:, slab * LANES:(slab + 1) * LANES], bc_ref[j])

    cw = 8 * 1024 // nb
    for cb in range(ns // cw):
        c0 = cb * cw
        lre = jnp.broadcast_to(lre_ref[:, c0:c0 + cw], (nb, cw))
        lim = jnp.broadcast_to(lim_ref[:, c0:c0 + cw], (nb, cw))

        def step(t, carry, c0=c0, lre=lre, lim=lim):
            sr, si = carry
            r0 = pl.multiple_of(t * nb, nb)
            nr = lre * sr - lim * si + bu_ref[pl.ds(r0, nb), c0:c0 + cw]
            ni = lre * si + lim * sr + bu_ref[pl.ds(r0, nb), ns + c0:ns + c0 + cw]
            bu_ref[pl.ds(r0, nb), c0:c0 + cw] = nr
            bu_ref[pl.ds(r0, nb), ns + c0:ns + c0 + cw] = ni
            return nr, ni

        sr, si = lax.fori_loop(0, tc, step, (sre_ref[:, c0:c0 + cw], sim_ref[:, c0:c0 + cw]))
        sre_ref[:, c0:c0 + cw] = sr
        sim_ref[:, c0:c0 + cw] = si

    width = u.shape[1]
    kt = ns * MXU_N // width
    for n in range(width // MXU_N):
        s_re = bu_ref[:, n * kt:(n + 1) * kt].astype(BF16)
        s_im = bu_ref[:, ns + n * kt:ns + (n + 1) * kt].astype(BF16)
        y = _dot(s_re, cre_ref[n]) + _dot(s_im, cim_ref[n])
        cols = slice(n * MXU_N, (n + 1) * MXU_N)
        y_ref[:, cols] = y + d_ref[:, cols] * u[:, cols].astype(F32)


def _ssm(u_rows, bc, cre, cim, lre, lim, dskip, s0re, s0im, *, nb, tc):
    rows, width = u_rows.shape
    ns = lre.shape[1]
    r = tc * nb
    return pl.pallas_call(
        functools.partial(_ssm_kernel, nb=nb, tc=tc, ns=ns),
        grid=(rows // r,),
        in_specs=[pl.BlockSpec((r, width), lambda i: (i, 0)), _full(bc.shape), _full(cre.shape),
                  _full(cim.shape), _full((1, ns)), _full((1, ns)), _full((1, width)),
                  _full((nb, ns)), _full((nb, ns))],
        out_specs=(pl.BlockSpec((r, width), lambda i: (i, 0)), _full((nb, ns)), _full((nb, ns))),
        out_shape=(jax.ShapeDtypeStruct((rows, width), F32),
                   jax.ShapeDtypeStruct((nb, ns), F32), jax.ShapeDtypeStruct((nb, ns), F32)),
        scratch_shapes=[pltpu.VMEM((r, 2 * ns), F32)],
        compiler_params=_cparams(("arbitrary",)),
        name="ssm",
    )(u_rows, bc, cre, cim, lre, lim, dskip, s0re, s0im)


def _half_select(shape):
    lane = lax.broadcasted_iota(jnp.int32, shape, 1)
    return lane < ATT_HEAD_DIM


def _head_masks(first):
    m0 = jnp.where(first, 1.0, 0.0).astype(BF16)
    return m0, (1.0 - m0.astype(F32)).astype(BF16)


def _band_prompt_kernel(q_ref, k_ref, v_ref, bias_ref, o_ref, kp_ref, vp_ref, s_scr, p_scr, l_scr, *, t_len):
    width = q_ref.shape[1]
    n_hp = width // LANES
    kp_ref[0:PAD_ROWS, :] = jnp.zeros((PAD_ROWS, width), BF16)
    vp_ref[0:PAD_ROWS, :] = jnp.zeros((PAD_ROWS, width), BF16)
    kp_ref[PAD_ROWS:, :] = k_ref[...]
    vp_ref[PAD_ROWS:, :] = v_ref[...]
    first = _half_select((PAIR_ROWS, LANES))
    head_mask = _head_masks(first)
    col = lax.broadcasted_iota(jnp.int32, (SOFTMAX_ROWS, PAIR_BAND), 1)

    def pair(pi, carry, *, masked):
        r0 = pl.multiple_of(pi * PAIR_ROWS, PAIR_ROWS)
        for hp in range(n_hp):
            lanes = slice(hp * LANES, (hp + 1) * LANES)
            qp = q_ref[pl.ds(r0, PAIR_ROWS), lanes]
            qs = jnp.concatenate([qp * head_mask[0], qp * head_mask[1]], axis=0)
            s_scr[2 * hp * PAIR_ROWS:2 * (hp + 1) * PAIR_ROWS, :] = _dot_nt(
                qs, kp_ref[pl.ds(r0, PAIR_BAND), lanes])
        for h in range(2 * n_hp):
            for rb in range(0, PAIR_ROWS, SOFTMAX_ROWS):
                rows = slice(h * PAIR_ROWS + rb, h * PAIR_ROWS + rb + SOFTMAX_ROWS)
                s = s_scr[rows, :] + bias_ref[h, rb:rb + SOFTMAX_ROWS, :]
                if masked:
                    s = jnp.where(col >= PAD_ROWS - r0, s, NEG_INF)
                p = jnp.exp(s - jnp.max(s, axis=-1, keepdims=True))
                p_scr[rows, :] = p.astype(BF16)
                l_scr[rows, :] = jnp.broadcast_to(1.0 / jnp.sum(p, axis=-1, keepdims=True),
                                                  (SOFTMAX_ROWS, LANES))
        for hp in range(n_hp):
            lanes = slice(hp * LANES, (hp + 1) * LANES)
            rows = slice(2 * hp * PAIR_ROWS, 2 * (hp + 1) * PAIR_ROWS)
            o2 = _dot(p_scr[rows, :], vp_ref[pl.ds(r0, PAIR_BAND), lanes]) * l_scr[rows, :]
            o_ref[pl.ds(r0, PAIR_ROWS), lanes] = jnp.where(
                first, o2[:PAIR_ROWS], o2[PAIR_ROWS:]).astype(BF16)
        return carry

    n_masked = PAD_ROWS // PAIR_ROWS
    lax.fori_loop(0, n_masked, functools.partial(pair, masked=True), 0)
    lax.fori_loop(n_masked, t_len // PAIR_ROWS, functools.partial(pair, masked=False), 0)


def _band_prompt(q, k, v, bias, *, nb, t_len):
    width = q.shape[1]
    heads = width // ATT_HEAD_DIM
    blk = pl.BlockSpec((t_len, width), lambda b: (b, 0))
    return pl.pallas_call(
        functools.partial(_band_prompt_kernel, t_len=t_len),
        grid=(nb,),
        in_specs=[blk, blk, blk, _full(bias.shape)],
        out_specs=blk,
        out_shape=jax.ShapeDtypeStruct(q.shape, BF16),
        scratch_shapes=[pltpu.VMEM((t_len + PAD_ROWS, width), BF16),
                        pltpu.VMEM((t_len + PAD_ROWS, width), BF16),
                        pltpu.VMEM((heads * PAIR_ROWS, PAIR_BAND), F32),
                        pltpu.VMEM((heads * PAIR_ROWS, PAIR_BAND), BF16),
                        pltpu.VMEM((heads * PAIR_ROWS, LANES), F32)],
        compiler_params=_cparams(("arbitrary",)),
        name="band_prompt",
    )(q, k, v, bias)


def _band_sample_kernel(q_ref, k_ref, v_ref, ck_ref, cv_ref, bc_ref, bn_ref, o_ref):
    for h in range(ck_ref.shape[1]):
        cols = slice(h * ATT_HEAD_DIM, (h + 1) * ATT_HEAD_DIM)
        qh = q_ref[:, cols]
        sc = _dot_nt(qh, ck_ref[:, h, :].astype(BF16)) + bc_ref[h]
        sn = _dot_nt(qh, k_ref[:, cols]) + bn_ref[h]
        m = jnp.maximum(jnp.max(sc, axis=-1, keepdims=True), jnp.max(sn, axis=-1, keepdims=True))
        pc = jnp.exp(sc - m)
        pn = jnp.exp(sn - m)
        l = jnp.sum(pc, axis=-1, keepdims=True) + jnp.sum(pn, axis=-1, keepdims=True)
        o = _dot(pc.astype(BF16), cv_ref[:, h, :].astype(BF16)) + _dot(pn.astype(BF16), v_ref[:, cols])
        o_ref[:, cols] = (o / l).astype(BF16)


def _band_sample(q_tm, k_tm, v_tm, cache_k, cache_v, bias_c, bias_n, *, nb, s_len):
    width = q_tm.shape[1] // nb
    w_rows = cache_k.shape[0] // nb
    col = pl.BlockSpec((s_len, width), lambda b: (0, b))
    cache = pl.BlockSpec((w_rows,) + cache_k.shape[1:], lambda b: (b, 0, 0))
    return pl.pallas_call(
        _band_sample_kernel,
        grid=(nb,),
        in_specs=[col, col, col, cache, cache, _full(bias_c.shape), _full(bias_n.shape)],
        out_specs=col,
        out_shape=jax.ShapeDtypeStruct(q_tm.shape, BF16),
        compiler_params=_cparams(("arbitrary",)),
        name="band_sample",
    )(q_tm, k_tm, v_tm, cache_k, cache_v, bias_c, bias_n)


def _mem_kv_kernel(m_ref, g_ref, wk_ref, wv_ref, kg_ref, k_ref, v_ref, kb_ref, vb_ref):
    m = _rms(m_ref[...], g_ref[...]).astype(BF16)
    k = _dot(m, wk_ref[...])
    v = _dot(m, wv_ref[...])
    hd = kg_ref.shape[1]
    for h in range(k.shape[1] // hd):
        cols = slice(h * hd, (h + 1) * hd)
        kh = _rms(k[:, cols], kg_ref[...])
        k_ref[:, h, :] = kh
        v_ref[:, h, :] = v[:, cols]
        kb_ref[:, cols] = kh.astype(BF16)
    vb_ref[...] = v.astype(BF16)


def _mem_kv(mem2d, gain, wk, wv, kgain, *, tm):
    rows, d = mem2d.shape
    hd = kgain.shape[1]
    blk = pl.BlockSpec((tm, d), lambda i: (i, 0))
    hblk = pl.BlockSpec((tm, d // hd, hd), lambda i: (i, 0, 0))
    return pl.pallas_call(
        _mem_kv_kernel,
        grid=(rows // tm,),
        in_specs=[blk, _full((1, d)), _full(wk.shape), _full(wv.shape), _full(kgain.shape)],
        out_specs=(hblk, hblk, blk, blk),
        out_shape=(jax.ShapeDtypeStruct((rows, d // hd, hd), F32),
                   jax.ShapeDtypeStruct((rows, d // hd, hd), F32),
                   jax.ShapeDtypeStruct((rows, d), BF16), jax.ShapeDtypeStruct((rows, d), BF16)),
        compiler_params=_cparams(("arbitrary",)),
        name="mem_kv",
    )(mem2d, gain, wk, wv, kgain)


def _route(logits):
    lane = lax.broadcasted_iota(jnp.int32, logits.shape, 1).astype(F32)
    big = float(ROUTER_LANES)
    is_g = lane < N_GROUPS
    lg = jnp.where(is_g, logits, NEG_INF)
    gmax = jnp.max(lg, axis=-1, keepdims=True)
    p1 = 1.0 / jnp.sum(jnp.where(is_g, jnp.exp(lg - gmax), 0.0), axis=-1, keepdims=True)
    g_idx = jnp.min(jnp.where(lg == gmax, lane, big), axis=-1, keepdims=True)
    lo = ROUTER_E0 + g_idx * EXPERTS_PER_GROUP
    le = jnp.where((lane >= lo) & (lane < lo + EXPERTS_PER_GROUP), logits, NEG_INF)
    v1 = jnp.max(le, axis=-1, keepdims=True)
    i1 = jnp.min(jnp.where(le == v1, lane, big), axis=-1, keepdims=True)
    le2 = jnp.where(lane == i1, NEG_INF, le)
    v2 = jnp.max(le2, axis=-1, keepdims=True)
    i2 = jnp.min(jnp.where(le2 == v2, lane, big), axis=-1, keepdims=True)
    e2 = jnp.exp(v2 - v1)
    w1 = p1 / (1.0 + e2)
    w2 = p1 * e2 / (1.0 + e2)
    gates = jnp.where(lane == i1, w1, 0.0) + jnp.where(lane == i2, w2, 0.0)
    a = jnp.minimum(i1, i2) - lo
    b = jnp.maximum(i1, i2) - lo
    pair = a * (2 * EXPERTS_PER_GROUP - 1 - a) * 0.5 + (b - a - 1.0)
    cls = g_idx * PAIRS_PER_GROUP + pair
    w_a = jnp.where(i1 < i2, w1, w2)
    w_b = jnp.where(i1 < i2, w2, w1)
    return (gates + jnp.where(lane == ROUTE_CLS, cls, 0.0) + jnp.where(lane == ROUTE_WA, w_a, 0.0)
            + jnp.where(lane == ROUTE_WB, w_b, 0.0))


def _mix_mem_kernel(x_ref, ys_ref, ya_ref, mk_ref, mv_ref, wglu_ref, ons_ref, ona_ref, wout_ref,
                    nmem_ref, wq_ref, qg_ref, wo_ref, nffn_ref, wrh_ref, wrl_ref, rb_ref,
                    *outs, routed):
    if routed:
        x2_ref, o_scr = outs
    else:
        x2_ref, xn_ref, gate_ref, o_scr = outs
    width = ys_ref.shape[1]
    g = _dot(_gelu_tanh(ys_ref[...]).astype(BF16), wglu_ref[...])
    y_s = g[:, :width] * _sigmoid(g[:, width:])
    cat_s = _rms(y_s, ons_ref[...]).astype(BF16)
    cat_a = _rms(ya_ref[...].astype(F32), ona_ref[...]).astype(BF16)
    x1 = x_ref[...] + _dot(cat_s, wout_ref[0:width, :]) + _dot(cat_a, wout_ref[width:, :])

    q = _dot(_rms(x1, nmem_ref[...]).astype(BF16), wq_ref[...])
    hd = qg_ref.shape[1]
    scale = hd ** -0.5
    for h in range(q.shape[1] // hd):
        cols = slice(h * hd, (h + 1) * hd)
        qh = _rms(q[:, cols], qg_ref[...]).astype(BF16)
        head = (slice(None), h, slice(None)) if len(mk_ref.shape) == 3 else (slice(None), cols)
        s = _dot_nt(qh, mk_ref[head].astype(BF16)) * scale
        p = jnp.exp(s - jnp.max(s, axis=-1, keepdims=True))
        l = jnp.sum(p, axis=-1, keepdims=True)
        o_scr[:, cols] = (_dot(p.astype(BF16), mv_ref[head].astype(BF16)) / l).astype(BF16)
    acc = x1 + _dot(o_scr[...], wo_ref[...])
    d = acc.shape[1]
    x2_ref[:, :d] = acc

    xn = _rms(acc, nffn_ref[...])
    xh, xl = _split_bf16(xn)
    logits = _dot(xh, wrh_ref[...]) + _dot(xl, wrh_ref[...]) + _dot(xh, wrl_ref[...]) + rb_ref[...]
    if routed:
        x2_ref[:, d:] = _route(logits)
    else:
        xn_ref[...] = xh
        gate_ref[...] = _route(logits)


def _mix_mem(x, ys, ya, mk, mv, weights, *, grid, tm, row_map, ssm_map, mem_map, routed):
    d = weights["w_out"].shape[1]
    width = weights["w_glu"].shape[0]
    names = ("w_glu", "out_norm_ssm", "out_norm_att", "w_out", "norm_mem", "w_mem_q", "mem_q_gain",
             "w_mem_o", "norm_ffn", "w_router_hi", "w_router_lo", "router_bias")
    ws = [weights[n] for n in names]
    mem_rows = weights["n_mem"]
    xspec = pl.BlockSpec((tm, d), row_map)
    hspec = pl.BlockSpec((tm, width), row_map)
    sspec = pl.BlockSpec((tm, width), ssm_map)
    mspec = pl.BlockSpec((mem_rows,) + mk.shape[1:], mem_map)
    n_col = x.shape[1] // d
    if routed:
        out_specs = pl.BlockSpec((tm, d + ROUTER_LANES), row_map)
        out_shape = jax.ShapeDtypeStruct((x.shape[0], n_col * (d + ROUTER_LANES)), F32)
    else:
        out_specs = (xspec, xspec, pl.BlockSpec((tm, ROUTER_LANES), row_map))
        out_shape = (jax.ShapeDtypeStruct(x.shape, F32), jax.ShapeDtypeStruct(x.shape, BF16),
                     jax.ShapeDtypeStruct((x.shape[0], n_col * ROUTER_LANES), F32))
    return pl.pallas_call(
        functools.partial(_mix_mem_kernel, routed=routed),
        grid=grid,
        in_specs=[xspec, sspec, hspec, mspec, mspec] + [_full(w.shape) for w in ws],
        out_specs=out_specs,
        out_shape=out_shape,
        scratch_shapes=[pltpu.VMEM((tm, d), BF16)],
        compiler_params=_cparams(("arbitrary",) * len(grid)),
        name="mix_mem",
    )(x, ys, ya, mk, mv, *ws)


def _moe_kernel(xn_ref, x2_ref, gate_ref, wg_ref, wu_ref, wd_ref, o_ref):
    e = pl.program_id(1)

    @pl.when(e == 0)
    def _():
        o_ref[...] = x2_ref[...]

    gates = gate_ref[...]
    lane = lax.broadcasted_iota(jnp.int32, gates.shape, 1)
    ge = jnp.sum(jnp.where(lane == e + ROUTER_E0, gates, 0.0), axis=-1, keepdims=True)
    xn = xn_ref[...]
    a = _dot(xn, wg_ref[...])
    h = a * _sigmoid(a) * _dot(xn, wu_ref[...])
    o_ref[...] += ge * _dot(h.astype(BF16), wd_ref[...])


def _moe(xn, x2, gates, wg, wu, wd, *, tm):
    rows, d = xn.shape
    n_exp, _, dff = wg.shape
    row = lambda i, e: (i, 0)
    return pl.pallas_call(
        _moe_kernel,
        grid=(rows // tm, n_exp),
        in_specs=[pl.BlockSpec((tm, d), row), pl.BlockSpec((tm, d), row),
                  pl.BlockSpec((tm, ROUTER_LANES), row),
                  pl.BlockSpec((None, d, dff), lambda i, e: (e, 0, 0)),
                  pl.BlockSpec((None, d, dff), lambda i, e: (e, 0, 0)),
                  pl.BlockSpec((None, dff, d), lambda i, e: (e, 0, 0))],
        out_specs=pl.BlockSpec((tm, d), row),
        out_shape=jax.ShapeDtypeStruct((rows, d), F32),
        compiler_params=_cparams(("arbitrary", "arbitrary")),
        name="moe",
    )(xn, x2, gates, wg, wu, wd)


def _moe_routed_kernel(tile_ref, ea_ref, eb_ref, lo_ref, hi_ref, valid_ref, tok_ref,
                       x_hbm, nffn_ref, wga_ref, wua_ref, wda_ref, wgb_ref, wub_ref, wdb_ref,
                       y_hbm, *scratch, tm, n_tiles):
    nb = MOE_BUFS
    xbuf, obuf = scratch[:nb], scratch[nb:2 * nb]
    acc_ref, gsem, ssem = scratch[2 * nb:]
    w = pl.program_id(0)
    t = tile_ref[w]
    lo = lo_ref[w]
    hi = hi_ref[w]
    valid = valid_ref[w] == 1
    d = y_hbm.shape[1]

    def row_copy_in(tile, s, r):
        tok = tok_ref[tile * tm + r]
        return pltpu.make_async_copy(x_hbm.at[pl.ds(tok, 1)], xbuf[s].at[pl.ds(r, 1)], gsem.at[s])

    def row_copy_out(tile, s, r):
        tok = tok_ref[tile * tm + r]
        return pltpu.make_async_copy(obuf[s].at[pl.ds(r, 1)], y_hbm.at[pl.ds(tok, 1)], ssem.at[s])

    def start_gather(tile, s):
        for r in range(tm):
            row_copy_in(tile, s, r).start()

    def start_scatter(tile, s):
        for r in range(tm):
            row_copy_out(tile, s, r).start()

    def wait_gather(s):
        pltpu.make_async_copy(x_hbm.at[pl.ds(0, tm)], xbuf[s], gsem.at[s]).wait()

    def wait_scatter(s):
        pltpu.make_async_copy(obuf[s], y_hbm.at[pl.ds(0, tm)], ssem.at[s]).wait()

    def compute(s, first):
        xe = xbuf[s][...]
        xn = _rms(xe[:, :d], nffn_ref[...]).astype(BF16)
        row = lax.broadcasted_iota(jnp.int32, (tm, 1), 0)
        inseg = jnp.logical_and(row >= lo, row < hi)
        w_a = jnp.where(inseg, xe[:, d + ROUTE_WA:d + ROUTE_WA + 1], 0.0)
        w_b = jnp.where(inseg, xe[:, d + ROUTE_WB:d + ROUTE_WB + 1], 0.0)
        a = _dot(xn, wga_ref[...])
        h_a = (a * _sigmoid(a) * _dot(xn, wua_ref[...])).astype(BF16)
        b = _dot(xn, wgb_ref[...])
        h_b = (b * _sigmoid(b) * _dot(xn, wub_ref[...])).astype(BF16)
        upd = w_a * _dot(h_a, wda_ref[...]) + w_b * _dot(h_b, wdb_ref[...])
        if first:
            acc_ref[...] = upd
        else:
            acc_ref[...] += upd

    def when(*conds):
        c = conds[0]
        for extra in conds[1:]:
            c = jnp.logical_and(c, extra)
        return pl.when(c)

    ahead = nb - 1

    @pl.when(w == 0)
    def _():
        for k in range(ahead):
            start_gather(k, k)

    is_first = jnp.logical_and(valid, lo == 0)
    is_last = jnp.logical_and(valid, hi == tm)
    last_tile = n_tiles - 1
    for s in range(nb):
        mine = t % nb == s
        nxt, prv = (s + ahead) % nb, (s - 1) % nb

        if s == 0:
            @when(is_first, t == 0)
            def _(nxt=nxt):
                wait_gather(0)
                start_gather(ahead, nxt)
                compute(0, True)

        @when(is_first, mine, t > 0, t + ahead <= last_tile)
        def _(s=s, nxt=nxt, prv=prv):
            wait_gather(s)
            start_gather(t + ahead, nxt)
            start_scatter(t - 1, prv)
            compute(s, True)

        @when(is_first, mine, t + ahead > last_tile)
        def _(s=s, prv=prv):
            wait_gather(s)
            start_scatter(t - 1, prv)
            compute(s, True)

        @when(valid, mine, lo > 0)
        def _(s=s):
            compute(s, False)

        @when(is_last, mine)
        def _(s=s):
            @pl.when(t >= nb)
            def _():
                wait_scatter(s)

            obuf[s][...] = xbuf[s][:, :d] + acc_ref[...]

            if s == last_tile % nb:
                @pl.when(t == last_tile)
                def _():
                    start_scatter(t, s)
                    for k in range(nb):
                        wait_scatter(k)


def _moe_routed(x2ext, plan, nffn, wg, wu, wd, *, tm):
    rows, de = x2ext.shape
    d = de - ROUTER_LANES
    n_exp, _, dff = wg.shape
    n_tiles = rows // tm
    assert rows % tm == 0 and n_tiles > 2 * MOE_BUFS
    n_items = plan[0].shape[0]
    ea = lambda w, tile, ea_, eb_, *_: (ea_[w], 0, 0)
    eb = lambda w, tile, ea_, eb_, *_: (eb_[w], 0, 0)
    grid_spec = pltpu.PrefetchScalarGridSpec(
        num_scalar_prefetch=len(plan),
        grid=(n_items,),
        in_specs=[pl.BlockSpec(memory_space=pl.ANY),
                  pl.BlockSpec((1, d), lambda w, *_: (0, 0)),
                  pl.BlockSpec((None, d, dff), ea), pl.BlockSpec((None, d, dff), ea),
                  pl.BlockSpec((None, dff, d), ea),
                  pl.BlockSpec((None, d, dff), eb), pl.BlockSpec((None, d, dff), eb),
                  pl.BlockSpec((None, dff, d), eb)],
        out_specs=pl.BlockSpec(memory_space=pl.ANY),
        scratch_shapes=([pltpu.VMEM((tm, de), F32)] * MOE_BUFS + [pltpu.VMEM((tm, d), F32)] * MOE_BUFS
                        + [pltpu.VMEM((tm, d), F32),
                           pltpu.SemaphoreType.DMA((MOE_BUFS,)), pltpu.SemaphoreType.DMA((MOE_BUFS,))]),
    )
    return pl.pallas_call(
        functools.partial(_moe_routed_kernel, tm=tm, n_tiles=n_tiles),
        grid_spec=grid_spec,
        out_shape=jax.ShapeDtypeStruct((rows, d), F32),
        compiler_params=_cparams(("arbitrary",)),
        name="moe_routed",
    )(*plan, x2ext, nffn, wg, wu, wd, wg, wu, wd)


def _route_plan(cls, *, tm):
    n = cls.shape[0]
    n_tiles = n // tm
    order = jnp.argsort(cls).astype(jnp.int32)
    classes = jnp.arange(N_CLASSES, dtype=jnp.int32)
    class_end = jnp.sum((cls[:, None] <= classes[None, :]).astype(jnp.int32), axis=0)
    class_start = jnp.concatenate([jnp.zeros((1,), jnp.int32), class_end[:-1]])
    bounds = jnp.concatenate([jnp.arange(n_tiles, dtype=jnp.int32) * tm,
                              jnp.where(class_end > class_start, class_start, n)])
    n_items = bounds.shape[0]
    idx = jnp.arange(n_items, dtype=jnp.int32)
    before = jnp.logical_or(bounds[None, :] < bounds[:, None],
                            jnp.logical_and(bounds[None, :] == bounds[:, None], idx[None, :] < idx[:, None]))
    rank = jnp.sum(before.astype(jnp.int32), axis=1)
    start = jnp.sum(jnp.where(rank[:, None] == idx[None, :], bounds[:, None], 0), axis=0)
    stop = jnp.concatenate([start[1:], jnp.full((1,), n, jnp.int32)])
    valid = stop > start
    tile = jnp.minimum(start // tm, n_tiles - 1)
    lo = start - tile * tm
    hi = stop - tile * tm
    c = jnp.minimum(jnp.sum((class_end[None, :] <= start[:, None]).astype(jnp.int32), axis=1),
                    N_CLASSES - 1)
    g = c // PAIRS_PER_GROUP
    pair = c % PAIRS_PER_GROUP
    a = jnp.zeros_like(pair)
    for k in range(1, EXPERTS_PER_GROUP - 1):
        a = a + (pair >= k * (2 * EXPERTS_PER_GROUP - 1 - k) // 2).astype(jnp.int32)
    b = pair - a * (2 * EXPERTS_PER_GROUP - 1 - a) // 2 + a + 1
    e_a = g * EXPERTS_PER_GROUP + a
    e_b = g * EXPERTS_PER_GROUP + b
    i32 = lambda v: v.astype(jnp.int32)
    return (i32(tile), i32(e_a), i32(e_b), i32(lo), i32(hi), i32(valid), order)


def _ssm_params(lam_re, lam_im, log_step, b_re, b_im, c_re, c_im):
    n_g, n_p = lam_re.shape
    step = jnp.exp(log_step.astype(F32))[:, None]
    mag = jnp.exp(lam_re * step)
    lb_re = mag * jnp.cos(lam_im * step)
    lb_im = mag * jnp.sin(lam_im * step)
    den = lam_re * lam_re + lam_im * lam_im
    f_re = ((lb_re - 1.0) * lam_re + lb_im * lam_im) / den
    f_im = (lb_im * lam_re - (lb_re - 1.0) * lam_im) / den
    bb_re = f_re[..., None] * b_re - f_im[..., None] * b_im
    bb_im = f_re[..., None] * b_im + f_im[..., None] * b_re
    eye = jnp.eye(n_g, dtype=F32)
    ns = n_g * n_p
    width = n_g * SSM_GROUP
    b_full = jnp.concatenate(
        [jnp.einsum("hg,gpc->hcgp", eye, bb_re).reshape(width, ns),
         jnp.einsum("hg,gpc->hcgp", eye, bb_im).reshape(width, ns)], axis=1)
    n_tiles = 2 * ns // MXU_N
    bc = jnp.stack([
        b_full[((j % (n_tiles // 2)) // 2) * LANES:((j % (n_tiles // 2)) // 2 + 1) * LANES,
               j * MXU_N:(j + 1) * MXU_N] for j in range(n_tiles)]).astype(BF16)
    c_full_re = jnp.einsum("gh,gcp->gphc", eye, c_re).reshape(ns, width)
    c_full_im = -jnp.einsum("gh,gcp->gphc", eye, c_im).reshape(ns, width)
    kt = ns * MXU_N // width
    tiles = range(width // MXU_N)
    cre = jnp.stack([c_full_re[n * kt:(n + 1) * kt, n * MXU_N:(n + 1) * MXU_N] for n in tiles]).astype(BF16)
    cim = jnp.stack([c_full_im[n * kt:(n + 1) * kt, n * MXU_N:(n + 1) * MXU_N] for n in tiles]).astype(BF16)
    return bc, cre, cim, lb_re.reshape(1, ns), lb_im.reshape(1, ns)


def _rel_bias(rel_bias, q0, n_q, n_k):
    n_r = n_q + n_k - 1
    dist = q0 + n_q - 1 - jnp.arange(n_r)
    r = rel_bias.astype(F32)[:, jnp.clip(dist, -REL_CLIP, REL_CLIP) + REL_CLIP]
    r = jnp.pad(r, ((0, 0), (0, 1)))
    rows = jnp.tile(r, (1, n_q))[:, :n_q * n_r].reshape(-1, n_q, n_r)
    return rows[:, :, n_q - 1:n_q - 1 + n_k]


def _layer(xp, xs, mem_p, ck, cv, s_re, s_im, cmk, cmv, w, dims):
    batch, seq, dec_batch, dec_seq, d = dims
    width = w["w_in"].shape[1] // 4
    ns = w["lam_re"].shape[1]
    n_mem = w["n_mem"]
    tm_p = 512
    nt_p = seq // tm_p
    n_dec = dec_batch * dec_seq

    u, q, k, v, kk, vk = _in_proj(xp, w["norm_mix"], w["w_in"], w["qg"], w["kg"], w["head_mean"],
                                  nb=batch, nt=nt_p, tm=tm_p)
    zeros = jnp.zeros((batch, ns), F32)
    y_ssm, pre, pim = _ssm(u.reshape(seq * batch, width), w["bc"], w["cre"], w["cim"], w["lam_re"],
                           w["lam_im"], w["ssm_d"], zeros, zeros, nb=batch, tc=32)
    y_att = _band_prompt(q, k, v, w["bias_p"], nb=batch, t_len=seq)
    mk, mv, mkb, mvb = _mem_kv(mem_p, w["mem_in_norm"], w["w_mem_k"], w["w_mem_v"], w["mem_k_gain"], tm=512)
    tm_d = 256
    nt_d = seq // tm_d
    x2ext = _mix_mem(xp, y_ssm.reshape(seq, batch * width), y_att, mkb, mvb, w,
                     grid=(batch, nt_d), tm=tm_d,
                     row_map=lambda b, t: (b * nt_d + t, 0), ssm_map=lambda b, t: (t, b),
                     mem_map=lambda b, t: (b, 0), routed=True)
    plan = _route_plan(x2ext[:, d + ROUTE_CLS].astype(jnp.int32), tm=MOE_TILE)
    yp = _moe_routed(x2ext, plan, w["norm_ffn"], w["exp_w_gate"], w["exp_w_up"], w["exp_w_down"],
                     tm=MOE_TILE)

    us, qs, ks, vs, kks, vks = _in_proj(xs, w["norm_mix"], w["w_in"], w["qg"], w["kg"], w["head_mean"],
                                        nb=1, nt=1, tm=n_dec)
    ys_ssm, sre, sim = _ssm(us, w["bc"], w["cre"], w["cim"], w["lam_re"], w["lam_im"], w["ssm_d"],
                            s_re, s_im, nb=dec_batch, tc=dec_seq)
    tmv = lambda a: a.reshape(dec_seq, dec_batch * a.shape[1])
    ys_att = _band_sample(tmv(qs), tmv(ks), tmv(vs), ck, cv, w["bias_sc"], w["bias_sn"],
                          nb=dec_batch, s_len=dec_seq)
    x2s, xns, gates_s = _mix_mem(tmv(xs), tmv(ys_ssm), ys_att, cmk, cmv, w,
                                 grid=(dec_batch,), tm=dec_seq,
                                 row_map=lambda b: (0, b), ssm_map=lambda b: (0, b),
                                 mem_map=lambda b: (b, 0, 0), routed=False)
    ys = _moe(xns.reshape(n_dec, d), x2s.reshape(n_dec, d), gates_s.reshape(n_dec, ROUTER_LANES),
              w["exp_w_gate"], w["exp_w_up"], w["exp_w_down"], tm=n_dec)
    return yp, ys, (kk, vk, pre, pim, mk, mv), (kks, vks, sre, sim)


def kernel(x_prompt, x_sample, mem_prompt, cache_attn_k, cache_attn_v, state_ssm_re, state_ssm_im, cache_mem_k, cache_mem_v, norm_mix, w_in, ssm_lambda_re, ssm_lambda_im, ssm_log_step, ssm_b_re, ssm_b_im, ssm_c_re, ssm_c_im, ssm_d, w_glu, att_q_gain, att_k_gain, att_rel_bias, out_norm_ssm, out_norm_att, w_out, norm_mem, mem_in_norm, w_mem_q, w_mem_k, w_mem_v, w_mem_o, mem_q_gain, mem_k_gain, norm_ffn, router_g_w, router_g_b, router_e_w, router_e_b, exp_w_gate, exp_w_up, exp_w_down):
    depth = norm_mix.shape[0]
    batch, seq, d = x_prompt.shape
    dec_batch, dec_seq, _ = x_sample.shape
    n_mem = mem_prompt.shape[1]
    att_rows = cache_attn_k.shape[2]
    n_g, n_p = ssm_lambda_re.shape[1:]
    width = n_g * SSM_GROUP
    heads = width // ATT_HEAD_DIM
    ns = n_g * n_p
    assert seq % 512 == 0 and att_rows == PAD_ROWS and seq >= PAD_ROWS
    assert (dec_batch * dec_seq) % 8 == 0 and dec_seq % 16 == 0

    xp = x_prompt.reshape(batch * seq, d)
    xs = x_sample.transpose(1, 0, 2).reshape(dec_seq * dec_batch, d)
    mem_p = mem_prompt.reshape(batch * n_mem, d)
    row = lambda a: a.reshape(1, -1).astype(F32)
    head_mean = jnp.kron(jnp.eye(heads, dtype=F32),
                         jnp.full((ATT_HEAD_DIM, ATT_HEAD_DIM), 1.0 / ATT_HEAD_DIM, F32)).astype(BF16)
    pr = jnp.arange(PAIR_ROWS)[:, None]
    pc = jnp.arange(PAIR_BAND)[None, :]
    pair_ok = jnp.where(pr < CHUNK, pc < BAND, pc >= CHUNK)
    att_scale = ATT_HEAD_DIM ** -0.5

    p_out, s_out = [], []
    for l in range(depth):
        bc, cre, cim, lb_re, lb_im = _ssm_params(ssm_lambda_re[l], ssm_lambda_im[l], ssm_log_step[l],
                                                 ssm_b_re[l], ssm_b_im[l], ssm_c_re[l], ssm_c_im[l])
        w_router = jnp.concatenate(
            [router_g_w[l], router_e_w[l].transpose(1, 0, 2).reshape(d, N_GROUPS * EXPERTS_PER_GROUP),
             jnp.zeros((d, ROUTER_LANES - N_GROUPS * (1 + EXPERTS_PER_GROUP)), F32)], axis=1)
        wr_hi, wr_lo = _split_bf16(w_router)
        r_bias = jnp.concatenate(
            [router_g_b[l], router_e_b[l].reshape(-1),
             jnp.zeros((ROUTER_LANES - N_GROUPS * (1 + EXPERTS_PER_GROUP),), F32)]).reshape(1, ROUTER_LANES)
        bias_s = _rel_bias(att_rel_bias[l], att_rows, dec_seq, att_rows + dec_seq)
        bias_p = jnp.where(pair_ok, _rel_bias(att_rel_bias[l], PAD_ROWS, PAIR_ROWS, PAIR_BAND), NEG_INF)
        w = dict(
            n_mem=n_mem,
            norm_mix=row(norm_mix[l]), w_in=w_in[l].astype(BF16),
            qg=row(jnp.tile(att_q_gain[l], heads) * att_scale), kg=row(jnp.tile(att_k_gain[l], heads)),
            head_mean=head_mean, bc=bc, cre=cre, cim=cim, lam_re=lb_re, lam_im=lb_im,
            ssm_d=row(ssm_d[l]), bias_p=bias_p,
            bias_sc=bias_s[:, :, :att_rows], bias_sn=bias_s[:, :, att_rows:],
            mem_in_norm=row(mem_in_norm[l]), w_mem_k=w_mem_k[l].astype(BF16),
            w_mem_v=w_mem_v[l].astype(BF16), mem_k_gain=row(mem_k_gain[l]),
            w_glu=w_glu[l].astype(BF16), out_norm_ssm=row(out_norm_ssm[l]),
            out_norm_att=row(out_norm_att[l]), w_out=w_out[l].astype(BF16), norm_mem=row(norm_mem[l]),
            w_mem_q=w_mem_q[l].astype(BF16), mem_q_gain=row(mem_q_gain[l]),
            w_mem_o=w_mem_o[l].astype(BF16), norm_ffn=row(norm_ffn[l]),
            w_router_hi=wr_hi, w_router_lo=wr_lo, router_bias=r_bias,
            exp_w_gate=exp_w_gate[l].astype(BF16), exp_w_up=exp_w_up[l].astype(BF16),
            exp_w_down=exp_w_down[l].astype(BF16),
        )
        xp, xs, p_new, s_new = _layer(
            xp, xs, mem_p,
            cache_attn_k[l].reshape(dec_batch * att_rows, heads, ATT_HEAD_DIM),
            cache_attn_v[l].reshape(dec_batch * att_rows, heads, ATT_HEAD_DIM),
            state_ssm_re[l].reshape(dec_batch, ns), state_ssm_im[l].reshape(dec_batch, ns),
            cache_mem_k[l].reshape(dec_batch * n_mem, MEM_HEADS, d // MEM_HEADS),
            cache_mem_v[l].reshape(dec_batch * n_mem, MEM_HEADS, d // MEM_HEADS),
            w, (batch, seq, dec_batch, dec_seq, d))
        p_out.append(p_new)
        s_out.append(s_new)

    sdt = state_ssm_re.dtype
    keep = min(PAD_ROWS, seq)
    kv_p = lambda a: a.reshape(batch, keep, heads, ATT_HEAD_DIM)
    kv_s = lambda a: a.reshape(dec_seq, dec_batch, heads, ATT_HEAD_DIM).transpose(1, 0, 2, 3)
    st = lambda a: a.reshape(a.shape[0], n_g, n_p).astype(sdt)
    mkv = lambda a: a.reshape(batch, n_mem, MEM_HEADS, d // MEM_HEADS)
    stack = lambda f, outs, i: jnp.stack([f(o[i]) for o in outs])
    yp = xp.reshape(batch, seq, d)
    ys = xs.reshape(dec_seq, dec_batch, d).transpose(1, 0, 2)
    return (yp, ys,
            stack(kv_p, p_out, 0), stack(kv_p, p_out, 1), stack(st, p_out, 2), stack(st, p_out, 3),
            stack(mkv, p_out, 4), stack(mkv, p_out, 5),
            stack(kv_s, s_out, 0), stack(kv_s, s_out, 1), stack(st, s_out, 2), stack(st, s_out, 3))
```

```python
import functools
import math

import jax
import jax.numpy as jnp
from jax import lax
from jax.experimental import pallas as pl
from jax.experimental.pallas import tpu as pltpu

F32 = jnp.float32
BF16 = jnp.bfloat16

CHUNK = 64
N_PREV_CHUNKS = 8
BAND = (N_PREV_CHUNKS + 1) * CHUNK
PAD_ROWS = N_PREV_CHUNKS * CHUNK
PAIR_ROWS = 2 * CHUNK
PAIR_BAND = BAND + CHUNK
SOFTMAX_ROWS = 32
SSM_GROUP = 16
SSM_STATE = 64
ATT_HEAD_DIM = 64
REL_CLIP = 128
MEM_HEADS = 4
N_GROUPS = 4
EXPERTS_PER_GROUP = 8
EPS = 1e-6
NEG_INF = -1e30

LANES = 128
MXU_N = 256
VMEM_LIMIT = 56 * 1024 * 1024
ROUTER_LANES = LANES
MOE_TILE = 128
MOE_BUFS = 3
ROUTER_E0 = N_GROUPS
PAIRS_PER_GROUP = EXPERTS_PER_GROUP * (EXPERTS_PER_GROUP - 1) // 2
N_CLASSES = N_GROUPS * PAIRS_PER_GROUP
ROUTE_CLS, ROUTE_WA, ROUTE_WB = 64, 65, 66


def _cparams(sem):
    return pltpu.CompilerParams(dimension_semantics=sem, vmem_limit_bytes=VMEM_LIMIT)


def _rms(x, gain):
    ms = jnp.mean(x * x, axis=-1, keepdims=True)
    return x * lax.rsqrt(ms + EPS) * gain


def _sigmoid(x):
    return 1.0 / (1.0 + jnp.exp(-x))


def _gelu_tanh(x):
    c = math.sqrt(2.0 / math.pi)
    return 0.5 * x * (1.0 + jnp.tanh(c * (x + 0.044715 * (x * x * x))))


def _dot(a, b):
    return jnp.dot(a, b, preferred_element_type=F32)


def _dot_nt(a, b):
    return lax.dot_general(a, b, (((1,), (1,)), ((), ())), preferred_element_type=F32)


def _split_bf16(x):
    hi = x.astype(BF16)
    lo = (x - hi.astype(F32)).astype(BF16)
    return hi, lo


def _full(shape):
    n = len(shape)
    return pl.BlockSpec(shape, lambda *_: (0,) * n)


def _in_proj_kernel(x_ref, g_ref, w_ref, qg_ref, kg_ref, hm_ref,
                    u_ref, q_ref, k_ref, v_ref, kk_ref, vk_ref, *, width, nt):
    h = _rms(x_ref[...], g_ref[...]).astype(BF16)
    z = _dot(h, w_ref[...])
    u_ref[...] = z[:, :width].astype(BF16)

    def head_norm(a, gain):
        hi, lo = _split_bf16(a * a)
        ms = _dot(hi, hm_ref[...]) + _dot(lo, hm_ref[...])
        return a * lax.rsqrt(ms + EPS) * gain

    q = head_norm(z[:, width:2 * width], qg_ref[...])
    k = head_norm(z[:, 2 * width:3 * width], kg_ref[...])
    v = z[:, 3 * width:]
    q_ref[...] = q.astype(BF16)
    k_ref[...] = k.astype(BF16)
    v_ref[...] = v.astype(BF16)

    @pl.when(pl.program_id(1) == nt - 1)
    def _():
        if len(kk_ref.shape) == 3:
            kt, vt = k.T, v.T
            for hd in range(width // ATT_HEAD_DIM):
                rows = slice(hd * ATT_HEAD_DIM, (hd + 1) * ATT_HEAD_DIM)
                kk_ref[hd] = kt[rows, :]
                vk_ref[hd] = vt[rows, :]
        else:
            kk_ref[...] = k
            vk_ref[...] = v


def _in_proj(x2d, gain, w_bf, qg, kg, hm, *, nb, nt, tm, keep_transposed):
    rows, d = x2d.shape
    width = w_bf.shape[1] // 4
    heads = width // ATT_HEAD_DIM
    tok = lambda b, t: (b * nt + t, 0)
    out_shape = (
        jax.ShapeDtypeStruct((nt * tm, nb * width), BF16),
        jax.ShapeDtypeStruct((rows, width), BF16),
        jax.ShapeDtypeStruct((rows, width), BF16),
        jax.ShapeDtypeStruct((rows, width), BF16),
    )
    if keep_transposed:
        kept = jax.ShapeDtypeStruct((nb * heads, ATT_HEAD_DIM, tm), F32)
        keep = pl.BlockSpec((heads, ATT_HEAD_DIM, tm), lambda b, t: (b, 0, 0))
    else:
        kept = jax.ShapeDtypeStruct((nb * tm, width), F32)
        keep = pl.BlockSpec((tm, width), lambda b, t: (b, 0))
    out_shape = out_shape + (kept, kept)
    return pl.pallas_call(
        functools.partial(_in_proj_kernel, width=width, nt=nt),
        grid=(nb, nt),
        in_specs=[pl.BlockSpec((tm, d), tok), _full((1, d)), _full(w_bf.shape),
                  _full((1, width)), _full((1, width)), _full((width, width))],
        out_specs=(pl.BlockSpec((tm, width), lambda b, t: (t, b)),
                   pl.BlockSpec((tm, width), tok), pl.BlockSpec((tm, width), tok),
                   pl.BlockSpec((tm, width), tok),
                   keep, keep),
        out_shape=out_shape,
        compiler_params=_cparams(("arbitrary", "arbitrary")),
        name="in_proj",
    )(x2d, gain, w_bf, qg, kg, hm)


def _ssm_kernel(u_ref, bc_ref, cre_ref, cim_ref, lre_ref, lim_ref, d_ref, s0re_ref, s0im_ref,
                y_ref, sre_ref, sim_ref, bu_ref, *, nb, tc, ns):
    @pl.when(pl.program_id(0) == 0)
    def _():
        sre_ref[...] = s0re_ref[...]
        sim_ref[...] = s0im_ref[...]

    u = u_ref[...]
    n_tiles = 2 * ns // MXU_N
    for j in range(n_tiles):
        slab = (j % (n_tiles // 2)) // 2
        bu_ref[:, j * MXU_N:(j + 1) * MXU_N] = _dot(u[:, slab * LANES:(slab + 1) * LANES], bc_ref[j])

    cw = 8 * 1024 // nb
    for cb in range(ns // cw):
        c0 = cb * cw
        lre = jnp.broadcast_to(lre_ref[:, c0:c0 + cw], (nb, cw))
        lim = jnp.broadcast_to(lim_ref[:, c0:c0 + cw], (nb, cw))

        def step(t, carry, c0=c0, lre=lre, lim=lim):
            sr, si = carry
            r0 = pl.multiple_of(t * nb, nb)
            nr = lre * sr - lim * si + bu_ref[pl.ds(r0, nb), c0:c0 + cw]
            ni = lre * si + lim * sr + bu_ref[pl.ds(r0, nb), ns + c0:ns + c0 + cw]
            bu_ref[pl.ds(r0, nb), c0:c0 + cw] = nr
            bu_ref[pl.ds(r0, nb), ns + c0:ns + c0 + cw] = ni
            return nr, ni

        sr, si = lax.fori_loop(0, tc, step, (sre_ref[:, c0:c0 + cw], sim_ref[:, c0:c0 + cw]))
        sre_ref[:, c0:c0 + cw] = sr
        sim_ref[:, c0:c0 + cw] = si

    width = u.shape[1]
    kt = ns * MXU_N // width
    for n in range(width // MXU_N):
        s_re = bu_ref[:, n * kt:(n + 1) * kt].astype(BF16)
        s_im = bu_ref[:, ns + n * kt:ns + (n + 1) * kt].astype(BF16)
        y = _dot(s_re, cre_ref[n]) + _dot(s_im, cim_ref[n])
        cols = slice(n * MXU_N, (n + 1) * MXU_N)
        y_ref[:, cols] = y + d_ref[:, cols] * u[:, cols].astype(F32)


def _ssm(u_rows, bc, cre, cim, lre, lim, dskip, s0re, s0im, *, nb, tc):
    rows, width = u_rows.shape
    ns = lre.shape[1]
    r = tc * nb
    return pl.pallas_call(
        functools.partial(_ssm_kernel, nb=nb, tc=tc, ns=ns),
        grid=(rows // r,),
        in_specs=[pl.BlockSpec((r, width), lambda i: (i, 0)), _full(bc.shape), _full(cre.shape),
                  _full(cim.shape), _full((1, ns)), _full((1, ns)), _full((1, width)),
                  _full((nb, ns)), _full((nb, ns))],
        out_specs=(pl.BlockSpec((r, width), lambda i: (i, 0)), _full((nb, ns)), _full((nb, ns))),
        out_shape=(jax.ShapeDtypeStruct((rows, width), F32),
                   jax.ShapeDtypeStruct((nb, ns), F32), jax.ShapeDtypeStruct((nb, ns), F32)),
        scratch_shapes=[pltpu.VMEM((r, 2 * ns), F32)],
        compiler_params=_cparams(("arbitrary",)),
        name="ssm",
    )(u_rows, bc, cre, cim, lre, lim, dskip, s0re, s0im)


def _half_select(shape):
    lane = lax.broadcasted_iota(jnp.int32, shape, 1)
    return lane < ATT_HEAD_DIM


def _head_masks(first):
    m0 = jnp.where(first, 1.0, 0.0).astype(BF16)
    return m0, (1.0 - m0.astype(F32)).astype(BF16)


def _band_prompt_kernel(q_ref, k_ref, v_ref, bias_ref, o_ref, kp_ref, vp_ref, s_scr, p_scr, l_scr, *, t_len):
    width = q_ref.shape[1]
    n_hp = width // LANES
    kp_ref[0:PAD_ROWS, :] = jnp.zeros((PAD_ROWS, width), BF16)
    vp_ref[0:PAD_ROWS, :] = jnp.zeros((PAD_ROWS, width), BF16)
    kp_ref[PAD_ROWS:, :] = k_ref[...]
    vp_ref[PAD_ROWS:, :] = v_ref[...]
    first = _half_select((PAIR_ROWS, LANES))
    head_mask = _head_masks(first)
    col = lax.broadcasted_iota(jnp.int32, (SOFTMAX_ROWS, PAIR_BAND), 1)

    def pair(pi, carry, *, masked):
        r0 = pl.multiple_of(pi * PAIR_ROWS, PAIR_ROWS)
        for hp in range(n_hp):
            lanes = slice(hp * LANES, (hp + 1) * LANES)
            qp = q_ref[pl.ds(r0, PAIR_ROWS), lanes]
            qs = jnp.concatenate([qp * head_mask[0], qp * head_mask[1]], axis=0)
            s_scr[2 * hp * PAIR_ROWS:2 * (hp + 1) * PAIR_ROWS, :] = _dot_nt(
                qs, kp_ref[pl.ds(r0, PAIR_BAND), lanes])
        for h in range(2 * n_hp):
            for rb in range(0, PAIR_ROWS, SOFTMAX_ROWS):
                rows = slice(h * PAIR_ROWS + rb, h * PAIR_ROWS + rb + SOFTMAX_ROWS)
                s = s_scr[rows, :] + bias_ref[h, rb:rb + SOFTMAX_ROWS, :]
                if masked:
                    s = jnp.where(col >= PAD_ROWS - r0, s, NEG_INF)
                p = jnp.exp(s - jnp.max(s, axis=-1, keepdims=True))
                p_scr[rows, :] = p.astype(BF16)
                l_scr[rows, :] = jnp.broadcast_to(1.0 / jnp.sum(p, axis=-1, keepdims=True),
                                                  (SOFTMAX_ROWS, LANES))
        for hp in range(n_hp):
            lanes = slice(hp * LANES, (hp + 1) * LANES)
            rows = slice(2 * hp * PAIR_ROWS, 2 * (hp + 1) * PAIR_ROWS)
            o2 = _dot(p_scr[rows, :], vp_ref[pl.ds(r0, PAIR_BAND), lanes]) * l_scr[rows, :]
            o_ref[pl.ds(r0, PAIR_ROWS), lanes] = jnp.where(
                first, o2[:PAIR_ROWS], o2[PAIR_ROWS:]).astype(BF16)
        return carry

    n_masked = PAD_ROWS // PAIR_ROWS
    lax.fori_loop(0, n_masked, functools.partial(pair, masked=True), 0)
    lax.fori_loop(n_masked, t_len // PAIR_ROWS, functools.partial(pair, masked=False), 0)


def _band_prompt(q, k, v, bias, *, nb, t_len):
    width = q.shape[1]
    heads = width // ATT_HEAD_DIM
    blk = pl.BlockSpec((t_len, width), lambda b: (b, 0))
    return pl.pallas_call(
        functools.partial(_band_prompt_kernel, t_len=t_len),
        grid=(nb,),
        in_specs=[blk, blk, blk, _full(bias.shape)],
        out_specs=blk,
        out_shape=jax.ShapeDtypeStruct(q.shape, BF16),
        scratch_shapes=[pltpu.VMEM((t_len + PAD_ROWS, width), BF16),
                        pltpu.VMEM((t_len + PAD_ROWS, width), BF16),
                        pltpu.VMEM((heads * PAIR_ROWS, PAIR_BAND), F32),
                        pltpu.VMEM((heads * PAIR_ROWS, PAIR_BAND), BF16),
                        pltpu.VMEM((heads * PAIR_ROWS, LANES), F32)],
        compiler_params=_cparams(("arbitrary",)),
        name="band_prompt",
    )(q, k, v, bias)


def _band_sample_kernel(q_ref, k_ref, v_ref, ck_ref, cv_ref, bc_ref, bn_ref, o_ref):
    for h in range(ck_ref.shape[0]):
        cols = slice(h * ATT_HEAD_DIM, (h + 1) * ATT_HEAD_DIM)
        qh = q_ref[:, cols]
        sc = _dot(qh, ck_ref[h].astype(BF16)) + bc_ref[h]
        sn = _dot_nt(qh, k_ref[:, cols]) + bn_ref[h]
        m = jnp.maximum(jnp.max(sc, axis=-1, keepdims=True), jnp.max(sn, axis=-1, keepdims=True))
        pc = jnp.exp(sc - m)
        pn = jnp.exp(sn - m)
        l = jnp.sum(pc, axis=-1, keepdims=True) + jnp.sum(pn, axis=-1, keepdims=True)
        o = _dot_nt(pc.astype(BF16), cv_ref[h].astype(BF16)) + _dot(pn.astype(BF16), v_ref[:, cols])
        o_ref[:, cols] = (o / l).astype(BF16)


def _band_sample(q_tm, k_tm, v_tm, cache_kt, cache_vt, bias_c, bias_n, *, nb, s_len):
    width = q_tm.shape[1] // nb
    heads = cache_kt.shape[0] // nb
    col = pl.BlockSpec((s_len, width), lambda b: (0, b))
    cache = pl.BlockSpec((heads,) + cache_kt.shape[1:], lambda b: (b, 0, 0))
    return pl.pallas_call(
        _band_sample_kernel,
        grid=(nb,),
        in_specs=[col, col, col, cache, cache, _full(bias_c.shape), _full(bias_n.shape)],
        out_specs=col,
        out_shape=jax.ShapeDtypeStruct(q_tm.shape, BF16),
        compiler_params=_cparams(("arbitrary",)),
        name="band_sample",
    )(q_tm, k_tm, v_tm, cache_kt, cache_vt, bias_c, bias_n)


def _mem_kv_kernel(m_ref, g_ref, wk_ref, wv_ref, kg_ref, k_ref, v_ref, kb_ref, vb_ref):
    m = _rms(m_ref[...], g_ref[...]).astype(BF16)
    k = _dot(m, wk_ref[...])
    v = _dot(m, wv_ref[...])
    hd = kg_ref.shape[1]
    for h in range(k.shape[1] // hd):
        cols = slice(h * hd, (h + 1) * hd)
        kh = _rms(k[:, cols], kg_ref[...])
        k_ref[:, h, :] = kh
        v_ref[:, h, :] = v[:, cols]
        kb_ref[:, cols] = kh.astype(BF16)
    vb_ref[...] = v.astype(BF16)


def _mem_kv(mem2d, gain, wk, wv, kgain, *, tm):
    rows, d = mem2d.shape
    hd = kgain.shape[1]
    blk = pl.BlockSpec((tm, d), lambda i: (i, 0))
    hblk = pl.BlockSpec((tm, d // hd, hd), lambda i: (i, 0, 0))
    return pl.pallas_call(
        _mem_kv_kernel,
        grid=(rows // tm,),
        in_specs=[blk, _full((1, d)), _full(wk.shape), _full(wv.shape), _full(kgain.shape)],
        out_specs=(hblk, hblk, blk, blk),
        out_shape=(jax.ShapeDtypeStruct((rows, d // hd, hd), F32),
                   jax.ShapeDtypeStruct((rows, d // hd, hd), F32),
                   jax.ShapeDtypeStruct((rows, d), BF16), jax.ShapeDtypeStruct((rows, d), BF16)),
        compiler_params=_cparams(("arbitrary",)),
        name="mem_kv",
    )(mem2d, gain, wk, wv, kgain)


def _route(logits):
    lane = lax.broadcasted_iota(jnp.int32, logits.shape, 1).astype(F32)
    big = float(ROUTER_LANES)
    is_g = lane < N_GROUPS
    lg = jnp.where(is_g, logits, NEG_INF)
    gmax = jnp.max(lg, axis=-1, keepdims=True)
    p1 = 1.0 / jnp.sum(jnp.where(is_g, jnp.exp(lg - gmax), 0.0), axis=-1, keepdims=True)
    g_idx = jnp.min(jnp.where(lg == gmax, lane, big), axis=-1, keepdims=True)
    lo = ROUTER_E0 + g_idx * EXPERTS_PER_GROUP
    le = jnp.where((lane >= lo) & (lane < lo + EXPERTS_PER_GROUP), logits, NEG_INF)
    v1 = jnp.max(le, axis=-1, keepdims=True)
    i1 = jnp.min(jnp.where(le == v1, lane, big), axis=-1, keepdims=True)
    le2 = jnp.where(lane == i1, NEG_INF, le)
    v2 = jnp.max(le2, axis=-1, keepdims=True)
    i2 = jnp.min(jnp.where(le2 == v2, lane, big), axis=-1, keepdims=True)
    e2 = jnp.exp(v2 - v1)
    w1 = p1 / (1.0 + e2)
    w2 = p1 * e2 / (1.0 + e2)
    gates = jnp.where(lane == i1, w1, 0.0) + jnp.where(lane == i2, w2, 0.0)
    a = jnp.minimum(i1, i2) - lo
    b = jnp.maximum(i1, i2) - lo
    pair = a * (2 * EXPERTS_PER_GROUP - 1 - a) * 0.5 + (b - a - 1.0)
    cls = g_idx * PAIRS_PER_GROUP + pair
    w_a = jnp.where(i1 < i2, w1, w2)
    w_b = jnp.where(i1 < i2, w2, w1)
    return (gates + jnp.where(lane == ROUTE_CLS, cls, 0.0) + jnp.where(lane == ROUTE_WA, w_a, 0.0)
            + jnp.where(lane == ROUTE_WB, w_b, 0.0))


def _mix_mem_kernel(x_ref, ys_ref, ya_ref, mk_ref, mv_ref, wglu_ref, ons_ref, ona_ref, wout_ref,
                    nmem_ref, wq_ref, qg_ref, wo_ref, nffn_ref, wrh_ref, wrl_ref, rb_ref,
                    *outs, routed):
    if routed:
        x2_ref, o_scr = outs
    else:
        x2_ref, xn_ref, gate_ref, o_scr = outs
    width = ys_ref.shape[1]
    g = _dot(_gelu_tanh(ys_ref[...]).astype(BF16), wglu_ref[...])
    y_s = g[:, :width] * _sigmoid(g[:, width:])
    cat_s = _rms(y_s, ons_ref[...]).astype(BF16)
    cat_a = _rms(ya_ref[...].astype(F32), ona_ref[...]).astype(BF16)
    x1 = x_ref[...] + _dot(cat_s, wout_ref[0:width, :]) + _dot(cat_a, wout_ref[width:, :])

    q = _dot(_rms(x1, nmem_ref[...]).astype(BF16), wq_ref[...])
    hd = qg_ref.shape[1]
    scale = hd ** -0.5
    for h in range(q.shape[1] // hd):
        cols = slice(h * hd, (h + 1) * hd)
        qh = _rms(q[:, cols], qg_ref[...]).astype(BF16)
        head = (slice(None), h, slice(None)) if len(mk_ref.shape) == 3 else (slice(None), cols)
        s = _dot_nt(qh, mk_ref[head].astype(BF16)) * scale
        p = jnp.exp(s - jnp.max(s, axis=-1, keepdims=True))
        l = jnp.sum(p, axis=-1, keepdims=True)
        o_scr[:, cols] = (_dot(p.astype(BF16), mv_ref[head].astype(BF16)) / l).astype(BF16)
    acc = x1 + _dot(o_scr[...], wo_ref[...])
    d = acc.shape[1]
    x2_ref[:, :d] = acc

    xn = _rms(acc, nffn_ref[...])
    xh, xl = _split_bf16(xn)
    logits = _dot(xh, wrh_ref[...]) + _dot(xl, wrh_ref[...]) + _dot(xh, wrl_ref[...]) + rb_ref[...]
    if routed:
        x2_ref[:, d:] = _route(logits)
    else:
        xn_ref[...] = xh
        gate_ref[...] = _route(logits)


def _mix_mem(x, ys, ya, mk, mv, weights, *, grid, tm, row_map, ssm_map, mem_map, routed):
    d = weights["w_out"].shape[1]
    width = weights["w_glu"].shape[0]
    names = ("w_glu", "out_norm_ssm", "out_norm_att", "w_out", "norm_mem", "w_mem_q", "mem_q_gain",
             "w_mem_o", "norm_ffn", "w_router_hi", "w_router_lo", "router_bias")
    ws = [weights[n] for n in names]
    mem_rows = weights["n_mem"]
    xspec = pl.BlockSpec((tm, d), row_map)
    hspec = pl.BlockSpec((tm, width), row_map)
    sspec = pl.BlockSpec((tm, width), ssm_map)
    mspec = pl.BlockSpec((mem_rows,) + mk.shape[1:], mem_map)
    n_col = x.shape[1] // d
    if routed:
        out_specs = pl.BlockSpec((tm, d + ROUTER_LANES), row_map)
        out_shape = jax.ShapeDtypeStruct((x.shape[0], n_col * (d + ROUTER_LANES)), F32)
    else:
        out_specs = (xspec, xspec, pl.BlockSpec((tm, ROUTER_LANES), row_map))
        out_shape = (jax.ShapeDtypeStruct(x.shape, F32), jax.ShapeDtypeStruct(x.shape, BF16),
                     jax.ShapeDtypeStruct((x.shape[0], n_col * ROUTER_LANES), F32))
    return pl.pallas_call(
        functools.partial(_mix_mem_kernel, routed=routed),
        grid=grid,
        in_specs=[xspec, sspec, hspec, mspec, mspec] + [_full(w.shape) for w in ws],
        out_specs=out_specs,
        out_shape=out_shape,
        scratch_shapes=[pltpu.VMEM((tm, d), BF16)],
        compiler_params=_cparams(("arbitrary",) * len(grid)),
        name="mix_mem",
    )(x, ys, ya, mk, mv, *ws)


def _moe_kernel(xn_ref, x2_ref, gate_ref, wg_ref, wu_ref, wd_ref, o_ref):
    e = pl.program_id(1)

    @pl.when(e == 0)
    def _():
        o_ref[...] = x2_ref[...]

    gates = gate_ref[...]
    lane = lax.broadcasted_iota(jnp.int32, gates.shape, 1)
    ge = jnp.sum(jnp.where(lane == e + ROUTER_E0, gates, 0.0), axis=-1, keepdims=True)
    xn = xn_ref[...]
    a = _dot(xn, wg_ref[...])
    h = a * _sigmoid(a) * _dot(xn, wu_ref[...])
    o_ref[...] += ge * _dot(h.astype(BF16), wd_ref[...])


def _moe(xn, x2, gates, wg, wu, wd, *, tm):
    rows, d = xn.shape
    n_exp, _, dff = wg.shape
    row = lambda i, e: (i, 0)
    return pl.pallas_call(
        _moe_kernel,
        grid=(rows // tm, n_exp),
        in_specs=[pl.BlockSpec((tm, d), row), pl.BlockSpec((tm, d), row),
                  pl.BlockSpec((tm, ROUTER_LANES), row),
                  pl.BlockSpec((None, d, dff), lambda i, e: (e, 0, 0)),
                  pl.BlockSpec((None, d, dff), lambda i, e: (e, 0, 0)),
                  pl.BlockSpec((None, dff, d), lambda i, e: (e, 0, 0))],
        out_specs=pl.BlockSpec((tm, d), row),
        out_shape=jax.ShapeDtypeStruct((rows, d), F32),
        compiler_params=_cparams(("arbitrary", "arbitrary")),
        name="moe",
    )(xn, x2, gates, wg, wu, wd)


def _moe_routed_kernel(tile_ref, ea_ref, eb_ref, lo_ref, hi_ref, valid_ref, tok_ref,
                       x_hbm, nffn_ref, wga_ref, wua_ref, wda_ref, wgb_ref, wub_ref, wdb_ref,
                       y_hbm, *scratch, tm, n_tiles):
    nb = MOE_BUFS
    xbuf, obuf = scratch[:nb], scratch[nb:2 * nb]
    acc_ref, gsem, ssem = scratch[2 * nb:]
    w = pl.program_id(0)
    t = tile_ref[w]
    lo = lo_ref[w]
    hi = hi_ref[w]
    valid = valid_ref[w] == 1
    d = y_hbm.shape[1]

    def row_copy_in(tile, s, r):
        tok = tok_ref[tile * tm + r]
        return pltpu.make_async_copy(x_hbm.at[pl.ds(tok, 1)], xbuf[s].at[pl.ds(r, 1)], gsem.at[s])

    def row_copy_out(tile, s, r):
        tok = tok_ref[tile * tm + r]
        return pltpu.make_async_copy(obuf[s].at[pl.ds(r, 1)], y_hbm.at[pl.ds(tok, 1)], ssem.at[s])

    def start_gather(tile, s):
        for r in range(tm):
            row_copy_in(tile, s, r).start()

    def start_scatter(tile, s):
        for r in range(tm):
            row_copy_out(tile, s, r).start()

    def wait_gather(s):
        pltpu.make_async_copy(x_hbm.at[pl.ds(0, tm)], xbuf[s], gsem.at[s]).wait()

    def wait_scatter(s):
        pltpu.make_async_copy(obuf[s], y_hbm.at[pl.ds(0, tm)], ssem.at[s]).wait()

    def compute(s, first):
        xe = xbuf[s][...]
        xn = _rms(xe[:, :d], nffn_ref[...]).astype(BF16)
        row = lax.broadcasted_iota(jnp.int32, (tm, 1), 0)
        inseg = jnp.logical_and(row >= lo, row < hi)
        w_a = jnp.where(inseg, xe[:, d + ROUTE_WA:d + ROUTE_WA + 1], 0.0)
        w_b = jnp.where(inseg, xe[:, d + ROUTE_WB:d + ROUTE_WB + 1], 0.0)
        a = _dot(xn, wga_ref[...])
        h_a = (a * _sigmoid(a) * _dot(xn, wua_ref[...])).astype(BF16)
        b = _dot(xn, wgb_ref[...])
        h_b = (b * _sigmoid(b) * _dot(xn, wub_ref[...])).astype(BF16)
        upd = w_a * _dot(h_a, wda_ref[...]) + w_b * _dot(h_b, wdb_ref[...])
        if first:
            acc_ref[...] = upd
        else:
            acc_ref[...] += upd

    def when(*conds):
        c = conds[0]
        for extra in conds[1:]:
            c = jnp.logical_and(c, extra)
        return pl.when(c)

    ahead = nb - 1

    @pl.when(w == 0)
    def _():
        for k in range(ahead):
            start_gather(k, k)

    is_first = jnp.logical_and(valid, lo == 0)
    is_last = jnp.logical_and(valid, hi == tm)
    last_tile = n_tiles - 1
    for s in range(nb):
        mine = t % nb == s
        nxt, prv = (s + ahead) % nb, (s - 1) % nb

        if s == 0:
            @when(is_first, t == 0)
            def _(nxt=nxt):
                wait_gather(0)
                start_gather(ahead, nxt)
                compute(0, True)

        @when(is_first, mine, t > 0, t + ahead <= last_tile)
        def _(s=s, nxt=nxt, prv=prv):
            wait_gather(s)
            start_gather(t + ahead, nxt)
            start_scatter(t - 1, prv)
            compute(s, True)

        @when(is_first, mine, t + ahead > last_tile)
        def _(s=s, prv=prv):
            wait_gather(s)
            start_scatter(t - 1, prv)
            compute(s, True)

        @when(valid, mine, lo > 0)
        def _(s=s):
            compute(s, False)

        @when(is_last, mine)
        def _(s=s):
            @pl.when(t >= nb)
            def _():
                wait_scatter(s)

            obuf[s][...] = xbuf[s][:, :d] + acc_ref[...]

            if s == last_tile % nb:
                @pl.when(t == last_tile)
                def _():
                    start_scatter(t, s)
                    for k in range(nb):
                        wait_scatter(k)


def _moe_routed(x2ext, plan, nffn, wg, wu, wd, *, tm):
    rows, de = x2ext.shape
    d = de - ROUTER_LANES
    n_exp, _, dff = wg.shape
    n_tiles = rows // tm
    assert rows % tm == 0 and n_tiles > 2 * MOE_BUFS
    n_items = plan[0].shape[0]
    ea = lambda w, tile, ea_, eb_, *_: (ea_[w], 0, 0)
    eb = lambda w, tile, ea_, eb_, *_: (eb_[w], 0, 0)
    grid_spec = pltpu.PrefetchScalarGridSpec(
        num_scalar_prefetch=len(plan),
        grid=(n_items,),
        in_specs=[pl.BlockSpec(memory_space=pl.ANY),
                  pl.BlockSpec((1, d), lambda w, *_: (0, 0)),
                  pl.BlockSpec((None, d, dff), ea), pl.BlockSpec((None, d, dff), ea),
                  pl.BlockSpec((None, dff, d), ea),
                  pl.BlockSpec((None, d, dff), eb), pl.BlockSpec((None, d, dff), eb),
                  pl.BlockSpec((None, dff, d), eb)],
        out_specs=pl.BlockSpec(memory_space=pl.ANY),
        scratch_shapes=([pltpu.VMEM((tm, de), F32)] * MOE_BUFS + [pltpu.VMEM((tm, d), F32)] * MOE_BUFS
                        + [pltpu.VMEM((tm, d), F32),
                           pltpu.SemaphoreType.DMA((MOE_BUFS,)), pltpu.SemaphoreType.DMA((MOE_BUFS,))]),
    )
    return pl.pallas_call(
        functools.partial(_moe_routed_kernel, tm=tm, n_tiles=n_tiles),
        grid_spec=grid_spec,
        out_shape=jax.ShapeDtypeStruct((rows, d), F32),
        compiler_params=_cparams(("arbitrary",)),
        name="moe_routed",
    )(*plan, x2ext, nffn, wg, wu, wd, wg, wu, wd)


def _route_plan(cls, *, tm):
    n = cls.shape[0]
    n_tiles = n // tm
    order = jnp.argsort(cls).astype(jnp.int32)
    classes = jnp.arange(N_CLASSES, dtype=jnp.int32)
    class_end = jnp.sum((cls[:, None] <= classes[None, :]).astype(jnp.int32), axis=0)
    class_start = jnp.concatenate([jnp.zeros((1,), jnp.int32), class_end[:-1]])
    bounds = jnp.concatenate([jnp.arange(n_tiles, dtype=jnp.int32) * tm,
                              jnp.where(class_end > class_start, class_start, n)])
    n_items = bounds.shape[0]
    idx = jnp.arange(n_items, dtype=jnp.int32)
    before = jnp.logical_or(bounds[None, :] < bounds[:, None],
                            jnp.logical_and(bounds[None, :] == bounds[:, None], idx[None, :] < idx[:, None]))
    rank = jnp.sum(before.astype(jnp.int32), axis=1)
    start = jnp.sum(jnp.where(rank[:, None] == idx[None, :], bounds[:, None], 0), axis=0)
    stop = jnp.concatenate([start[1:], jnp.full((1,), n, jnp.int32)])
    valid = stop > start
    tile = jnp.minimum(start // tm, n_tiles - 1)
    lo = start - tile * tm
    hi = stop - tile * tm
    c = jnp.minimum(jnp.sum((class_end[None, :] <= start[:, None]).astype(jnp.int32), axis=1),
                    N_CLASSES - 1)
    g = c // PAIRS_PER_GROUP
    pair = c % PAIRS_PER_GROUP
    a = jnp.zeros_like(pair)
    for k in range(1, EXPERTS_PER_GROUP - 1):
        a = a + (pair >= k * (2 * EXPERTS_PER_GROUP - 1 - k) // 2).astype(jnp.int32)
    b = pair - a * (2 * EXPERTS_PER_GROUP - 1 - a) // 2 + a + 1
    e_a = g * EXPERTS_PER_GROUP + a
    e_b = g * EXPERTS_PER_GROUP + b
    i32 = lambda v: v.astype(jnp.int32)
    return (i32(tile), i32(e_a), i32(e_b), i32(lo), i32(hi), i32(valid), order)


def _ssm_params(lam_re, lam_im, log_step, b_re, b_im, c_re, c_im):
    n_g, n_p = lam_re.shape
    step = jnp.exp(log_step.astype(F32))[:, None]
    mag = jnp.exp(lam_re * step)
    lb_re = mag * jnp.cos(lam_im * step)
    lb_im = mag * jnp.sin(lam_im * step)
    den = lam_re * lam_re + lam_im * lam_im
    f_re = ((lb_re - 1.0) * lam_re + lb_im * lam_im) / den
    f_im = (lb_im * lam_re - (lb_re - 1.0) * lam_im) / den
    bb_re = f_re[..., None] * b_re - f_im[..., None] * b_im
    bb_im = f_re[..., None] * b_im + f_im[..., None] * b_re
    eye = jnp.eye(n_g, dtype=F32)
    ns = n_g * n_p
    width = n_g * SSM_GROUP
    b_full = jnp.concatenate(
        [jnp.einsum("hg,gpc->hcgp", eye, bb_re).reshape(width, ns),
         jnp.einsum("hg,gpc->hcgp", eye, bb_im).reshape(width, ns)], axis=1)
    n_tiles = 2 * ns // MXU_N
    bc = jnp.stack([
        b_full[((j % (n_tiles // 2)) // 2) * LANES:((j % (n_tiles // 2)) // 2 + 1) * LANES,
               j * MXU_N:(j + 1) * MXU_N] for j in range(n_tiles)]).astype(BF16)
    c_full_re = jnp.einsum("gh,gcp->gphc", eye, c_re).reshape(ns, width)
    c_full_im = -jnp.einsum("gh,gcp->gphc", eye, c_im).reshape(ns, width)
    kt = ns * MXU_N // width
    tiles = range(width // MXU_N)
    cre = jnp.stack([c_full_re[n * kt:(n + 1) * kt, n * MXU_N:(n + 1) * MXU_N] for n in tiles]).astype(BF16)
    cim = jnp.stack([c_full_im[n * kt:(n + 1) * kt, n * MXU_N:(n + 1) * MXU_N] for n in tiles]).astype(BF16)
    return bc, cre, cim, lb_re.reshape(1, ns), lb_im.reshape(1, ns)


def _rel_bias(rel_bias, q0, n_q, n_k):
    n_r = n_q + n_k - 1
    dist = q0 + n_q - 1 - jnp.arange(n_r)
    r = rel_bias.astype(F32)[:, jnp.clip(dist, -REL_CLIP, REL_CLIP) + REL_CLIP]
    r = jnp.pad(r, ((0, 0), (0, 1)))
    rows = jnp.tile(r, (1, n_q))[:, :n_q * n_r].reshape(-1, n_q, n_r)
    return rows[:, :, n_q - 1:n_q - 1 + n_k]


def _layer(xp, xs, mem_p, ck, cv, s_re, s_im, cmk, cmv, w, dims):
    batch, seq, dec_batch, dec_seq, d = dims
    width = w["w_in"].shape[1] // 4
    ns = w["lam_re"].shape[1]
    n_mem = w["n_mem"]
    tm_p = 512
    nt_p = seq // tm_p
    n_dec = dec_batch * dec_seq

    u, q, k, v, kk, vk = _in_proj(xp, w["norm_mix"], w["w_in"], w["qg"], w["kg"], w["head_mean"],
                                  nb=batch, nt=nt_p, tm=tm_p, keep_transposed=True)
    zeros = jnp.zeros((batch, ns), F32)
    y_ssm, pre, pim = _ssm(u.reshape(seq * batch, width), w["bc"], w["cre"], w["cim"], w["lam_re"],
                           w["lam_im"], w["ssm_d"], zeros, zeros, nb=batch, tc=32)
    y_att = _band_prompt(q, k, v, w["bias_p"], nb=batch, t_len=seq)
    mk, mv, mkb, mvb = _mem_kv(mem_p, w["mem_in_norm"], w["w_mem_k"], w["w_mem_v"], w["mem_k_gain"], tm=512)
    tm_d = 256
    nt_d = seq // tm_d
    x2ext = _mix_mem(xp, y_ssm.reshape(seq, batch * width), y_att, mkb, mvb, w,
                     grid=(batch, nt_d), tm=tm_d,
                     row_map=lambda b, t: (b * nt_d + t, 0), ssm_map=lambda b, t: (t, b),
                     mem_map=lambda b, t: (b, 0), routed=True)
    plan = _route_plan(x2ext[:, d + ROUTE_CLS].astype(jnp.int32), tm=MOE_TILE)
    yp = _moe_routed(x2ext, plan, w["norm_ffn"], w["exp_w_gate"], w["exp_w_up"], w["exp_w_down"],
                     tm=MOE_TILE)

    us, qs, ks, vs, kks, vks = _in_proj(xs, w["norm_mix"], w["w_in"], w["qg"], w["kg"], w["head_mean"],
                                        nb=1, nt=1, tm=n_dec, keep_transposed=False)
    ys_ssm, sre, sim = _ssm(us, w["bc"], w["cre"], w["cim"], w["lam_re"], w["lam_im"], w["ssm_d"],
                            s_re, s_im, nb=dec_batch, tc=dec_seq)
    tmv = lambda a: a.reshape(dec_seq, dec_batch * a.shape[1])
    ys_att = _band_sample(tmv(qs), tmv(ks), tmv(vs), ck, cv, w["bias_sc"], w["bias_sn"],
                          nb=dec_batch, s_len=dec_seq)
    x2s, xns, gates_s = _mix_mem(tmv(xs), tmv(ys_ssm), ys_att, cmk, cmv, w,
                                 grid=(dec_batch,), tm=dec_seq,
                                 row_map=lambda b: (0, b), ssm_map=lambda b: (0, b),
                                 mem_map=lambda b: (b, 0), routed=False)
    ys = _moe(xns.reshape(n_dec, d), x2s.reshape(n_dec, d), gates_s.reshape(n_dec, ROUTER_LANES),
              w["exp_w_gate"], w["exp_w_up"], w["exp_w_down"], tm=n_dec)
    return yp, ys, (kk, vk, pre, pim, mk, mv), (kks, vks, sre, sim)


def kernel(x_prompt, x_sample, mem_prompt, cache_attn_k, cache_attn_v, state_ssm_re, state_ssm_im, cache_mem_k, cache_mem_v, norm_mix, w_in, ssm_lambda_re, ssm_lambda_im, ssm_log_step, ssm_b_re, ssm_b_im, ssm_c_re, ssm_c_im, ssm_d, w_glu, att_q_gain, att_k_gain, att_rel_bias, out_norm_ssm, out_norm_att, w_out, norm_mem, mem_in_norm, w_mem_q, w_mem_k, w_mem_v, w_mem_o, mem_q_gain, mem_k_gain, norm_ffn, router_g_w, router_g_b, router_e_w, router_e_b, exp_w_gate, exp_w_up, exp_w_down):
    depth = norm_mix.shape[0]
    batch, seq, d = x_prompt.shape
    dec_batch, dec_seq, _ = x_sample.shape
    n_mem = mem_prompt.shape[1]
    att_rows = cache_attn_k.shape[2]
    n_g, n_p = ssm_lambda_re.shape[1:]
    width = n_g * SSM_GROUP
    heads = width // ATT_HEAD_DIM
    ns = n_g * n_p
    assert seq % 512 == 0 and att_rows == PAD_ROWS and seq >= PAD_ROWS
    assert (dec_batch * dec_seq) % 8 == 0 and dec_seq % 16 == 0

    xp = x_prompt.reshape(batch * seq, d)
    xs = x_sample.transpose(1, 0, 2).reshape(dec_seq * dec_batch, d)
    mem_p = mem_prompt.reshape(batch * n_mem, d)
    row = lambda a: a.reshape(1, -1).astype(F32)
    head_mean = jnp.kron(jnp.eye(heads, dtype=F32),
                         jnp.full((ATT_HEAD_DIM, ATT_HEAD_DIM), 1.0 / ATT_HEAD_DIM, F32)).astype(BF16)
    pr = jnp.arange(PAIR_ROWS)[:, None]
    pc = jnp.arange(PAIR_BAND)[None, :]
    pair_ok = jnp.where(pr < CHUNK, pc < BAND, pc >= CHUNK)
    att_scale = ATT_HEAD_DIM ** -0.5

    p_out, s_out = [], []
    for l in range(depth):
        bc, cre, cim, lb_re, lb_im = _ssm_params(ssm_lambda_re[l], ssm_lambda_im[l], ssm_log_step[l],
                                                 ssm_b_re[l], ssm_b_im[l], ssm_c_re[l], ssm_c_im[l])
        w_router = jnp.concatenate(
            [router_g_w[l], router_e_w[l].transpose(1, 0, 2).reshape(d, N_GROUPS * EXPERTS_PER_GROUP),
             jnp.zeros((d, ROUTER_LANES - N_GROUPS * (1 + EXPERTS_PER_GROUP)), F32)], axis=1)
        wr_hi, wr_lo = _split_bf16(w_router)
        r_bias = jnp.concatenate(
            [router_g_b[l], router_e_b[l].reshape(-1),
             jnp.zeros((ROUTER_LANES - N_GROUPS * (1 + EXPERTS_PER_GROUP),), F32)]).reshape(1, ROUTER_LANES)
        bias_s = _rel_bias(att_rel_bias[l], att_rows, dec_seq, att_rows + dec_seq)
        bias_p = jnp.where(pair_ok, _rel_bias(att_rel_bias[l], PAD_ROWS, PAIR_ROWS, PAIR_BAND), NEG_INF)
        w = dict(
            n_mem=n_mem,
            norm_mix=row(norm_mix[l]), w_in=w_in[l].astype(BF16),
            qg=row(jnp.tile(att_q_gain[l], heads) * att_scale), kg=row(jnp.tile(att_k_gain[l], heads)),
            head_mean=head_mean, bc=bc, cre=cre, cim=cim, lam_re=lb_re, lam_im=lb_im,
            ssm_d=row(ssm_d[l]), bias_p=bias_p,
            bias_sc=bias_s[:, :, :att_rows], bias_sn=bias_s[:, :, att_rows:],
            mem_in_norm=row(mem_in_norm[l]), w_mem_k=w_mem_k[l].astype(BF16),
            w_mem_v=w_mem_v[l].astype(BF16), mem_k_gain=row(mem_k_gain[l]),
            w_glu=w_glu[l].astype(BF16), out_norm_ssm=row(out_norm_ssm[l]),
            out_norm_att=row(out_norm_att[l]), w_out=w_out[l].astype(BF16), norm_mem=row(norm_mem[l]),
            w_mem_q=w_mem_q[l].astype(BF16), mem_q_gain=row(mem_q_gain[l]),
            w_mem_o=w_mem_o[l].astype(BF16), norm_ffn=row(norm_ffn[l]),
            w_router_hi=wr_hi, w_router_lo=wr_lo, router_bias=r_bias,
            exp_w_gate=exp_w_gate[l].astype(BF16), exp_w_up=exp_w_up[l].astype(BF16),
            exp_w_down=exp_w_down[l].astype(BF16),
        )
        xp, xs, p_new, s_new = _layer(
            xp, xs, mem_p,
            cache_attn_k[l].transpose(0, 2, 3, 1).reshape(dec_batch * heads, ATT_HEAD_DIM, att_rows),
            cache_attn_v[l].transpose(0, 2, 3, 1).reshape(dec_batch * heads, ATT_HEAD_DIM, att_rows),
            state_ssm_re[l].reshape(dec_batch, ns), state_ssm_im[l].reshape(dec_batch, ns),
            cache_mem_k[l].reshape(dec_batch * n_mem, d), cache_mem_v[l].reshape(dec_batch * n_mem, d),
            w, (batch, seq, dec_batch, dec_seq, d))
        p_out.append(p_new)
        s_out.append(s_new)

    sdt = state_ssm_re.dtype
    keep = min(PAD_ROWS, seq)
    kv_p = lambda a: a.reshape(batch, heads, ATT_HEAD_DIM, keep).transpose(0, 3, 1, 2)
    kv_s = lambda a: a.reshape(dec_seq, dec_batch, heads, ATT_HEAD_DIM).transpose(1, 0, 2, 3)
    st = lambda a: a.reshape(a.shape[0], n_g, n_p).astype(sdt)
    mkv = lambda a: a.reshape(batch, n_mem, MEM_HEADS, d // MEM_HEADS)
    stack = lambda f, outs, i: jnp.stack([f(o[i]) for o in outs])
    yp = xp.reshape(batch, seq, d)
    ys = xs.reshape(dec_seq, dec_batch, d).transpose(1, 0, 2)
    return (yp, ys,
            stack(kv_p, p_out, 0), stack(kv_p, p_out, 1), stack(st, p_out, 2), stack(st, p_out, 3),
            stack(mkv, p_out, 4), stack(mkv, p_out, 5),
            stack(kv_s, s_out, 0), stack(kv_s, s_out, 1), stack(st, s_out, 2), stack(st, s_out, 3))
```

```python
import functools
import math

import jax
import jax.numpy as jnp
from jax import lax
from jax.experimental import pallas as pl
from jax.experimental.pallas import tpu as pltpu

F32 = jnp.float32
BF16 = jnp.bfloat16

CHUNK = 64
N_PREV_CHUNKS = 8
BAND = (N_PREV_CHUNKS + 1) * CHUNK
PAD_ROWS = N_PREV_CHUNKS * CHUNK
PAIR_ROWS = 2 * CHUNK
PAIR_BAND = BAND + CHUNK
SOFTMAX_ROWS = 32
SSM_GROUP = 16
SSM_STATE = 64
ATT_HEAD_DIM = 64
REL_CLIP = 128
MEM_HEADS = 4
N_GROUPS = 4
EXPERTS_PER_GROUP = 8
EPS = 1e-6
NEG_INF = -1e30

LANES = 128
MXU_N = 256
VMEM_LIMIT = 56 * 1024 * 1024
ROUTER_LANES = LANES
MOE_TILE = 256
MOE_BUFS = 3
ROUTER_E0 = N_GROUPS
PAIRS_PER_GROUP = EXPERTS_PER_GROUP * (EXPERTS_PER_GROUP - 1) // 2
N_CLASSES = N_GROUPS * PAIRS_PER_GROUP
ROUTE_CLS, ROUTE_WA, ROUTE_WB = 64, 65, 66


def _cparams(sem):
    return pltpu.CompilerParams(dimension_semantics=sem, vmem_limit_bytes=VMEM_LIMIT)


def _rms(x, gain):
    ms = jnp.mean(x * x, axis=-1, keepdims=True)
    return x * lax.rsqrt(ms + EPS) * gain


def _sigmoid(x):
    return 1.0 / (1.0 + jnp.exp(-x))


def _gelu_tanh(x):
    c = math.sqrt(2.0 / math.pi)
    return 0.5 * x * (1.0 + jnp.tanh(c * (x + 0.044715 * (x * x * x))))


def _dot(a, b):
    return jnp.dot(a, b, preferred_element_type=F32)


def _dot_nt(a, b):
    return lax.dot_general(a, b, (((1,), (1,)), ((), ())), preferred_element_type=F32)


def _split_bf16(x):
    hi = x.astype(BF16)
    lo = (x - hi.astype(F32)).astype(BF16)
    return hi, lo


def _full(shape):
    n = len(shape)
    return pl.BlockSpec(shape, lambda *_: (0,) * n)


def _in_proj_kernel(x_ref, g_ref, w_ref, qg_ref, kg_ref, hm_ref,
                    u_ref, q_ref, k_ref, v_ref, kk_ref, vk_ref, *, width, nt):
    h = _rms(x_ref[...], g_ref[...]).astype(BF16)
    z = _dot(h, w_ref[...])
    u_ref[...] = z[:, :width].astype(BF16)

    def head_norm(a, gain):
        hi, lo = _split_bf16(a * a)
        ms = _dot(hi, hm_ref[...]) + _dot(lo, hm_ref[...])
        return a * lax.rsqrt(ms + EPS) * gain

    q = head_norm(z[:, width:2 * width], qg_ref[...])
    k = head_norm(z[:, 2 * width:3 * width], kg_ref[...])
    v = z[:, 3 * width:]
    q_ref[...] = q.astype(BF16)
    k_ref[...] = k.astype(BF16)
    v_ref[...] = v.astype(BF16)

    @pl.when(pl.program_id(1) == nt - 1)
    def _():
        if len(kk_ref.shape) == 3:
            kt, vt = k.T, v.T
            for hd in range(width // ATT_HEAD_DIM):
                rows = slice(hd * ATT_HEAD_DIM, (hd + 1) * ATT_HEAD_DIM)
                kk_ref[hd] = kt[rows, :]
                vk_ref[hd] = vt[rows, :]
        else:
            kk_ref[...] = k
            vk_ref[...] = v


def _in_proj(x2d, gain, w_bf, qg, kg, hm, *, nb, nt, tm, keep_transposed):
    rows, d = x2d.shape
    width = w_bf.shape[1] // 4
    heads = width // ATT_HEAD_DIM
    tok = lambda b, t: (b * nt + t, 0)
    out_shape = (
        jax.ShapeDtypeStruct((nt * tm, nb * width), BF16),
        jax.ShapeDtypeStruct((rows, width), BF16),
        jax.ShapeDtypeStruct((rows, width), BF16),
        jax.ShapeDtypeStruct((rows, width), BF16),
    )
    if keep_transposed:
        kept = jax.ShapeDtypeStruct((nb * heads, ATT_HEAD_DIM, tm), F32)
        keep = pl.BlockSpec((heads, ATT_HEAD_DIM, tm), lambda b, t: (b, 0, 0))
    else:
        kept = jax.ShapeDtypeStruct((nb * tm, width), F32)
        keep = pl.BlockSpec((tm, width), lambda b, t: (b, 0))
    out_shape = out_shape + (kept, kept)
    return pl.pallas_call(
        functools.partial(_in_proj_kernel, width=width, nt=nt),
        grid=(nb, nt),
        in_specs=[pl.BlockSpec((tm, d), tok), _full((1, d)), _full(w_bf.shape),
                  _full((1, width)), _full((1, width)), _full((width, width))],
        out_specs=(pl.BlockSpec((tm, width), lambda b, t: (t, b)),
                   pl.BlockSpec((tm, width), tok), pl.BlockSpec((tm, width), tok),
                   pl.BlockSpec((tm, width), tok),
                   keep, keep),
        out_shape=out_shape,
        compiler_params=_cparams(("arbitrary", "arbitrary")),
        name="in_proj",
    )(x2d, gain, w_bf, qg, kg, hm)


def _ssm_kernel(u_ref, bc_ref, cre_ref, cim_ref, lre_ref, lim_ref, d_ref, s0re_ref, s0im_ref,
                y_ref, sre_ref, sim_ref, bu_ref, *, nb, tc, ns):
    @pl.when(pl.program_id(0) == 0)
    def _():
        sre_ref[...] = s0re_ref[...]
        sim_ref[...] = s0im_ref[...]

    u = u_ref[...]
    n_tiles = 2 * ns // MXU_N
    for j in range(n_tiles):
        slab = (j % (n_tiles // 2)) // 2
        bu_ref[:, j * MXU_N:(j + 1) * MXU_N] = _dot(u[:, slab * LANES:(slab + 1) * LANES], bc_ref[j])

    cw = 8 * 1024 // nb
    for cb in range(ns // cw):
        c0 = cb * cw
        lre = jnp.broadcast_to(lre_ref[:, c0:c0 + cw], (nb, cw))
        lim = jnp.broadcast_to(lim_ref[:, c0:c0 + cw], (nb, cw))

        def step(t, carry, c0=c0, lre=lre, lim=lim):
            sr, si = carry
            r0 = pl.multiple_of(t * nb, nb)
            nr = lre * sr - lim * si + bu_ref[pl.ds(r0, nb), c0:c0 + cw]
            ni = lre * si + lim * sr + bu_ref[pl.ds(r0, nb), ns + c0:ns + c0 + cw]
            bu_ref[pl.ds(r0, nb), c0:c0 + cw] = nr
            bu_ref[pl.ds(r0, nb), ns + c0:ns + c0 + cw] = ni
            return nr, ni

        sr, si = lax.fori_loop(0, tc, step, (sre_ref[:, c0:c0 + cw], sim_ref[:, c0:c0 + cw]))
        sre_ref[:, c0:c0 + cw] = sr
        sim_ref[:, c0:c0 + cw] = si

    width = u.shape[1]
    kt = ns * MXU_N // width
    for n in range(width // MXU_N):
        s_re = bu_ref[:, n * kt:(n + 1) * kt].astype(BF16)
        s_im = bu_ref[:, ns + n * kt:ns + (n + 1) * kt].astype(BF16)
        y = _dot(s_re, cre_ref[n]) + _dot(s_im, cim_ref[n])
        cols = slice(n * MXU_N, (n + 1) * MXU_N)
        y_ref[:, cols] = y + d_ref[:, cols] * u[:, cols].astype(F32)


def _ssm(u_rows, bc, cre, cim, lre, lim, dskip, s0re, s0im, *, nb, tc):
    rows, width = u_rows.shape
    ns = lre.shape[1]
    r = tc * nb
    return pl.pallas_call(
        functools.partial(_ssm_kernel, nb=nb, tc=tc, ns=ns),
        grid=(rows // r,),
        in_specs=[pl.BlockSpec((r, width), lambda i: (i, 0)), _full(bc.shape), _full(cre.shape),
                  _full(cim.shape), _full((1, ns)), _full((1, ns)), _full((1, width)),
                  _full((nb, ns)), _full((nb, ns))],
        out_specs=(pl.BlockSpec((r, width), lambda i: (i, 0)), _full((nb, ns)), _full((nb, ns))),
        out_shape=(jax.ShapeDtypeStruct((rows, width), F32),
                   jax.ShapeDtypeStruct((nb, ns), F32), jax.ShapeDtypeStruct((nb, ns), F32)),
        scratch_shapes=[pltpu.VMEM((r, 2 * ns), F32)],
        compiler_params=_cparams(("arbitrary",)),
        name="ssm",
    )(u_rows, bc, cre, cim, lre, lim, dskip, s0re, s0im)


def _half_select(shape):
    lane = lax.broadcasted_iota(jnp.int32, shape, 1)
    return lane < ATT_HEAD_DIM


def _head_masks(first):
    m0 = jnp.where(first, 1.0, 0.0).astype(BF16)
    return m0, (1.0 - m0.astype(F32)).astype(BF16)


def _band_prompt_kernel(q_ref, k_ref, v_ref, bias_ref, o_ref, kp_ref, vp_ref, s_scr, p_scr, l_scr, *, t_len):
    width = q_ref.shape[1]
    n_hp = width // LANES
    kp_ref[0:PAD_ROWS, :] = jnp.zeros((PAD_ROWS, width), BF16)
    vp_ref[0:PAD_ROWS, :] = jnp.zeros((PAD_ROWS, width), BF16)
    kp_ref[PAD_ROWS:, :] = k_ref[...]
    vp_ref[PAD_ROWS:, :] = v_ref[...]
    first = _half_select((PAIR_ROWS, LANES))
    head_mask = _head_masks(first)
    col = lax.broadcasted_iota(jnp.int32, (SOFTMAX_ROWS, PAIR_BAND), 1)

    def pair(pi, carry, *, masked):
        r0 = pl.multiple_of(pi * PAIR_ROWS, PAIR_ROWS)
        for hp in range(n_hp):
            lanes = slice(hp * LANES, (hp + 1) * LANES)
            qp = q_ref[pl.ds(r0, PAIR_ROWS), lanes]
            qs = jnp.concatenate([qp * head_mask[0], qp * head_mask[1]], axis=0)
            s_scr[2 * hp * PAIR_ROWS:2 * (hp + 1) * PAIR_ROWS, :] = _dot_nt(
                qs, kp_ref[pl.ds(r0, PAIR_BAND), lanes])
        for h in range(2 * n_hp):
            for rb in range(0, PAIR_ROWS, SOFTMAX_ROWS):
                rows = slice(h * PAIR_ROWS + rb, h * PAIR_ROWS + rb + SOFTMAX_ROWS)
                s = s_scr[rows, :] + bias_ref[h, rb:rb + SOFTMAX_ROWS, :]
                if masked:
                    s = jnp.where(col >= PAD_ROWS - r0, s, NEG_INF)
                p = jnp.exp(s - jnp.max(s, axis=-1, keepdims=True))
                p_scr[rows, :] = p.astype(BF16)
                l_scr[rows, :] = jnp.broadcast_to(1.0 / jnp.sum(p, axis=-1, keepdims=True),
                                                  (SOFTMAX_ROWS, LANES))
        for hp in range(n_hp):
            lanes = slice(hp * LANES, (hp + 1) * LANES)
            rows = slice(2 * hp * PAIR_ROWS, 2 * (hp + 1) * PAIR_ROWS)
            o2 = _dot(p_scr[rows, :], vp_ref[pl.ds(r0, PAIR_BAND), lanes]) * l_scr[rows, :]
            o_ref[pl.ds(r0, PAIR_ROWS), lanes] = jnp.where(
                first, o2[:PAIR_ROWS], o2[PAIR_ROWS:]).astype(BF16)
        return carry

    n_masked = PAD_ROWS // PAIR_ROWS
    lax.fori_loop(0, n_masked, functools.partial(pair, masked=True), 0)
    lax.fori_loop(n_masked, t_len // PAIR_ROWS, functools.partial(pair, masked=False), 0)


def _band_prompt(q, k, v, bias, *, nb, t_len):
    width = q.shape[1]
    heads = width // ATT_HEAD_DIM
    blk = pl.BlockSpec((t_len, width), lambda b: (b, 0))
    return pl.pallas_call(
        functools.partial(_band_prompt_kernel, t_len=t_len),
        grid=(nb,),
        in_specs=[blk, blk, blk, _full(bias.shape)],
        out_specs=blk,
        out_shape=jax.ShapeDtypeStruct(q.shape, BF16),
        scratch_shapes=[pltpu.VMEM((t_len + PAD_ROWS, width), BF16),
                        pltpu.VMEM((t_len + PAD_ROWS, width), BF16),
                        pltpu.VMEM((heads * PAIR_ROWS, PAIR_BAND), F32),
                        pltpu.VMEM((heads * PAIR_ROWS, PAIR_BAND), BF16),
                        pltpu.VMEM((heads * PAIR_ROWS, LANES), F32)],
        compiler_params=_cparams(("arbitrary",)),
        name="band_prompt",
    )(q, k, v, bias)


def _band_sample_kernel(q_ref, k_ref, v_ref, ck_ref, cv_ref, bc_ref, bn_ref, o_ref):
    for h in range(ck_ref.shape[0]):
        cols = slice(h * ATT_HEAD_DIM, (h + 1) * ATT_HEAD_DIM)
        qh = q_ref[:, cols]
        sc = _dot(qh, ck_ref[h].astype(BF16)) + bc_ref[h]
        sn = _dot_nt(qh, k_ref[:, cols]) + bn_ref[h]
        m = jnp.maximum(jnp.max(sc, axis=-1, keepdims=True), jnp.max(sn, axis=-1, keepdims=True))
        pc = jnp.exp(sc - m)
        pn = jnp.exp(sn - m)
        l = jnp.sum(pc, axis=-1, keepdims=True) + jnp.sum(pn, axis=-1, keepdims=True)
        o = _dot_nt(pc.astype(BF16), cv_ref[h].astype(BF16)) + _dot(pn.astype(BF16), v_ref[:, cols])
        o_ref[:, cols] = (o / l).astype(BF16)


def _band_sample(q_tm, k_tm, v_tm, cache_kt, cache_vt, bias_c, bias_n, *, nb, s_len):
    width = q_tm.shape[1] // nb
    heads = cache_kt.shape[0] // nb
    col = pl.BlockSpec((s_len, width), lambda b: (0, b))
    cache = pl.BlockSpec((heads,) + cache_kt.shape[1:], lambda b: (b, 0, 0))
    return pl.pallas_call(
        _band_sample_kernel,
        grid=(nb,),
        in_specs=[col, col, col, cache, cache, _full(bias_c.shape), _full(bias_n.shape)],
        out_specs=col,
        out_shape=jax.ShapeDtypeStruct(q_tm.shape, BF16),
        compiler_params=_cparams(("arbitrary",)),
        name="band_sample",
    )(q_tm, k_tm, v_tm, cache_kt, cache_vt, bias_c, bias_n)


def _mem_kv_kernel(m_ref, g_ref, wk_ref, wv_ref, kg_ref, k_ref, v_ref, kb_ref, vb_ref):
    m = _rms(m_ref[...], g_ref[...]).astype(BF16)
    k = _dot(m, wk_ref[...])
    v = _dot(m, wv_ref[...])
    hd = kg_ref.shape[1]
    for h in range(k.shape[1] // hd):
        cols = slice(h * hd, (h + 1) * hd)
        kh = _rms(k[:, cols], kg_ref[...])
        k_ref[:, h, :] = kh
        v_ref[:, h, :] = v[:, cols]
        kb_ref[:, cols] = kh.astype(BF16)
    vb_ref[...] = v.astype(BF16)


def _mem_kv(mem2d, gain, wk, wv, kgain, *, tm):
    rows, d = mem2d.shape
    hd = kgain.shape[1]
    blk = pl.BlockSpec((tm, d), lambda i: (i, 0))
    hblk = pl.BlockSpec((tm, d // hd, hd), lambda i: (i, 0, 0))
    return pl.pallas_call(
        _mem_kv_kernel,
        grid=(rows // tm,),
        in_specs=[blk, _full((1, d)), _full(wk.shape), _full(wv.shape), _full(kgain.shape)],
        out_specs=(hblk, hblk, blk, blk),
        out_shape=(jax.ShapeDtypeStruct((rows, d // hd, hd), F32),
                   jax.ShapeDtypeStruct((rows, d // hd, hd), F32),
                   jax.ShapeDtypeStruct((rows, d), BF16), jax.ShapeDtypeStruct((rows, d), BF16)),
        compiler_params=_cparams(("arbitrary",)),
        name="mem_kv",
    )(mem2d, gain, wk, wv, kgain)


def _route(logits):
    lane = lax.broadcasted_iota(jnp.int32, logits.shape, 1).astype(F32)
    big = float(ROUTER_LANES)
    is_g = lane < N_GROUPS
    lg = jnp.where(is_g, logits, NEG_INF)
    gmax = jnp.max(lg, axis=-1, keepdims=True)
    p1 = 1.0 / jnp.sum(jnp.where(is_g, jnp.exp(lg - gmax), 0.0), axis=-1, keepdims=True)
    g_idx = jnp.min(jnp.where(lg == gmax, lane, big), axis=-1, keepdims=True)
    lo = ROUTER_E0 + g_idx * EXPERTS_PER_GROUP
    le = jnp.where((lane >= lo) & (lane < lo + EXPERTS_PER_GROUP), logits, NEG_INF)
    v1 = jnp.max(le, axis=-1, keepdims=True)
    i1 = jnp.min(jnp.where(le == v1, lane, big), axis=-1, keepdims=True)
    le2 = jnp.where(lane == i1, NEG_INF, le)
    v2 = jnp.max(le2, axis=-1, keepdims=True)
    i2 = jnp.min(jnp.where(le2 == v2, lane, big), axis=-1, keepdims=True)
    e2 = jnp.exp(v2 - v1)
    w1 = p1 / (1.0 + e2)
    w2 = p1 * e2 / (1.0 + e2)
    gates = jnp.where(lane == i1, w1, 0.0) + jnp.where(lane == i2, w2, 0.0)
    a = jnp.minimum(i1, i2) - lo
    b = jnp.maximum(i1, i2) - lo
    pair = a * (2 * EXPERTS_PER_GROUP - 1 - a) * 0.5 + (b - a - 1.0)
    cls = g_idx * PAIRS_PER_GROUP + pair
    w_a = jnp.where(i1 < i2, w1, w2)
    w_b = jnp.where(i1 < i2, w2, w1)
    return (gates + jnp.where(lane == ROUTE_CLS, cls, 0.0) + jnp.where(lane == ROUTE_WA, w_a, 0.0)
            + jnp.where(lane == ROUTE_WB, w_b, 0.0))


def _mix_mem_kernel(x_ref, ys_ref, ya_ref, mk_ref, mv_ref, wglu_ref, ons_ref, ona_ref, wout_ref,
                    nmem_ref, wq_ref, qg_ref, wo_ref, nffn_ref, wrh_ref, wrl_ref, rb_ref,
                    *outs, routed):
    if routed:
        x2_ref, o_scr = outs
    else:
        x2_ref, xn_ref, gate_ref, o_scr = outs
    width = ys_ref.shape[1]
    g = _dot(_gelu_tanh(ys_ref[...]).astype(BF16), wglu_ref[...])
    y_s = g[:, :width] * _sigmoid(g[:, width:])
    cat_s = _rms(y_s, ons_ref[...]).astype(BF16)
    cat_a = _rms(ya_ref[...].astype(F32), ona_ref[...]).astype(BF16)
    x1 = x_ref[...] + _dot(cat_s, wout_ref[0:width, :]) + _dot(cat_a, wout_ref[width:, :])

    q = _dot(_rms(x1, nmem_ref[...]).astype(BF16), wq_ref[...])
    hd = qg_ref.shape[1]
    scale = hd ** -0.5
    for h in range(q.shape[1] // hd):
        cols = slice(h * hd, (h + 1) * hd)
        qh = _rms(q[:, cols], qg_ref[...]).astype(BF16)
        head = (slice(None), h, slice(None)) if len(mk_ref.shape) == 3 else (slice(None), cols)
        s = _dot_nt(qh, mk_ref[head].astype(BF16)) * scale
        p = jnp.exp(s - jnp.max(s, axis=-1, keepdims=True))
        l = jnp.sum(p, axis=-1, keepdims=True)
        o_scr[:, cols] = (_dot(p.astype(BF16), mv_ref[head].astype(BF16)) / l).astype(BF16)
    acc = x1 + _dot(o_scr[...], wo_ref[...])
    d = acc.shape[1]
    x2_ref[:, :d] = acc

    xn = _rms(acc, nffn_ref[...])
    xh, xl = _split_bf16(xn)
    logits = _dot(xh, wrh_ref[...]) + _dot(xl, wrh_ref[...]) + _dot(xh, wrl_ref[...]) + rb_ref[...]
    if routed:
        x2_ref[:, d:] = _route(logits)
    else:
        xn_ref[...] = xh
        gate_ref[...] = _route(logits)


def _mix_mem(x, ys, ya, mk, mv, weights, *, grid, tm, row_map, ssm_map, mem_map, routed):
    d = weights["w_out"].shape[1]
    width = weights["w_glu"].shape[0]
    names = ("w_glu", "out_norm_ssm", "out_norm_att", "w_out", "norm_mem", "w_mem_q", "mem_q_gain",
             "w_mem_o", "norm_ffn", "w_router_hi", "w_router_lo", "router_bias")
    ws = [weights[n] for n in names]
    mem_rows = weights["n_mem"]
    xspec = pl.BlockSpec((tm, d), row_map)
    hspec = pl.BlockSpec((tm, width), row_map)
    sspec = pl.BlockSpec((tm, width), ssm_map)
    mspec = pl.BlockSpec((mem_rows,) + mk.shape[1:], mem_map)
    n_col = x.shape[1] // d
    if routed:
        out_specs = pl.BlockSpec((tm, d + ROUTER_LANES), row_map)
        out_shape = jax.ShapeDtypeStruct((x.shape[0], n_col * (d + ROUTER_LANES)), F32)
    else:
        out_specs = (xspec, xspec, pl.BlockSpec((tm, ROUTER_LANES), row_map))
        out_shape = (jax.ShapeDtypeStruct(x.shape, F32), jax.ShapeDtypeStruct(x.shape, BF16),
                     jax.ShapeDtypeStruct((x.shape[0], n_col * ROUTER_LANES), F32))
    return pl.pallas_call(
        functools.partial(_mix_mem_kernel, routed=routed),
        grid=grid,
        in_specs=[xspec, sspec, hspec, mspec, mspec] + [_full(w.shape) for w in ws],
        out_specs=out_specs,
        out_shape=out_shape,
        scratch_shapes=[pltpu.VMEM((tm, d), BF16)],
        compiler_params=_cparams(("arbitrary",) * len(grid)),
        name="mix_mem",
    )(x, ys, ya, mk, mv, *ws)


def _moe_kernel(xn_ref, x2_ref, gate_ref, wg_ref, wu_ref, wd_ref, o_ref):
    e = pl.program_id(1)

    @pl.when(e == 0)
    def _():
        o_ref[...] = x2_ref[...]

    gates = gate_ref[...]
    lane = lax.broadcasted_iota(jnp.int32, gates.shape, 1)
    ge = jnp.sum(jnp.where(lane == e + ROUTER_E0, gates, 0.0), axis=-1, keepdims=True)
    xn = xn_ref[...]
    a = _dot(xn, wg_ref[...])
    h = a * _sigmoid(a) * _dot(xn, wu_ref[...])
    o_ref[...] += ge * _dot(h.astype(BF16), wd_ref[...])


def _moe(xn, x2, gates, wg, wu, wd, *, tm):
    rows, d = xn.shape
    n_exp, _, dff = wg.shape
    row = lambda i, e: (i, 0)
    return pl.pallas_call(
        _moe_kernel,
        grid=(rows // tm, n_exp),
        in_specs=[pl.BlockSpec((tm, d), row), pl.BlockSpec((tm, d), row),
                  pl.BlockSpec((tm, ROUTER_LANES), row),
                  pl.BlockSpec((None, d, dff), lambda i, e: (e, 0, 0)),
                  pl.BlockSpec((None, d, dff), lambda i, e: (e, 0, 0)),
                  pl.BlockSpec((None, dff, d), lambda i, e: (e, 0, 0))],
        out_specs=pl.BlockSpec((tm, d), row),
        out_shape=jax.ShapeDtypeStruct((rows, d), F32),
        compiler_params=_cparams(("arbitrary", "arbitrary")),
        name="moe",
    )(xn, x2, gates, wg, wu, wd)


def _moe_routed_kernel(tile_ref, ea_ref, eb_ref, lo_ref, hi_ref, valid_ref, tok_ref,
                       x_hbm, nffn_ref, wga_ref, wua_ref, wda_ref, wgb_ref, wub_ref, wdb_ref,
                       y_hbm, *scratch, tm, n_tiles):
    nb = MOE_BUFS
    xbuf, obuf = scratch[:nb], scratch[nb:2 * nb]
    acc_ref, gsem, ssem = scratch[2 * nb:]
    w = pl.program_id(0)
    t = tile_ref[w]
    lo = lo_ref[w]
    hi = hi_ref[w]
    valid = valid_ref[w] == 1
    d = y_hbm.shape[1]

    def row_copy_in(tile, s, r):
        tok = tok_ref[tile * tm + r]
        return pltpu.make_async_copy(x_hbm.at[pl.ds(tok, 1)], xbuf[s].at[pl.ds(r, 1)], gsem.at[s])

    def row_copy_out(tile, s, r):
        tok = tok_ref[tile * tm + r]
        return pltpu.make_async_copy(obuf[s].at[pl.ds(r, 1)], y_hbm.at[pl.ds(tok, 1)], ssem.at[s])

    def start_gather(tile, s):
        for r in range(tm):
            row_copy_in(tile, s, r).start()

    def start_scatter(tile, s):
        for r in range(tm):
            row_copy_out(tile, s, r).start()

    def wait_gather(s):
        pltpu.make_async_copy(x_hbm.at[pl.ds(0, tm)], xbuf[s], gsem.at[s]).wait()

    def wait_scatter(s):
        pltpu.make_async_copy(obuf[s], y_hbm.at[pl.ds(0, tm)], ssem.at[s]).wait()

    def compute(s, first):
        xe = xbuf[s][...]
        xn = _rms(xe[:, :d], nffn_ref[...]).astype(BF16)
        row = lax.broadcasted_iota(jnp.int32, (tm, 1), 0)
        inseg = jnp.logical_and(row >= lo, row < hi)
        w_a = jnp.where(inseg, xe[:, d + ROUTE_WA:d + ROUTE_WA + 1], 0.0)
        w_b = jnp.where(inseg, xe[:, d + ROUTE_WB:d + ROUTE_WB + 1], 0.0)
        a = _dot(xn, wga_ref[...])
        h_a = (a * _sigmoid(a) * _dot(xn, wua_ref[...])).astype(BF16)
        b = _dot(xn, wgb_ref[...])
        h_b = (b * _sigmoid(b) * _dot(xn, wub_ref[...])).astype(BF16)
        upd = w_a * _dot(h_a, wda_ref[...]) + w_b * _dot(h_b, wdb_ref[...])
        if first:
            acc_ref[...] = upd
        else:
            acc_ref[...] += upd

    def when(*conds):
        c = conds[0]
        for extra in conds[1:]:
            c = jnp.logical_and(c, extra)
        return pl.when(c)

    ahead = nb - 1

    @pl.when(w == 0)
    def _():
        for k in range(ahead):
            start_gather(k, k)

    is_first = jnp.logical_and(valid, lo == 0)
    is_last = jnp.logical_and(valid, hi == tm)
    last_tile = n_tiles - 1
    for s in range(nb):
        mine = t % nb == s
        nxt, prv = (s + ahead) % nb, (s - 1) % nb

        if s == 0:
            @when(is_first, t == 0)
            def _(nxt=nxt):
                wait_gather(0)
                start_gather(ahead, nxt)
                compute(0, True)

        @when(is_first, mine, t > 0, t + ahead <= last_tile)
        def _(s=s, nxt=nxt, prv=prv):
            wait_gather(s)
            start_gather(t + ahead, nxt)
            start_scatter(t - 1, prv)
            compute(s, True)

        @when(is_first, mine, t + ahead > last_tile)
        def _(s=s, prv=prv):
            wait_gather(s)
            start_scatter(t - 1, prv)
            compute(s, True)

        @when(valid, mine, lo > 0)
        def _(s=s):
            compute(s, False)

        @when(is_last, mine)
        def _(s=s):
            @pl.when(t >= nb)
            def _():
                wait_scatter(s)

            obuf[s][...] = xbuf[s][:, :d] + acc_ref[...]

            if s == last_tile % nb:
                @pl.when(t == last_tile)
                def _():
                    start_scatter(t, s)
                    for k in range(nb):
                        wait_scatter(k)


def _moe_routed(x2ext, plan, nffn, wg, wu, wd, *, tm):
    rows, de = x2ext.shape
    d = de - ROUTER_LANES
    n_exp, _, dff = wg.shape
    n_tiles = rows // tm
    assert rows % tm == 0 and n_tiles > 2 * MOE_BUFS
    n_items = plan[0].shape[0]
    ea = lambda w, tile, ea_, eb_, *_: (ea_[w], 0, 0)
    eb = lambda w, tile, ea_, eb_, *_: (eb_[w], 0, 0)
    grid_spec = pltpu.PrefetchScalarGridSpec(
        num_scalar_prefetch=len(plan),
        grid=(n_items,),
        in_specs=[pl.BlockSpec(memory_space=pl.ANY),
                  pl.BlockSpec((1, d), lambda w, *_: (0, 0)),
                  pl.BlockSpec((None, d, dff), ea), pl.BlockSpec((None, d, dff), ea),
                  pl.BlockSpec((None, dff, d), ea),
                  pl.BlockSpec((None, d, dff), eb), pl.BlockSpec((None, d, dff), eb),
                  pl.BlockSpec((None, dff, d), eb)],
        out_specs=pl.BlockSpec(memory_space=pl.ANY),
        scratch_shapes=([pltpu.VMEM((tm, de), F32)] * MOE_BUFS + [pltpu.VMEM((tm, d), F32)] * MOE_BUFS
                        + [pltpu.VMEM((tm, d), F32),
                           pltpu.SemaphoreType.DMA((MOE_BUFS,)), pltpu.SemaphoreType.DMA((MOE_BUFS,))]),
    )
    return pl.pallas_call(
        functools.partial(_moe_routed_kernel, tm=tm, n_tiles=n_tiles),
        grid_spec=grid_spec,
        out_shape=jax.ShapeDtypeStruct((rows, d), F32),
        compiler_params=_cparams(("arbitrary",)),
        name="moe_routed",
    )(*plan, x2ext, nffn, wg, wu, wd, wg, wu, wd)


def _route_plan(cls, *, tm):
    n = cls.shape[0]
    n_tiles = n // tm
    order = jnp.argsort(cls).astype(jnp.int32)
    classes = jnp.arange(N_CLASSES, dtype=jnp.int32)
    class_end = jnp.sum((cls[:, None] <= classes[None, :]).astype(jnp.int32), axis=0)
    class_start = jnp.concatenate([jnp.zeros((1,), jnp.int32), class_end[:-1]])
    bounds = jnp.concatenate([jnp.arange(n_tiles, dtype=jnp.int32) * tm,
                              jnp.where(class_end > class_start, class_start, n)])
    n_items = bounds.shape[0]
    idx = jnp.arange(n_items, dtype=jnp.int32)
    before = jnp.logical_or(bounds[None, :] < bounds[:, None],
                            jnp.logical_and(bounds[None, :] == bounds[:, None], idx[None, :] < idx[:, None]))
    rank = jnp.sum(before.astype(jnp.int32), axis=1)
    start = jnp.sum(jnp.where(rank[:, None] == idx[None, :], bounds[:, None], 0), axis=0)
    stop = jnp.concatenate([start[1:], jnp.full((1,), n, jnp.int32)])
    valid = stop > start
    tile = jnp.minimum(start // tm, n_tiles - 1)
    lo = start - tile * tm
    hi = stop - tile * tm
    c = jnp.minimum(jnp.sum((class_end[None, :] <= start[:, None]).astype(jnp.int32), axis=1),
                    N_CLASSES - 1)
    g = c // PAIRS_PER_GROUP
    pair = c % PAIRS_PER_GROUP
    a = jnp.zeros_like(pair)
    for k in range(1, EXPERTS_PER_GROUP - 1):
        a = a + (pair >= k * (2 * EXPERTS_PER_GROUP - 1 - k) // 2).astype(jnp.int32)
    b = pair - a * (2 * EXPERTS_PER_GROUP - 1 - a) // 2 + a + 1
    e_a = g * EXPERTS_PER_GROUP + a
    e_b = g * EXPERTS_PER_GROUP + b
    i32 = lambda v: v.astype(jnp.int32)
    return (i32(tile), i32(e_a), i32(e_b), i32(lo), i32(hi), i32(valid), order)


def _ssm_params(lam_re, lam_im, log_step, b_re, b_im, c_re, c_im):
    n_g, n_p = lam_re.shape
    step = jnp.exp(log_step.astype(F32))[:, None]
    mag = jnp.exp(lam_re * step)
    lb_re = mag * jnp.cos(lam_im * step)
    lb_im = mag * jnp.sin(lam_im * step)
    den = lam_re * lam_re + lam_im * lam_im
    f_re = ((lb_re - 1.0) * lam_re + lb_im * lam_im) / den
    f_im = (lb_im * lam_re - (lb_re - 1.0) * lam_im) / den
    bb_re = f_re[..., None] * b_re - f_im[..., None] * b_im
    bb_im = f_re[..., None] * b_im + f_im[..., None] * b_re
    eye = jnp.eye(n_g, dtype=F32)
    ns = n_g * n_p
    width = n_g * SSM_GROUP
    b_full = jnp.concatenate(
        [jnp.einsum("hg,gpc->hcgp", eye, bb_re).reshape(width, ns),
         jnp.einsum("hg,gpc->hcgp", eye, bb_im).reshape(width, ns)], axis=1)
    n_tiles = 2 * ns // MXU_N
    bc = jnp.stack([
        b_full[((j % (n_tiles // 2)) // 2) * LANES:((j % (n_tiles // 2)) // 2 + 1) * LANES,
               j * MXU_N:(j + 1) * MXU_N] for j in range(n_tiles)]).astype(BF16)
    c_full_re = jnp.einsum("gh,gcp->gphc", eye, c_re).reshape(ns, width)
    c_full_im = -jnp.einsum("gh,gcp->gphc", eye, c_im).reshape(ns, width)
    kt = ns * MXU_N // width
    tiles = range(width // MXU_N)
    cre = jnp.stack([c_full_re[n * kt:(n + 1) * kt, n * MXU_N:(n + 1) * MXU_N] for n in tiles]).astype(BF16)
    cim = jnp.stack([c_full_im[n * kt:(n + 1) * kt, n * MXU_N:(n + 1) * MXU_N] for n in tiles]).astype(BF16)
    return bc, cre, cim, lb_re.reshape(1, ns), lb_im.reshape(1, ns)


def _rel_bias(rel_bias, q0, n_q, n_k):
    n_r = n_q + n_k - 1
    dist = q0 + n_q - 1 - jnp.arange(n_r)
    r = rel_bias.astype(F32)[:, jnp.clip(dist, -REL_CLIP, REL_CLIP) + REL_CLIP]
    r = jnp.pad(r, ((0, 0), (0, 1)))
    rows = jnp.tile(r, (1, n_q))[:, :n_q * n_r].reshape(-1, n_q, n_r)
    return rows[:, :, n_q - 1:n_q - 1 + n_k]


def _layer(xp, xs, mem_p, ck, cv, s_re, s_im, cmk, cmv, w, dims):
    batch, seq, dec_batch, dec_seq, d = dims
    width = w["w_in"].shape[1] // 4
    ns = w["lam_re"].shape[1]
    n_mem = w["n_mem"]
    tm_p = 512
    nt_p = seq // tm_p
    n_dec = dec_batch * dec_seq

    u, q, k, v, kk, vk = _in_proj(xp, w["norm_mix"], w["w_in"], w["qg"], w["kg"], w["head_mean"],
                                  nb=batch, nt=nt_p, tm=tm_p, keep_transposed=True)
    zeros = jnp.zeros((batch, ns), F32)
    y_ssm, pre, pim = _ssm(u.reshape(seq * batch, width), w["bc"], w["cre"], w["cim"], w["lam_re"],
                           w["lam_im"], w["ssm_d"], zeros, zeros, nb=batch, tc=32)
    y_att = _band_prompt(q, k, v, w["bias_p"], nb=batch, t_len=seq)
    mk, mv, mkb, mvb = _mem_kv(mem_p, w["mem_in_norm"], w["w_mem_k"], w["w_mem_v"], w["mem_k_gain"], tm=512)
    tm_d = 512
    nt_d = seq // tm_d
    x2ext = _mix_mem(xp, y_ssm.reshape(seq, batch * width), y_att, mkb, mvb, w,
                     grid=(batch, nt_d), tm=tm_d,
                     row_map=lambda b, t: (b * nt_d + t, 0), ssm_map=lambda b, t: (t, b),
                     mem_map=lambda b, t: (b, 0), routed=True)
    plan = _route_plan(x2ext[:, d + ROUTE_CLS].astype(jnp.int32), tm=MOE_TILE)
    yp = _moe_routed(x2ext, plan, w["norm_ffn"], w["exp_w_gate"], w["exp_w_up"], w["exp_w_down"],
                     tm=MOE_TILE)

    us, qs, ks, vs, kks, vks = _in_proj(xs, w["norm_mix"], w["w_in"], w["qg"], w["kg"], w["head_mean"],
                                        nb=1, nt=1, tm=n_dec, keep_transposed=False)
    ys_ssm, sre, sim = _ssm(us, w["bc"], w["cre"], w["cim"], w["lam_re"], w["lam_im"], w["ssm_d"],
                            s_re, s_im, nb=dec_batch, tc=dec_seq)
    tmv = lambda a: a.reshape(dec_seq, dec_batch * a.shape[1])
    ys_att = _band_sample(tmv(qs), tmv(ks), tmv(vs), ck, cv, w["bias_sc"], w["bias_sn"],
                          nb=dec_batch, s_len=dec_seq)
    x2s, xns, gates_s = _mix_mem(tmv(xs), tmv(ys_ssm), ys_att, cmk, cmv, w,
                                 grid=(dec_batch,), tm=dec_seq,
                                 row_map=lambda b: (0, b), ssm_map=lambda b: (0, b),
                                 mem_map=lambda b: (b, 0), routed=False)
    ys = _moe(xns.reshape(n_dec, d), x2s.reshape(n_dec, d), gates_s.reshape(n_dec, ROUTER_LANES),
              w["exp_w_gate"], w["exp_w_up"], w["exp_w_down"], tm=n_dec)
    return yp, ys, (kk, vk, pre, pim, mk, mv), (kks, vks, sre, sim)


def kernel(x_prompt, x_sample, mem_prompt, cache_attn_k, cache_attn_v, state_ssm_re, state_ssm_im, cache_mem_k, cache_mem_v, norm_mix, w_in, ssm_lambda_re, ssm_lambda_im, ssm_log_step, ssm_b_re, ssm_b_im, ssm_c_re, ssm_c_im, ssm_d, w_glu, att_q_gain, att_k_gain, att_rel_bias, out_norm_ssm, out_norm_att, w_out, norm_mem, mem_in_norm, w_mem_q, w_mem_k, w_mem_v, w_mem_o, mem_q_gain, mem_k_gain, norm_ffn, router_g_w, router_g_b, router_e_w, router_e_b, exp_w_gate, exp_w_up, exp_w_down):
    depth = norm_mix.shape[0]
    batch, seq, d = x_prompt.shape
    dec_batch, dec_seq, _ = x_sample.shape
    n_mem = mem_prompt.shape[1]
    att_rows = cache_attn_k.shape[2]
    n_g, n_p = ssm_lambda_re.shape[1:]
    width = n_g * SSM_GROUP
    heads = width // ATT_HEAD_DIM
    ns = n_g * n_p
    assert seq % 512 == 0 and att_rows == PAD_ROWS and seq >= PAD_ROWS
    assert (dec_batch * dec_seq) % 8 == 0 and dec_seq % 16 == 0

    xp = x_prompt.reshape(batch * seq, d)
    xs = x_sample.transpose(1, 0, 2).reshape(dec_seq * dec_batch, d)
    mem_p = mem_prompt.reshape(batch * n_mem, d)
    row = lambda a: a.reshape(1, -1).astype(F32)
    head_mean = jnp.kron(jnp.eye(heads, dtype=F32),
                         jnp.full((ATT_HEAD_DIM, ATT_HEAD_DIM), 1.0 / ATT_HEAD_DIM, F32)).astype(BF16)
    pr = jnp.arange(PAIR_ROWS)[:, None]
    pc = jnp.arange(PAIR_BAND)[None, :]
    pair_ok = jnp.where(pr < CHUNK, pc < BAND, pc >= CHUNK)
    att_scale = ATT_HEAD_DIM ** -0.5

    p_out, s_out = [], []
    for l in range(depth):
        bc, cre, cim, lb_re, lb_im = _ssm_params(ssm_lambda_re[l], ssm_lambda_im[l], ssm_log_step[l],
                                                 ssm_b_re[l], ssm_b_im[l], ssm_c_re[l], ssm_c_im[l])
        w_router = jnp.concatenate(
            [router_g_w[l], router_e_w[l].transpose(1, 0, 2).reshape(d, N_GROUPS * EXPERTS_PER_GROUP),
             jnp.zeros((d, ROUTER_LANES - N_GROUPS * (1 + EXPERTS_PER_GROUP)), F32)], axis=1)
        wr_hi, wr_lo = _split_bf16(w_router)
        r_bias = jnp.concatenate(
            [router_g_b[l], router_e_b[l].reshape(-1),
             jnp.zeros((ROUTER_LANES - N_GROUPS * (1 + EXPERTS_PER_GROUP),), F32)]).reshape(1, ROUTER_LANES)
        bias_s = _rel_bias(att_rel_bias[l], att_rows, dec_seq, att_rows + dec_seq)
        bias_p = jnp.where(pair_ok, _rel_bias(att_rel_bias[l], PAD_ROWS, PAIR_ROWS, PAIR_BAND), NEG_INF)
        w = dict(
            n_mem=n_mem,
            norm_mix=row(norm_mix[l]), w_in=w_in[l].astype(BF16),
            qg=row(jnp.tile(att_q_gain[l], heads) * att_scale), kg=row(jnp.tile(att_k_gain[l], heads)),
            head_mean=head_mean, bc=bc, cre=cre, cim=cim, lam_re=lb_re, lam_im=lb_im,
            ssm_d=row(ssm_d[l]), bias_p=bias_p,
            bias_sc=bias_s[:, :, :att_rows], bias_sn=bias_s[:, :, att_rows:],
            mem_in_norm=row(mem_in_norm[l]), w_mem_k=w_mem_k[l].astype(BF16),
            w_mem_v=w_mem_v[l].astype(BF16), mem_k_gain=row(mem_k_gain[l]),
            w_glu=w_glu[l].astype(BF16), out_norm_ssm=row(out_norm_ssm[l]),
            out_norm_att=row(out_norm_att[l]), w_out=w_out[l].astype(BF16), norm_mem=row(norm_mem[l]),
            w_mem_q=w_mem_q[l].astype(BF16), mem_q_gain=row(mem_q_gain[l]),
            w_mem_o=w_mem_o[l].astype(BF16), norm_ffn=row(norm_ffn[l]),
            w_router_hi=wr_hi, w_router_lo=wr_lo, router_bias=r_bias,
            exp_w_gate=exp_w_gate[l].astype(BF16), exp_w_up=exp_w_up[l].astype(BF16),
            exp_w_down=exp_w_down[l].astype(BF16),
        )
        xp, xs, p_new, s_new = _layer(
            xp, xs, mem_p,
            cache_attn_k[l].transpose(0, 2, 3, 1).reshape(dec_batch * heads, ATT_HEAD_DIM, att_rows),
            cache_attn_v[l].transpose(0, 2, 3, 1).reshape(dec_batch * heads, ATT_HEAD_DIM, att_rows),
            state_ssm_re[l].reshape(dec_batch, ns), state_ssm_im[l].reshape(dec_batch, ns),
            cache_mem_k[l].reshape(dec_batch * n_mem, d), cache_mem_v[l].reshape(dec_batch * n_mem, d),
            w, (batch, seq, dec_batch, dec_seq, d))
        p_out.append(p_new)
        s_out.append(s_new)

    sdt = state_ssm_re.dtype
    keep = min(PAD_ROWS, seq)
    kv_p = lambda a: a.reshape(batch, heads, ATT_HEAD_DIM, keep).transpose(0, 3, 1, 2)
    kv_s = lambda a: a.reshape(dec_seq, dec_batch, heads, ATT_HEAD_DIM).transpose(1, 0, 2, 3)
    st = lambda a: a.reshape(a.shape[0], n_g, n_p).astype(sdt)
    mkv = lambda a: a.reshape(batch, n_mem, MEM_HEADS, d // MEM_HEADS)
    stack = lambda f, outs, i: jnp.stack([f(o[i]) for o in outs])
    yp = xp.reshape(batch, seq, d)
    ys = xs.reshape(dec_seq, dec_batch, d).transpose(1, 0, 2)
    return (yp, ys,
            stack(kv_p, p_out, 0), stack(kv_p, p_out, 1), stack(st, p_out, 2), stack(st, p_out, 3),
            stack(mkv, p_out, 4), stack(mkv, p_out, 5),
            stack(kv_s, s_out, 0), stack(kv_s, s_out, 1), stack(st, s_out, 2), stack(st, s_out, 3))
```

```python
import functools
import math

import jax
import jax.numpy as jnp
from jax import lax
from jax.experimental import pallas as pl
from jax.experimental.pallas import tpu as pltpu

F32 = jnp.float32
BF16 = jnp.bfloat16

CHUNK = 64
N_PREV_CHUNKS = 8
BAND = (N_PREV_CHUNKS + 1) * CHUNK
PAD_ROWS = N_PREV_CHUNKS * CHUNK
PAIR_ROWS = 2 * CHUNK
PAIR_BAND = BAND + CHUNK
SOFTMAX_ROWS = 32
SSM_GROUP = 16
SSM_STATE = 64
ATT_HEAD_DIM = 64
REL_CLIP = 128
MEM_HEADS = 4
N_GROUPS = 4
EXPERTS_PER_GROUP = 8
EPS = 1e-6
NEG_INF = -1e30

LANES = 128
MXU_N = 256
VMEM_LIMIT = 56 * 1024 * 1024
ROUTER_LANES = LANES
MOE_TILE = 256
SAMPLE_STREAMS = 4
MOE_BUFS = 3
ROUTER_E0 = N_GROUPS
PAIRS_PER_GROUP = EXPERTS_PER_GROUP * (EXPERTS_PER_GROUP - 1) // 2
N_CLASSES = N_GROUPS * PAIRS_PER_GROUP
ROUTE_CLS, ROUTE_WA, ROUTE_WB = 64, 65, 66


def _cparams(sem):
    return pltpu.CompilerParams(dimension_semantics=sem, vmem_limit_bytes=VMEM_LIMIT)


def _rms(x, gain):
    ms = jnp.mean(x * x, axis=-1, keepdims=True)
    return x * lax.rsqrt(ms + EPS) * gain


def _sigmoid(x):
    return 1.0 / (1.0 + jnp.exp(-x))


def _gelu_tanh(x):
    c = math.sqrt(2.0 / math.pi)
    return 0.5 * x * (1.0 + jnp.tanh(c * (x + 0.044715 * (x * x * x))))


def _dot(a, b):
    return jnp.dot(a, b, preferred_element_type=F32)


def _dot_nt(a, b):
    return lax.dot_general(a, b, (((1,), (1,)), ((), ())), preferred_element_type=F32)


def _split_bf16(x):
    hi = x.astype(BF16)
    lo = (x - hi.astype(F32)).astype(BF16)
    return hi, lo


def _full(shape):
    n = len(shape)
    return pl.BlockSpec(shape, lambda *_: (0,) * n)


def _in_proj_kernel(x_ref, g_ref, w_ref, qg_ref, kg_ref, hm_ref,
                    u_ref, q_ref, k_ref, v_ref, kk_ref, vk_ref, *, width, nt):
    h = _rms(x_ref[...], g_ref[...]).astype(BF16)
    z = _dot(h, w_ref[...])
    u_ref[...] = z[:, :width].astype(u_ref.dtype)

    def head_norm(a, gain):
        hi, lo = _split_bf16(a * a)
        ms = _dot(hi, hm_ref[...]) + _dot(lo, hm_ref[...])
        return a * lax.rsqrt(ms + EPS) * gain

    q = head_norm(z[:, width:2 * width], qg_ref[...])
    k = head_norm(z[:, 2 * width:3 * width], kg_ref[...])
    v = z[:, 3 * width:]
    q_ref[...] = q.astype(BF16)
    k_ref[...] = k.astype(BF16)
    v_ref[...] = v.astype(BF16)

    @pl.when(pl.program_id(1) == nt - 1)
    def _():
        if len(kk_ref.shape) == 3:
            kt, vt = k.T, v.T
            for hd in range(width // ATT_HEAD_DIM):
                rows = slice(hd * ATT_HEAD_DIM, (hd + 1) * ATT_HEAD_DIM)
                kk_ref[hd] = kt[rows, :]
                vk_ref[hd] = vt[rows, :]
        else:
            kk_ref[...] = k
            vk_ref[...] = v


def _in_proj(x2d, gain, w_bf, qg, kg, hm, *, nb, nt, tm, keep_transposed):
    rows, d = x2d.shape
    width = w_bf.shape[1] // 4
    heads = width // ATT_HEAD_DIM
    tok = lambda b, t: (b * nt + t, 0)
    out_shape = (
        jax.ShapeDtypeStruct((rows, width), F32) if keep_transposed
        else jax.ShapeDtypeStruct((nt * tm, nb * width), BF16),
        jax.ShapeDtypeStruct((rows, width), BF16),
        jax.ShapeDtypeStruct((rows, width), BF16),
        jax.ShapeDtypeStruct((rows, width), BF16),
    )
    if keep_transposed:
        kept = jax.ShapeDtypeStruct((nb * heads, ATT_HEAD_DIM, tm), F32)
        keep = pl.BlockSpec((heads, ATT_HEAD_DIM, tm), lambda b, t: (b, 0, 0))
    else:
        kept = jax.ShapeDtypeStruct((nb * tm, width), F32)
        keep = pl.BlockSpec((tm, width), lambda b, t: (b, 0))
    out_shape = out_shape + (kept, kept)
    return pl.pallas_call(
        functools.partial(_in_proj_kernel, width=width, nt=nt),
        grid=(nb, nt),
        in_specs=[pl.BlockSpec((tm, d), tok), _full((1, d)), _full(w_bf.shape),
                  _full((1, width)), _full((1, width)), _full((width, width))],
        out_specs=(pl.BlockSpec((tm, width), tok if keep_transposed else (lambda b, t: (t, b))),
                   pl.BlockSpec((tm, width), tok), pl.BlockSpec((tm, width), tok),
                   pl.BlockSpec((tm, width), tok),
                   keep, keep),
        out_shape=out_shape,
        compiler_params=_cparams(("arbitrary", "arbitrary")),
        name="in_proj",
    )(x2d, gain, w_bf, qg, kg, hm)


def _ssm_chunk(u, bc_ref, cre_ref, cim_ref, lre_ref, lim_ref, d_ref, sre_ref, sim_ref, bu_ref, *, nb, tc, ns):
    n_tiles = 2 * ns // MXU_N
    for j in range(n_tiles):
        slab = (j % (n_tiles // 2)) // 2
        bu_ref[:, j * MXU_N:(j + 1) * MXU_N] = _dot(u[:, slab * LANES:(slab + 1) * LANES], bc_ref[j])

    cw = 8 * 1024 // nb
    for cb in range(ns // cw):
        c0 = cb * cw
        lre = jnp.broadcast_to(lre_ref[:, c0:c0 + cw], (nb, cw))
        lim = jnp.broadcast_to(lim_ref[:, c0:c0 + cw], (nb, cw))

        def step(t, carry, c0=c0, lre=lre, lim=lim):
            sr, si = carry
            r0 = pl.multiple_of(t * nb, nb)
            nr = lre * sr - lim * si + bu_ref[pl.ds(r0, nb), c0:c0 + cw]
            ni = lre * si + lim * sr + bu_ref[pl.ds(r0, nb), ns + c0:ns + c0 + cw]
            bu_ref[pl.ds(r0, nb), c0:c0 + cw] = nr
            bu_ref[pl.ds(r0, nb), ns + c0:ns + c0 + cw] = ni
            return nr, ni

        sr, si = lax.fori_loop(0, tc, step, (sre_ref[:, c0:c0 + cw], sim_ref[:, c0:c0 + cw]))
        sre_ref[:, c0:c0 + cw] = sr
        sim_ref[:, c0:c0 + cw] = si

    width = u.shape[1]
    kt = ns * MXU_N // width
    ys = []
    for n in range(width // MXU_N):
        s_re = bu_ref[:, n * kt:(n + 1) * kt].astype(BF16)
        s_im = bu_ref[:, ns + n * kt:ns + (n + 1) * kt].astype(BF16)
        cols = slice(n * MXU_N, (n + 1) * MXU_N)
        ys.append(_dot(s_re, cre_ref[n]) + _dot(s_im, cim_ref[n]) + d_ref[:, cols] * u[:, cols].astype(F32))
    return jnp.concatenate(ys, axis=1)


def _ssm_kernel(u_ref, bc_ref, cre_ref, cim_ref, lre_ref, lim_ref, d_ref, s0re_ref, s0im_ref,
                y_ref, sre_ref, sim_ref, bu_ref, *, nb, tc, ns):
    @pl.when(pl.program_id(0) == 0)
    def _():
        sre_ref[...] = s0re_ref[...]
        sim_ref[...] = s0im_ref[...]

    y_ref[...] = _ssm_chunk(u_ref[...], bc_ref, cre_ref, cim_ref, lre_ref, lim_ref, d_ref,
                            sre_ref, sim_ref, bu_ref, nb=nb, tc=tc, ns=ns)


def _ssm(u_rows, bc, cre, cim, lre, lim, dskip, s0re, s0im, *, nb, tc):
    rows, width = u_rows.shape
    ns = lre.shape[1]
    r = tc * nb
    return pl.pallas_call(
        functools.partial(_ssm_kernel, nb=nb, tc=tc, ns=ns),
        grid=(rows // r,),
        in_specs=[pl.BlockSpec((r, width), lambda i: (i, 0)), _full(bc.shape), _full(cre.shape),
                  _full(cim.shape), _full((1, ns)), _full((1, ns)), _full((1, width)),
                  _full((nb, ns)), _full((nb, ns))],
        out_specs=(pl.BlockSpec((r, width), lambda i: (i, 0)), _full((nb, ns)), _full((nb, ns))),
        out_shape=(jax.ShapeDtypeStruct((rows, width), F32),
                   jax.ShapeDtypeStruct((nb, ns), F32), jax.ShapeDtypeStruct((nb, ns), F32)),
        scratch_shapes=[pltpu.VMEM((r, 2 * ns), F32)],
        compiler_params=_cparams(("arbitrary",)),
        name="ssm",
    )(u_rows, bc, cre, cim, lre, lim, dskip, s0re, s0im)


def _ssm_tokens_kernel(u_hbm, bc_ref, cre_ref, cim_ref, lre_ref, lim_ref, d_ref,
                       y_hbm, sre_ref, sim_ref, bu_ref, ubuf, ybuf, isem, osem, *, nb, tc, ns, t_len):
    i = pl.program_id(0)
    n_steps = t_len // tc
    slot = i % 2
    width = ubuf.shape[3]

    def copy_in(step, s, b):
        return pltpu.make_async_copy(u_hbm.at[pl.ds(b * t_len + step * tc, tc)], ubuf.at[s, :, b, :], isem.at[s])

    def copy_out(step, s, b):
        return pltpu.make_async_copy(ybuf.at[s, :, b, :], y_hbm.at[pl.ds(b * t_len + step * tc, tc)], osem.at[s])

    @pl.when(i == 0)
    def _():
        sre_ref[...] = jnp.zeros_like(sre_ref)
        sim_ref[...] = jnp.zeros_like(sim_ref)
        for b in range(nb):
            copy_in(0, 0, b).start()

    @pl.when(i + 1 < n_steps)
    def _():
        for b in range(nb):
            copy_in(i + 1, 1 - slot, b).start()

    for b in range(nb):
        copy_in(i, slot, b).wait()
    u = ubuf[slot].reshape(tc * nb, width).astype(BF16)
    y = _ssm_chunk(u, bc_ref, cre_ref, cim_ref, lre_ref, lim_ref, d_ref, sre_ref, sim_ref, bu_ref,
                   nb=nb, tc=tc, ns=ns)

    @pl.when(i >= 2)
    def _():
        for b in range(nb):
            copy_out(i - 2, slot, b).wait()

    ybuf[slot] = y.reshape(tc, nb, width)
    for b in range(nb):
        copy_out(i, slot, b).start()

    @pl.when(i == n_steps - 1)
    def _():
        for b in range(nb):
            copy_out(i, slot, b).wait()
        if n_steps > 1:
            for b in range(nb):
                copy_out(i - 1, 1 - slot, b).wait()


def _ssm_tokens(u_tok, bc, cre, cim, lre, lim, dskip, *, nb, tc):
    rows, width = u_tok.shape
    ns = lre.shape[1]
    t_len = rows // nb
    r = tc * nb
    any_spec = pl.BlockSpec(memory_space=pl.ANY)
    return pl.pallas_call(
        functools.partial(_ssm_tokens_kernel, nb=nb, tc=tc, ns=ns, t_len=t_len),
        grid=(t_len // tc,),
        in_specs=[any_spec, _full(bc.shape), _full(cre.shape), _full(cim.shape), _full((1, ns)),
                  _full((1, ns)), _full((1, width))],
        out_specs=(any_spec, _full((nb, ns)), _full((nb, ns))),
        out_shape=(jax.ShapeDtypeStruct((rows, width), F32),
                   jax.ShapeDtypeStruct((nb, ns), F32), jax.ShapeDtypeStruct((nb, ns), F32)),
        scratch_shapes=[pltpu.VMEM((r, 2 * ns), F32), pltpu.VMEM((2, tc, nb, width), F32),
                        pltpu.VMEM((2, tc, nb, width), F32),
                        pltpu.SemaphoreType.DMA((2,)), pltpu.SemaphoreType.DMA((2,))],
        compiler_params=_cparams(("arbitrary",)),
        name="ssm_tokens",
    )(u_tok, bc, cre, cim, lre, lim, dskip)


def _half_select(shape):
    lane = lax.broadcasted_iota(jnp.int32, shape, 1)
    return lane < ATT_HEAD_DIM


def _head_masks(first):
    m0 = jnp.where(first, 1.0, 0.0).astype(BF16)
    return m0, (1.0 - m0.astype(F32)).astype(BF16)


def _band_prompt_kernel(q_ref, k_ref, v_ref, bias_ref, o_ref, kp_ref, vp_ref, s_scr, p_scr, l_scr, *, t_len):
    width = q_ref.shape[1]
    n_hp = width // LANES
    kp_ref[0:PAD_ROWS, :] = jnp.zeros((PAD_ROWS, width), BF16)
    vp_ref[0:PAD_ROWS, :] = jnp.zeros((PAD_ROWS, width), BF16)
    kp_ref[PAD_ROWS:, :] = k_ref[...]
    vp_ref[PAD_ROWS:, :] = v_ref[...]
    first = _half_select((PAIR_ROWS, LANES))
    head_mask = _head_masks(first)
    col = lax.broadcasted_iota(jnp.int32, (SOFTMAX_ROWS, PAIR_BAND), 1)

    def pair(pi, carry, *, masked):
        r0 = pl.multiple_of(pi * PAIR_ROWS, PAIR_ROWS)
        for hp in range(n_hp):
            lanes = slice(hp * LANES, (hp + 1) * LANES)
            qp = q_ref[pl.ds(r0, PAIR_ROWS), lanes]
            qs = jnp.concatenate([qp * head_mask[0], qp * head_mask[1]], axis=0)
            s_scr[2 * hp * PAIR_ROWS:2 * (hp + 1) * PAIR_ROWS, :] = _dot_nt(
                qs, kp_ref[pl.ds(r0, PAIR_BAND), lanes])
        for h in range(2 * n_hp):
            for rb in range(0, PAIR_ROWS, SOFTMAX_ROWS):
                rows = slice(h * PAIR_ROWS + rb, h * PAIR_ROWS + rb + SOFTMAX_ROWS)
                s = s_scr[rows, :] + bias_ref[h, rb:rb + SOFTMAX_ROWS, :]
                if masked:
                    s = jnp.where(col >= PAD_ROWS - r0, s, NEG_INF)
                p = jnp.exp(s - jnp.max(s, axis=-1, keepdims=True))
                p_scr[rows, :] = p.astype(BF16)
                l_scr[rows, :] = jnp.broadcast_to(1.0 / jnp.sum(p, axis=-1, keepdims=True),
                                                  (SOFTMAX_ROWS, LANES))
        for hp in range(n_hp):
            lanes = slice(hp * LANES, (hp + 1) * LANES)
            rows = slice(2 * hp * PAIR_ROWS, 2 * (hp + 1) * PAIR_ROWS)
            o2 = _dot(p_scr[rows, :], vp_ref[pl.ds(r0, PAIR_BAND), lanes]) * l_scr[rows, :]
            o_ref[pl.ds(r0, PAIR_ROWS), lanes] = jnp.where(
                first, o2[:PAIR_ROWS], o2[PAIR_ROWS:]).astype(BF16)
        return carry

    n_masked = PAD_ROWS // PAIR_ROWS
    lax.fori_loop(0, n_masked, functools.partial(pair, masked=True), 0)
    lax.fori_loop(n_masked, t_len // PAIR_ROWS, functools.partial(pair, masked=False), 0)


def _band_prompt(q, k, v, bias, *, nb, t_len):
    width = q.shape[1]
    heads = width // ATT_HEAD_DIM
    blk = pl.BlockSpec((t_len, width), lambda b: (b, 0))
    return pl.pallas_call(
        functools.partial(_band_prompt_kernel, t_len=t_len),
        grid=(nb,),
        in_specs=[blk, blk, blk, _full(bias.shape)],
        out_specs=blk,
        out_shape=jax.ShapeDtypeStruct(q.shape, BF16),
        scratch_shapes=[pltpu.VMEM((t_len + PAD_ROWS, width), BF16),
                        pltpu.VMEM((t_len + PAD_ROWS, width), BF16),
                        pltpu.VMEM((heads * PAIR_ROWS, PAIR_BAND), F32),
                        pltpu.VMEM((heads * PAIR_ROWS, PAIR_BAND), BF16),
                        pltpu.VMEM((heads * PAIR_ROWS, LANES), F32)],
        compiler_params=_cparams(("arbitrary",)),
        name="band_prompt",
    )(q, k, v, bias)


def _band_sample_kernel(q_ref, k_ref, v_ref, ck_ref, cv_ref, bc_ref, bn_ref, o_ref):
    for h in range(ck_ref.shape[0]):
        cols = slice(h * ATT_HEAD_DIM, (h + 1) * ATT_HEAD_DIM)
        qh = q_ref[:, cols]
        sc = _dot(qh, ck_ref[h].astype(BF16)) + bc_ref[h]
        sn = _dot_nt(qh, k_ref[:, cols]) + bn_ref[h]
        m = jnp.maximum(jnp.max(sc, axis=-1, keepdims=True), jnp.max(sn, axis=-1, keepdims=True))
        pc = jnp.exp(sc - m)
        pn = jnp.exp(sn - m)
        l = jnp.sum(pc, axis=-1, keepdims=True) + jnp.sum(pn, axis=-1, keepdims=True)
        o = _dot_nt(pc.astype(BF16), cv_ref[h].astype(BF16)) + _dot(pn.astype(BF16), v_ref[:, cols])
        o_ref[:, cols] = (o / l).astype(BF16)


def _band_sample(q_tm, k_tm, v_tm, cache_kt, cache_vt, bias_c, bias_n, *, nb, s_len):
    width = q_tm.shape[1] // nb
    heads = cache_kt.shape[0] // nb
    col = pl.BlockSpec((s_len, width), lambda b: (0, b))
    cache = pl.BlockSpec((heads,) + cache_kt.shape[1:], lambda b: (b, 0, 0))
    return pl.pallas_call(
        _band_sample_kernel,
        grid=(nb,),
        in_specs=[col, col, col, cache, cache, _full(bias_c.shape), _full(bias_n.shape)],
        out_specs=col,
        out_shape=jax.ShapeDtypeStruct(q_tm.shape, BF16),
        compiler_params=_cparams(("arbitrary",)),
        name="band_sample",
    )(q_tm, k_tm, v_tm, cache_kt, cache_vt, bias_c, bias_n)


def _mem_kv_kernel(m_ref, g_ref, wk_ref, wv_ref, kg_ref, k_ref, v_ref, kb_ref, vb_ref):
    m = _rms(m_ref[...], g_ref[...]).astype(BF16)
    k = _dot(m, wk_ref[...])
    v = _dot(m, wv_ref[...])
    hd = kg_ref.shape[1]
    for h in range(k.shape[1] // hd):
        cols = slice(h * hd, (h + 1) * hd)
        kh = _rms(k[:, cols], kg_ref[...])
        k_ref[:, h, :] = kh
        v_ref[:, h, :] = v[:, cols]
        kb_ref[:, cols] = kh.astype(BF16)
    vb_ref[...] = v.astype(BF16)


def _mem_kv(mem2d, gain, wk, wv, kgain, *, tm):
    rows, d = mem2d.shape
    hd = kgain.shape[1]
    blk = pl.BlockSpec((tm, d), lambda i: (i, 0))
    hblk = pl.BlockSpec((tm, d // hd, hd), lambda i: (i, 0, 0))
    return pl.pallas_call(
        _mem_kv_kernel,
        grid=(rows // tm,),
        in_specs=[blk, _full((1, d)), _full(wk.shape), _full(wv.shape), _full(kgain.shape)],
        out_specs=(hblk, hblk, blk, blk),
        out_shape=(jax.ShapeDtypeStruct((rows, d // hd, hd), F32),
                   jax.ShapeDtypeStruct((rows, d // hd, hd), F32),
                   jax.ShapeDtypeStruct((rows, d), BF16), jax.ShapeDtypeStruct((rows, d), BF16)),
        compiler_params=_cparams(("arbitrary",)),
        name="mem_kv",
    )(mem2d, gain, wk, wv, kgain)


def _route(logits):
    lane = lax.broadcasted_iota(jnp.int32, logits.shape, 1).astype(F32)
    big = float(ROUTER_LANES)
    is_g = lane < N_GROUPS
    lg = jnp.where(is_g, logits, NEG_INF)
    gmax = jnp.max(lg, axis=-1, keepdims=True)
    p1 = 1.0 / jnp.sum(jnp.where(is_g, jnp.exp(lg - gmax), 0.0), axis=-1, keepdims=True)
    g_idx = jnp.min(jnp.where(lg == gmax, lane, big), axis=-1, keepdims=True)
    lo = ROUTER_E0 + g_idx * EXPERTS_PER_GROUP
    le = jnp.where((lane >= lo) & (lane < lo + EXPERTS_PER_GROUP), logits, NEG_INF)
    v1 = jnp.max(le, axis=-1, keepdims=True)
    i1 = jnp.min(jnp.where(le == v1, lane, big), axis=-1, keepdims=True)
    le2 = jnp.where(lane == i1, NEG_INF, le)
    v2 = jnp.max(le2, axis=-1, keepdims=True)
    i2 = jnp.min(jnp.where(le2 == v2, lane, big), axis=-1, keepdims=True)
    e2 = jnp.exp(v2 - v1)
    w1 = p1 / (1.0 + e2)
    w2 = p1 * e2 / (1.0 + e2)
    gates = jnp.where(lane == i1, w1, 0.0) + jnp.where(lane == i2, w2, 0.0)
    a = jnp.minimum(i1, i2) - lo
    b = jnp.maximum(i1, i2) - lo
    pair = a * (2 * EXPERTS_PER_GROUP - 1 - a) * 0.5 + (b - a - 1.0)
    cls = g_idx * PAIRS_PER_GROUP + pair
    w_a = jnp.where(i1 < i2, w1, w2)
    w_b = jnp.where(i1 < i2, w2, w1)
    return (gates + jnp.where(lane == ROUTE_CLS, cls, 0.0) + jnp.where(lane == ROUTE_WA, w_a, 0.0)
            + jnp.where(lane == ROUTE_WB, w_b, 0.0))


def _mix_mem_kernel(x_ref, ys_ref, ya_ref, mk_ref, mv_ref, wglu_ref, ons_ref, ona_ref, wout_ref,
                    nmem_ref, wq_ref, qg_ref, wo_ref, nffn_ref, wrh_ref, wrl_ref, rb_ref,
                    *outs, routed, nbat):
    if routed:
        x2_ref, o_scr = outs
    else:
        x2_ref, xn_ref, gate_ref, o_scr = outs
    tm = x_ref.shape[0]
    d = x_ref.shape[1] // nbat
    width = ys_ref.shape[1] // nbat
    n_mem = mk_ref.shape[0] // nbat

    def stacked(ref, w):
        return jnp.concatenate([ref[:, j * w:(j + 1) * w] for j in range(nbat)], axis=0)

    g = _dot(_gelu_tanh(stacked(ys_ref, width)).astype(BF16), wglu_ref[...])
    y_s = g[:, :width] * _sigmoid(g[:, width:])
    cat_s = _rms(y_s, ons_ref[...]).astype(BF16)
    cat_a = _rms(stacked(ya_ref, width).astype(F32), ona_ref[...]).astype(BF16)
    x1 = stacked(x_ref, d) + _dot(cat_s, wout_ref[0:width, :]) + _dot(cat_a, wout_ref[width:, :])

    q = _dot(_rms(x1, nmem_ref[...]).astype(BF16), wq_ref[...])
    hd = qg_ref.shape[1]
    scale = hd ** -0.5
    for j in range(nbat):
        rows = slice(j * tm, (j + 1) * tm)
        mem = slice(j * n_mem, (j + 1) * n_mem)
        for h in range(d // hd):
            cols = slice(h * hd, (h + 1) * hd)
            qh = _rms(q[rows, cols], qg_ref[...]).astype(BF16)
            s = _dot_nt(qh, mk_ref[mem, cols].astype(BF16)) * scale
            p = jnp.exp(s - jnp.max(s, axis=-1, keepdims=True))
            l = jnp.sum(p, axis=-1, keepdims=True)
            o_scr[rows, cols] = (_dot(p.astype(BF16), mv_ref[mem, cols].astype(BF16)) / l).astype(BF16)
    acc = x1 + _dot(o_scr[...], wo_ref[...])

    xn = _rms(acc, nffn_ref[...])
    xh, xl = _split_bf16(xn)
    logits = _dot(xh, wrh_ref[...]) + _dot(xl, wrh_ref[...]) + _dot(xh, wrl_ref[...]) + rb_ref[...]
    record = _route(logits)
    if routed:
        x2_ref[:, :d] = acc
        x2_ref[:, d:] = record
    else:
        for j in range(nbat):
            rows = slice(j * tm, (j + 1) * tm)
            x2_ref[:, j * d:(j + 1) * d] = acc[rows]
            xn_ref[:, j * d:(j + 1) * d] = xh[rows]
            gate_ref[:, j * ROUTER_LANES:(j + 1) * ROUTER_LANES] = record[rows]


def _mix_mem(x, ys, ya, mk, mv, weights, *, grid, tm, row_map, ssm_map, mem_map, routed, nbat=1):
    d = weights["w_out"].shape[1]
    width = weights["w_glu"].shape[0]
    names = ("w_glu", "out_norm_ssm", "out_norm_att", "w_out", "norm_mem", "w_mem_q", "mem_q_gain",
             "w_mem_o", "norm_ffn", "w_router_hi", "w_router_lo", "router_bias")
    ws = [weights[n] for n in names]
    mem_rows = weights["n_mem"]
    xspec = pl.BlockSpec((tm, nbat * d), row_map)
    hspec = pl.BlockSpec((tm, nbat * width), row_map)
    sspec = pl.BlockSpec((tm, nbat * width), ssm_map)
    mspec = pl.BlockSpec((nbat * mem_rows, d), mem_map)
    n_col = x.shape[1] // d
    if routed:
        assert nbat == 1
        out_specs = pl.BlockSpec((tm, d + ROUTER_LANES), row_map)
        out_shape = jax.ShapeDtypeStruct((x.shape[0], n_col * (d + ROUTER_LANES)), F32)
    else:
        out_specs = (xspec, xspec, pl.BlockSpec((tm, nbat * ROUTER_LANES), row_map))
        out_shape = (jax.ShapeDtypeStruct(x.shape, F32), jax.ShapeDtypeStruct(x.shape, BF16),
                     jax.ShapeDtypeStruct((x.shape[0], n_col * ROUTER_LANES), F32))
    return pl.pallas_call(
        functools.partial(_mix_mem_kernel, routed=routed, nbat=nbat),
        grid=grid,
        in_specs=[xspec, sspec, hspec, mspec, mspec] + [_full(w.shape) for w in ws],
        out_specs=out_specs,
        out_shape=out_shape,
        scratch_shapes=[pltpu.VMEM((nbat * tm, d), BF16)],
        compiler_params=_cparams(("arbitrary",) * len(grid)),
        name="mix_mem",
    )(x, ys, ya, mk, mv, *ws)


def _moe_kernel(xn_ref, x2_ref, gate_ref, wg_ref, wu_ref, wd_ref, o_ref):
    e = pl.program_id(1)

    @pl.when(e == 0)
    def _():
        o_ref[...] = x2_ref[...]

    gates = gate_ref[...]
    lane = lax.broadcasted_iota(jnp.int32, gates.shape, 1)
    ge = jnp.sum(jnp.where(lane == e + ROUTER_E0, gates, 0.0), axis=-1, keepdims=True)
    xn = xn_ref[...]
    a = _dot(xn, wg_ref[...])
    h = a * _sigmoid(a) * _dot(xn, wu_ref[...])
    o_ref[...] += ge * _dot(h.astype(BF16), wd_ref[...])


def _moe(xn, x2, gates, wg, wu, wd, *, tm):
    rows, d = xn.shape
    n_exp, _, dff = wg.shape
    row = lambda i, e: (i, 0)
    return pl.pallas_call(
        _moe_kernel,
        grid=(rows // tm, n_exp),
        in_specs=[pl.BlockSpec((tm, d), row), pl.BlockSpec((tm, d), row),
                  pl.BlockSpec((tm, ROUTER_LANES), row),
                  pl.BlockSpec((None, d, dff), lambda i, e: (e, 0, 0)),
                  pl.BlockSpec((None, d, dff), lambda i, e: (e, 0, 0)),
                  pl.BlockSpec((None, dff, d), lambda i, e: (e, 0, 0))],
        out_specs=pl.BlockSpec((tm, d), row),
        out_shape=jax.ShapeDtypeStruct((rows, d), F32),
        compiler_params=_cparams(("arbitrary", "arbitrary")),
        name="moe",
    )(xn, x2, gates, wg, wu, wd)


def _moe_routed_kernel(tile_ref, ea_ref, eb_ref, lo_ref, hi_ref, valid_ref, tok_ref,
                       x_hbm, nffn_ref, wga_ref, wua_ref, wda_ref, wgb_ref, wub_ref, wdb_ref,
                       y_hbm, *scratch, tm, n_tiles):
    nb = MOE_BUFS
    xbuf, obuf = scratch[:nb], scratch[nb:2 * nb]
    acc_ref, gsem, ssem = scratch[2 * nb:]
    w = pl.program_id(0)
    t = tile_ref[w]
    lo = lo_ref[w]
    hi = hi_ref[w]
    valid = valid_ref[w] == 1
    d = y_hbm.shape[1]

    def row_copy_in(tile, s, r):
        tok = tok_ref[tile * tm + r]
        return pltpu.make_async_copy(x_hbm.at[pl.ds(tok, 1)], xbuf[s].at[pl.ds(r, 1)], gsem.at[s])

    def row_copy_out(tile, s, r):
        tok = tok_ref[tile * tm + r]
        return pltpu.make_async_copy(obuf[s].at[pl.ds(r, 1)], y_hbm.at[pl.ds(tok, 1)], ssem.at[s])

    def start_gather(tile, s):
        for r in range(tm):
            row_copy_in(tile, s, r).start()

    def start_scatter(tile, s):
        for r in range(tm):
            row_copy_out(tile, s, r).start()

    def wait_gather(s):
        pltpu.make_async_copy(x_hbm.at[pl.ds(0, tm)], xbuf[s], gsem.at[s]).wait()

    def wait_scatter(s):
        pltpu.make_async_copy(obuf[s], y_hbm.at[pl.ds(0, tm)], ssem.at[s]).wait()

    def compute(s, first):
        xe = xbuf[s][...]
        xn = _rms(xe[:, :d], nffn_ref[...]).astype(BF16)
        row = lax.broadcasted_iota(jnp.int32, (tm, 1), 0)
        inseg = jnp.logical_and(row >= lo, row < hi)
        w_a = jnp.where(inseg, xe[:, d + ROUTE_WA:d + ROUTE_WA + 1], 0.0)
        w_b = jnp.where(inseg, xe[:, d + ROUTE_WB:d + ROUTE_WB + 1], 0.0)
        a = _dot(xn, wga_ref[...])
        h_a = (a * _sigmoid(a) * _dot(xn, wua_ref[...])).astype(BF16)
        b = _dot(xn, wgb_ref[...])
        h_b = (b * _sigmoid(b) * _dot(xn, wub_ref[...])).astype(BF16)
        upd = w_a * _dot(h_a, wda_ref[...]) + w_b * _dot(h_b, wdb_ref[...])
        if first:
            acc_ref[...] = upd
        else:
            acc_ref[...] += upd

    def when(*conds):
        c = conds[0]
        for extra in conds[1:]:
            c = jnp.logical_and(c, extra)
        return pl.when(c)

    ahead = nb - 1

    @pl.when(w == 0)
    def _():
        for k in range(ahead):
            start_gather(k, k)

    is_first = jnp.logical_and(valid, lo == 0)
    is_last = jnp.logical_and(valid, hi == tm)
    last_tile = n_tiles - 1
    for s in range(nb):
        mine = t % nb == s
        nxt, prv = (s + ahead) % nb, (s - 1) % nb

        if s == 0:
            @when(is_first, t == 0)
            def _(nxt=nxt):
                wait_gather(0)
                start_gather(ahead, nxt)
                compute(0, True)

        @when(is_first, mine, t > 0, t + ahead <= last_tile)
        def _(s=s, nxt=nxt, prv=prv):
            wait_gather(s)
            start_gather(t + ahead, nxt)
            start_scatter(t - 1, prv)
            compute(s, True)

        @when(is_first, mine, t + ahead > last_tile)
        def _(s=s, prv=prv):
            wait_gather(s)
            start_scatter(t - 1, prv)
            compute(s, True)

        @when(valid, mine, lo > 0)
        def _(s=s):
            compute(s, False)

        @when(is_last, mine)
        def _(s=s):
            @pl.when(t >= nb)
            def _():
                wait_scatter(s)

            obuf[s][...] = xbuf[s][:, :d] + acc_ref[...]

            if s == last_tile % nb:
                @pl.when(t == last_tile)
                def _():
                    start_scatter(t, s)
                    for k in range(nb):
                        wait_scatter(k)


def _moe_routed(x2ext, plan, nffn, wg, wu, wd, *, tm):
    rows, de = x2ext.shape
    d = de - ROUTER_LANES
    n_exp, _, dff = wg.shape
    n_tiles = rows // tm
    assert rows % tm == 0 and n_tiles > 2 * MOE_BUFS
    n_items = plan[0].shape[0]
    ea = lambda w, tile, ea_, eb_, *_: (ea_[w], 0, 0)
    eb = lambda w, tile, ea_, eb_, *_: (eb_[w], 0, 0)
    grid_spec = pltpu.PrefetchScalarGridSpec(
        num_scalar_prefetch=len(plan),
        grid=(n_items,),
        in_specs=[pl.BlockSpec(memory_space=pl.ANY),
                  pl.BlockSpec((1, d), lambda w, *_: (0, 0)),
                  pl.BlockSpec((None, d, dff), ea), pl.BlockSpec((None, d, dff), ea),
                  pl.BlockSpec((None, dff, d), ea),
                  pl.BlockSpec((None, d, dff), eb), pl.BlockSpec((None, d, dff), eb),
                  pl.BlockSpec((None, dff, d), eb)],
        out_specs=pl.BlockSpec(memory_space=pl.ANY),
        scratch_shapes=([pltpu.VMEM((tm, de), F32)] * MOE_BUFS + [pltpu.VMEM((tm, d), F32)] * MOE_BUFS
                        + [pltpu.VMEM((tm, d), F32),
                           pltpu.SemaphoreType.DMA((MOE_BUFS,)), pltpu.SemaphoreType.DMA((MOE_BUFS,))]),
    )
    return pl.pallas_call(
        functools.partial(_moe_routed_kernel, tm=tm, n_tiles=n_tiles),
        grid_spec=grid_spec,
        out_shape=jax.ShapeDtypeStruct((rows, d), F32),
        compiler_params=_cparams(("arbitrary",)),
        name="moe_routed",
    )(*plan, x2ext, nffn, wg, wu, wd, wg, wu, wd)


def _route_plan(cls, *, tm):
    n = cls.shape[0]
    n_tiles = n // tm
    order = jnp.argsort(cls).astype(jnp.int32)
    classes = jnp.arange(N_CLASSES, dtype=jnp.int32)
    class_end = jnp.sum((cls[:, None] <= classes[None, :]).astype(jnp.int32), axis=0)
    class_start = jnp.concatenate([jnp.zeros((1,), jnp.int32), class_end[:-1]])
    bounds = jnp.concatenate([jnp.arange(n_tiles, dtype=jnp.int32) * tm,
                              jnp.where(class_end > class_start, class_start, n)])
    n_items = bounds.shape[0]
    idx = jnp.arange(n_items, dtype=jnp.int32)
    before = jnp.logical_or(bounds[None, :] < bounds[:, None],
                            jnp.logical_and(bounds[None, :] == bounds[:, None], idx[None, :] < idx[:, None]))
    rank = jnp.sum(before.astype(jnp.int32), axis=1)
    start = jnp.sum(jnp.where(rank[:, None] == idx[None, :], bounds[:, None], 0), axis=0)
    stop = jnp.concatenate([start[1:], jnp.full((1,), n, jnp.int32)])
    valid = stop > start
    tile = jnp.minimum(start // tm, n_tiles - 1)
    lo = start - tile * tm
    hi = stop - tile * tm
    c = jnp.minimum(jnp.sum((class_end[None, :] <= start[:, None]).astype(jnp.int32), axis=1),
                    N_CLASSES - 1)
    g = c // PAIRS_PER_GROUP
    pair = c % PAIRS_PER_GROUP
    a = jnp.zeros_like(pair)
    for k in range(1, EXPERTS_PER_GROUP - 1):
        a = a + (pair >= k * (2 * EXPERTS_PER_GROUP - 1 - k) // 2).astype(jnp.int32)
    b = pair - a * (2 * EXPERTS_PER_GROUP - 1 - a) // 2 + a + 1
    e_a = g * EXPERTS_PER_GROUP + a
    e_b = g * EXPERTS_PER_GROUP + b
    i32 = lambda v: v.astype(jnp.int32)
    return (i32(tile), i32(e_a), i32(e_b), i32(lo), i32(hi), i32(valid), order)


def _ssm_params(lam_re, lam_im, log_step, b_re, b_im, c_re, c_im):
    n_g, n_p = lam_re.shape
    step = jnp.exp(log_step.astype(F32))[:, None]
    mag = jnp.exp(lam_re * step)
    lb_re = mag * jnp.cos(lam_im * step)
    lb_im = mag * jnp.sin(lam_im * step)
    den = lam_re * lam_re + lam_im * lam_im
    f_re = ((lb_re - 1.0) * lam_re + lb_im * lam_im) / den
    f_im = (lb_im * lam_re - (lb_re - 1.0) * lam_im) / den
    bb_re = f_re[..., None] * b_re - f_im[..., None] * b_im
    bb_im = f_re[..., None] * b_im + f_im[..., None] * b_re
    eye = jnp.eye(n_g, dtype=F32)
    ns = n_g * n_p
    width = n_g * SSM_GROUP
    b_full = jnp.concatenate(
        [jnp.einsum("hg,gpc->hcgp", eye, bb_re).reshape(width, ns),
         jnp.einsum("hg,gpc->hcgp", eye, bb_im).reshape(width, ns)], axis=1)
    n_tiles = 2 * ns // MXU_N
    bc = jnp.stack([
        b_full[((j % (n_tiles // 2)) // 2) * LANES:((j % (n_tiles // 2)) // 2 + 1) * LANES,
               j * MXU_N:(j + 1) * MXU_N] for j in range(n_tiles)]).astype(BF16)
    c_full_re = jnp.einsum("gh,gcp->gphc", eye, c_re).reshape(ns, width)
    c_full_im = -jnp.einsum("gh,gcp->gphc", eye, c_im).reshape(ns, width)
    kt = ns * MXU_N // width
    tiles = range(width // MXU_N)
    cre = jnp.stack([c_full_re[n * kt:(n + 1) * kt, n * MXU_N:(n + 1) * MXU_N] for n in tiles]).astype(BF16)
    cim = jnp.stack([c_full_im[n * kt:(n + 1) * kt, n * MXU_N:(n + 1) * MXU_N] for n in tiles]).astype(BF16)
    return bc, cre, cim, lb_re.reshape(1, ns), lb_im.reshape(1, ns)


def _rel_bias(rel_bias, q0, n_q, n_k):
    n_r = n_q + n_k - 1
    dist = q0 + n_q - 1 - jnp.arange(n_r)
    r = rel_bias.astype(F32)[:, jnp.clip(dist, -REL_CLIP, REL_CLIP) + REL_CLIP]
    r = jnp.pad(r, ((0, 0), (0, 1)))
    rows = jnp.tile(r, (1, n_q))[:, :n_q * n_r].reshape(-1, n_q, n_r)
    return rows[:, :, n_q - 1:n_q - 1 + n_k]


def _layer(xp, xs, mem_p, ck, cv, s_re, s_im, cmk, cmv, w, dims):
    batch, seq, dec_batch, dec_seq, d = dims
    width = w["w_in"].shape[1] // 4
    ns = w["lam_re"].shape[1]
    n_mem = w["n_mem"]
    tm_p = 512
    nt_p = seq // tm_p
    n_dec = dec_batch * dec_seq

    u, q, k, v, kk, vk = _in_proj(xp, w["norm_mix"], w["w_in"], w["qg"], w["kg"], w["head_mean"],
                                  nb=batch, nt=nt_p, tm=tm_p, keep_transposed=True)
    y_ssm, pre, pim = _ssm_tokens(u, w["bc"], w["cre"], w["cim"], w["lam_re"], w["lam_im"], w["ssm_d"],
                                  nb=batch, tc=32)
    y_att = _band_prompt(q, k, v, w["bias_p"], nb=batch, t_len=seq)
    mk, mv, mkb, mvb = _mem_kv(mem_p, w["mem_in_norm"], w["w_mem_k"], w["w_mem_v"], w["mem_k_gain"], tm=512)
    tm_d = 512
    nt_d = seq // tm_d
    x2ext = _mix_mem(xp, y_ssm, y_att, mkb, mvb, w,
                     grid=(batch, nt_d), tm=tm_d,
                     row_map=lambda b, t: (b * nt_d + t, 0), ssm_map=lambda b, t: (b * nt_d + t, 0),
                     mem_map=lambda b, t: (b, 0), routed=True)
    plan = _route_plan(x2ext[:, d + ROUTE_CLS].astype(jnp.int32), tm=MOE_TILE)
    yp = _moe_routed(x2ext, plan, w["norm_ffn"], w["exp_w_gate"], w["exp_w_up"], w["exp_w_down"],
                     tm=MOE_TILE)

    us, qs, ks, vs, kks, vks = _in_proj(xs, w["norm_mix"], w["w_in"], w["qg"], w["kg"], w["head_mean"],
                                        nb=1, nt=1, tm=n_dec, keep_transposed=False)
    ys_ssm, sre, sim = _ssm(us, w["bc"], w["cre"], w["cim"], w["lam_re"], w["lam_im"], w["ssm_d"],
                            s_re, s_im, nb=dec_batch, tc=dec_seq)
    tmv = lambda a: a.reshape(dec_seq, dec_batch * a.shape[1])
    ys_att = _band_sample(tmv(qs), tmv(ks), tmv(vs), ck, cv, w["bias_sc"], w["bias_sn"],
                          nb=dec_batch, s_len=dec_seq)
    x2s, xns, gates_s = _mix_mem(tmv(xs), tmv(ys_ssm), ys_att, cmk, cmv, w,
                                 grid=(dec_batch // SAMPLE_STREAMS,), tm=dec_seq,
                                 row_map=lambda b: (0, b), ssm_map=lambda b: (0, b),
                                 mem_map=lambda b: (b, 0), routed=False, nbat=SAMPLE_STREAMS)
    ys = _moe(xns.reshape(n_dec, d), x2s.reshape(n_dec, d), gates_s.reshape(n_dec, ROUTER_LANES),
              w["exp_w_gate"], w["exp_w_up"], w["exp_w_down"], tm=n_dec)
    return yp, ys, (kk, vk, pre, pim, mk, mv), (kks, vks, sre, sim)


def kernel(x_prompt, x_sample, mem_prompt, cache_attn_k, cache_attn_v, state_ssm_re, state_ssm_im, cache_mem_k, cache_mem_v, norm_mix, w_in, ssm_lambda_re, ssm_lambda_im, ssm_log_step, ssm_b_re, ssm_b_im, ssm_c_re, ssm_c_im, ssm_d, w_glu, att_q_gain, att_k_gain, att_rel_bias, out_norm_ssm, out_norm_att, w_out, norm_mem, mem_in_norm, w_mem_q, w_mem_k, w_mem_v, w_mem_o, mem_q_gain, mem_k_gain, norm_ffn, router_g_w, router_g_b, router_e_w, router_e_b, exp_w_gate, exp_w_up, exp_w_down):
    depth = norm_mix.shape[0]
    batch, seq, d = x_prompt.shape
    dec_batch, dec_seq, _ = x_sample.shape
    n_mem = mem_prompt.shape[1]
    att_rows = cache_attn_k.shape[2]
    n_g, n_p = ssm_lambda_re.shape[1:]
    width = n_g * SSM_GROUP
    heads = width // ATT_HEAD_DIM
    ns = n_g * n_p
    assert seq % 512 == 0 and att_rows == PAD_ROWS and seq >= PAD_ROWS
    assert (dec_batch * dec_seq) % 8 == 0 and dec_seq % 16 == 0

    xp = x_prompt.reshape(batch * seq, d)
    xs = x_sample.transpose(1, 0, 2).reshape(dec_seq * dec_batch, d)
    mem_p = mem_prompt.reshape(batch * n_mem, d)
    row = lambda a: a.reshape(1, -1).astype(F32)
    head_mean = jnp.kron(jnp.eye(heads, dtype=F32),
                         jnp.full((ATT_HEAD_DIM, ATT_HEAD_DIM), 1.0 / ATT_HEAD_DIM, F32)).astype(BF16)
    pr = jnp.arange(PAIR_ROWS)[:, None]
    pc = jnp.arange(PAIR_BAND)[None, :]
    pair_ok = jnp.where(pr < CHUNK, pc < BAND, pc >= CHUNK)
    att_scale = ATT_HEAD_DIM ** -0.5

    p_out, s_out = [], []
    for l in range(depth):
        bc, cre, cim, lb_re, lb_im = _ssm_params(ssm_lambda_re[l], ssm_lambda_im[l], ssm_log_step[l],
                                                 ssm_b_re[l], ssm_b_im[l], ssm_c_re[l], ssm_c_im[l])
        w_router = jnp.concatenate(
            [router_g_w[l], router_e_w[l].transpose(1, 0, 2).reshape(d, N_GROUPS * EXPERTS_PER_GROUP),
             jnp.zeros((d, ROUTER_LANES - N_GROUPS * (1 + EXPERTS_PER_GROUP)), F32)], axis=1)
        wr_hi, wr_lo = _split_bf16(w_router)
        r_bias = jnp.concatenate(
            [router_g_b[l], router_e_b[l].reshape(-1),
             jnp.zeros((ROUTER_LANES - N_GROUPS * (1 + EXPERTS_PER_GROUP),), F32)]).reshape(1, ROUTER_LANES)
        bias_s = _rel_bias(att_rel_bias[l], att_rows, dec_seq, att_rows + dec_seq)
        bias_p = jnp.where(pair_ok, _rel_bias(att_rel_bias[l], PAD_ROWS, PAIR_ROWS, PAIR_BAND), NEG_INF)
        w = dict(
            n_mem=n_mem,
            norm_mix=row(norm_mix[l]), w_in=w_in[l].astype(BF16),
            qg=row(jnp.tile(att_q_gain[l], heads) * att_scale), kg=row(jnp.tile(att_k_gain[l], heads)),
            head_mean=head_mean, bc=bc, cre=cre, cim=cim, lam_re=lb_re, lam_im=lb_im,
            ssm_d=row(ssm_d[l]), bias_p=bias_p,
            bias_sc=bias_s[:, :, :att_rows], bias_sn=bias_s[:, :, att_rows:],
            mem_in_norm=row(mem_in_norm[l]), w_mem_k=w_mem_k[l].astype(BF16),
            w_mem_v=w_mem_v[l].astype(BF16), mem_k_gain=row(mem_k_gain[l]),
            w_glu=w_glu[l].astype(BF16), out_norm_ssm=row(out_norm_ssm[l]),
            out_norm_att=row(out_norm_att[l]), w_out=w_out[l].astype(BF16), norm_mem=row(norm_mem[l]),
            w_mem_q=w_mem_q[l].astype(BF16), mem_q_gain=row(mem_q_gain[l]),
            w_mem_o=w_mem_o[l].astype(BF16), norm_ffn=row(norm_ffn[l]),
            w_router_hi=wr_hi, w_router_lo=wr_lo, router_bias=r_bias,
            exp_w_gate=exp_w_gate[l].astype(BF16), exp_w_up=exp_w_up[l].astype(BF16),
            exp_w_down=exp_w_down[l].astype(BF16),
        )
        xp, xs, p_new, s_new = _layer(
            xp, xs, mem_p,
            cache_attn_k[l].transpose(0, 2, 3, 1).reshape(dec_batch * heads, ATT_HEAD_DIM, att_rows),
            cache_attn_v[l].transpose(0, 2, 3, 1).reshape(dec_batch * heads, ATT_HEAD_DIM, att_rows),
            state_ssm_re[l].reshape(dec_batch, ns), state_ssm_im[l].reshape(dec_batch, ns),
            cache_mem_k[l].reshape(dec_batch * n_mem, d), cache_mem_v[l].reshape(dec_batch * n_mem, d),
            w, (batch, seq, dec_batch, dec_seq, d))
        p_out.append(p_new)
        s_out.append(s_new)

    sdt = state_ssm_re.dtype
    keep = min(PAD_ROWS, seq)
    kv_p = lambda a: a.reshape(batch, heads, ATT_HEAD_DIM, keep).transpose(0, 3, 1, 2)
    kv_s = lambda a: a.reshape(dec_seq, dec_batch, heads, ATT_HEAD_DIM).transpose(1, 0, 2, 3)
    st = lambda a: a.reshape(a.shape[0], n_g, n_p).astype(sdt)
    mkv = lambda a: a.reshape(batch, n_mem, MEM_HEADS, d // MEM_HEADS)
    stack = lambda f, outs, i: jnp.stack([f(o[i]) for o in outs])
    yp = xp.reshape(batch, seq, d)
    ys = xs.reshape(dec_seq, dec_batch, d).transpose(1, 0, 2)
    return (yp, ys,
            stack(kv_p, p_out, 0), stack(kv_p, p_out, 1), stack(st, p_out, 2), stack(st, p_out, 3),
            stack(mkv, p_out, 4), stack(mkv, p_out, 5),
            stack(kv_s, s_out, 0), stack(kv_s, s_out, 1), stack(st, s_out, 2), stack(st, s_out, 3))
```

```python
import functools
import math

import jax
import jax.numpy as jnp
from jax import lax
from jax.experimental import pallas as pl
from jax.experimental.pallas import tpu as pltpu

F32 = jnp.float32
BF16 = jnp.bfloat16

CHUNK = 64
N_PREV_CHUNKS = 8
BAND = (N_PREV_CHUNKS + 1) * CHUNK
PAD_ROWS = N_PREV_CHUNKS * CHUNK
PAIR_ROWS = 2 * CHUNK
PAIR_BAND = BAND + CHUNK
SOFTMAX_ROWS = 32
SSM_GROUP = 16
SSM_STATE = 64
ATT_HEAD_DIM = 64
REL_CLIP = 128
MEM_HEADS = 4
N_GROUPS = 4
EXPERTS_PER_GROUP = 8
EPS = 1e-6
NEG_INF = -1e30

LANES = 128
MXU_N = 256
VMEM_LIMIT = 56 * 1024 * 1024
ROUTER_LANES = LANES
MOE_TILE = 256
SAMPLE_STREAMS = 8
MOE_BUFS = 3
ROUTER_E0 = N_GROUPS
PAIRS_PER_GROUP = EXPERTS_PER_GROUP * (EXPERTS_PER_GROUP - 1) // 2
N_CLASSES = N_GROUPS * PAIRS_PER_GROUP
ROUTE_CLS, ROUTE_WA, ROUTE_WB = 64, 65, 66


def _cparams(sem):
    return pltpu.CompilerParams(dimension_semantics=sem, vmem_limit_bytes=VMEM_LIMIT)


def _rms(x, gain):
    ms = jnp.mean(x * x, axis=-1, keepdims=True)
    return x * lax.rsqrt(ms + EPS) * gain


def _sigmoid(x):
    return 1.0 / (1.0 + jnp.exp(-x))


def _gelu_tanh(x):
    c = math.sqrt(2.0 / math.pi)
    return 0.5 * x * (1.0 + jnp.tanh(c * (x + 0.044715 * (x * x * x))))


def _dot(a, b):
    return jnp.dot(a, b, preferred_element_type=F32)


def _dot_nt(a, b):
    return lax.dot_general(a, b, (((1,), (1,)), ((), ())), preferred_element_type=F32)


def _full(shape):
    n = len(shape)
    return pl.BlockSpec(shape, lambda *_: (0,) * n)


def _in_proj_kernel(x_ref, g_ref, w_ref, qg_ref, kg_ref, hm_ref,
                    u_ref, q_ref, k_ref, v_ref, kk_ref, vk_ref, *, width, nt):
    h = _rms(x_ref[...], g_ref[...]).astype(BF16)
    z = _dot(h, w_ref[...])
    u_ref[...] = z[:, :width].astype(u_ref.dtype)

    def head_norm(a, gain):
        ms = _dot((a * a).astype(BF16), hm_ref[...])
        return a * lax.rsqrt(ms + EPS) * gain

    q = head_norm(z[:, width:2 * width], qg_ref[...])
    k = head_norm(z[:, 2 * width:3 * width], kg_ref[...])
    v = z[:, 3 * width:]
    q_ref[...] = q.astype(BF16)
    k_ref[...] = k.astype(BF16)
    v_ref[...] = v.astype(BF16)

    @pl.when(pl.program_id(1) == nt - 1)
    def _():
        if len(kk_ref.shape) == 3:
            kt, vt = k.T, v.T
            for hd in range(width // ATT_HEAD_DIM):
                rows = slice(hd * ATT_HEAD_DIM, (hd + 1) * ATT_HEAD_DIM)
                kk_ref[hd] = kt[rows, :]
                vk_ref[hd] = vt[rows, :]
        else:
            kk_ref[...] = k
            vk_ref[...] = v


def _in_proj(x2d, gain, w_bf, qg, kg, hm, *, nb, nt, tm, keep_transposed):
    rows, d = x2d.shape
    width = w_bf.shape[1] // 4
    heads = width // ATT_HEAD_DIM
    tok = lambda b, t: (b * nt + t, 0)
    out_shape = (
        jax.ShapeDtypeStruct((rows, width), F32) if keep_transposed
        else jax.ShapeDtypeStruct((nt * tm, nb * width), BF16),
        jax.ShapeDtypeStruct((rows, width), BF16),
        jax.ShapeDtypeStruct((rows, width), BF16),
        jax.ShapeDtypeStruct((rows, width), BF16),
    )
    if keep_transposed:
        kept = jax.ShapeDtypeStruct((nb * heads, ATT_HEAD_DIM, tm), F32)
        keep = pl.BlockSpec((heads, ATT_HEAD_DIM, tm), lambda b, t: (b, 0, 0))
    else:
        kept = jax.ShapeDtypeStruct((nb * tm, width), F32)
        keep = pl.BlockSpec((tm, width), lambda b, t: (b, 0))
    out_shape = out_shape + (kept, kept)
    return pl.pallas_call(
        functools.partial(_in_proj_kernel, width=width, nt=nt),
        grid=(nb, nt),
        in_specs=[pl.BlockSpec((tm, d), tok), _full((1, d)), _full(w_bf.shape),
                  _full((1, width)), _full((1, width)), _full((width, width))],
        out_specs=(pl.BlockSpec((tm, width), tok if keep_transposed else (lambda b, t: (t, b))),
                   pl.BlockSpec((tm, width), tok), pl.BlockSpec((tm, width), tok),
                   pl.BlockSpec((tm, width), tok),
                   keep, keep),
        out_shape=out_shape,
        compiler_params=_cparams(("arbitrary", "arbitrary")),
        name="in_proj",
    )(x2d, gain, w_bf, qg, kg, hm)


def _ssm_chunk(u, bc_ref, cre_ref, cim_ref, lre_ref, lim_ref, d_ref, sre_ref, sim_ref, bu_ref, *, nb, tc, ns):
    n_tiles = 2 * ns // MXU_N
    for j in range(n_tiles):
        slab = (j % (n_tiles // 2)) // 2
        bu_ref[:, j * MXU_N:(j + 1) * MXU_N] = _dot(u[:, slab * LANES:(slab + 1) * LANES], bc_ref[j])

    cw = 8 * 1024 // nb
    for cb in range(ns // cw):
        c0 = cb * cw
        lre = jnp.broadcast_to(lre_ref[:, c0:c0 + cw], (nb, cw))
        lim = jnp.broadcast_to(lim_ref[:, c0:c0 + cw], (nb, cw))

        def step(t, carry, c0=c0, lre=lre, lim=lim):
            sr, si = carry
            r0 = pl.multiple_of(t * nb, nb)
            nr = lre * sr - lim * si + bu_ref[pl.ds(r0, nb), c0:c0 + cw]
            ni = lre * si + lim * sr + bu_ref[pl.ds(r0, nb), ns + c0:ns + c0 + cw]
            bu_ref[pl.ds(r0, nb), c0:c0 + cw] = nr
            bu_ref[pl.ds(r0, nb), ns + c0:ns + c0 + cw] = ni
            return nr, ni

        sr, si = lax.fori_loop(0, tc, step, (sre_ref[:, c0:c0 + cw], sim_ref[:, c0:c0 + cw]))
        sre_ref[:, c0:c0 + cw] = sr
        sim_ref[:, c0:c0 + cw] = si

    width = u.shape[1]
    kt = ns * MXU_N // width
    ys = []
    for n in range(width // MXU_N):
        s_re = bu_ref[:, n * kt:(n + 1) * kt].astype(BF16)
        s_im = bu_ref[:, ns + n * kt:ns + (n + 1) * kt].astype(BF16)
        cols = slice(n * MXU_N, (n + 1) * MXU_N)
        ys.append(_dot(s_re, cre_ref[n]) + _dot(s_im, cim_ref[n]) + d_ref[:, cols] * u[:, cols].astype(F32))
    return jnp.concatenate(ys, axis=1)


def _ssm_kernel(u_ref, bc_ref, cre_ref, cim_ref, lre_ref, lim_ref, d_ref, s0re_ref, s0im_ref,
                y_ref, sre_ref, sim_ref, bu_ref, *, nb, tc, ns):
    @pl.when(pl.program_id(0) == 0)
    def _():
        sre_ref[...] = s0re_ref[...]
        sim_ref[...] = s0im_ref[...]

    y_ref[...] = _ssm_chunk(u_ref[...], bc_ref, cre_ref, cim_ref, lre_ref, lim_ref, d_ref,
                            sre_ref, sim_ref, bu_ref, nb=nb, tc=tc, ns=ns)


def _ssm(u_rows, bc, cre, cim, lre, lim, dskip, s0re, s0im, *, nb, tc):
    rows, width = u_rows.shape
    ns = lre.shape[1]
    r = tc * nb
    return pl.pallas_call(
        functools.partial(_ssm_kernel, nb=nb, tc=tc, ns=ns),
        grid=(rows // r,),
        in_specs=[pl.BlockSpec((r, width), lambda i: (i, 0)), _full(bc.shape), _full(cre.shape),
                  _full(cim.shape), _full((1, ns)), _full((1, ns)), _full((1, width)),
                  _full((nb, ns)), _full((nb, ns))],
        out_specs=(pl.BlockSpec((r, width), lambda i: (i, 0)), _full((nb, ns)), _full((nb, ns))),
        out_shape=(jax.ShapeDtypeStruct((rows, width), F32),
                   jax.ShapeDtypeStruct((nb, ns), F32), jax.ShapeDtypeStruct((nb, ns), F32)),
        scratch_shapes=[pltpu.VMEM((r, 2 * ns), F32)],
        compiler_params=_cparams(("arbitrary",)),
        name="ssm",
    )(u_rows, bc, cre, cim, lre, lim, dskip, s0re, s0im)


def _ssm_tokens_kernel(u_hbm, bc_ref, cre_ref, cim_ref, lre_ref, lim_ref, d_ref,
                       y_hbm, sre_ref, sim_ref, bu_ref, ubuf, ybuf, isem, osem, *, nb, tc, ns, t_len):
    i = pl.program_id(0)
    n_steps = t_len // tc
    slot = i % 2
    width = ubuf.shape[3]

    def copy_in(step, s, b):
        return pltpu.make_async_copy(u_hbm.at[pl.ds(b * t_len + step * tc, tc)], ubuf.at[s, :, b, :], isem.at[s])

    def copy_out(step, s, b):
        return pltpu.make_async_copy(ybuf.at[s, :, b, :], y_hbm.at[pl.ds(b * t_len + step * tc, tc)], osem.at[s])

    @pl.when(i == 0)
    def _():
        sre_ref[...] = jnp.zeros_like(sre_ref)
        sim_ref[...] = jnp.zeros_like(sim_ref)
        for b in range(nb):
            copy_in(0, 0, b).start()

    @pl.when(i + 1 < n_steps)
    def _():
        for b in range(nb):
            copy_in(i + 1, 1 - slot, b).start()

    for b in range(nb):
        copy_in(i, slot, b).wait()
    u = ubuf[slot].reshape(tc * nb, width).astype(BF16)
    y = _ssm_chunk(u, bc_ref, cre_ref, cim_ref, lre_ref, lim_ref, d_ref, sre_ref, sim_ref, bu_ref,
                   nb=nb, tc=tc, ns=ns)

    @pl.when(i >= 2)
    def _():
        for b in range(nb):
            copy_out(i - 2, slot, b).wait()

    ybuf[slot] = y.reshape(tc, nb, width)
    for b in range(nb):
        copy_out(i, slot, b).start()

    @pl.when(i == n_steps - 1)
    def _():
        for b in range(nb):
            copy_out(i, slot, b).wait()
        if n_steps > 1:
            for b in range(nb):
                copy_out(i - 1, 1 - slot, b).wait()


def _ssm_tokens(u_tok, bc, cre, cim, lre, lim, dskip, *, nb, tc):
    rows, width = u_tok.shape
    ns = lre.shape[1]
    t_len = rows // nb
    r = tc * nb
    any_spec = pl.BlockSpec(memory_space=pl.ANY)
    return pl.pallas_call(
        functools.partial(_ssm_tokens_kernel, nb=nb, tc=tc, ns=ns, t_len=t_len),
        grid=(t_len // tc,),
        in_specs=[any_spec, _full(bc.shape), _full(cre.shape), _full(cim.shape), _full((1, ns)),
                  _full((1, ns)), _full((1, width))],
        out_specs=(any_spec, _full((nb, ns)), _full((nb, ns))),
        out_shape=(jax.ShapeDtypeStruct((rows, width), F32),
                   jax.ShapeDtypeStruct((nb, ns), F32), jax.ShapeDtypeStruct((nb, ns), F32)),
        scratch_shapes=[pltpu.VMEM((r, 2 * ns), F32), pltpu.VMEM((2, tc, nb, width), F32),
                        pltpu.VMEM((2, tc, nb, width), F32),
                        pltpu.SemaphoreType.DMA((2,)), pltpu.SemaphoreType.DMA((2,))],
        compiler_params=_cparams(("arbitrary",)),
        name="ssm_tokens",
    )(u_tok, bc, cre, cim, lre, lim, dskip)


def _half_select(shape):
    lane = lax.broadcasted_iota(jnp.int32, shape, 1)
    return lane < ATT_HEAD_DIM


def _head_masks(first):
    m0 = jnp.where(first, 1.0, 0.0).astype(BF16)
    return m0, (1.0 - m0.astype(F32)).astype(BF16)


def _band_prompt_kernel(q_ref, k_ref, v_ref, bias_ref, o_ref, kp_ref, vp_ref, s_scr, p_scr, l_scr, *, t_len):
    width = q_ref.shape[1]
    n_hp = width // LANES
    kp_ref[0:PAD_ROWS, :] = jnp.zeros((PAD_ROWS, width), BF16)
    vp_ref[0:PAD_ROWS, :] = jnp.zeros((PAD_ROWS, width), BF16)
    kp_ref[PAD_ROWS:, :] = k_ref[...]
    vp_ref[PAD_ROWS:, :] = v_ref[...]
    first = _half_select((PAIR_ROWS, LANES))
    head_mask = _head_masks(first)
    col = lax.broadcasted_iota(jnp.int32, (SOFTMAX_ROWS, PAIR_BAND), 1)

    def pair(pi, carry, *, masked):
        r0 = pl.multiple_of(pi * PAIR_ROWS, PAIR_ROWS)
        for hp in range(n_hp):
            lanes = slice(hp * LANES, (hp + 1) * LANES)
            qp = q_ref[pl.ds(r0, PAIR_ROWS), lanes]
            qs = jnp.concatenate([qp * head_mask[0], qp * head_mask[1]], axis=0)
            s_scr[2 * hp * PAIR_ROWS:2 * (hp + 1) * PAIR_ROWS, :] = _dot_nt(
                qs, kp_ref[pl.ds(r0, PAIR_BAND), lanes])
        for h in range(2 * n_hp):
            for rb in range(0, PAIR_ROWS, SOFTMAX_ROWS):
                rows = slice(h * PAIR_ROWS + rb, h * PAIR_ROWS + rb + SOFTMAX_ROWS)
                s = s_scr[rows, :] + bias_ref[h, rb:rb + SOFTMAX_ROWS, :]
                if masked:
                    s = jnp.where(col >= PAD_ROWS - r0, s, NEG_INF)
                p = jnp.exp(s - jnp.max(s, axis=-1, keepdims=True))
                p_scr[rows, :] = p.astype(BF16)
                l_scr[rows, :] = jnp.broadcast_to(1.0 / jnp.sum(p, axis=-1, keepdims=True),
                                                  (SOFTMAX_ROWS, LANES))
        for hp in range(n_hp):
            lanes = slice(hp * LANES, (hp + 1) * LANES)
            rows = slice(2 * hp * PAIR_ROWS, 2 * (hp + 1) * PAIR_ROWS)
            o2 = _dot(p_scr[rows, :], vp_ref[pl.ds(r0, PAIR_BAND), lanes]) * l_scr[rows, :]
            o_ref[pl.ds(r0, PAIR_ROWS), lanes] = jnp.where(
                first, o2[:PAIR_ROWS], o2[PAIR_ROWS:]).astype(BF16)
        return carry

    n_masked = PAD_ROWS // PAIR_ROWS
    lax.fori_loop(0, n_masked, functools.partial(pair, masked=True), 0)
    lax.fori_loop(n_masked, t_len // PAIR_ROWS, functools.partial(pair, masked=False), 0)


def _band_prompt(q, k, v, bias, *, nb, t_len):
    width = q.shape[1]
    heads = width // ATT_HEAD_DIM
    blk = pl.BlockSpec((t_len, width), lambda b: (b, 0))
    return pl.pallas_call(
        functools.partial(_band_prompt_kernel, t_len=t_len),
        grid=(nb,),
        in_specs=[blk, blk, blk, _full(bias.shape)],
        out_specs=blk,
        out_shape=jax.ShapeDtypeStruct(q.shape, BF16),
        scratch_shapes=[pltpu.VMEM((t_len + PAD_ROWS, width), BF16),
                        pltpu.VMEM((t_len + PAD_ROWS, width), BF16),
                        pltpu.VMEM((heads * PAIR_ROWS, PAIR_BAND), F32),
                        pltpu.VMEM((heads * PAIR_ROWS, PAIR_BAND), BF16),
                        pltpu.VMEM((heads * PAIR_ROWS, LANES), F32)],
        compiler_params=_cparams(("arbitrary",)),
        name="band_prompt",
    )(q, k, v, bias)


def _band_sample_kernel(q_ref, k_ref, v_ref, ck_ref, cv_ref, bc_ref, bn_ref, o_ref):
    for h in range(ck_ref.shape[0]):
        cols = slice(h * ATT_HEAD_DIM, (h + 1) * ATT_HEAD_DIM)
        qh = q_ref[:, cols]
        sc = _dot(qh, ck_ref[h].astype(BF16)) + bc_ref[h]
        sn = _dot_nt(qh, k_ref[:, cols]) + bn_ref[h]
        m = jnp.maximum(jnp.max(sc, axis=-1, keepdims=True), jnp.max(sn, axis=-1, keepdims=True))
        pc = jnp.exp(sc - m)
        pn = jnp.exp(sn - m)
        l = jnp.sum(pc, axis=-1, keepdims=True) + jnp.sum(pn, axis=-1, keepdims=True)
        o = _dot_nt(pc.astype(BF16), cv_ref[h].astype(BF16)) + _dot(pn.astype(BF16), v_ref[:, cols])
        o_ref[:, cols] = (o / l).astype(BF16)


def _band_sample(q_tm, k_tm, v_tm, cache_kt, cache_vt, bias_c, bias_n, *, nb, s_len):
    width = q_tm.shape[1] // nb
    heads = cache_kt.shape[0] // nb
    col = pl.BlockSpec((s_len, width), lambda b: (0, b))
    cache = pl.BlockSpec((heads,) + cache_kt.shape[1:], lambda b: (b, 0, 0))
    return pl.pallas_call(
        _band_sample_kernel,
        grid=(nb,),
        in_specs=[col, col, col, cache, cache, _full(bias_c.shape), _full(bias_n.shape)],
        out_specs=col,
        out_shape=jax.ShapeDtypeStruct(q_tm.shape, BF16),
        compiler_params=_cparams(("arbitrary",)),
        name="band_sample",
    )(q_tm, k_tm, v_tm, cache_kt, cache_vt, bias_c, bias_n)


def _mem_kv_kernel(m_ref, g_ref, wk_ref, wv_ref, kg_ref, k_ref, v_ref, kb_ref, vb_ref):
    m = _rms(m_ref[...], g_ref[...]).astype(BF16)
    k = _dot(m, wk_ref[...])
    v = _dot(m, wv_ref[...])
    hd = kg_ref.shape[1]
    for h in range(k.shape[1] // hd):
        cols = slice(h * hd, (h + 1) * hd)
        kh = _rms(k[:, cols], kg_ref[...])
        k_ref[:, h, :] = kh
        v_ref[:, h, :] = v[:, cols]
        kb_ref[:, cols] = kh.astype(BF16)
    vb_ref[...] = v.astype(BF16)


def _mem_kv(mem2d, gain, wk, wv, kgain, *, tm):
    rows, d = mem2d.shape
    hd = kgain.shape[1]
    blk = pl.BlockSpec((tm, d), lambda i: (i, 0))
    hblk = pl.BlockSpec((tm, d // hd, hd), lambda i: (i, 0, 0))
    return pl.pallas_call(
        _mem_kv_kernel,
        grid=(rows // tm,),
        in_specs=[blk, _full((1, d)), _full(wk.shape), _full(wv.shape), _full(kgain.shape)],
        out_specs=(hblk, hblk, blk, blk),
        out_shape=(jax.ShapeDtypeStruct((rows, d // hd, hd), F32),
                   jax.ShapeDtypeStruct((rows, d // hd, hd), F32),
                   jax.ShapeDtypeStruct((rows, d), BF16), jax.ShapeDtypeStruct((rows, d), BF16)),
        compiler_params=_cparams(("arbitrary",)),
        name="mem_kv",
    )(mem2d, gain, wk, wv, kgain)


def _route(logits):
    lane = lax.broadcasted_iota(jnp.int32, logits.shape, 1).astype(F32)
    big = float(ROUTER_LANES)
    is_g = lane < N_GROUPS
    lg = jnp.where(is_g, logits, NEG_INF)
    gmax = jnp.max(lg, axis=-1, keepdims=True)
    p1 = 1.0 / jnp.sum(jnp.where(is_g, jnp.exp(lg - gmax), 0.0), axis=-1, keepdims=True)
    g_idx = jnp.min(jnp.where(lg == gmax, lane, big), axis=-1, keepdims=True)
    lo = ROUTER_E0 + g_idx * EXPERTS_PER_GROUP
    le = jnp.where((lane >= lo) & (lane < lo + EXPERTS_PER_GROUP), logits, NEG_INF)
    v1 = jnp.max(le, axis=-1, keepdims=True)
    i1 = jnp.min(jnp.where(le == v1, lane, big), axis=-1, keepdims=True)
    le2 = jnp.where(lane == i1, NEG_INF, le)
    v2 = jnp.max(le2, axis=-1, keepdims=True)
    i2 = jnp.min(jnp.where(le2 == v2, lane, big), axis=-1, keepdims=True)
    e2 = jnp.exp(v2 - v1)
    w1 = p1 / (1.0 + e2)
    w2 = p1 * e2 / (1.0 + e2)
    gates = jnp.where(lane == i1, w1, 0.0) + jnp.where(lane == i2, w2, 0.0)
    a = jnp.minimum(i1, i2) - lo
    b = jnp.maximum(i1, i2) - lo
    pair = a * (2 * EXPERTS_PER_GROUP - 1 - a) * 0.5 + (b - a - 1.0)
    cls = g_idx * PAIRS_PER_GROUP + pair
    w_a = jnp.where(i1 < i2, w1, w2)
    w_b = jnp.where(i1 < i2, w2, w1)
    return (gates + jnp.where(lane == ROUTE_CLS, cls, 0.0) + jnp.where(lane == ROUTE_WA, w_a, 0.0)
            + jnp.where(lane == ROUTE_WB, w_b, 0.0))


def _mix_mem_kernel(x_ref, ys_ref, ya_ref, mk_ref, mv_ref, wglu_ref, ons_ref, ona_ref, wout_ref,
                    nmem_ref, wq_ref, qg_ref, wo_ref, nffn_ref, wr_ref, rb_ref,
                    *outs, routed, nbat):
    if routed:
        x2_ref, o_scr = outs
    else:
        x2_ref, xn_ref, gate_ref, o_scr = outs
    tm = x_ref.shape[0]
    d = x_ref.shape[1] // nbat
    width = ys_ref.shape[1] // nbat
    n_mem = mk_ref.shape[0] // nbat

    def stacked(ref, w):
        return jnp.concatenate([ref[:, j * w:(j + 1) * w] for j in range(nbat)], axis=0)

    g = _dot(_gelu_tanh(stacked(ys_ref, width)).astype(BF16), wglu_ref[...])
    y_s = g[:, :width] * _sigmoid(g[:, width:])
    cat_s = _rms(y_s, ons_ref[...]).astype(BF16)
    cat_a = _rms(stacked(ya_ref, width).astype(F32), ona_ref[...]).astype(BF16)
    x1 = stacked(x_ref, d) + _dot(cat_s, wout_ref[0:width, :]) + _dot(cat_a, wout_ref[width:, :])

    q = _dot(_rms(x1, nmem_ref[...]).astype(BF16), wq_ref[...])
    hd = qg_ref.shape[1]
    scale = hd ** -0.5
    for j in range(nbat):
        rows = slice(j * tm, (j + 1) * tm)
        mem = slice(j * n_mem, (j + 1) * n_mem)
        for h in range(d // hd):
            cols = slice(h * hd, (h + 1) * hd)
            qh = _rms(q[rows, cols], qg_ref[...]).astype(BF16)
            s = _dot_nt(qh, mk_ref[mem, cols].astype(BF16)) * scale
            p = jnp.exp(s - jnp.max(s, axis=-1, keepdims=True))
            l = jnp.sum(p, axis=-1, keepdims=True)
            o_scr[rows, cols] = (_dot(p.astype(BF16), mv_ref[mem, cols].astype(BF16)) / l).astype(BF16)
    acc = x1 + _dot(o_scr[...], wo_ref[...])

    xn = _rms(acc, nffn_ref[...])
    xh = xn.astype(BF16)
    logits = _dot(xh, wr_ref[...]) + rb_ref[...]
    record = _route(logits)
    if routed:
        x2_ref[:, :d] = acc
        x2_ref[:, d:] = record
    else:
        for j in range(nbat):
            rows = slice(j * tm, (j + 1) * tm)
            x2_ref[:, j * d:(j + 1) * d] = acc[rows]
            xn_ref[:, j * d:(j + 1) * d] = xh[rows]
            gate_ref[:, j * ROUTER_LANES:(j + 1) * ROUTER_LANES] = record[rows]


def _mix_mem(x, ys, ya, mk, mv, weights, *, grid, tm, row_map, ssm_map, mem_map, routed, nbat=1):
    d = weights["w_out"].shape[1]
    width = weights["w_glu"].shape[0]
    names = ("w_glu", "out_norm_ssm", "out_norm_att", "w_out", "norm_mem", "w_mem_q", "mem_q_gain",
             "w_mem_o", "norm_ffn", "w_router", "router_bias")
    ws = [weights[n] for n in names]
    mem_rows = weights["n_mem"]
    xspec = pl.BlockSpec((tm, nbat * d), row_map)
    hspec = pl.BlockSpec((tm, nbat * width), row_map)
    sspec = pl.BlockSpec((tm, nbat * width), ssm_map)
    mspec = pl.BlockSpec((nbat * mem_rows, d), mem_map)
    n_col = x.shape[1] // d
    if routed:
        assert nbat == 1
        out_specs = pl.BlockSpec((tm, d + ROUTER_LANES), row_map)
        out_shape = jax.ShapeDtypeStruct((x.shape[0], n_col * (d + ROUTER_LANES)), F32)
    else:
        out_specs = (xspec, xspec, pl.BlockSpec((tm, nbat * ROUTER_LANES), row_map))
        out_shape = (jax.ShapeDtypeStruct(x.shape, F32), jax.ShapeDtypeStruct(x.shape, BF16),
                     jax.ShapeDtypeStruct((x.shape[0], n_col * ROUTER_LANES), F32))
    return pl.pallas_call(
        functools.partial(_mix_mem_kernel, routed=routed, nbat=nbat),
        grid=grid,
        in_specs=[xspec, sspec, hspec, mspec, mspec] + [_full(w.shape) for w in ws],
        out_specs=out_specs,
        out_shape=out_shape,
        scratch_shapes=[pltpu.VMEM((nbat * tm, d), BF16)],
        compiler_params=_cparams(("arbitrary",) * len(grid)),
        name="mix_mem",
    )(x, ys, ya, mk, mv, *ws)


def _moe_kernel(xn_ref, x2_ref, gate_ref, wg_ref, wu_ref, wd_ref, o_ref):
    e = pl.program_id(1)

    @pl.when(e == 0)
    def _():
        o_ref[...] = x2_ref[...]

    gates = gate_ref[...]
    lane = lax.broadcasted_iota(jnp.int32, gates.shape, 1)
    ge = jnp.sum(jnp.where(lane == e + ROUTER_E0, gates, 0.0), axis=-1, keepdims=True)
    xn = xn_ref[...]
    a = _dot(xn, wg_ref[...])
    h = a * _sigmoid(a) * _dot(xn, wu_ref[...])
    o_ref[...] += ge * _dot(h.astype(BF16), wd_ref[...])


def _moe(xn, x2, gates, wg, wu, wd, *, tm):
    rows, d = xn.shape
    n_exp, _, dff = wg.shape
    row = lambda i, e: (i, 0)
    return pl.pallas_call(
        _moe_kernel,
        grid=(rows // tm, n_exp),
        in_specs=[pl.BlockSpec((tm, d), row), pl.BlockSpec((tm, d), row),
                  pl.BlockSpec((tm, ROUTER_LANES), row),
                  pl.BlockSpec((None, d, dff), lambda i, e: (e, 0, 0)),
                  pl.BlockSpec((None, d, dff), lambda i, e: (e, 0, 0)),
                  pl.BlockSpec((None, dff, d), lambda i, e: (e, 0, 0))],
        out_specs=pl.BlockSpec((tm, d), row),
        out_shape=jax.ShapeDtypeStruct((rows, d), F32),
        compiler_params=_cparams(("arbitrary", "arbitrary")),
        name="moe",
    )(xn, x2, gates, wg, wu, wd)


def _moe_routed_kernel(tile_ref, ea_ref, eb_ref, lo_ref, hi_ref, valid_ref, tok_ref,
                       x_hbm, nffn_ref, wga_ref, wua_ref, wda_ref, wgb_ref, wub_ref, wdb_ref,
                       y_hbm, *scratch, tm, n_tiles):
    nb = MOE_BUFS
    xbuf, obuf = scratch[:nb], scratch[nb:2 * nb]
    acc_ref, gsem, ssem = scratch[2 * nb:]
    w = pl.program_id(0)
    t = tile_ref[w]
    lo = lo_ref[w]
    hi = hi_ref[w]
    valid = valid_ref[w] == 1
    d = y_hbm.shape[1]

    def row_copy_in(tile, s, r):
        tok = tok_ref[tile * tm + r]
        return pltpu.make_async_copy(x_hbm.at[pl.ds(tok, 1)], xbuf[s].at[pl.ds(r, 1)], gsem.at[s])

    def row_copy_out(tile, s, r):
        tok = tok_ref[tile * tm + r]
        return pltpu.make_async_copy(obuf[s].at[pl.ds(r, 1)], y_hbm.at[pl.ds(tok, 1)], ssem.at[s])

    def start_gather(tile, s):
        for r in range(tm):
            row_copy_in(tile, s, r).start()

    def start_scatter(tile, s):
        for r in range(tm):
            row_copy_out(tile, s, r).start()

    def wait_gather(s):
        pltpu.make_async_copy(x_hbm.at[pl.ds(0, tm)], xbuf[s], gsem.at[s]).wait()

    def wait_scatter(s):
        pltpu.make_async_copy(obuf[s], y_hbm.at[pl.ds(0, tm)], ssem.at[s]).wait()

    def compute(s, first):
        xe = xbuf[s][...]
        xn = _rms(xe[:, :d], nffn_ref[...]).astype(BF16)
        row = lax.broadcasted_iota(jnp.int32, (tm, 1), 0)
        inseg = jnp.logical_and(row >= lo, row < hi)
        w_a = jnp.where(inseg, xe[:, d + ROUTE_WA:d + ROUTE_WA + 1], 0.0)
        w_b = jnp.where(inseg, xe[:, d + ROUTE_WB:d + ROUTE_WB + 1], 0.0)
        a = _dot(xn, wga_ref[...])
        h_a = (a * _sigmoid(a) * _dot(xn, wua_ref[...])).astype(BF16)
        b = _dot(xn, wgb_ref[...])
        h_b = (b * _sigmoid(b) * _dot(xn, wub_ref[...])).astype(BF16)
        upd = w_a * _dot(h_a, wda_ref[...]) + w_b * _dot(h_b, wdb_ref[...])
        if first:
            acc_ref[...] = upd
        else:
            acc_ref[...] += upd

    def when(*conds):
        c = conds[0]
        for extra in conds[1:]:
            c = jnp.logical_and(c, extra)
        return pl.when(c)

    ahead = nb - 1

    @pl.when(w == 0)
    def _():
        for k in range(ahead):
            start_gather(k, k)

    is_first = jnp.logical_and(valid, lo == 0)
    is_last = jnp.logical_and(valid, hi == tm)
    last_tile = n_tiles - 1
    for s in range(nb):
        mine = t % nb == s
        nxt, prv = (s + ahead) % nb, (s - 1) % nb

        if s == 0:
            @when(is_first, t == 0)
            def _(nxt=nxt):
                wait_gather(0)
                start_gather(ahead, nxt)
                compute(0, True)

        @when(is_first, mine, t > 0, t + ahead <= last_tile)
        def _(s=s, nxt=nxt, prv=prv):
            wait_gather(s)
            start_gather(t + ahead, nxt)
            start_scatter(t - 1, prv)
            compute(s, True)

        @when(is_first, mine, t + ahead > last_tile)
        def _(s=s, prv=prv):
            wait_gather(s)
            start_scatter(t - 1, prv)
            compute(s, True)

        @when(valid, mine, lo > 0)
        def _(s=s):
            compute(s, False)

        @when(is_last, mine)
        def _(s=s):
            @pl.when(t >= nb)
            def _():
                wait_scatter(s)

            obuf[s][...] = xbuf[s][:, :d] + acc_ref[...]

            if s == last_tile % nb:
                @pl.when(t == last_tile)
                def _():
                    start_scatter(t, s)
                    for k in range(nb):
                        wait_scatter(k)


def _moe_routed(x2ext, plan, nffn, wg, wu, wd, *, tm):
    rows, de = x2ext.shape
    d = de - ROUTER_LANES
    n_exp, _, dff = wg.shape
    n_tiles = rows // tm
    assert rows % tm == 0 and n_tiles > 2 * MOE_BUFS
    n_items = plan[0].shape[0]
    ea = lambda w, tile, ea_, eb_, *_: (ea_[w], 0, 0)
    eb = lambda w, tile, ea_, eb_, *_: (eb_[w], 0, 0)
    grid_spec = pltpu.PrefetchScalarGridSpec(
        num_scalar_prefetch=len(plan),
        grid=(n_items,),
        in_specs=[pl.BlockSpec(memory_space=pl.ANY),
                  pl.BlockSpec((1, d), lambda w, *_: (0, 0)),
                  pl.BlockSpec((None, d, dff), ea), pl.BlockSpec((None, d, dff), ea),
                  pl.BlockSpec((None, dff, d), ea),
                  pl.BlockSpec((None, d, dff), eb), pl.BlockSpec((None, d, dff), eb),
                  pl.BlockSpec((None, dff, d), eb)],
        out_specs=pl.BlockSpec(memory_space=pl.ANY),
        scratch_shapes=([pltpu.VMEM((tm, de), F32)] * MOE_BUFS + [pltpu.VMEM((tm, d), F32)] * MOE_BUFS
                        + [pltpu.VMEM((tm, d), F32),
                           pltpu.SemaphoreType.DMA((MOE_BUFS,)), pltpu.SemaphoreType.DMA((MOE_BUFS,))]),
    )
    return pl.pallas_call(
        functools.partial(_moe_routed_kernel, tm=tm, n_tiles=n_tiles),
        grid_spec=grid_spec,
        out_shape=jax.ShapeDtypeStruct((rows, d), F32),
        compiler_params=_cparams(("arbitrary",)),
        name="moe_routed",
    )(*plan, x2ext, nffn, wg, wu, wd, wg, wu, wd)


def _route_plan(cls, *, tm):
    n = cls.shape[0]
    n_tiles = n // tm
    order = jnp.argsort(cls).astype(jnp.int32)
    classes = jnp.arange(N_CLASSES, dtype=jnp.int32)
    class_end = jnp.sum((cls[:, None] <= classes[None, :]).astype(jnp.int32), axis=0)
    class_start = jnp.concatenate([jnp.zeros((1,), jnp.int32), class_end[:-1]])
    bounds = jnp.concatenate([jnp.arange(n_tiles, dtype=jnp.int32) * tm,
                              jnp.where(class_end > class_start, class_start, n)])
    n_items = bounds.shape[0]
    idx = jnp.arange(n_items, dtype=jnp.int32)
    before = jnp.logical_or(bounds[None, :] < bounds[:, None],
                            jnp.logical_and(bounds[None, :] == bounds[:, None], idx[None, :] < idx[:, None]))
    rank = jnp.sum(before.astype(jnp.int32), axis=1)
    start = jnp.sum(jnp.where(rank[:, None] == idx[None, :], bounds[:, None], 0), axis=0)
    stop = jnp.concatenate([start[1:], jnp.full((1,), n, jnp.int32)])
    valid = stop > start
    tile = jnp.minimum(start // tm, n_tiles - 1)
    lo = start - tile * tm
    hi = stop - tile * tm
    c = jnp.minimum(jnp.sum((class_end[None, :] <= start[:, None]).astype(jnp.int32), axis=1),
                    N_CLASSES - 1)
    g = c // PAIRS_PER_GROUP
    pair = c % PAIRS_PER_GROUP
    a = jnp.zeros_like(pair)
    for k in range(1, EXPERTS_PER_GROUP - 1):
        a = a + (pair >= k * (2 * EXPERTS_PER_GROUP - 1 - k) // 2).astype(jnp.int32)
    b = pair - a * (2 * EXPERTS_PER_GROUP - 1 - a) // 2 + a + 1
    e_a = g * EXPERTS_PER_GROUP + a
    e_b = g * EXPERTS_PER_GROUP + b
    i32 = lambda v: v.astype(jnp.int32)
    return (i32(tile), i32(e_a), i32(e_b), i32(lo), i32(hi), i32(valid), order)


def _ssm_params(lam_re, lam_im, log_step, b_re, b_im, c_re, c_im):
    n_g, n_p = lam_re.shape
    step = jnp.exp(log_step.astype(F32))[:, None]
    mag = jnp.exp(lam_re * step)
    lb_re = mag * jnp.cos(lam_im * step)
    lb_im = mag * jnp.sin(lam_im * step)
    den = lam_re * lam_re + lam_im * lam_im
    f_re = ((lb_re - 1.0) * lam_re + lb_im * lam_im) / den
    f_im = (lb_im * lam_re - (lb_re - 1.0) * lam_im) / den
    bb_re = f_re[..., None] * b_re - f_im[..., None] * b_im
    bb_im = f_re[..., None] * b_im + f_im[..., None] * b_re
    eye = jnp.eye(n_g, dtype=F32)
    ns = n_g * n_p
    width = n_g * SSM_GROUP
    b_full = jnp.concatenate(
        [jnp.einsum("hg,gpc->hcgp", eye, bb_re).reshape(width, ns),
         jnp.einsum("hg,gpc->hcgp", eye, bb_im).reshape(width, ns)], axis=1)
    n_tiles = 2 * ns // MXU_N
    bc = jnp.stack([
        b_full[((j % (n_tiles // 2)) // 2) * LANES:((j % (n_tiles // 2)) // 2 + 1) * LANES,
               j * MXU_N:(j + 1) * MXU_N] for j in range(n_tiles)]).astype(BF16)
    c_full_re = jnp.einsum("gh,gcp->gphc", eye, c_re).reshape(ns, width)
    c_full_im = -jnp.einsum("gh,gcp->gphc", eye, c_im).reshape(ns, width)
    kt = ns * MXU_N // width
    tiles = range(width // MXU_N)
    cre = jnp.stack([c_full_re[n * kt:(n + 1) * kt, n * MXU_N:(n + 1) * MXU_N] for n in tiles]).astype(BF16)
    cim = jnp.stack([c_full_im[n * kt:(n + 1) * kt, n * MXU_N:(n + 1) * MXU_N] for n in tiles]).astype(BF16)
    return bc, cre, cim, lb_re.reshape(1, ns), lb_im.reshape(1, ns)


def _rel_bias(rel_bias, q0, n_q, n_k):
    n_r = n_q + n_k - 1
    dist = q0 + n_q - 1 - jnp.arange(n_r)
    r = rel_bias.astype(F32)[:, jnp.clip(dist, -REL_CLIP, REL_CLIP) + REL_CLIP]
    r = jnp.pad(r, ((0, 0), (0, 1)))
    rows = jnp.tile(r, (1, n_q))[:, :n_q * n_r].reshape(-1, n_q, n_r)
    return rows[:, :, n_q - 1:n_q - 1 + n_k]


def _layer(xp, xs, mem_p, ck, cv, s_re, s_im, cmk, cmv, w, dims):
    batch, seq, dec_batch, dec_seq, d = dims
    width = w["w_in"].shape[1] // 4
    ns = w["lam_re"].shape[1]
    n_mem = w["n_mem"]
    tm_p = 512
    nt_p = seq // tm_p
    n_dec = dec_batch * dec_seq

    u, q, k, v, kk, vk = _in_proj(xp, w["norm_mix"], w["w_in"], w["qg"], w["kg"], w["head_mean"],
                                  nb=batch, nt=nt_p, tm=tm_p, keep_transposed=True)
    y_ssm, pre, pim = _ssm_tokens(u, w["bc"], w["cre"], w["cim"], w["lam_re"], w["lam_im"], w["ssm_d"],
                                  nb=batch, tc=32)
    y_att = _band_prompt(q, k, v, w["bias_p"], nb=batch, t_len=seq)
    mk, mv, mkb, mvb = _mem_kv(mem_p, w["mem_in_norm"], w["w_mem_k"], w["w_mem_v"], w["mem_k_gain"], tm=512)
    tm_d = 512
    nt_d = seq // tm_d
    x2ext = _mix_mem(xp, y_ssm, y_att, mkb, mvb, w,
                     grid=(batch, nt_d), tm=tm_d,
                     row_map=lambda b, t: (b * nt_d + t, 0), ssm_map=lambda b, t: (b * nt_d + t, 0),
                     mem_map=lambda b, t: (b, 0), routed=True)
    plan = _route_plan(x2ext[:, d + ROUTE_CLS].astype(jnp.int32), tm=MOE_TILE)
    yp = _moe_routed(x2ext, plan, w["norm_ffn"], w["exp_w_gate"], w["exp_w_up"], w["exp_w_down"],
                     tm=MOE_TILE)

    us, qs, ks, vs, kks, vks = _in_proj(xs, w["norm_mix"], w["w_in"], w["qg"], w["kg"], w["head_mean"],
                                        nb=1, nt=1, tm=n_dec, keep_transposed=False)
    ys_ssm, sre, sim = _ssm(us, w["bc"], w["cre"], w["cim"], w["lam_re"], w["lam_im"], w["ssm_d"],
                            s_re, s_im, nb=dec_batch, tc=dec_seq)
    tmv = lambda a: a.reshape(dec_seq, dec_batch * a.shape[1])
    ys_att = _band_sample(tmv(qs), tmv(ks), tmv(vs), ck, cv, w["bias_sc"], w["bias_sn"],
                          nb=dec_batch, s_len=dec_seq)
    x2s, xns, gates_s = _mix_mem(tmv(xs), tmv(ys_ssm), ys_att, cmk, cmv, w,
                                 grid=(dec_batch // SAMPLE_STREAMS,), tm=dec_seq,
                                 row_map=lambda b: (0, b), ssm_map=lambda b: (0, b),
                                 mem_map=lambda b: (b, 0), routed=False, nbat=SAMPLE_STREAMS)
    ys = _moe(xns.reshape(n_dec, d), x2s.reshape(n_dec, d), gates_s.reshape(n_dec, ROUTER_LANES),
              w["exp_w_gate"], w["exp_w_up"], w["exp_w_down"], tm=n_dec)
    return yp, ys, (kk, vk, pre, pim, mk, mv), (kks, vks, sre, sim)


def kernel(x_prompt, x_sample, mem_prompt, cache_attn_k, cache_attn_v, state_ssm_re, state_ssm_im, cache_mem_k, cache_mem_v, norm_mix, w_in, ssm_lambda_re, ssm_lambda_im, ssm_log_step, ssm_b_re, ssm_b_im, ssm_c_re, ssm_c_im, ssm_d, w_glu, att_q_gain, att_k_gain, att_rel_bias, out_norm_ssm, out_norm_att, w_out, norm_mem, mem_in_norm, w_mem_q, w_mem_k, w_mem_v, w_mem_o, mem_q_gain, mem_k_gain, norm_ffn, router_g_w, router_g_b, router_e_w, router_e_b, exp_w_gate, exp_w_up, exp_w_down):
    depth = norm_mix.shape[0]
    batch, seq, d = x_prompt.shape
    dec_batch, dec_seq, _ = x_sample.shape
    n_mem = mem_prompt.shape[1]
    att_rows = cache_attn_k.shape[2]
    n_g, n_p = ssm_lambda_re.shape[1:]
    width = n_g * SSM_GROUP
    heads = width // ATT_HEAD_DIM
    ns = n_g * n_p
    assert seq % 512 == 0 and att_rows == PAD_ROWS and seq >= PAD_ROWS
    assert (dec_batch * dec_seq) % 8 == 0 and dec_seq % 16 == 0

    xp = x_prompt.reshape(batch * seq, d)
    xs = x_sample.transpose(1, 0, 2).reshape(dec_seq * dec_batch, d)
    mem_p = mem_prompt.reshape(batch * n_mem, d)
    row = lambda a: a.reshape(1, -1).astype(F32)
    head_mean = jnp.kron(jnp.eye(heads, dtype=F32),
                         jnp.full((ATT_HEAD_DIM, ATT_HEAD_DIM), 1.0 / ATT_HEAD_DIM, F32)).astype(BF16)
    pr = jnp.arange(PAIR_ROWS)[:, None]
    pc = jnp.arange(PAIR_BAND)[None, :]
    pair_ok = jnp.where(pr < CHUNK, pc < BAND, pc >= CHUNK)
    att_scale = ATT_HEAD_DIM ** -0.5

    p_out, s_out = [], []
    for l in range(depth):
        bc, cre, cim, lb_re, lb_im = _ssm_params(ssm_lambda_re[l], ssm_lambda_im[l], ssm_log_step[l],
                                                 ssm_b_re[l], ssm_b_im[l], ssm_c_re[l], ssm_c_im[l])
        w_router = jnp.concatenate(
            [router_g_w[l], router_e_w[l].transpose(1, 0, 2).reshape(d, N_GROUPS * EXPERTS_PER_GROUP),
             jnp.zeros((d, ROUTER_LANES - N_GROUPS * (1 + EXPERTS_PER_GROUP)), F32)], axis=1)
        r_bias = jnp.concatenate(
            [router_g_b[l], router_e_b[l].reshape(-1),
             jnp.zeros((ROUTER_LANES - N_GROUPS * (1 + EXPERTS_PER_GROUP),), F32)]).reshape(1, ROUTER_LANES)
        bias_s = _rel_bias(att_rel_bias[l], att_rows, dec_seq, att_rows + dec_seq)
        bias_p = jnp.where(pair_ok, _rel_bias(att_rel_bias[l], PAD_ROWS, PAIR_ROWS, PAIR_BAND), NEG_INF)
        w = dict(
            n_mem=n_mem,
            norm_mix=row(norm_mix[l]), w_in=w_in[l].astype(BF16),
            qg=row(jnp.tile(att_q_gain[l], heads) * att_scale), kg=row(jnp.tile(att_k_gain[l], heads)),
            head_mean=head_mean, bc=bc, cre=cre, cim=cim, lam_re=lb_re, lam_im=lb_im,
            ssm_d=row(ssm_d[l]), bias_p=bias_p,
            bias_sc=bias_s[:, :, :att_rows], bias_sn=bias_s[:, :, att_rows:],
            mem_in_norm=row(mem_in_norm[l]), w_mem_k=w_mem_k[l].astype(BF16),
            w_mem_v=w_mem_v[l].astype(BF16), mem_k_gain=row(mem_k_gain[l]),
            w_glu=w_glu[l].astype(BF16), out_norm_ssm=row(out_norm_ssm[l]),
            out_norm_att=row(out_norm_att[l]), w_out=w_out[l].astype(BF16), norm_mem=row(norm_mem[l]),
            w_mem_q=w_mem_q[l].astype(BF16), mem_q_gain=row(mem_q_gain[l]),
            w_mem_o=w_mem_o[l].astype(BF16), norm_ffn=row(norm_ffn[l]),
            w_router=w_router.astype(BF16), router_bias=r_bias,
            exp_w_gate=exp_w_gate[l].astype(BF16), exp_w_up=exp_w_up[l].astype(BF16),
            exp_w_down=exp_w_down[l].astype(BF16),
        )
        xp, xs, p_new, s_new = _layer(
            xp, xs, mem_p,
            cache_attn_k[l].transpose(0, 2, 3, 1).reshape(dec_batch * heads, ATT_HEAD_DIM, att_rows),
            cache_attn_v[l].transpose(0, 2, 3, 1).reshape(dec_batch * heads, ATT_HEAD_DIM, att_rows),
            state_ssm_re[l].reshape(dec_batch, ns), state_ssm_im[l].reshape(dec_batch, ns),
            cache_mem_k[l].reshape(dec_batch * n_mem, d), cache_mem_v[l].reshape(dec_batch * n_mem, d),
            w, (batch, seq, dec_batch, dec_seq, d))
        p_out.append(p_new)
        s_out.append(s_new)

    sdt = state_ssm_re.dtype
    keep = min(PAD_ROWS, seq)
    kv_p = lambda a: a.reshape(batch, heads, ATT_HEAD_DIM, keep).transpose(0, 3, 1, 2)
    kv_s = lambda a: a.reshape(dec_seq, dec_batch, heads, ATT_HEAD_DIM).transpose(1, 0, 2, 3)
    st = lambda a: a.reshape(a.shape[0], n_g, n_p).astype(sdt)
    mkv = lambda a: a.reshape(batch, n_mem, MEM_HEADS, d // MEM_HEADS)
    stack = lambda f, outs, i: jnp.stack([f(o[i]) for o in outs])
    yp = xp.reshape(batch, seq, d)
    ys = xs.reshape(dec_seq, dec_batch, d).transpose(1, 0, 2)
    return (yp, ys,
            stack(kv_p, p_out, 0), stack(kv_p, p_out, 1), stack(st, p_out, 2), stack(st, p_out, 3),
            stack(mkv, p_out, 4), stack(mkv, p_out, 5),
            stack(kv_s, s_out, 0), stack(kv_s, s_out, 1), stack(st, s_out, 2), stack(st, s_out, 3))
```

```python
import functools
import math

import jax
import jax.numpy as jnp
from jax import lax
from jax.experimental import pallas as pl
from jax.experimental.pallas import tpu as pltpu

F32 = jnp.float32
BF16 = jnp.bfloat16

CHUNK = 64
N_PREV_CHUNKS = 8
BAND = (N_PREV_CHUNKS + 1) * CHUNK
PAD_ROWS = N_PREV_CHUNKS * CHUNK
PAIR_ROWS = 2 * CHUNK
PAIR_BAND = BAND + CHUNK
SOFTMAX_ROWS = 32
SSM_GROUP = 16
SSM_STATE = 64
ATT_HEAD_DIM = 64
REL_CLIP = 128
MEM_HEADS = 4
N_GROUPS = 4
EXPERTS_PER_GROUP = 8
EPS = 1e-6
NEG_INF = -1e30

LANES = 128
MXU_N = 256
VMEM_LIMIT = 56 * 1024 * 1024
ROUTER_LANES = LANES
MOE_TILE = 256
SAMPLE_STREAMS = 8
MOE_BUFS = 3
ROUTER_E0 = N_GROUPS
PAIRS_PER_GROUP = EXPERTS_PER_GROUP * (EXPERTS_PER_GROUP - 1) // 2
N_CLASSES = N_GROUPS * PAIRS_PER_GROUP
ROUTE_CLS, ROUTE_WA, ROUTE_WB = 64, 65, 66


def _cparams(sem):
    return pltpu.CompilerParams(dimension_semantics=sem, vmem_limit_bytes=VMEM_LIMIT)


def _rms(x, gain):
    ms = jnp.mean(x * x, axis=-1, keepdims=True)
    return x * lax.rsqrt(ms + EPS) * gain


def _sigmoid(x):
    return 1.0 / (1.0 + jnp.exp(-x))


def _gelu_tanh(x):
    c = math.sqrt(2.0 / math.pi)
    return 0.5 * x * (1.0 + jnp.tanh(c * (x + 0.044715 * (x * x * x))))


def _dot(a, b):
    return jnp.dot(a, b, preferred_element_type=F32)


def _dot_nt(a, b):
    return lax.dot_general(a, b, (((1,), (1,)), ((), ())), preferred_element_type=F32)


def _full(shape):
    n = len(shape)
    return pl.BlockSpec(shape, lambda *_: (0,) * n)


def _in_proj_kernel(x_ref, g_ref, w_ref, qg_ref, kg_ref, hm_ref,
                    u_ref, q_ref, k_ref, v_ref, kk_ref, vk_ref, *, width, nt):
    h = _rms(x_ref[...], g_ref[...]).astype(BF16)
    z = _dot(h, w_ref[...])
    u_ref[...] = z[:, :width].astype(u_ref.dtype)

    def head_norm(a, gain):
        ms = _dot((a * a).astype(BF16), hm_ref[...])
        return a * lax.rsqrt(ms + EPS) * gain

    q = head_norm(z[:, width:2 * width], qg_ref[...])
    k = head_norm(z[:, 2 * width:3 * width], kg_ref[...])
    v = z[:, 3 * width:]
    q_ref[...] = q.astype(BF16)
    k_ref[...] = k.astype(BF16)
    v_ref[...] = v.astype(BF16)

    @pl.when(pl.program_id(1) == nt - 1)
    def _():
        if len(kk_ref.shape) == 3:
            kt, vt = k.T, v.T
            for hd in range(width // ATT_HEAD_DIM):
                rows = slice(hd * ATT_HEAD_DIM, (hd + 1) * ATT_HEAD_DIM)
                kk_ref[hd] = kt[rows, :]
                vk_ref[hd] = vt[rows, :]
        else:
            kk_ref[...] = k
            vk_ref[...] = v


def _in_proj(x2d, gain, w_bf, qg, kg, hm, *, nb, nt, tm, keep_transposed):
    rows, d = x2d.shape
    width = w_bf.shape[1] // 4
    heads = width // ATT_HEAD_DIM
    tok = lambda b, t: (b * nt + t, 0)
    out_shape = (
        jax.ShapeDtypeStruct((rows, width), F32) if keep_transposed
        else jax.ShapeDtypeStruct((nt * tm, nb * width), BF16),
        jax.ShapeDtypeStruct((rows, width), BF16),
        jax.ShapeDtypeStruct((rows, width), BF16),
        jax.ShapeDtypeStruct((rows, width), BF16),
    )
    if keep_transposed:
        kept = jax.ShapeDtypeStruct((nb * heads, ATT_HEAD_DIM, tm), F32)
        keep = pl.BlockSpec((heads, ATT_HEAD_DIM, tm), lambda b, t: (b, 0, 0))
    else:
        kept = jax.ShapeDtypeStruct((nb * tm, width), F32)
        keep = pl.BlockSpec((tm, width), lambda b, t: (b, 0))
    out_shape = out_shape + (kept, kept)
    return pl.pallas_call(
        functools.partial(_in_proj_kernel, width=width, nt=nt),
        grid=(nb, nt),
        in_specs=[pl.BlockSpec((tm, d), tok), _full((1, d)), _full(w_bf.shape),
                  _full((1, width)), _full((1, width)), _full((width, width))],
        out_specs=(pl.BlockSpec((tm, width), tok if keep_transposed else (lambda b, t: (t, b))),
                   pl.BlockSpec((tm, width), tok), pl.BlockSpec((tm, width), tok),
                   pl.BlockSpec((tm, width), tok),
                   keep, keep),
        out_shape=out_shape,
        compiler_params=_cparams(("arbitrary", "arbitrary")),
        name="in_proj",
    )(x2d, gain, w_bf, qg, kg, hm)


def _ssm_chunk(u, bc_ref, cre_ref, cim_ref, lre_ref, lim_ref, d_ref, sre_ref, sim_ref, bu_ref, *, nb, tc, ns):
    n_tiles = 2 * ns // MXU_N
    for j in range(n_tiles):
        slab = (j % (n_tiles // 2)) // 2
        bu_ref[:, j * MXU_N:(j + 1) * MXU_N] = _dot(u[:, slab * LANES:(slab + 1) * LANES], bc_ref[j])

    cw = 8 * 1024 // nb
    for cb in range(ns // cw):
        c0 = cb * cw
        lre = jnp.broadcast_to(lre_ref[:, c0:c0 + cw], (nb, cw))
        lim = jnp.broadcast_to(lim_ref[:, c0:c0 + cw], (nb, cw))

        def step(t, carry, c0=c0, lre=lre, lim=lim):
            sr, si = carry
            r0 = pl.multiple_of(t * nb, nb)
            nr = lre * sr - lim * si + bu_ref[pl.ds(r0, nb), c0:c0 + cw]
            ni = lre * si + lim * sr + bu_ref[pl.ds(r0, nb), ns + c0:ns + c0 + cw]
            bu_ref[pl.ds(r0, nb), c0:c0 + cw] = nr
            bu_ref[pl.ds(r0, nb), ns + c0:ns + c0 + cw] = ni
            return nr, ni

        sr, si = lax.fori_loop(0, tc, step, (sre_ref[:, c0:c0 + cw], sim_ref[:, c0:c0 + cw]))
        sre_ref[:, c0:c0 + cw] = sr
        sim_ref[:, c0:c0 + cw] = si

    width = u.shape[1]
    kt = ns * MXU_N // width
    ys = []
    for n in range(width // MXU_N):
        s_re = bu_ref[:, n * kt:(n + 1) * kt].astype(BF16)
        s_im = bu_ref[:, ns + n * kt:ns + (n + 1) * kt].astype(BF16)
        cols = slice(n * MXU_N, (n + 1) * MXU_N)
        ys.append(_dot(s_re, cre_ref[n]) + _dot(s_im, cim_ref[n]) + d_ref[:, cols] * u[:, cols].astype(F32))
    return jnp.concatenate(ys, axis=1)


def _ssm_kernel(u_ref, bc_ref, cre_ref, cim_ref, lre_ref, lim_ref, d_ref, s0re_ref, s0im_ref,
                y_ref, sre_ref, sim_ref, bu_ref, *, nb, tc, ns):
    @pl.when(pl.program_id(0) == 0)
    def _():
        sre_ref[...] = s0re_ref[...]
        sim_ref[...] = s0im_ref[...]

    y_ref[...] = _ssm_chunk(u_ref[...], bc_ref, cre_ref, cim_ref, lre_ref, lim_ref, d_ref,
                            sre_ref, sim_ref, bu_ref, nb=nb, tc=tc, ns=ns)


def _ssm(u_rows, bc, cre, cim, lre, lim, dskip, s0re, s0im, *, nb, tc):
    rows, width = u_rows.shape
    ns = lre.shape[1]
    r = tc * nb
    return pl.pallas_call(
        functools.partial(_ssm_kernel, nb=nb, tc=tc, ns=ns),
        grid=(rows // r,),
        in_specs=[pl.BlockSpec((r, width), lambda i: (i, 0)), _full(bc.shape), _full(cre.shape),
                  _full(cim.shape), _full((1, ns)), _full((1, ns)), _full((1, width)),
                  _full((nb, ns)), _full((nb, ns))],
        out_specs=(pl.BlockSpec((r, width), lambda i: (i, 0)), _full((nb, ns)), _full((nb, ns))),
        out_shape=(jax.ShapeDtypeStruct((rows, width), F32),
                   jax.ShapeDtypeStruct((nb, ns), F32), jax.ShapeDtypeStruct((nb, ns), F32)),
        scratch_shapes=[pltpu.VMEM((r, 2 * ns), F32)],
        compiler_params=_cparams(("arbitrary",)),
        name="ssm",
    )(u_rows, bc, cre, cim, lre, lim, dskip, s0re, s0im)


def _ssm_tokens_kernel(u_hbm, bc_ref, cre_ref, cim_ref, lre_ref, lim_ref, d_ref,
                       y_hbm, sre_ref, sim_ref, bu_ref, ubuf, ybuf, isem, osem, *, nb, tc, ns, t_len):
    i = pl.program_id(0)
    n_steps = t_len // tc
    slot = i % 2
    width = ubuf.shape[3]

    def copy_in(step, s, b):
        return pltpu.make_async_copy(u_hbm.at[pl.ds(b * t_len + step * tc, tc)], ubuf.at[s, :, b, :], isem.at[s])

    def copy_out(step, s, b):
        return pltpu.make_async_copy(ybuf.at[s, :, b, :], y_hbm.at[pl.ds(b * t_len + step * tc, tc)], osem.at[s])

    @pl.when(i == 0)
    def _():
        sre_ref[...] = jnp.zeros_like(sre_ref)
        sim_ref[...] = jnp.zeros_like(sim_ref)
        for b in range(nb):
            copy_in(0, 0, b).start()

    @pl.when(i + 1 < n_steps)
    def _():
        for b in range(nb):
            copy_in(i + 1, 1 - slot, b).start()

    for b in range(nb):
        copy_in(i, slot, b).wait()
    u = ubuf[slot].reshape(tc * nb, width).astype(BF16)
    y = _ssm_chunk(u, bc_ref, cre_ref, cim_ref, lre_ref, lim_ref, d_ref, sre_ref, sim_ref, bu_ref,
                   nb=nb, tc=tc, ns=ns)

    @pl.when(i >= 2)
    def _():
        for b in range(nb):
            copy_out(i - 2, slot, b).wait()

    ybuf[slot] = y.reshape(tc, nb, width)
    for b in range(nb):
        copy_out(i, slot, b).start()

    @pl.when(i == n_steps - 1)
    def _():
        for b in range(nb):
            copy_out(i, slot, b).wait()
        if n_steps > 1:
            for b in range(nb):
                copy_out(i - 1, 1 - slot, b).wait()


def _ssm_tokens(u_tok, bc, cre, cim, lre, lim, dskip, *, nb, tc):
    rows, width = u_tok.shape
    ns = lre.shape[1]
    t_len = rows // nb
    r = tc * nb
    any_spec = pl.BlockSpec(memory_space=pl.ANY)
    return pl.pallas_call(
        functools.partial(_ssm_tokens_kernel, nb=nb, tc=tc, ns=ns, t_len=t_len),
        grid=(t_len // tc,),
        in_specs=[any_spec, _full(bc.shape), _full(cre.shape), _full(cim.shape), _full((1, ns)),
                  _full((1, ns)), _full((1, width))],
        out_specs=(any_spec, _full((nb, ns)), _full((nb, ns))),
        out_shape=(jax.ShapeDtypeStruct((rows, width), F32),
                   jax.ShapeDtypeStruct((nb, ns), F32), jax.ShapeDtypeStruct((nb, ns), F32)),
        scratch_shapes=[pltpu.VMEM((r, 2 * ns), F32), pltpu.VMEM((2, tc, nb, width), F32),
                        pltpu.VMEM((2, tc, nb, width), F32),
                        pltpu.SemaphoreType.DMA((2,)), pltpu.SemaphoreType.DMA((2,))],
        compiler_params=_cparams(("arbitrary",)),
        name="ssm_tokens",
    )(u_tok, bc, cre, cim, lre, lim, dskip)


def _half_select(shape):
    lane = lax.broadcasted_iota(jnp.int32, shape, 1)
    return lane < ATT_HEAD_DIM


def _head_masks(first):
    m0 = jnp.where(first, 1.0, 0.0).astype(BF16)
    return m0, (1.0 - m0.astype(F32)).astype(BF16)


def _band_prompt_kernel(q_ref, k_ref, v_ref, bias_ref, o_ref, kp_ref, vp_ref, s_scr, p_scr, l_scr, *, t_len):
    width = q_ref.shape[1]
    n_hp = width // LANES
    kp_ref[0:PAD_ROWS, :] = jnp.zeros((PAD_ROWS, width), BF16)
    vp_ref[0:PAD_ROWS, :] = jnp.zeros((PAD_ROWS, width), BF16)
    kp_ref[PAD_ROWS:, :] = k_ref[...]
    vp_ref[PAD_ROWS:, :] = v_ref[...]
    first = _half_select((PAIR_ROWS, LANES))
    head_mask = _head_masks(first)
    col = lax.broadcasted_iota(jnp.int32, (SOFTMAX_ROWS, PAIR_BAND), 1)

    def pair(pi, carry, *, masked):
        r0 = pl.multiple_of(pi * PAIR_ROWS, PAIR_ROWS)
        for hp in range(n_hp):
            lanes = slice(hp * LANES, (hp + 1) * LANES)
            qp = q_ref[pl.ds(r0, PAIR_ROWS), lanes]
            qs = jnp.concatenate([qp * head_mask[0], qp * head_mask[1]], axis=0)
            s_scr[2 * hp * PAIR_ROWS:2 * (hp + 1) * PAIR_ROWS, :] = _dot_nt(
                qs, kp_ref[pl.ds(r0, PAIR_BAND), lanes])
        for h in range(2 * n_hp):
            for rb in range(0, PAIR_ROWS, SOFTMAX_ROWS):
                rows = slice(h * PAIR_ROWS + rb, h * PAIR_ROWS + rb + SOFTMAX_ROWS)
                s = s_scr[rows, :] + bias_ref[h, rb:rb + SOFTMAX_ROWS, :]
                if masked:
                    s = jnp.where(col >= PAD_ROWS - r0, s, NEG_INF)
                p = jnp.exp(s - jnp.max(s, axis=-1, keepdims=True))
                p_scr[rows, :] = p.astype(BF16)
                l_scr[rows, :] = jnp.broadcast_to(1.0 / jnp.sum(p, axis=-1, keepdims=True),
                                                  (SOFTMAX_ROWS, LANES))
        for hp in range(n_hp):
            lanes = slice(hp * LANES, (hp + 1) * LANES)
            rows = slice(2 * hp * PAIR_ROWS, 2 * (hp + 1) * PAIR_ROWS)
            o2 = _dot(p_scr[rows, :], vp_ref[pl.ds(r0, PAIR_BAND), lanes]) * l_scr[rows, :]
            o_ref[pl.ds(r0, PAIR_ROWS), lanes] = jnp.where(
                first, o2[:PAIR_ROWS], o2[PAIR_ROWS:]).astype(BF16)
        return carry

    n_masked = PAD_ROWS // PAIR_ROWS
    lax.fori_loop(0, n_masked, functools.partial(pair, masked=True), 0)
    lax.fori_loop(n_masked, t_len // PAIR_ROWS, functools.partial(pair, masked=False), 0)


def _band_prompt(q, k, v, bias, *, nb, t_len):
    width = q.shape[1]
    heads = width // ATT_HEAD_DIM
    blk = pl.BlockSpec((t_len, width), lambda b: (b, 0))
    return pl.pallas_call(
        functools.partial(_band_prompt_kernel, t_len=t_len),
        grid=(nb,),
        in_specs=[blk, blk, blk, _full(bias.shape)],
        out_specs=blk,
        out_shape=jax.ShapeDtypeStruct(q.shape, BF16),
        scratch_shapes=[pltpu.VMEM((t_len + PAD_ROWS, width), BF16),
                        pltpu.VMEM((t_len + PAD_ROWS, width), BF16),
                        pltpu.VMEM((heads * PAIR_ROWS, PAIR_BAND), F32),
                        pltpu.VMEM((heads * PAIR_ROWS, PAIR_BAND), BF16),
                        pltpu.VMEM((heads * PAIR_ROWS, LANES), F32)],
        compiler_params=_cparams(("arbitrary",)),
        name="band_prompt",
    )(q, k, v, bias)


def _band_sample_kernel(q_ref, k_ref, v_ref, ck_ref, cv_ref, bc_ref, bn_ref, o_ref):
    for h in range(ck_ref.shape[0]):
        cols = slice(h * ATT_HEAD_DIM, (h + 1) * ATT_HEAD_DIM)
        qh = q_ref[:, cols]
        sc = _dot(qh, ck_ref[h].astype(BF16)) + bc_ref[h]
        sn = _dot_nt(qh, k_ref[:, cols]) + bn_ref[h]
        m = jnp.maximum(jnp.max(sc, axis=-1, keepdims=True), jnp.max(sn, axis=-1, keepdims=True))
        pc = jnp.exp(sc - m)
        pn = jnp.exp(sn - m)
        l = jnp.sum(pc, axis=-1, keepdims=True) + jnp.sum(pn, axis=-1, keepdims=True)
        o = _dot_nt(pc.astype(BF16), cv_ref[h].astype(BF16)) + _dot(pn.astype(BF16), v_ref[:, cols])
        o_ref[:, cols] = (o / l).astype(BF16)


def _band_sample(q_tm, k_tm, v_tm, cache_kt, cache_vt, bias_c, bias_n, *, nb, s_len):
    width = q_tm.shape[1] // nb
    heads = cache_kt.shape[0] // nb
    col = pl.BlockSpec((s_len, width), lambda b: (0, b))
    cache = pl.BlockSpec((heads,) + cache_kt.shape[1:], lambda b: (b, 0, 0))
    return pl.pallas_call(
        _band_sample_kernel,
        grid=(nb,),
        in_specs=[col, col, col, cache, cache, _full(bias_c.shape), _full(bias_n.shape)],
        out_specs=col,
        out_shape=jax.ShapeDtypeStruct(q_tm.shape, BF16),
        compiler_params=_cparams(("arbitrary",)),
        name="band_sample",
    )(q_tm, k_tm, v_tm, cache_kt, cache_vt, bias_c, bias_n)


def _mem_kv_kernel(m_ref, g_ref, wk_ref, wv_ref, kg_ref, k_ref, v_ref, kb_ref, vb_ref):
    m = _rms(m_ref[...], g_ref[...]).astype(BF16)
    k = _dot(m, wk_ref[...])
    v = _dot(m, wv_ref[...])
    hd = kg_ref.shape[1]
    for h in range(k.shape[1] // hd):
        cols = slice(h * hd, (h + 1) * hd)
        kh = _rms(k[:, cols], kg_ref[...])
        k_ref[:, h, :] = kh
        v_ref[:, h, :] = v[:, cols]
        kb_ref[:, cols] = kh.astype(BF16)
    vb_ref[...] = v.astype(BF16)


def _mem_kv(mem2d, gain, wk, wv, kgain, *, tm):
    rows, d = mem2d.shape
    hd = kgain.shape[1]
    blk = pl.BlockSpec((tm, d), lambda i: (i, 0))
    hblk = pl.BlockSpec((tm, d // hd, hd), lambda i: (i, 0, 0))
    return pl.pallas_call(
        _mem_kv_kernel,
        grid=(rows // tm,),
        in_specs=[blk, _full((1, d)), _full(wk.shape), _full(wv.shape), _full(kgain.shape)],
        out_specs=(hblk, hblk, blk, blk),
        out_shape=(jax.ShapeDtypeStruct((rows, d // hd, hd), F32),
                   jax.ShapeDtypeStruct((rows, d // hd, hd), F32),
                   jax.ShapeDtypeStruct((rows, d), BF16), jax.ShapeDtypeStruct((rows, d), BF16)),
        compiler_params=_cparams(("arbitrary",)),
        name="mem_kv",
    )(mem2d, gain, wk, wv, kgain)


def _route(logits):
    lane = lax.broadcasted_iota(jnp.int32, logits.shape, 1).astype(F32)
    big = float(ROUTER_LANES)
    is_g = lane < N_GROUPS
    lg = jnp.where(is_g, logits, NEG_INF)
    gmax = jnp.max(lg, axis=-1, keepdims=True)
    p1 = 1.0 / jnp.sum(jnp.where(is_g, jnp.exp(lg - gmax), 0.0), axis=-1, keepdims=True)
    g_idx = jnp.min(jnp.where(lg == gmax, lane, big), axis=-1, keepdims=True)
    lo = ROUTER_E0 + g_idx * EXPERTS_PER_GROUP
    le = jnp.where((lane >= lo) & (lane < lo + EXPERTS_PER_GROUP), logits, NEG_INF)
    v1 = jnp.max(le, axis=-1, keepdims=True)
    i1 = jnp.min(jnp.where(le == v1, lane, big), axis=-1, keepdims=True)
    le2 = jnp.where(lane == i1, NEG_INF, le)
    v2 = jnp.max(le2, axis=-1, keepdims=True)
    i2 = jnp.min(jnp.where(le2 == v2, lane, big), axis=-1, keepdims=True)
    e2 = jnp.exp(v2 - v1)
    w1 = p1 / (1.0 + e2)
    w2 = p1 * e2 / (1.0 + e2)
    gates = jnp.where(lane == i1, w1, 0.0) + jnp.where(lane == i2, w2, 0.0)
    a = jnp.minimum(i1, i2) - lo
    b = jnp.maximum(i1, i2) - lo
    pair = a * (2 * EXPERTS_PER_GROUP - 1 - a) * 0.5 + (b - a - 1.0)
    cls = g_idx * PAIRS_PER_GROUP + pair
    w_a = jnp.where(i1 < i2, w1, w2)
    w_b = jnp.where(i1 < i2, w2, w1)
    return (gates + jnp.where(lane == ROUTE_CLS, cls, 0.0) + jnp.where(lane == ROUTE_WA, w_a, 0.0)
            + jnp.where(lane == ROUTE_WB, w_b, 0.0))


def _mix_mem_kernel(x_ref, ys_ref, ya_ref, mk_ref, mv_ref, wglu_ref, ons_ref, ona_ref, wout_ref,
                    nmem_ref, wq_ref, qg_ref, wo_ref, nffn_ref, wr_ref, rb_ref,
                    *outs, routed, nbat):
    if routed:
        x2_ref, o_scr = outs
    else:
        x2_ref, xn_ref, gate_ref, o_scr = outs
    tm = x_ref.shape[0]
    d = x_ref.shape[1] // nbat
    width = ys_ref.shape[1] // nbat
    n_mem = mk_ref.shape[0] // nbat

    def stacked(ref, w):
        return jnp.concatenate([ref[:, j * w:(j + 1) * w] for j in range(nbat)], axis=0)

    g = _dot(_gelu_tanh(stacked(ys_ref, width)).astype(BF16), wglu_ref[...])
    y_s = g[:, :width] * _sigmoid(g[:, width:])
    cat_s = _rms(y_s, ons_ref[...]).astype(BF16)
    cat_a = _rms(stacked(ya_ref, width).astype(F32), ona_ref[...]).astype(BF16)
    x1 = stacked(x_ref, d) + _dot(cat_s, wout_ref[0:width, :]) + _dot(cat_a, wout_ref[width:, :])

    q = _dot(_rms(x1, nmem_ref[...]).astype(BF16), wq_ref[...])
    hd = qg_ref.shape[1]
    scale = hd ** -0.5
    for j in range(nbat):
        rows = slice(j * tm, (j + 1) * tm)
        mem = slice(j * n_mem, (j + 1) * n_mem)
        for h in range(d // hd):
            cols = slice(h * hd, (h + 1) * hd)
            qh = _rms(q[rows, cols], qg_ref[...]).astype(BF16)
            s = _dot_nt(qh, mk_ref[mem, cols].astype(BF16)) * scale
            p = jnp.exp(s - jnp.max(s, axis=-1, keepdims=True))
            l = jnp.sum(p, axis=-1, keepdims=True)
            o_scr[rows, cols] = (_dot(p.astype(BF16), mv_ref[mem, cols].astype(BF16)) / l).astype(BF16)
    acc = x1 + _dot(o_scr[...], wo_ref[...])

    xn = _rms(acc, nffn_ref[...])
    xh = xn.astype(BF16)
    logits = _dot(xh, wr_ref[...]) + rb_ref[...]
    record = _route(logits)
    if routed:
        x2_ref[:, :d] = acc
        x2_ref[:, d:] = record
    else:
        for j in range(nbat):
            rows = slice(j * tm, (j + 1) * tm)
            x2_ref[:, j * d:(j + 1) * d] = acc[rows]
            xn_ref[:, j * d:(j + 1) * d] = xh[rows]
            gate_ref[:, j * ROUTER_LANES:(j + 1) * ROUTER_LANES] = record[rows]


def _mix_mem(x, ys, ya, mk, mv, weights, *, grid, tm, row_map, ssm_map, mem_map, routed, nbat=1):
    d = weights["w_out"].shape[1]
    width = weights["w_glu"].shape[0]
    names = ("w_glu", "out_norm_ssm", "out_norm_att", "w_out", "norm_mem", "w_mem_q", "mem_q_gain",
             "w_mem_o", "norm_ffn", "w_router", "router_bias")
    ws = [weights[n] for n in names]
    mem_rows = weights["n_mem"]
    xspec = pl.BlockSpec((tm, nbat * d), row_map)
    hspec = pl.BlockSpec((tm, nbat * width), row_map)
    sspec = pl.BlockSpec((tm, nbat * width), ssm_map)
    mspec = pl.BlockSpec((nbat * mem_rows, d), mem_map)
    n_col = x.shape[1] // d
    if routed:
        assert nbat == 1
        out_specs = pl.BlockSpec((tm, d + ROUTER_LANES), row_map)
        out_shape = jax.ShapeDtypeStruct((x.shape[0], n_col * (d + ROUTER_LANES)), F32)
    else:
        out_specs = (xspec, xspec, pl.BlockSpec((tm, nbat * ROUTER_LANES), row_map))
        out_shape = (jax.ShapeDtypeStruct(x.shape, F32), jax.ShapeDtypeStruct(x.shape, BF16),
                     jax.ShapeDtypeStruct((x.shape[0], n_col * ROUTER_LANES), F32))
    return pl.pallas_call(
        functools.partial(_mix_mem_kernel, routed=routed, nbat=nbat),
        grid=grid,
        in_specs=[xspec, sspec, hspec, mspec, mspec] + [_full(w.shape) for w in ws],
        out_specs=out_specs,
        out_shape=out_shape,
        scratch_shapes=[pltpu.VMEM((nbat * tm, d), BF16)],
        compiler_params=_cparams(("arbitrary",) * len(grid)),
        name="mix_mem",
    )(x, ys, ya, mk, mv, *ws)


def _moe_kernel(xn_ref, x2_ref, gate_ref, wg_ref, wu_ref, wd_ref, o_ref):
    e = pl.program_id(1)

    @pl.when(e == 0)
    def _():
        o_ref[...] = x2_ref[...]

    gates = gate_ref[...]
    lane = lax.broadcasted_iota(jnp.int32, gates.shape, 1)
    ge = jnp.sum(jnp.where(lane == e + ROUTER_E0, gates, 0.0), axis=-1, keepdims=True)
    xn = xn_ref[...]
    a = _dot(xn, wg_ref[...])
    h = a * _sigmoid(a) * _dot(xn, wu_ref[...])
    o_ref[...] += ge * _dot(h.astype(BF16), wd_ref[...])


def _moe(xn, x2, gates, wg, wu, wd, *, tm):
    rows, d = xn.shape
    n_exp, _, dff = wg.shape
    row = lambda i, e: (i, 0)
    return pl.pallas_call(
        _moe_kernel,
        grid=(rows // tm, n_exp),
        in_specs=[pl.BlockSpec((tm, d), row), pl.BlockSpec((tm, d), row),
                  pl.BlockSpec((tm, ROUTER_LANES), row),
                  pl.BlockSpec((None, d, dff), lambda i, e: (e, 0, 0)),
                  pl.BlockSpec((None, d, dff), lambda i, e: (e, 0, 0)),
                  pl.BlockSpec((None, dff, d), lambda i, e: (e, 0, 0))],
        out_specs=pl.BlockSpec((tm, d), row),
        out_shape=jax.ShapeDtypeStruct((rows, d), F32),
        compiler_params=_cparams(("arbitrary", "arbitrary")),
        name="moe",
    )(xn, x2, gates, wg, wu, wd)


def _moe_routed_kernel(tile_ref, ea_ref, eb_ref, lo_ref, hi_ref, valid_ref, tok_ref,
                       x_hbm, nffn_ref, wga_ref, wua_ref, wda_ref, wgb_ref, wub_ref, wdb_ref,
                       y_hbm, *scratch, tm, n_tiles):
    nb = MOE_BUFS
    xbuf, obuf = scratch[:nb], scratch[nb:2 * nb]
    acc_ref, gsem, ssem = scratch[2 * nb:]
    w = pl.program_id(0)
    t = tile_ref[w]
    lo = lo_ref[w]
    hi = hi_ref[w]
    valid = valid_ref[w] == 1
    d = y_hbm.shape[1]

    def row_copy_in(tile, s, r):
        tok = tok_ref[tile * tm + r]
        return pltpu.make_async_copy(x_hbm.at[pl.ds(tok, 1)], xbuf[s].at[pl.ds(r, 1)], gsem.at[s])

    def row_copy_out(tile, s, r):
        tok = tok_ref[tile * tm + r]
        return pltpu.make_async_copy(obuf[s].at[pl.ds(r, 1)], y_hbm.at[pl.ds(tok, 1)], ssem.at[s])

    def start_gather(tile, s):
        for r in range(tm):
            row_copy_in(tile, s, r).start()

    def start_scatter(tile, s):
        for r in range(tm):
            row_copy_out(tile, s, r).start()

    def wait_gather(s):
        pltpu.make_async_copy(x_hbm.at[pl.ds(0, tm)], xbuf[s], gsem.at[s]).wait()

    def wait_scatter(s):
        pltpu.make_async_copy(obuf[s], y_hbm.at[pl.ds(0, tm)], ssem.at[s]).wait()

    def compute(s, first):
        xe = xbuf[s][...]
        xn = _rms(xe[:, :d], nffn_ref[...]).astype(BF16)
        row = lax.broadcasted_iota(jnp.int32, (tm, 1), 0)
        inseg = jnp.logical_and(row >= lo, row < hi)
        w_a = jnp.where(inseg, xe[:, d + ROUTE_WA:d + ROUTE_WA + 1], 0.0)
        w_b = jnp.where(inseg, xe[:, d + ROUTE_WB:d + ROUTE_WB + 1], 0.0)
        a = _dot(xn, wga_ref[...])
        h_a = (a * _sigmoid(a) * _dot(xn, wua_ref[...])).astype(BF16)
        b = _dot(xn, wgb_ref[...])
        h_b = (b * _sigmoid(b) * _dot(xn, wub_ref[...])).astype(BF16)
        upd = w_a * _dot(h_a, wda_ref[...]) + w_b * _dot(h_b, wdb_ref[...])
        if first:
            acc_ref[...] = upd
        else:
            acc_ref[...] += upd

    def when(*conds):
        c = conds[0]
        for extra in conds[1:]:
            c = jnp.logical_and(c, extra)
        return pl.when(c)

    ahead = nb - 1

    @pl.when(w == 0)
    def _():
        for k in range(ahead):
            start_gather(k, k)

    is_first = jnp.logical_and(valid, lo == 0)
    is_last = jnp.logical_and(valid, hi == tm)
    last_tile = n_tiles - 1
    for s in range(nb):
        mine = t % nb == s
        nxt, prv = (s + ahead) % nb, (s - 1) % nb

        if s == 0:
            @when(is_first, t == 0)
            def _(nxt=nxt):
                wait_gather(0)
                start_gather(ahead, nxt)
                compute(0, True)

        @when(is_first, mine, t > 0, t + ahead <= last_tile)
        def _(s=s, nxt=nxt, prv=prv):
            wait_gather(s)
            start_gather(t + ahead, nxt)
            start_scatter(t - 1, prv)
            compute(s, True)

        @when(is_first, mine, t + ahead > last_tile)
        def _(s=s, prv=prv):
            wait_gather(s)
            start_scatter(t - 1, prv)
            compute(s, True)

        @when(valid, mine, lo > 0)
        def _(s=s):
            compute(s, False)

        @when(is_last, mine)
        def _(s=s):
            @pl.when(t >= nb)
            def _():
                wait_scatter(s)

            obuf[s][...] = xbuf[s][:, :d] + acc_ref[...]

            if s == last_tile % nb:
                @pl.when(t == last_tile)
                def _():
                    start_scatter(t, s)
                    for k in range(nb):
                        wait_scatter(k)


def _moe_routed(x2ext, plan, nffn, wg, wu, wd, *, tm):
    rows, de = x2ext.shape
    d = de - ROUTER_LANES
    n_exp, _, dff = wg.shape
    n_tiles = rows // tm
    assert rows % tm == 0 and n_tiles > 2 * MOE_BUFS
    n_items = plan[0].shape[0]
    ea = lambda w, tile, ea_, eb_, *_: (ea_[w], 0, 0)
    eb = lambda w, tile, ea_, eb_, *_: (eb_[w], 0, 0)
    grid_spec = pltpu.PrefetchScalarGridSpec(
        num_scalar_prefetch=len(plan),
        grid=(n_items,),
        in_specs=[pl.BlockSpec(memory_space=pl.ANY),
                  pl.BlockSpec((1, d), lambda w, *_: (0, 0)),
                  pl.BlockSpec((None, d, dff), ea), pl.BlockSpec((None, d, dff), ea),
                  pl.BlockSpec((None, dff, d), ea),
                  pl.BlockSpec((None, d, dff), eb), pl.BlockSpec((None, d, dff), eb),
                  pl.BlockSpec((None, dff, d), eb)],
        out_specs=pl.BlockSpec(memory_space=pl.ANY),
        scratch_shapes=([pltpu.VMEM((tm, de), F32)] * MOE_BUFS + [pltpu.VMEM((tm, d), F32)] * MOE_BUFS
                        + [pltpu.VMEM((tm, d), F32),
                           pltpu.SemaphoreType.DMA((MOE_BUFS,)), pltpu.SemaphoreType.DMA((MOE_BUFS,))]),
    )
    return pl.pallas_call(
        functools.partial(_moe_routed_kernel, tm=tm, n_tiles=n_tiles),
        grid_spec=grid_spec,
        out_shape=jax.ShapeDtypeStruct((rows, d), F32),
        compiler_params=_cparams(("arbitrary",)),
        name="moe_routed",
    )(*plan, x2ext, nffn, wg, wu, wd, wg, wu, wd)


def _route_plan(cls, *, tm):
    n = cls.shape[0]
    n_tiles = n // tm
    order = jnp.argsort(cls).astype(jnp.int32)
    classes = jnp.arange(N_CLASSES, dtype=jnp.int32)
    class_end = jnp.sum((cls[:, None] <= classes[None, :]).astype(jnp.int32), axis=0)
    class_start = jnp.concatenate([jnp.zeros((1,), jnp.int32), class_end[:-1]])
    bounds = jnp.concatenate([jnp.arange(n_tiles, dtype=jnp.int32) * tm,
                              jnp.where(class_end > class_start, class_start, n)])
    n_items = bounds.shape[0]
    idx = jnp.arange(n_items, dtype=jnp.int32)
    before = jnp.logical_or(bounds[None, :] < bounds[:, None],
                            jnp.logical_and(bounds[None, :] == bounds[:, None], idx[None, :] < idx[:, None]))
    rank = jnp.sum(before.astype(jnp.int32), axis=1)
    start = jnp.sum(jnp.where(rank[:, None] == idx[None, :], bounds[:, None], 0), axis=0)
    stop = jnp.concatenate([start[1:], jnp.full((1,), n, jnp.int32)])
    valid = stop > start
    tile = jnp.minimum(start // tm, n_tiles - 1)
    lo = start - tile * tm
    hi = stop - tile * tm
    c = jnp.minimum(jnp.sum((class_end[None, :] <= start[:, None]).astype(jnp.int32), axis=1),
                    N_CLASSES - 1)
    g = c // PAIRS_PER_GROUP
    pair = c % PAIRS_PER_GROUP
    a = jnp.zeros_like(pair)
    for k in range(1, EXPERTS_PER_GROUP - 1):
        a = a + (pair >= k * (2 * EXPERTS_PER_GROUP - 1 - k) // 2).astype(jnp.int32)
    b = pair - a * (2 * EXPERTS_PER_GROUP - 1 - a) // 2 + a + 1
    e_a = g * EXPERTS_PER_GROUP + a
    e_b = g * EXPERTS_PER_GROUP + b
    i32 = lambda v: v.astype(jnp.int32)
    return (i32(tile), i32(e_a), i32(e_b), i32(lo), i32(hi), i32(valid), order)


def _ssm_params(lam_re, lam_im, log_step, b_re, b_im, c_re, c_im):
    n_g, n_p = lam_re.shape
    step = jnp.exp(log_step.astype(F32))[:, None]
    mag = jnp.exp(lam_re * step)
    lb_re = mag * jnp.cos(lam_im * step)
    lb_im = mag * jnp.sin(lam_im * step)
    den = lam_re * lam_re + lam_im * lam_im
    f_re = ((lb_re - 1.0) * lam_re + lb_im * lam_im) / den
    f_im = (lb_im * lam_re - (lb_re - 1.0) * lam_im) / den
    bb_re = f_re[..., None] * b_re - f_im[..., None] * b_im
    bb_im = f_re[..., None] * b_im + f_im[..., None] * b_re
    eye = jnp.eye(n_g, dtype=F32)
    ns = n_g * n_p
    width = n_g * SSM_GROUP
    b_full = jnp.concatenate(
        [jnp.einsum("hg,gpc->hcgp", eye, bb_re).reshape(width, ns),
         jnp.einsum("hg,gpc->hcgp", eye, bb_im).reshape(width, ns)], axis=1)
    n_tiles = 2 * ns // MXU_N
    bc = jnp.stack([
        b_full[((j % (n_tiles // 2)) // 2) * LANES:((j % (n_tiles // 2)) // 2 + 1) * LANES,
               j * MXU_N:(j + 1) * MXU_N] for j in range(n_tiles)]).astype(BF16)
    c_full_re = jnp.einsum("gh,gcp->gphc", eye, c_re).reshape(ns, width)
    c_full_im = -jnp.einsum("gh,gcp->gphc", eye, c_im).reshape(ns, width)
    kt = ns * MXU_N // width
    tiles = range(width // MXU_N)
    cre = jnp.stack([c_full_re[n * kt:(n + 1) * kt, n * MXU_N:(n + 1) * MXU_N] for n in tiles]).astype(BF16)
    cim = jnp.stack([c_full_im[n * kt:(n + 1) * kt, n * MXU_N:(n + 1) * MXU_N] for n in tiles]).astype(BF16)
    return bc, cre, cim, lb_re.reshape(1, ns), lb_im.reshape(1, ns)


def _rel_bias(rel_bias, q0, n_q, n_k):
    n_r = n_q + n_k - 1
    dist = q0 + n_q - 1 - jnp.arange(n_r)
    r = rel_bias.astype(F32)[:, jnp.clip(dist, -REL_CLIP, REL_CLIP) + REL_CLIP]
    r = jnp.pad(r, ((0, 0), (0, 1)))
    rows = jnp.tile(r, (1, n_q))[:, :n_q * n_r].reshape(-1, n_q, n_r)
    return rows[:, :, n_q - 1:n_q - 1 + n_k]


def _layer(xp, xs, mem_p, ck, cv, s_re, s_im, cmk, cmv, w, dims):
    batch, seq, dec_batch, dec_seq, d = dims
    width = w["w_in"].shape[1] // 4
    ns = w["lam_re"].shape[1]
    n_mem = w["n_mem"]
    tm_p = 512
    nt_p = seq // tm_p
    n_dec = dec_batch * dec_seq

    u, q, k, v, kk, vk = _in_proj(xp, w["norm_mix"], w["w_in"], w["qg"], w["kg"], w["head_mean"],
                                  nb=batch, nt=nt_p, tm=tm_p, keep_transposed=True)
    y_ssm, pre, pim = _ssm_tokens(u, w["bc"], w["cre"], w["cim"], w["lam_re"], w["lam_im"], w["ssm_d"],
                                  nb=batch, tc=64)
    y_att = _band_prompt(q, k, v, w["bias_p"], nb=batch, t_len=seq)
    mk, mv, mkb, mvb = _mem_kv(mem_p, w["mem_in_norm"], w["w_mem_k"], w["w_mem_v"], w["mem_k_gain"], tm=512)
    tm_d = 512
    nt_d = seq // tm_d
    x2ext = _mix_mem(xp, y_ssm, y_att, mkb, mvb, w,
                     grid=(batch, nt_d), tm=tm_d,
                     row_map=lambda b, t: (b * nt_d + t, 0), ssm_map=lambda b, t: (b * nt_d + t, 0),
                     mem_map=lambda b, t: (b, 0), routed=True)
    plan = _route_plan(x2ext[:, d + ROUTE_CLS].astype(jnp.int32), tm=MOE_TILE)
    yp = _moe_routed(x2ext, plan, w["norm_ffn"], w["exp_w_gate"], w["exp_w_up"], w["exp_w_down"],
                     tm=MOE_TILE)

    us, qs, ks, vs, kks, vks = _in_proj(xs, w["norm_mix"], w["w_in"], w["qg"], w["kg"], w["head_mean"],
                                        nb=1, nt=1, tm=n_dec, keep_transposed=False)
    ys_ssm, sre, sim = _ssm(us, w["bc"], w["cre"], w["cim"], w["lam_re"], w["lam_im"], w["ssm_d"],
                            s_re, s_im, nb=dec_batch, tc=dec_seq)
    tmv = lambda a: a.reshape(dec_seq, dec_batch * a.shape[1])
    ys_att = _band_sample(tmv(qs), tmv(ks), tmv(vs), ck, cv, w["bias_sc"], w["bias_sn"],
                          nb=dec_batch, s_len=dec_seq)
    x2s, xns, gates_s = _mix_mem(tmv(xs), tmv(ys_ssm), ys_att, cmk, cmv, w,
                                 grid=(dec_batch // SAMPLE_STREAMS,), tm=dec_seq,
                                 row_map=lambda b: (0, b), ssm_map=lambda b: (0, b),
                                 mem_map=lambda b: (b, 0), routed=False, nbat=SAMPLE_STREAMS)
    ys = _moe(xns.reshape(n_dec, d), x2s.reshape(n_dec, d), gates_s.reshape(n_dec, ROUTER_LANES),
              w["exp_w_gate"], w["exp_w_up"], w["exp_w_down"], tm=n_dec)
    return yp, ys, (kk, vk, pre, pim, mk, mv), (kks, vks, sre, sim)


def kernel(x_prompt, x_sample, mem_prompt, cache_attn_k, cache_attn_v, state_ssm_re, state_ssm_im, cache_mem_k, cache_mem_v, norm_mix, w_in, ssm_lambda_re, ssm_lambda_im, ssm_log_step, ssm_b_re, ssm_b_im, ssm_c_re, ssm_c_im, ssm_d, w_glu, att_q_gain, att_k_gain, att_rel_bias, out_norm_ssm, out_norm_att, w_out, norm_mem, mem_in_norm, w_mem_q, w_mem_k, w_mem_v, w_mem_o, mem_q_gain, mem_k_gain, norm_ffn, router_g_w, router_g_b, router_e_w, router_e_b, exp_w_gate, exp_w_up, exp_w_down):
    depth = norm_mix.shape[0]
    batch, seq, d = x_prompt.shape
    dec_batch, dec_seq, _ = x_sample.shape
    n_mem = mem_prompt.shape[1]
    att_rows = cache_attn_k.shape[2]
    n_g, n_p = ssm_lambda_re.shape[1:]
    width = n_g * SSM_GROUP
    heads = width // ATT_HEAD_DIM
    ns = n_g * n_p
    assert seq % 512 == 0 and att_rows == PAD_ROWS and seq >= PAD_ROWS
    assert (dec_batch * dec_seq) % 8 == 0 and dec_seq % 16 == 0

    xp = x_prompt.reshape(batch * seq, d)
    xs = x_sample.transpose(1, 0, 2).reshape(dec_seq * dec_batch, d)
    mem_p = mem_prompt.reshape(batch * n_mem, d)
    row = lambda a: a.reshape(1, -1).astype(F32)
    head_mean = jnp.kron(jnp.eye(heads, dtype=F32),
                         jnp.full((ATT_HEAD_DIM, ATT_HEAD_DIM), 1.0 / ATT_HEAD_DIM, F32)).astype(BF16)
    pr = jnp.arange(PAIR_ROWS)[:, None]
    pc = jnp.arange(PAIR_BAND)[None, :]
    pair_ok = jnp.where(pr < CHUNK, pc < BAND, pc >= CHUNK)
    att_scale = ATT_HEAD_DIM ** -0.5

    p_out, s_out = [], []
    for l in range(depth):
        bc, cre, cim, lb_re, lb_im = _ssm_params(ssm_lambda_re[l], ssm_lambda_im[l], ssm_log_step[l],
                                                 ssm_b_re[l], ssm_b_im[l], ssm_c_re[l], ssm_c_im[l])
        w_router = jnp.concatenate(
            [router_g_w[l], router_e_w[l].transpose(1, 0, 2).reshape(d, N_GROUPS * EXPERTS_PER_GROUP),
             jnp.zeros((d, ROUTER_LANES - N_GROUPS * (1 + EXPERTS_PER_GROUP)), F32)], axis=1)
        r_bias = jnp.concatenate(
            [router_g_b[l], router_e_b[l].reshape(-1),
             jnp.zeros((ROUTER_LANES - N_GROUPS * (1 + EXPERTS_PER_GROUP),), F32)]).reshape(1, ROUTER_LANES)
        bias_s = _rel_bias(att_rel_bias[l], att_rows, dec_seq, att_rows + dec_seq)
        bias_p = jnp.where(pair_ok, _rel_bias(att_rel_bias[l], PAD_ROWS, PAIR_ROWS, PAIR_BAND), NEG_INF)
        w = dict(
            n_mem=n_mem,
            norm_mix=row(norm_mix[l]), w_in=w_in[l].astype(BF16),
            qg=row(jnp.tile(att_q_gain[l], heads) * att_scale), kg=row(jnp.tile(att_k_gain[l], heads)),
            head_mean=head_mean, bc=bc, cre=cre, cim=cim, lam_re=lb_re, lam_im=lb_im,
            ssm_d=row(ssm_d[l]), bias_p=bias_p,
            bias_sc=bias_s[:, :, :att_rows], bias_sn=bias_s[:, :, att_rows:],
            mem_in_norm=row(mem_in_norm[l]), w_mem_k=w_mem_k[l].astype(BF16),
            w_mem_v=w_mem_v[l].astype(BF16), mem_k_gain=row(mem_k_gain[l]),
            w_glu=w_glu[l].astype(BF16), out_norm_ssm=row(out_norm_ssm[l]),
            out_norm_att=row(out_norm_att[l]), w_out=w_out[l].astype(BF16), norm_mem=row(norm_mem[l]),
            w_mem_q=w_mem_q[l].astype(BF16), mem_q_gain=row(mem_q_gain[l]),
            w_mem_o=w_mem_o[l].astype(BF16), norm_ffn=row(norm_ffn[l]),
            w_router=w_router.astype(BF16), router_bias=r_bias,
            exp_w_gate=exp_w_gate[l].astype(BF16), exp_w_up=exp_w_up[l].astype(BF16),
            exp_w_down=exp_w_down[l].astype(BF16),
        )
        xp, xs, p_new, s_new = _layer(
            xp, xs, mem_p,
            cache_attn_k[l].transpose(0, 2, 3, 1).reshape(dec_batch * heads, ATT_HEAD_DIM, att_rows),
            cache_attn_v[l].transpose(0, 2, 3, 1).reshape(dec_batch * heads, ATT_HEAD_DIM, att_rows),
            state_ssm_re[l].reshape(dec_batch, ns), state_ssm_im[l].reshape(dec_batch, ns),
            cache_mem_k[l].reshape(dec_batch * n_mem, d), cache_mem_v[l].reshape(dec_batch * n_mem, d),
            w, (batch, seq, dec_batch, dec_seq, d))
        p_out.append(p_new)
        s_out.append(s_new)

    sdt = state_ssm_re.dtype
    keep = min(PAD_ROWS, seq)
    kv_p = lambda a: a.reshape(batch, heads, ATT_HEAD_DIM, keep).transpose(0, 3, 1, 2)
    kv_s = lambda a: a.reshape(dec_seq, dec_batch, heads, ATT_HEAD_DIM).transpose(1, 0, 2, 3)
    st = lambda a: a.reshape(a.shape[0], n_g, n_p).astype(sdt)
    mkv = lambda a: a.reshape(batch, n_mem, MEM_HEADS, d // MEM_HEADS)
    stack = lambda f, outs, i: jnp.stack([f(o[i]) for o in outs])
    yp = xp.reshape(batch, seq, d)
    ys = xs.reshape(dec_seq, dec_batch, d).transpose(1, 0, 2)
    return (yp, ys,
            stack(kv_p, p_out, 0), stack(kv_p, p_out, 1), stack(st, p_out, 2), stack(st, p_out, 3),
            stack(mkv, p_out, 4), stack(mkv, p_out, 5),
            stack(kv_s, s_out, 0), stack(kv_s, s_out, 1), stack(st, s_out, 2), stack(st, s_out, 3))
```

```python
import functools
import math

import jax
import jax.numpy as jnp
from jax import lax
from jax.experimental import pallas as pl
from jax.experimental.pallas import tpu as pltpu

F32 = jnp.float32
BF16 = jnp.bfloat16

CHUNK = 64
N_PREV_CHUNKS = 8
BAND = (N_PREV_CHUNKS + 1) * CHUNK
PAD_ROWS = N_PREV_CHUNKS * CHUNK
PAIR_ROWS = 2 * CHUNK
PAIR_BAND = BAND + CHUNK
SOFTMAX_ROWS = 32
SSM_GROUP = 16
SSM_STATE = 64
ATT_HEAD_DIM = 64
REL_CLIP = 128
MEM_HEADS = 4
N_GROUPS = 4
EXPERTS_PER_GROUP = 8
EPS = 1e-6
NEG_INF = -1e30

LANES = 128
MXU_N = 256
VMEM_LIMIT = 56 * 1024 * 1024
ROUTER_LANES = LANES
MOE_TILE = 256
SAMPLE_STREAMS = 8
MOE_BUFS = 3
ROUTER_E0 = N_GROUPS
PAIRS_PER_GROUP = EXPERTS_PER_GROUP * (EXPERTS_PER_GROUP - 1) // 2
N_CLASSES = N_GROUPS * PAIRS_PER_GROUP
ROUTE_CLS, ROUTE_WA, ROUTE_WB = 64, 65, 66


def _cparams(sem):
    return pltpu.CompilerParams(dimension_semantics=sem, vmem_limit_bytes=VMEM_LIMIT)


def _rms(x, gain):
    ms = jnp.mean(x * x, axis=-1, keepdims=True)
    return x * lax.rsqrt(ms + EPS) * gain


def _sigmoid(x):
    return 1.0 / (1.0 + jnp.exp(-x))


def _gelu_tanh(x):
    c = math.sqrt(2.0 / math.pi)
    return 0.5 * x * (1.0 + jnp.tanh(c * (x + 0.044715 * (x * x * x))))


def _dot(a, b):
    return jnp.dot(a, b, preferred_element_type=F32)


def _dot_nt(a, b):
    return lax.dot_general(a, b, (((1,), (1,)), ((), ())), preferred_element_type=F32)


def _full(shape):
    n = len(shape)
    return pl.BlockSpec(shape, lambda *_: (0,) * n)


def _in_proj_kernel(x_ref, g_ref, w_ref, qg_ref, kg_ref, hm_ref,
                    u_ref, q_ref, k_ref, v_ref, kk_ref, vk_ref, *, width, nt):
    h = _rms(x_ref[...], g_ref[...]).astype(BF16)
    z = _dot(h, w_ref[...])
    u_ref[...] = z[:, :width].astype(u_ref.dtype)

    def head_norm(a, gain):
        ms = _dot((a * a).astype(BF16), hm_ref[...])
        return a * lax.rsqrt(ms + EPS) * gain

    q = head_norm(z[:, width:2 * width], qg_ref[...])
    k = head_norm(z[:, 2 * width:3 * width], kg_ref[...])
    v = z[:, 3 * width:]
    q_ref[...] = q.astype(BF16)
    k_ref[...] = k.astype(BF16)
    v_ref[...] = v.astype(BF16)

    @pl.when(pl.program_id(1) == nt - 1)
    def _():
        if len(kk_ref.shape) == 3:
            kt, vt = k.T, v.T
            for hd in range(width // ATT_HEAD_DIM):
                rows = slice(hd * ATT_HEAD_DIM, (hd + 1) * ATT_HEAD_DIM)
                kk_ref[hd] = kt[rows, :]
                vk_ref[hd] = vt[rows, :]
        else:
            kk_ref[...] = k
            vk_ref[...] = v


def _in_proj(x2d, gain, w_bf, qg, kg, hm, *, nb, nt, tm, keep_transposed):
    rows, d = x2d.shape
    width = w_bf.shape[1] // 4
    heads = width // ATT_HEAD_DIM
    tok = lambda b, t: (b * nt + t, 0)
    out_shape = (
        jax.ShapeDtypeStruct((rows, width), F32) if keep_transposed
        else jax.ShapeDtypeStruct((nt * tm, nb * width), BF16),
        jax.ShapeDtypeStruct((rows, width), BF16),
        jax.ShapeDtypeStruct((rows, width), BF16),
        jax.ShapeDtypeStruct((rows, width), BF16),
    )
    if keep_transposed:
        kept = jax.ShapeDtypeStruct((nb * heads, ATT_HEAD_DIM, tm), F32)
        keep = pl.BlockSpec((heads, ATT_HEAD_DIM, tm), lambda b, t: (b, 0, 0))
    else:
        kept = jax.ShapeDtypeStruct((nb * tm, width), F32)
        keep = pl.BlockSpec((tm, width), lambda b, t: (b, 0))
    out_shape = out_shape + (kept, kept)
    return pl.pallas_call(
        functools.partial(_in_proj_kernel, width=width, nt=nt),
        grid=(nb, nt),
        in_specs=[pl.BlockSpec((tm, d), tok), _full((1, d)), _full(w_bf.shape),
                  _full((1, width)), _full((1, width)), _full((width, width))],
        out_specs=(pl.BlockSpec((tm, width), tok if keep_transposed else (lambda b, t: (t, b))),
                   pl.BlockSpec((tm, width), tok), pl.BlockSpec((tm, width), tok),
                   pl.BlockSpec((tm, width), tok),
                   keep, keep),
        out_shape=out_shape,
        compiler_params=_cparams(("arbitrary", "arbitrary")),
        name="in_proj",
    )(x2d, gain, w_bf, qg, kg, hm)


def _ssm_chunk(u, bc_ref, cre_ref, cim_ref, lre_ref, lim_ref, d_ref, sre_ref, sim_ref, bu_ref, *, nb, tc, ns):
    n_tiles = 2 * ns // MXU_N
    for j in range(n_tiles):
        slab = (j % (n_tiles // 2)) // 2
        bu_ref[:, j * MXU_N:(j + 1) * MXU_N] = _dot(u[:, slab * LANES:(slab + 1) * LANES], bc_ref[j])

    cw = 8 * 1024 // nb
    for cb in range(ns // cw):
        c0 = cb * cw
        lre = jnp.broadcast_to(lre_ref[:, c0:c0 + cw], (nb, cw))
        lim = jnp.broadcast_to(lim_ref[:, c0:c0 + cw], (nb, cw))

        def step(t, carry, c0=c0, lre=lre, lim=lim):
            sr, si = carry
            r0 = pl.multiple_of(t * nb, nb)
            nr = lre * sr - lim * si + bu_ref[pl.ds(r0, nb), c0:c0 + cw]
            ni = lre * si + lim * sr + bu_ref[pl.ds(r0, nb), ns + c0:ns + c0 + cw]
            bu_ref[pl.ds(r0, nb), c0:c0 + cw] = nr
            bu_ref[pl.ds(r0, nb), ns + c0:ns + c0 + cw] = ni
            return nr, ni

        sr, si = lax.fori_loop(0, tc, step, (sre_ref[:, c0:c0 + cw], sim_ref[:, c0:c0 + cw]))
        sre_ref[:, c0:c0 + cw] = sr
        sim_ref[:, c0:c0 + cw] = si

    width = u.shape[1]
    kt = ns * MXU_N // width
    ys = []
    for n in range(width // MXU_N):
        s_re = bu_ref[:, n * kt:(n + 1) * kt].astype(BF16)
        s_im = bu_ref[:, ns + n * kt:ns + (n + 1) * kt].astype(BF16)
        cols = slice(n * MXU_N, (n + 1) * MXU_N)
        ys.append(_dot(s_re, cre_ref[n]) + _dot(s_im, cim_ref[n]) + d_ref[:, cols] * u[:, cols].astype(F32))
    return jnp.concatenate(ys, axis=1)


def _ssm_kernel(u_ref, bc_ref, cre_ref, cim_ref, lre_ref, lim_ref, d_ref, s0re_ref, s0im_ref,
                y_ref, sre_ref, sim_ref, bu_ref, *, nb, tc, ns):
    @pl.when(pl.program_id(0) == 0)
    def _():
        sre_ref[...] = s0re_ref[...]
        sim_ref[...] = s0im_ref[...]

    y_ref[...] = _ssm_chunk(u_ref[...], bc_ref, cre_ref, cim_ref, lre_ref, lim_ref, d_ref,
                            sre_ref, sim_ref, bu_ref, nb=nb, tc=tc, ns=ns)


def _ssm(u_rows, bc, cre, cim, lre, lim, dskip, s0re, s0im, *, nb, tc):
    rows, width = u_rows.shape
    ns = lre.shape[1]
    r = tc * nb
    return pl.pallas_call(
        functools.partial(_ssm_kernel, nb=nb, tc=tc, ns=ns),
        grid=(rows // r,),
        in_specs=[pl.BlockSpec((r, width), lambda i: (i, 0)), _full(bc.shape), _full(cre.shape),
                  _full(cim.shape), _full((1, ns)), _full((1, ns)), _full((1, width)),
                  _full((nb, ns)), _full((nb, ns))],
        out_specs=(pl.BlockSpec((r, width), lambda i: (i, 0)), _full((nb, ns)), _full((nb, ns))),
        out_shape=(jax.ShapeDtypeStruct((rows, width), F32),
                   jax.ShapeDtypeStruct((nb, ns), F32), jax.ShapeDtypeStruct((nb, ns), F32)),
        scratch_shapes=[pltpu.VMEM((r, 2 * ns), F32)],
        compiler_params=_cparams(("arbitrary",)),
        name="ssm",
    )(u_rows, bc, cre, cim, lre, lim, dskip, s0re, s0im)


def _ssm_tokens_kernel(u_hbm, bc_ref, cre_ref, cim_ref, lre_ref, lim_ref, d_ref,
                       y_hbm, sre_ref, sim_ref, bu_ref, ubuf, ybuf, isem, osem, *, nb, tc, ns, t_len):
    i = pl.program_id(0)
    n_steps = t_len // tc
    slot = i % 2
    width = ubuf.shape[3]

    def copy_in(step, s, b):
        return pltpu.make_async_copy(u_hbm.at[pl.ds(b * t_len + step * tc, tc)], ubuf.at[s, :, b, :], isem.at[s])

    def copy_out(step, s, b):
        return pltpu.make_async_copy(ybuf.at[s, :, b, :], y_hbm.at[pl.ds(b * t_len + step * tc, tc)], osem.at[s])

    @pl.when(i == 0)
    def _():
        sre_ref[...] = jnp.zeros_like(sre_ref)
        sim_ref[...] = jnp.zeros_like(sim_ref)
        for b in range(nb):
            copy_in(0, 0, b).start()

    @pl.when(i + 1 < n_steps)
    def _():
        for b in range(nb):
            copy_in(i + 1, 1 - slot, b).start()

    for b in range(nb):
        copy_in(i, slot, b).wait()
    u = ubuf[slot].reshape(tc * nb, width).astype(BF16)
    y = _ssm_chunk(u, bc_ref, cre_ref, cim_ref, lre_ref, lim_ref, d_ref, sre_ref, sim_ref, bu_ref,
                   nb=nb, tc=tc, ns=ns)

    @pl.when(i >= 2)
    def _():
        for b in range(nb):
            copy_out(i - 2, slot, b).wait()

    ybuf[slot] = y.reshape(tc, nb, width)
    for b in range(nb):
        copy_out(i, slot, b).start()

    @pl.when(i == n_steps - 1)
    def _():
        for b in range(nb):
            copy_out(i, slot, b).wait()
        if n_steps > 1:
            for b in range(nb):
                copy_out(i - 1, 1 - slot, b).wait()


def _ssm_tokens(u_tok, bc, cre, cim, lre, lim, dskip, *, nb, tc):
    rows, width = u_tok.shape
    ns = lre.shape[1]
    t_len = rows // nb
    r = tc * nb
    any_spec = pl.BlockSpec(memory_space=pl.ANY)
    return pl.pallas_call(
        functools.partial(_ssm_tokens_kernel, nb=nb, tc=tc, ns=ns, t_len=t_len),
        grid=(t_len // tc,),
        in_specs=[any_spec, _full(bc.shape), _full(cre.shape), _full(cim.shape), _full((1, ns)),
                  _full((1, ns)), _full((1, width))],
        out_specs=(any_spec, _full((nb, ns)), _full((nb, ns))),
        out_shape=(jax.ShapeDtypeStruct((rows, width), F32),
                   jax.ShapeDtypeStruct((nb, ns), F32), jax.ShapeDtypeStruct((nb, ns), F32)),
        scratch_shapes=[pltpu.VMEM((r, 2 * ns), F32), pltpu.VMEM((2, tc, nb, width), F32),
                        pltpu.VMEM((2, tc, nb, width), F32),
                        pltpu.SemaphoreType.DMA((2,)), pltpu.SemaphoreType.DMA((2,))],
        compiler_params=_cparams(("arbitrary",)),
        name="ssm_tokens",
    )(u_tok, bc, cre, cim, lre, lim, dskip)


def _half_select(shape):
    lane = lax.broadcasted_iota(jnp.int32, shape, 1)
    return lane < ATT_HEAD_DIM


def _head_masks(first):
    m0 = jnp.where(first, 1.0, 0.0).astype(BF16)
    return m0, (1.0 - m0.astype(F32)).astype(BF16)


def _band_prompt_kernel(q_ref, k_ref, v_ref, bias_ref, o_ref, kp_ref, vp_ref, s_scr, p_scr, l_scr, *, t_len):
    width = q_ref.shape[1]
    n_hp = width // LANES
    kp_ref[0:PAD_ROWS, :] = jnp.zeros((PAD_ROWS, width), BF16)
    vp_ref[0:PAD_ROWS, :] = jnp.zeros((PAD_ROWS, width), BF16)
    kp_ref[PAD_ROWS:, :] = k_ref[...]
    vp_ref[PAD_ROWS:, :] = v_ref[...]
    first = _half_select((PAIR_ROWS, LANES))
    head_mask = _head_masks(first)
    col = lax.broadcasted_iota(jnp.int32, (SOFTMAX_ROWS, PAIR_BAND), 1)

    def pair(pi, carry, *, masked):
        r0 = pl.multiple_of(pi * PAIR_ROWS, PAIR_ROWS)
        for hp in range(n_hp):
            lanes = slice(hp * LANES, (hp + 1) * LANES)
            qp = q_ref[pl.ds(r0, PAIR_ROWS), lanes]
            qs = jnp.concatenate([qp * head_mask[0], qp * head_mask[1]], axis=0)
            s_scr[2 * hp * PAIR_ROWS:2 * (hp + 1) * PAIR_ROWS, :] = _dot_nt(
                qs, kp_ref[pl.ds(r0, PAIR_BAND), lanes])
        for h in range(2 * n_hp):
            for rb in range(0, PAIR_ROWS, SOFTMAX_ROWS):
                rows = slice(h * PAIR_ROWS + rb, h * PAIR_ROWS + rb + SOFTMAX_ROWS)
                s = s_scr[rows, :] + bias_ref[h, rb:rb + SOFTMAX_ROWS, :]
                if masked:
                    s = jnp.where(col >= PAD_ROWS - r0, s, NEG_INF)
                p = jnp.exp(s - jnp.max(s, axis=-1, keepdims=True))
                p_scr[rows, :] = p.astype(BF16)
                l_scr[rows, :] = jnp.broadcast_to(1.0 / jnp.sum(p, axis=-1, keepdims=True),
                                                  (SOFTMAX_ROWS, LANES))
        for hp in range(n_hp):
            lanes = slice(hp * LANES, (hp + 1) * LANES)
            rows = slice(2 * hp * PAIR_ROWS, 2 * (hp + 1) * PAIR_ROWS)
            o2 = _dot(p_scr[rows, :], vp_ref[pl.ds(r0, PAIR_BAND), lanes]) * l_scr[rows, :]
            o_ref[pl.ds(r0, PAIR_ROWS), lanes] = jnp.where(
                first, o2[:PAIR_ROWS], o2[PAIR_ROWS:]).astype(BF16)
        return carry

    n_masked = PAD_ROWS // PAIR_ROWS
    lax.fori_loop(0, n_masked, functools.partial(pair, masked=True), 0)
    lax.fori_loop(n_masked, t_len // PAIR_ROWS, functools.partial(pair, masked=False), 0)


def _band_prompt(q, k, v, bias, *, nb, t_len):
    width = q.shape[1]
    heads = width // ATT_HEAD_DIM
    blk = pl.BlockSpec((t_len, width), lambda b: (b, 0))
    return pl.pallas_call(
        functools.partial(_band_prompt_kernel, t_len=t_len),
        grid=(nb,),
        in_specs=[blk, blk, blk, _full(bias.shape)],
        out_specs=blk,
        out_shape=jax.ShapeDtypeStruct(q.shape, BF16),
        scratch_shapes=[pltpu.VMEM((t_len + PAD_ROWS, width), BF16),
                        pltpu.VMEM((t_len + PAD_ROWS, width), BF16),
                        pltpu.VMEM((heads * PAIR_ROWS, PAIR_BAND), F32),
                        pltpu.VMEM((heads * PAIR_ROWS, PAIR_BAND), BF16),
                        pltpu.VMEM((heads * PAIR_ROWS, LANES), F32)],
        compiler_params=_cparams(("arbitrary",)),
        name="band_prompt",
    )(q, k, v, bias)


def _band_sample_kernel(q_ref, k_ref, v_ref, ck_ref, cv_ref, bc_ref, bn_ref, o_ref):
    for h in range(ck_ref.shape[0]):
        cols = slice(h * ATT_HEAD_DIM, (h + 1) * ATT_HEAD_DIM)
        qh = q_ref[:, cols]
        sc = _dot(qh, ck_ref[h].astype(BF16)) + bc_ref[h]
        sn = _dot_nt(qh, k_ref[:, cols]) + bn_ref[h]
        m = jnp.maximum(jnp.max(sc, axis=-1, keepdims=True), jnp.max(sn, axis=-1, keepdims=True))
        pc = jnp.exp(sc - m)
        pn = jnp.exp(sn - m)
        l = jnp.sum(pc, axis=-1, keepdims=True) + jnp.sum(pn, axis=-1, keepdims=True)
        o = _dot_nt(pc.astype(BF16), cv_ref[h].astype(BF16)) + _dot(pn.astype(BF16), v_ref[:, cols])
        o_ref[:, cols] = (o / l).astype(BF16)


def _band_sample(q_tm, k_tm, v_tm, cache_kt, cache_vt, bias_c, bias_n, *, nb, s_len):
    width = q_tm.shape[1] // nb
    heads = cache_kt.shape[0] // nb
    col = pl.BlockSpec((s_len, width), lambda b: (0, b))
    cache = pl.BlockSpec((heads,) + cache_kt.shape[1:], lambda b: (b, 0, 0))
    return pl.pallas_call(
        _band_sample_kernel,
        grid=(nb,),
        in_specs=[col, col, col, cache, cache, _full(bias_c.shape), _full(bias_n.shape)],
        out_specs=col,
        out_shape=jax.ShapeDtypeStruct(q_tm.shape, BF16),
        compiler_params=_cparams(("arbitrary",)),
        name="band_sample",
    )(q_tm, k_tm, v_tm, cache_kt, cache_vt, bias_c, bias_n)


def _mem_kv_kernel(m_ref, g_ref, wk_ref, wv_ref, kg_ref, k_ref, v_ref, kb_ref, vb_ref):
    m = _rms(m_ref[...], g_ref[...]).astype(BF16)
    k = _dot(m, wk_ref[...])
    v = _dot(m, wv_ref[...])
    hd = kg_ref.shape[1]
    for h in range(k.shape[1] // hd):
        cols = slice(h * hd, (h + 1) * hd)
        kh = _rms(k[:, cols], kg_ref[...])
        k_ref[:, h, :] = kh
        v_ref[:, h, :] = v[:, cols]
        kb_ref[:, cols] = kh.astype(BF16)
    vb_ref[...] = v.astype(BF16)


def _mem_kv(mem2d, gain, wk, wv, kgain, *, tm):
    rows, d = mem2d.shape
    hd = kgain.shape[1]
    blk = pl.BlockSpec((tm, d), lambda i: (i, 0))
    hblk = pl.BlockSpec((tm, d // hd, hd), lambda i: (i, 0, 0))
    return pl.pallas_call(
        _mem_kv_kernel,
        grid=(rows // tm,),
        in_specs=[blk, _full((1, d)), _full(wk.shape), _full(wv.shape), _full(kgain.shape)],
        out_specs=(hblk, hblk, blk, blk),
        out_shape=(jax.ShapeDtypeStruct((rows, d // hd, hd), F32),
                   jax.ShapeDtypeStruct((rows, d // hd, hd), F32),
                   jax.ShapeDtypeStruct((rows, d), BF16), jax.ShapeDtypeStruct((rows, d), BF16)),
        compiler_params=_cparams(("arbitrary",)),
        name="mem_kv",
    )(mem2d, gain, wk, wv, kgain)


def _route(logits):
    lane = lax.broadcasted_iota(jnp.int32, logits.shape, 1).astype(F32)
    big = float(ROUTER_LANES)
    is_g = lane < N_GROUPS
    lg = jnp.where(is_g, logits, NEG_INF)
    gmax = jnp.max(lg, axis=-1, keepdims=True)
    p1 = 1.0 / jnp.sum(jnp.where(is_g, jnp.exp(lg - gmax), 0.0), axis=-1, keepdims=True)
    g_idx = jnp.min(jnp.where(lg == gmax, lane, big), axis=-1, keepdims=True)
    lo = ROUTER_E0 + g_idx * EXPERTS_PER_GROUP
    le = jnp.where((lane >= lo) & (lane < lo + EXPERTS_PER_GROUP), logits, NEG_INF)
    v1 = jnp.max(le, axis=-1, keepdims=True)
    i1 = jnp.min(jnp.where(le == v1, lane, big), axis=-1, keepdims=True)
    le2 = jnp.where(lane == i1, NEG_INF, le)
    v2 = jnp.max(le2, axis=-1, keepdims=True)
    i2 = jnp.min(jnp.where(le2 == v2, lane, big), axis=-1, keepdims=True)
    e2 = jnp.exp(v2 - v1)
    w1 = p1 / (1.0 + e2)
    w2 = p1 * e2 / (1.0 + e2)
    gates = jnp.where(lane == i1, w1, 0.0) + jnp.where(lane == i2, w2, 0.0)
    a = jnp.minimum(i1, i2) - lo
    b = jnp.maximum(i1, i2) - lo
    pair = a * (2 * EXPERTS_PER_GROUP - 1 - a) * 0.5 + (b - a - 1.0)
    cls = g_idx * PAIRS_PER_GROUP + pair
    w_a = jnp.where(i1 < i2, w1, w2)
    w_b = jnp.where(i1 < i2, w2, w1)
    return (gates + jnp.where(lane == ROUTE_CLS, cls, 0.0) + jnp.where(lane == ROUTE_WA, w_a, 0.0)
            + jnp.where(lane == ROUTE_WB, w_b, 0.0))


def _mix_mem_kernel(x_ref, ys_ref, ya_ref, mk_ref, mv_ref, wglu_ref, ons_ref, ona_ref, wout_ref,
                    nmem_ref, wq_ref, qg_ref, wo_ref, nffn_ref, wr_ref, rb_ref,
                    *outs, routed, nbat):
    if routed:
        x2_ref, o_scr = outs
    else:
        x2_ref, xn_ref, gate_ref, o_scr = outs
    tm = x_ref.shape[0]
    d = x_ref.shape[1] // nbat
    width = ys_ref.shape[1] // nbat
    n_mem = mk_ref.shape[0] // nbat

    def stacked(ref, w):
        return jnp.concatenate([ref[:, j * w:(j + 1) * w] for j in range(nbat)], axis=0)

    g = _dot(_gelu_tanh(stacked(ys_ref, width)).astype(BF16), wglu_ref[...])
    y_s = g[:, :width] * _sigmoid(g[:, width:])
    cat_s = _rms(y_s, ons_ref[...]).astype(BF16)
    cat_a = _rms(stacked(ya_ref, width).astype(F32), ona_ref[...]).astype(BF16)
    x1 = stacked(x_ref, d) + _dot(cat_s, wout_ref[0:width, :]) + _dot(cat_a, wout_ref[width:, :])

    q = _dot(_rms(x1, nmem_ref[...]).astype(BF16), wq_ref[...])
    hd = qg_ref.shape[1]
    scale = hd ** -0.5
    for j in range(nbat):
        rows = slice(j * tm, (j + 1) * tm)
        mem = slice(j * n_mem, (j + 1) * n_mem)
        for h in range(d // hd):
            cols = slice(h * hd, (h + 1) * hd)
            qh = _rms(q[rows, cols], qg_ref[...]).astype(BF16)
            s = _dot_nt(qh, mk_ref[mem, cols].astype(BF16)) * scale
            p = jnp.exp(s - jnp.max(s, axis=-1, keepdims=True))
            l = jnp.sum(p, axis=-1, keepdims=True)
            o_scr[rows, cols] = (_dot(p.astype(BF16), mv_ref[mem, cols].astype(BF16)) / l).astype(BF16)
    acc = x1 + _dot(o_scr[...], wo_ref[...])

    xn = _rms(acc, nffn_ref[...])
    xh = xn.astype(BF16)
    logits = _dot(xh, wr_ref[...]) + rb_ref[...]
    record = _route(logits)
    if routed:
        x2_ref[:, :d] = acc
        x2_ref[:, d:] = record
    else:
        for j in range(nbat):
            rows = slice(j * tm, (j + 1) * tm)
            x2_ref[:, j * d:(j + 1) * d] = acc[rows]
            xn_ref[:, j * d:(j + 1) * d] = xh[rows]
            gate_ref[:, j * ROUTER_LANES:(j + 1) * ROUTER_LANES] = record[rows]


def _mix_mem(x, ys, ya, mk, mv, weights, *, grid, tm, row_map, ssm_map, mem_map, routed, nbat=1):
    d = weights["w_out"].shape[1]
    width = weights["w_glu"].shape[0]
    names = ("w_glu", "out_norm_ssm", "out_norm_att", "w_out", "norm_mem", "w_mem_q", "mem_q_gain",
             "w_mem_o", "norm_ffn", "w_router", "router_bias")
    ws = [weights[n] for n in names]
    mem_rows = weights["n_mem"]
    xspec = pl.BlockSpec((tm, nbat * d), row_map)
    hspec = pl.BlockSpec((tm, nbat * width), row_map)
    sspec = pl.BlockSpec((tm, nbat * width), ssm_map)
    mspec = pl.BlockSpec((nbat * mem_rows, d), mem_map)
    n_col = x.shape[1] // d
    if routed:
        assert nbat == 1
        out_specs = pl.BlockSpec((tm, d + ROUTER_LANES), row_map)
        out_shape = jax.ShapeDtypeStruct((x.shape[0], n_col * (d + ROUTER_LANES)), F32)
    else:
        out_specs = (xspec, xspec, pl.BlockSpec((tm, nbat * ROUTER_LANES), row_map))
        out_shape = (jax.ShapeDtypeStruct(x.shape, F32), jax.ShapeDtypeStruct(x.shape, BF16),
                     jax.ShapeDtypeStruct((x.shape[0], n_col * ROUTER_LANES), F32))
    return pl.pallas_call(
        functools.partial(_mix_mem_kernel, routed=routed, nbat=nbat),
        grid=grid,
        in_specs=[xspec, sspec, hspec, mspec, mspec] + [_full(w.shape) for w in ws],
        out_specs=out_specs,
        out_shape=out_shape,
        scratch_shapes=[pltpu.VMEM((nbat * tm, d), BF16)],
        compiler_params=_cparams(("arbitrary",) * len(grid)),
        name="mix_mem",
    )(x, ys, ya, mk, mv, *ws)


def _moe_kernel(xn_ref, x2_ref, gate_ref, wg_ref, wu_ref, wd_ref, o_ref):
    e = pl.program_id(1)

    @pl.when(e == 0)
    def _():
        o_ref[...] = x2_ref[...]

    gates = gate_ref[...]
    lane = lax.broadcasted_iota(jnp.int32, gates.shape, 1)
    ge = jnp.sum(jnp.where(lane == e + ROUTER_E0, gates, 0.0), axis=-1, keepdims=True)
    xn = xn_ref[...]
    a = _dot(xn, wg_ref[...])
    h = a * _sigmoid(a) * _dot(xn, wu_ref[...])
    o_ref[...] += ge * _dot(h.astype(BF16), wd_ref[...])


def _moe(xn, x2, gates, wg, wu, wd, *, tm):
    rows, d = xn.shape
    n_exp, _, dff = wg.shape
    row = lambda i, e: (i, 0)
    return pl.pallas_call(
        _moe_kernel,
        grid=(rows // tm, n_exp),
        in_specs=[pl.BlockSpec((tm, d), row), pl.BlockSpec((tm, d), row),
                  pl.BlockSpec((tm, ROUTER_LANES), row),
                  pl.BlockSpec((None, d, dff), lambda i, e: (e, 0, 0)),
                  pl.BlockSpec((None, d, dff), lambda i, e: (e, 0, 0)),
                  pl.BlockSpec((None, dff, d), lambda i, e: (e, 0, 0))],
        out_specs=pl.BlockSpec((tm, d), row),
        out_shape=jax.ShapeDtypeStruct((rows, d), F32),
        compiler_params=_cparams(("arbitrary", "arbitrary")),
        name="moe",
    )(xn, x2, gates, wg, wu, wd)


def _moe_routed_kernel(tile_ref, ea_ref, eb_ref, lo_ref, hi_ref, valid_ref, tok_ref,
                       x_hbm, nffn_ref, wga_ref, wua_ref, wda_ref, wgb_ref, wub_ref, wdb_ref,
                       y_hbm, *scratch, tm, n_tiles):
    nb = MOE_BUFS
    xbuf, obuf = scratch[:nb], scratch[nb:2 * nb]
    acc_ref, gsem, ssem = scratch[2 * nb:]
    w = pl.program_id(0)
    t = tile_ref[w]
    lo = lo_ref[w]
    hi = hi_ref[w]
    valid = valid_ref[w] == 1
    d = y_hbm.shape[1]

    def row_copy_in(tile, s, r):
        tok = tok_ref[tile * tm + r]
        return pltpu.make_async_copy(x_hbm.at[pl.ds(tok, 1)], xbuf[s].at[pl.ds(r, 1)], gsem.at[s])

    def row_copy_out(tile, s, r):
        tok = tok_ref[tile * tm + r]
        return pltpu.make_async_copy(obuf[s].at[pl.ds(r, 1)], y_hbm.at[pl.ds(tok, 1)], ssem.at[s])

    def start_gather(tile, s):
        for r in range(tm):
            row_copy_in(tile, s, r).start(priority=r % 2)

    def start_scatter(tile, s):
        for r in range(tm):
            row_copy_out(tile, s, r).start(priority=r % 2)

    def wait_gather(s):
        pltpu.make_async_copy(x_hbm.at[pl.ds(0, tm)], xbuf[s], gsem.at[s]).wait()

    def wait_scatter(s):
        pltpu.make_async_copy(obuf[s], y_hbm.at[pl.ds(0, tm)], ssem.at[s]).wait()

    def compute(s, first):
        xe = xbuf[s][...]
        xn = _rms(xe[:, :d], nffn_ref[...]).astype(BF16)
        row = lax.broadcasted_iota(jnp.int32, (tm, 1), 0)
        inseg = jnp.logical_and(row >= lo, row < hi)
        w_a = jnp.where(inseg, xe[:, d + ROUTE_WA:d + ROUTE_WA + 1], 0.0)
        w_b = jnp.where(inseg, xe[:, d + ROUTE_WB:d + ROUTE_WB + 1], 0.0)
        a = _dot(xn, wga_ref[...])
        h_a = (a * _sigmoid(a) * _dot(xn, wua_ref[...])).astype(BF16)
        b = _dot(xn, wgb_ref[...])
        h_b = (b * _sigmoid(b) * _dot(xn, wub_ref[...])).astype(BF16)
        upd = w_a * _dot(h_a, wda_ref[...]) + w_b * _dot(h_b, wdb_ref[...])
        if first:
            acc_ref[...] = upd
        else:
            acc_ref[...] += upd

    def when(*conds):
        c = conds[0]
        for extra in conds[1:]:
            c = jnp.logical_and(c, extra)
        return pl.when(c)

    ahead = nb - 1

    @pl.when(w == 0)
    def _():
        for k in range(ahead):
            start_gather(k, k)

    is_first = jnp.logical_and(valid, lo == 0)
    is_last = jnp.logical_and(valid, hi == tm)
    last_tile = n_tiles - 1
    for s in range(nb):
        mine = t % nb == s
        nxt, prv = (s + ahead) % nb, (s - 1) % nb

        if s == 0:
            @when(is_first, t == 0)
            def _(nxt=nxt):
                wait_gather(0)
                start_gather(ahead, nxt)
                compute(0, True)

        @when(is_first, mine, t > 0, t + ahead <= last_tile)
        def _(s=s, nxt=nxt, prv=prv):
            wait_gather(s)
            start_gather(t + ahead, nxt)
            start_scatter(t - 1, prv)
            compute(s, True)

        @when(is_first, mine, t + ahead > last_tile)
        def _(s=s, prv=prv):
            wait_gather(s)
            start_scatter(t - 1, prv)
            compute(s, True)

        @when(valid, mine, lo > 0)
        def _(s=s):
            compute(s, False)

        @when(is_last, mine)
        def _(s=s):
            @pl.when(t >= nb)
            def _():
                wait_scatter(s)

            obuf[s][...] = xbuf[s][:, :d] + acc_ref[...]

            if s == last_tile % nb:
                @pl.when(t == last_tile)
                def _():
                    start_scatter(t, s)
                    for k in range(nb):
                        wait_scatter(k)


def _moe_routed(x2ext, plan, nffn, wg, wu, wd, *, tm):
    rows, de = x2ext.shape
    d = de - ROUTER_LANES
    n_exp, _, dff = wg.shape
    n_tiles = rows // tm
    assert rows % tm == 0 and n_tiles > 2 * MOE_BUFS
    n_items = plan[0].shape[0]
    ea = lambda w, tile, ea_, eb_, *_: (ea_[w], 0, 0)
    eb = lambda w, tile, ea_, eb_, *_: (eb_[w], 0, 0)
    grid_spec = pltpu.PrefetchScalarGridSpec(
        num_scalar_prefetch=len(plan),
        grid=(n_items,),
        in_specs=[pl.BlockSpec(memory_space=pl.ANY),
                  pl.BlockSpec((1, d), lambda w, *_: (0, 0)),
                  pl.BlockSpec((None, d, dff), ea), pl.BlockSpec((None, d, dff), ea),
                  pl.BlockSpec((None, dff, d), ea),
                  pl.BlockSpec((None, d, dff), eb), pl.BlockSpec((None, d, dff), eb),
                  pl.BlockSpec((None, dff, d), eb)],
        out_specs=pl.BlockSpec(memory_space=pl.ANY),
        scratch_shapes=([pltpu.VMEM((tm, de), F32)] * MOE_BUFS + [pltpu.VMEM((tm, d), F32)] * MOE_BUFS
                        + [pltpu.VMEM((tm, d), F32),
                           pltpu.SemaphoreType.DMA((MOE_BUFS,)), pltpu.SemaphoreType.DMA((MOE_BUFS,))]),
    )
    return pl.pallas_call(
        functools.partial(_moe_routed_kernel, tm=tm, n_tiles=n_tiles),
        grid_spec=grid_spec,
        out_shape=jax.ShapeDtypeStruct((rows, d), F32),
        compiler_params=_cparams(("arbitrary",)),
        name="moe_routed",
    )(*plan, x2ext, nffn, wg, wu, wd, wg, wu, wd)


def _route_plan(cls, *, tm):
    n = cls.shape[0]
    n_tiles = n // tm
    order = jnp.argsort(cls).astype(jnp.int32)
    classes = jnp.arange(N_CLASSES, dtype=jnp.int32)
    class_end = jnp.sum((cls[:, None] <= classes[None, :]).astype(jnp.int32), axis=0)
    class_start = jnp.concatenate([jnp.zeros((1,), jnp.int32), class_end[:-1]])
    bounds = jnp.concatenate([jnp.arange(n_tiles, dtype=jnp.int32) * tm,
                              jnp.where(class_end > class_start, class_start, n)])
    n_items = bounds.shape[0]
    idx = jnp.arange(n_items, dtype=jnp.int32)
    before = jnp.logical_or(bounds[None, :] < bounds[:, None],
                            jnp.logical_and(bounds[None, :] == bounds[:, None], idx[None, :] < idx[:, None]))
    rank = jnp.sum(before.astype(jnp.int32), axis=1)
    start = jnp.sum(jnp.where(rank[:, None] == idx[None, :], bounds[:, None], 0), axis=0)
    stop = jnp.concatenate([start[1:], jnp.full((1,), n, jnp.int32)])
    valid = stop > start
    tile = jnp.minimum(start // tm, n_tiles - 1)
    lo = start - tile * tm
    hi = stop - tile * tm
    c = jnp.minimum(jnp.sum((class_end[None, :] <= start[:, None]).astype(jnp.int32), axis=1),
                    N_CLASSES - 1)
    g = c // PAIRS_PER_GROUP
    pair = c % PAIRS_PER_GROUP
    a = jnp.zeros_like(pair)
    for k in range(1, EXPERTS_PER_GROUP - 1):
        a = a + (pair >= k * (2 * EXPERTS_PER_GROUP - 1 - k) // 2).astype(jnp.int32)
    b = pair - a * (2 * EXPERTS_PER_GROUP - 1 - a) // 2 + a + 1
    e_a = g * EXPERTS_PER_GROUP + a
    e_b = g * EXPERTS_PER_GROUP + b
    i32 = lambda v: v.astype(jnp.int32)
    return (i32(tile), i32(e_a), i32(e_b), i32(lo), i32(hi), i32(valid), order)


def _ssm_params(lam_re, lam_im, log_step, b_re, b_im, c_re, c_im):
    n_g, n_p = lam_re.shape
    step = jnp.exp(log_step.astype(F32))[:, None]
    mag = jnp.exp(lam_re * step)
    lb_re = mag * jnp.cos(lam_im * step)
    lb_im = mag * jnp.sin(lam_im * step)
    den = lam_re * lam_re + lam_im * lam_im
    f_re = ((lb_re - 1.0) * lam_re + lb_im * lam_im) / den
    f_im = (lb_im * lam_re - (lb_re - 1.0) * lam_im) / den
    bb_re = f_re[..., None] * b_re - f_im[..., None] * b_im
    bb_im = f_re[..., None] * b_im + f_im[..., None] * b_re
    eye = jnp.eye(n_g, dtype=F32)
    ns = n_g * n_p
    width = n_g * SSM_GROUP
    b_full = jnp.concatenate(
        [jnp.einsum("hg,gpc->hcgp", eye, bb_re).reshape(width, ns),
         jnp.einsum("hg,gpc->hcgp", eye, bb_im).reshape(width, ns)], axis=1)
    n_tiles = 2 * ns // MXU_N
    bc = jnp.stack([
        b_full[((j % (n_tiles // 2)) // 2) * LANES:((j % (n_tiles // 2)) // 2 + 1) * LANES,
               j * MXU_N:(j + 1) * MXU_N] for j in range(n_tiles)]).astype(BF16)
    c_full_re = jnp.einsum("gh,gcp->gphc", eye, c_re).reshape(ns, width)
    c_full_im = -jnp.einsum("gh,gcp->gphc", eye, c_im).reshape(ns, width)
    kt = ns * MXU_N // width
    tiles = range(width // MXU_N)
    cre = jnp.stack([c_full_re[n * kt:(n + 1) * kt, n * MXU_N:(n + 1) * MXU_N] for n in tiles]).astype(BF16)
    cim = jnp.stack([c_full_im[n * kt:(n + 1) * kt, n * MXU_N:(n + 1) * MXU_N] for n in tiles]).astype(BF16)
    return bc, cre, cim, lb_re.reshape(1, ns), lb_im.reshape(1, ns)


def _rel_bias(rel_bias, q0, n_q, n_k):
    n_r = n_q + n_k - 1
    dist = q0 + n_q - 1 - jnp.arange(n_r)
    r = rel_bias.astype(F32)[:, jnp.clip(dist, -REL_CLIP, REL_CLIP) + REL_CLIP]
    r = jnp.pad(r, ((0, 0), (0, 1)))
    rows = jnp.tile(r, (1, n_q))[:, :n_q * n_r].reshape(-1, n_q, n_r)
    return rows[:, :, n_q - 1:n_q - 1 + n_k]


def _layer(xp, xs, mem_p, ck, cv, s_re, s_im, cmk, cmv, w, dims):
    batch, seq, dec_batch, dec_seq, d = dims
    width = w["w_in"].shape[1] // 4
    ns = w["lam_re"].shape[1]
    n_mem = w["n_mem"]
    tm_p = 512
    nt_p = seq // tm_p
    n_dec = dec_batch * dec_seq

    u, q, k, v, kk, vk = _in_proj(xp, w["norm_mix"], w["w_in"], w["qg"], w["kg"], w["head_mean"],
                                  nb=batch, nt=nt_p, tm=tm_p, keep_transposed=True)
    y_ssm, pre, pim = _ssm_tokens(u, w["bc"], w["cre"], w["cim"], w["lam_re"], w["lam_im"], w["ssm_d"],
                                  nb=batch, tc=64)
    y_att = _band_prompt(q, k, v, w["bias_p"], nb=batch, t_len=seq)
    mk, mv, mkb, mvb = _mem_kv(mem_p, w["mem_in_norm"], w["w_mem_k"], w["w_mem_v"], w["mem_k_gain"], tm=512)
    tm_d = 512
    nt_d = seq // tm_d
    x2ext = _mix_mem(xp, y_ssm, y_att, mkb, mvb, w,
                     grid=(batch, nt_d), tm=tm_d,
                     row_map=lambda b, t: (b * nt_d + t, 0), ssm_map=lambda b, t: (b * nt_d + t, 0),
                     mem_map=lambda b, t: (b, 0), routed=True)
    plan = _route_plan(x2ext[:, d + ROUTE_CLS].astype(jnp.int32), tm=MOE_TILE)
    yp = _moe_routed(x2ext, plan, w["norm_ffn"], w["exp_w_gate"], w["exp_w_up"], w["exp_w_down"],
                     tm=MOE_TILE)

    us, qs, ks, vs, kks, vks = _in_proj(xs, w["norm_mix"], w["w_in"], w["qg"], w["kg"], w["head_mean"],
                                        nb=1, nt=1, tm=n_dec, keep_transposed=False)
    ys_ssm, sre, sim = _ssm(us, w["bc"], w["cre"], w["cim"], w["lam_re"], w["lam_im"], w["ssm_d"],
                            s_re, s_im, nb=dec_batch, tc=dec_seq)
    tmv = lambda a: a.reshape(dec_seq, dec_batch * a.shape[1])
    ys_att = _band_sample(tmv(qs), tmv(ks), tmv(vs), ck, cv, w["bias_sc"], w["bias_sn"],
                          nb=dec_batch, s_len=dec_seq)
    x2s, xns, gates_s = _mix_mem(tmv(xs), tmv(ys_ssm), ys_att, cmk, cmv, w,
                                 grid=(dec_batch // SAMPLE_STREAMS,), tm=dec_seq,
                                 row_map=lambda b: (0, b), ssm_map=lambda b: (0, b),
                                 mem_map=lambda b: (b, 0), routed=False, nbat=SAMPLE_STREAMS)
    ys = _moe(xns.reshape(n_dec, d), x2s.reshape(n_dec, d), gates_s.reshape(n_dec, ROUTER_LANES),
              w["exp_w_gate"], w["exp_w_up"], w["exp_w_down"], tm=n_dec)
    return yp, ys, (kk, vk, pre, pim, mk, mv), (kks, vks, sre, sim)


def kernel(x_prompt, x_sample, mem_prompt, cache_attn_k, cache_attn_v, state_ssm_re, state_ssm_im, cache_mem_k, cache_mem_v, norm_mix, w_in, ssm_lambda_re, ssm_lambda_im, ssm_log_step, ssm_b_re, ssm_b_im, ssm_c_re, ssm_c_im, ssm_d, w_glu, att_q_gain, att_k_gain, att_rel_bias, out_norm_ssm, out_norm_att, w_out, norm_mem, mem_in_norm, w_mem_q, w_mem_k, w_mem_v, w_mem_o, mem_q_gain, mem_k_gain, norm_ffn, router_g_w, router_g_b, router_e_w, router_e_b, exp_w_gate, exp_w_up, exp_w_down):
    depth = norm_mix.shape[0]
    batch, seq, d = x_prompt.shape
    dec_batch, dec_seq, _ = x_sample.shape
    n_mem = mem_prompt.shape[1]
    att_rows = cache_attn_k.shape[2]
    n_g, n_p = ssm_lambda_re.shape[1:]
    width = n_g * SSM_GROUP
    heads = width // ATT_HEAD_DIM
    ns = n_g * n_p
    assert seq % 512 == 0 and att_rows == PAD_ROWS and seq >= PAD_ROWS
    assert (dec_batch * dec_seq) % 8 == 0 and dec_seq % 16 == 0

    xp = x_prompt.reshape(batch * seq, d)
    xs = x_sample.transpose(1, 0, 2).reshape(dec_seq * dec_batch, d)
    mem_p = mem_prompt.reshape(batch * n_mem, d)
    row = lambda a: a.reshape(1, -1).astype(F32)
    head_mean = jnp.kron(jnp.eye(heads, dtype=F32),
                         jnp.full((ATT_HEAD_DIM, ATT_HEAD_DIM), 1.0 / ATT_HEAD_DIM, F32)).astype(BF16)
    pr = jnp.arange(PAIR_ROWS)[:, None]
    pc = jnp.arange(PAIR_BAND)[None, :]
    pair_ok = jnp.where(pr < CHUNK, pc < BAND, pc >= CHUNK)
    att_scale = ATT_HEAD_DIM ** -0.5

    p_out, s_out = [], []
    for l in range(depth):
        bc, cre, cim, lb_re, lb_im = _ssm_params(ssm_lambda_re[l], ssm_lambda_im[l], ssm_log_step[l],
                                                 ssm_b_re[l], ssm_b_im[l], ssm_c_re[l], ssm_c_im[l])
        w_router = jnp.concatenate(
            [router_g_w[l], router_e_w[l].transpose(1, 0, 2).reshape(d, N_GROUPS * EXPERTS_PER_GROUP),
             jnp.zeros((d, ROUTER_LANES - N_GROUPS * (1 + EXPERTS_PER_GROUP)), F32)], axis=1)
        r_bias = jnp.concatenate(
            [router_g_b[l], router_e_b[l].reshape(-1),
             jnp.zeros((ROUTER_LANES - N_GROUPS * (1 + EXPERTS_PER_GROUP),), F32)]).reshape(1, ROUTER_LANES)
        bias_s = _rel_bias(att_rel_bias[l], att_rows, dec_seq, att_rows + dec_seq)
        bias_p = jnp.where(pair_ok, _rel_bias(att_rel_bias[l], PAD_ROWS, PAIR_ROWS, PAIR_BAND), NEG_INF)
        w = dict(
            n_mem=n_mem,
            norm_mix=row(norm_mix[l]), w_in=w_in[l].astype(BF16),
            qg=row(jnp.tile(att_q_gain[l], heads) * att_scale), kg=row(jnp.tile(att_k_gain[l], heads)),
            head_mean=head_mean, bc=bc, cre=cre, cim=cim, lam_re=lb_re, lam_im=lb_im,
            ssm_d=row(ssm_d[l]), bias_p=bias_p,
            bias_sc=bias_s[:, :, :att_rows], bias_sn=bias_s[:, :, att_rows:],
            mem_in_norm=row(mem_in_norm[l]), w_mem_k=w_mem_k[l].astype(BF16),
            w_mem_v=w_mem_v[l].astype(BF16), mem_k_gain=row(mem_k_gain[l]),
            w_glu=w_glu[l].astype(BF16), out_norm_ssm=row(out_norm_ssm[l]),
            out_norm_att=row(out_norm_att[l]), w_out=w_out[l].astype(BF16), norm_mem=row(norm_mem[l]),
            w_mem_q=w_mem_q[l].astype(BF16), mem_q_gain=row(mem_q_gain[l]),
            w_mem_o=w_mem_o[l].astype(BF16), norm_ffn=row(norm_ffn[l]),
            w_router=w_router.astype(BF16), router_bias=r_bias,
            exp_w_gate=exp_w_gate[l].astype(BF16), exp_w_up=exp_w_up[l].astype(BF16),
            exp_w_down=exp_w_down[l].astype(BF16),
        )
        xp, xs, p_new, s_new = _layer(
            xp, xs, mem_p,
            cache_attn_k[l].transpose(0, 2, 3, 1).reshape(dec_batch * heads, ATT_HEAD_DIM, att_rows),
            cache_attn_v[l].transpose(0, 2, 3, 1).reshape(dec_batch * heads, ATT_HEAD_DIM, att_rows),
            state_ssm_re[l].reshape(dec_batch, ns), state_ssm_im[l].reshape(dec_batch, ns),
            cache_mem_k[l].reshape(dec_batch * n_mem, d), cache_mem_v[l].reshape(dec_batch * n_mem, d),
            w, (batch, seq, dec_batch, dec_seq, d))
        p_out.append(p_new)
        s_out.append(s_new)

    sdt = state_ssm_re.dtype
    keep = min(PAD_ROWS, seq)
    kv_p = lambda a: a.reshape(batch, heads, ATT_HEAD_DIM, keep).transpose(0, 3, 1, 2)
    kv_s = lambda a: a.reshape(dec_seq, dec_batch, heads, ATT_HEAD_DIM).transpose(1, 0, 2, 3)
    st = lambda a: a.reshape(a.shape[0], n_g, n_p).astype(sdt)
    mkv = lambda a: a.reshape(batch, n_mem, MEM_HEADS, d // MEM_HEADS)
    stack = lambda f, outs, i: jnp.stack([f(o[i]) for o in outs])
    yp = xp.reshape(batch, seq, d)
    ys = xs.reshape(dec_seq, dec_batch, d).transpose(1, 0, 2)
    return (yp, ys,
            stack(kv_p, p_out, 0), stack(kv_p, p_out, 1), stack(st, p_out, 2), stack(st, p_out, 3),
            stack(mkv, p_out, 4), stack(mkv, p_out, 5),
            stack(kv_s, s_out, 0), stack(kv_s, s_out, 1), stack(st, s_out, 2), stack(st, s_out, 3))
```

```python
import functools
import math

import jax
import jax.numpy as jnp
from jax import lax
from jax.experimental import pallas as pl
from jax.experimental.pallas import tpu as pltpu

F32 = jnp.float32
BF16 = jnp.bfloat16

CHUNK = 64
N_PREV_CHUNKS = 8
BAND = (N_PREV_CHUNKS + 1) * CHUNK
PAD_ROWS = N_PREV_CHUNKS * CHUNK
PAIR_ROWS = 2 * CHUNK
PAIR_BAND = BAND + CHUNK
SOFTMAX_ROWS = 32
SSM_GROUP = 16
SSM_STATE = 64
ATT_HEAD_DIM = 64
REL_CLIP = 128
MEM_HEADS = 4
N_GROUPS = 4
EXPERTS_PER_GROUP = 8
EPS = 1e-6
NEG_INF = -1e30

LANES = 128
MXU_N = 256
VMEM_LIMIT = 56 * 1024 * 1024
ROUTER_LANES = LANES
MOE_TILE = 256
SAMPLE_STREAMS = 8
MOE_BUFS = 3
ROUTER_E0 = N_GROUPS
PAIRS_PER_GROUP = EXPERTS_PER_GROUP * (EXPERTS_PER_GROUP - 1) // 2
N_CLASSES = N_GROUPS * PAIRS_PER_GROUP
ROUTE_CLS, ROUTE_WA, ROUTE_WB = 64, 65, 66


def _cparams(sem):
    return pltpu.CompilerParams(dimension_semantics=sem, vmem_limit_bytes=VMEM_LIMIT)


def _rms(x, gain):
    ms = jnp.mean(x * x, axis=-1, keepdims=True)
    return x * lax.rsqrt(ms + EPS) * gain


def _sigmoid(x):
    return 1.0 / (1.0 + jnp.exp(-x))


def _gelu_tanh(x):
    c = math.sqrt(2.0 / math.pi)
    return 0.5 * x * (1.0 + jnp.tanh(c * (x + 0.044715 * (x * x * x))))


def _dot(a, b):
    return jnp.dot(a, b, preferred_element_type=F32)


def _dot_nt(a, b):
    return lax.dot_general(a, b, (((1,), (1,)), ((), ())), preferred_element_type=F32)


def _full(shape):
    n = len(shape)
    return pl.BlockSpec(shape, lambda *_: (0,) * n)


def _in_proj_kernel(x_ref, g_ref, w_ref, qg_ref, kg_ref, hm_ref,
                    u_ref, q_ref, k_ref, v_ref, kk_ref, vk_ref, *, width, nt):
    h = _rms(x_ref[...], g_ref[...]).astype(BF16)
    z = _dot(h, w_ref[...])
    u_ref[...] = z[:, :width].astype(u_ref.dtype)

    def head_norm(a, gain):
        ms = _dot((a * a).astype(BF16), hm_ref[...])
        return a * lax.rsqrt(ms + EPS) * gain

    q = head_norm(z[:, width:2 * width], qg_ref[...])
    k = head_norm(z[:, 2 * width:3 * width], kg_ref[...])
    v = z[:, 3 * width:]
    q_ref[...] = q.astype(BF16)
    k_ref[...] = k.astype(BF16)
    v_ref[...] = v.astype(BF16)

    @pl.when(pl.program_id(1) == nt - 1)
    def _():
        if len(kk_ref.shape) == 3:
            kt, vt = k.T, v.T
            for hd in range(width // ATT_HEAD_DIM):
                rows = slice(hd * ATT_HEAD_DIM, (hd + 1) * ATT_HEAD_DIM)
                kk_ref[hd] = kt[rows, :]
                vk_ref[hd] = vt[rows, :]
        else:
            kk_ref[...] = k
            vk_ref[...] = v


def _in_proj(x2d, gain, w_bf, qg, kg, hm, *, nb, nt, tm, keep_transposed):
    rows, d = x2d.shape
    width = w_bf.shape[1] // 4
    heads = width // ATT_HEAD_DIM
    tok = lambda b, t: (b * nt + t, 0)
    out_shape = (
        jax.ShapeDtypeStruct((rows, width), F32) if keep_transposed
        else jax.ShapeDtypeStruct((nt * tm, nb * width), BF16),
        jax.ShapeDtypeStruct((rows, width), BF16),
        jax.ShapeDtypeStruct((rows, width), BF16),
        jax.ShapeDtypeStruct((rows, width), BF16),
    )
    if keep_transposed:
        kept = jax.ShapeDtypeStruct((nb * heads, ATT_HEAD_DIM, tm), F32)
        keep = pl.BlockSpec((heads, ATT_HEAD_DIM, tm), lambda b, t: (b, 0, 0))
    else:
        kept = jax.ShapeDtypeStruct((nb * tm, width), F32)
        keep = pl.BlockSpec((tm, width), lambda b, t: (b, 0))
    out_shape = out_shape + (kept, kept)
    return pl.pallas_call(
        functools.partial(_in_proj_kernel, width=width, nt=nt),
        grid=(nb, nt),
        in_specs=[pl.BlockSpec((tm, d), tok), _full((1, d)), _full(w_bf.shape),
                  _full((1, width)), _full((1, width)), _full((width, width))],
        out_specs=(pl.BlockSpec((tm, width), tok if keep_transposed else (lambda b, t: (t, b))),
                   pl.BlockSpec((tm, width), tok), pl.BlockSpec((tm, width), tok),
                   pl.BlockSpec((tm, width), tok),
                   keep, keep),
        out_shape=out_shape,
        compiler_params=_cparams(("arbitrary", "arbitrary")),
        name="in_proj",
    )(x2d, gain, w_bf, qg, kg, hm)


def _ssm_chunk(u, bc_ref, cre_ref, cim_ref, lre_ref, lim_ref, d_ref, sre_ref, sim_ref, bu_ref, *, nb, tc, ns):
    n_tiles = 2 * ns // MXU_N
    for j in range(n_tiles):
        slab = (j % (n_tiles // 2)) // 2
        bu_ref[:, j * MXU_N:(j + 1) * MXU_N] = _dot(u[:, slab * LANES:(slab + 1) * LANES], bc_ref[j])

    cw = 8 * 1024 // nb
    for cb in range(ns // cw):
        c0 = cb * cw
        lre = jnp.broadcast_to(lre_ref[:, c0:c0 + cw], (nb, cw))
        lim = jnp.broadcast_to(lim_ref[:, c0:c0 + cw], (nb, cw))

        def step(t, carry, c0=c0, lre=lre, lim=lim):
            sr, si = carry
            r0 = pl.multiple_of(t * nb, nb)
            nr = lre * sr - lim * si + bu_ref[pl.ds(r0, nb), c0:c0 + cw]
            ni = lre * si + lim * sr + bu_ref[pl.ds(r0, nb), ns + c0:ns + c0 + cw]
            bu_ref[pl.ds(r0, nb), c0:c0 + cw] = nr
            bu_ref[pl.ds(r0, nb), ns + c0:ns + c0 + cw] = ni
            return nr, ni

        sr, si = lax.fori_loop(0, tc, step, (sre_ref[:, c0:c0 + cw], sim_ref[:, c0:c0 + cw]), unroll=4)
        sre_ref[:, c0:c0 + cw] = sr
        sim_ref[:, c0:c0 + cw] = si

    width = u.shape[1]
    kt = ns * MXU_N // width
    ys = []
    for n in range(width // MXU_N):
        s_re = bu_ref[:, n * kt:(n + 1) * kt].astype(BF16)
        s_im = bu_ref[:, ns + n * kt:ns + (n + 1) * kt].astype(BF16)
        cols = slice(n * MXU_N, (n + 1) * MXU_N)
        ys.append(_dot(s_re, cre_ref[n]) + _dot(s_im, cim_ref[n]) + d_ref[:, cols] * u[:, cols].astype(F32))
    return jnp.concatenate(ys, axis=1)


def _ssm_kernel(u_ref, bc_ref, cre_ref, cim_ref, lre_ref, lim_ref, d_ref, s0re_ref, s0im_ref,
                y_ref, sre_ref, sim_ref, bu_ref, *, nb, tc, ns):
    @pl.when(pl.program_id(0) == 0)
    def _():
        sre_ref[...] = s0re_ref[...]
        sim_ref[...] = s0im_ref[...]

    y_ref[...] = _ssm_chunk(u_ref[...], bc_ref, cre_ref, cim_ref, lre_ref, lim_ref, d_ref,
                            sre_ref, sim_ref, bu_ref, nb=nb, tc=tc, ns=ns)


def _ssm(u_rows, bc, cre, cim, lre, lim, dskip, s0re, s0im, *, nb, tc):
    rows, width = u_rows.shape
    ns = lre.shape[1]
    r = tc * nb
    return pl.pallas_call(
        functools.partial(_ssm_kernel, nb=nb, tc=tc, ns=ns),
        grid=(rows // r,),
        in_specs=[pl.BlockSpec((r, width), lambda i: (i, 0)), _full(bc.shape), _full(cre.shape),
                  _full(cim.shape), _full((1, ns)), _full((1, ns)), _full((1, width)),
                  _full((nb, ns)), _full((nb, ns))],
        out_specs=(pl.BlockSpec((r, width), lambda i: (i, 0)), _full((nb, ns)), _full((nb, ns))),
        out_shape=(jax.ShapeDtypeStruct((rows, width), F32),
                   jax.ShapeDtypeStruct((nb, ns), F32), jax.ShapeDtypeStruct((nb, ns), F32)),
        scratch_shapes=[pltpu.VMEM((r, 2 * ns), F32)],
        compiler_params=_cparams(("arbitrary",)),
        name="ssm",
    )(u_rows, bc, cre, cim, lre, lim, dskip, s0re, s0im)


def _ssm_tokens_kernel(u_hbm, bc_ref, cre_ref, cim_ref, lre_ref, lim_ref, d_ref,
                       y_hbm, sre_ref, sim_ref, bu_ref, ubuf, ybuf, isem, osem, *, nb, tc, ns, t_len):
    i = pl.program_id(0)
    n_steps = t_len // tc
    slot = i % 2
    width = ubuf.shape[3]

    def copy_in(step, s, b):
        return pltpu.make_async_copy(u_hbm.at[pl.ds(b * t_len + step * tc, tc)], ubuf.at[s, :, b, :], isem.at[s])

    def copy_out(step, s, b):
        return pltpu.make_async_copy(ybuf.at[s, :, b, :], y_hbm.at[pl.ds(b * t_len + step * tc, tc)], osem.at[s])

    @pl.when(i == 0)
    def _():
        sre_ref[...] = jnp.zeros_like(sre_ref)
        sim_ref[...] = jnp.zeros_like(sim_ref)
        for b in range(nb):
            copy_in(0, 0, b).start()

    @pl.when(i + 1 < n_steps)
    def _():
        for b in range(nb):
            copy_in(i + 1, 1 - slot, b).start()

    for b in range(nb):
        copy_in(i, slot, b).wait()
    u = ubuf[slot].reshape(tc * nb, width).astype(BF16)
    y = _ssm_chunk(u, bc_ref, cre_ref, cim_ref, lre_ref, lim_ref, d_ref, sre_ref, sim_ref, bu_ref,
                   nb=nb, tc=tc, ns=ns)

    @pl.when(i >= 2)
    def _():
        for b in range(nb):
            copy_out(i - 2, slot, b).wait()

    ybuf[slot] = y.reshape(tc, nb, width)
    for b in range(nb):
        copy_out(i, slot, b).start()

    @pl.when(i == n_steps - 1)
    def _():
        for b in range(nb):
            copy_out(i, slot, b).wait()
        if n_steps > 1:
            for b in range(nb):
                copy_out(i - 1, 1 - slot, b).wait()


def _ssm_tokens(u_tok, bc, cre, cim, lre, lim, dskip, *, nb, tc):
    rows, width = u_tok.shape
    ns = lre.shape[1]
    t_len = rows // nb
    r = tc * nb
    any_spec = pl.BlockSpec(memory_space=pl.ANY)
    return pl.pallas_call(
        functools.partial(_ssm_tokens_kernel, nb=nb, tc=tc, ns=ns, t_len=t_len),
        grid=(t_len // tc,),
        in_specs=[any_spec, _full(bc.shape), _full(cre.shape), _full(cim.shape), _full((1, ns)),
                  _full((1, ns)), _full((1, width))],
        out_specs=(any_spec, _full((nb, ns)), _full((nb, ns))),
        out_shape=(jax.ShapeDtypeStruct((rows, width), F32),
                   jax.ShapeDtypeStruct((nb, ns), F32), jax.ShapeDtypeStruct((nb, ns), F32)),
        scratch_shapes=[pltpu.VMEM((r, 2 * ns), F32), pltpu.VMEM((2, tc, nb, width), F32),
                        pltpu.VMEM((2, tc, nb, width), F32),
                        pltpu.SemaphoreType.DMA((2,)), pltpu.SemaphoreType.DMA((2,))],
        compiler_params=_cparams(("arbitrary",)),
        name="ssm_tokens",
    )(u_tok, bc, cre, cim, lre, lim, dskip)


def _half_select(shape):
    lane = lax.broadcasted_iota(jnp.int32, shape, 1)
    return lane < ATT_HEAD_DIM


def _head_masks(first):
    m0 = jnp.where(first, 1.0, 0.0).astype(BF16)
    return m0, (1.0 - m0.astype(F32)).astype(BF16)


def _band_prompt_kernel(q_ref, k_ref, v_ref, bias_ref, o_ref, kp_ref, vp_ref, s_scr, p_scr, l_scr, *, t_len):
    width = q_ref.shape[1]
    n_hp = width // LANES
    kp_ref[0:PAD_ROWS, :] = jnp.zeros((PAD_ROWS, width), BF16)
    vp_ref[0:PAD_ROWS, :] = jnp.zeros((PAD_ROWS, width), BF16)
    kp_ref[PAD_ROWS:, :] = k_ref[...]
    vp_ref[PAD_ROWS:, :] = v_ref[...]
    first = _half_select((PAIR_ROWS, LANES))
    head_mask = _head_masks(first)
    col = lax.broadcasted_iota(jnp.int32, (SOFTMAX_ROWS, PAIR_BAND), 1)

    def pair(pi, carry, *, masked):
        r0 = pl.multiple_of(pi * PAIR_ROWS, PAIR_ROWS)
        for hp in range(n_hp):
            lanes = slice(hp * LANES, (hp + 1) * LANES)
            qp = q_ref[pl.ds(r0, PAIR_ROWS), lanes]
            qs = jnp.concatenate([qp * head_mask[0], qp * head_mask[1]], axis=0)
            s_scr[2 * hp * PAIR_ROWS:2 * (hp + 1) * PAIR_ROWS, :] = _dot_nt(
                qs, kp_ref[pl.ds(r0, PAIR_BAND), lanes])
        for h in range(2 * n_hp):
            for rb in range(0, PAIR_ROWS, SOFTMAX_ROWS):
                rows = slice(h * PAIR_ROWS + rb, h * PAIR_ROWS + rb + SOFTMAX_ROWS)
                s = s_scr[rows, :] + bias_ref[h, rb:rb + SOFTMAX_ROWS, :]
                if masked:
                    s = jnp.where(col >= PAD_ROWS - r0, s, NEG_INF)
                p = jnp.exp(s - jnp.max(s, axis=-1, keepdims=True))
                p_scr[rows, :] = p.astype(BF16)
                l_scr[rows, :] = jnp.broadcast_to(1.0 / jnp.sum(p, axis=-1, keepdims=True),
                                                  (SOFTMAX_ROWS, LANES))
        for hp in range(n_hp):
            lanes = slice(hp * LANES, (hp + 1) * LANES)
            rows = slice(2 * hp * PAIR_ROWS, 2 * (hp + 1) * PAIR_ROWS)
            o2 = _dot(p_scr[rows, :], vp_ref[pl.ds(r0, PAIR_BAND), lanes]) * l_scr[rows, :]
            o_ref[pl.ds(r0, PAIR_ROWS), lanes] = jnp.where(
                first, o2[:PAIR_ROWS], o2[PAIR_ROWS:]).astype(BF16)
        return carry

    n_masked = PAD_ROWS // PAIR_ROWS
    lax.fori_loop(0, n_masked, functools.partial(pair, masked=True), 0)
    lax.fori_loop(n_masked, t_len // PAIR_ROWS, functools.partial(pair, masked=False), 0)


def _band_prompt(q, k, v, bias, *, nb, t_len):
    width = q.shape[1]
    heads = width // ATT_HEAD_DIM
    blk = pl.BlockSpec((t_len, width), lambda b: (b, 0))
    return pl.pallas_call(
        functools.partial(_band_prompt_kernel, t_len=t_len),
        grid=(nb,),
        in_specs=[blk, blk, blk, _full(bias.shape)],
        out_specs=blk,
        out_shape=jax.ShapeDtypeStruct(q.shape, BF16),
        scratch_shapes=[pltpu.VMEM((t_len + PAD_ROWS, width), BF16),
                        pltpu.VMEM((t_len + PAD_ROWS, width), BF16),
                        pltpu.VMEM((heads * PAIR_ROWS, PAIR_BAND), F32),
                        pltpu.VMEM((heads * PAIR_ROWS, PAIR_BAND), BF16),
                        pltpu.VMEM((heads * PAIR_ROWS, LANES), F32)],
        compiler_params=_cparams(("arbitrary",)),
        name="band_prompt",
    )(q, k, v, bias)


def _band_sample_kernel(q_ref, k_ref, v_ref, ck_ref, cv_ref, bc_ref, bn_ref, o_ref):
    for h in range(ck_ref.shape[0]):
        cols = slice(h * ATT_HEAD_DIM, (h + 1) * ATT_HEAD_DIM)
        qh = q_ref[:, cols]
        sc = _dot(qh, ck_ref[h].astype(BF16)) + bc_ref[h]
        sn = _dot_nt(qh, k_ref[:, cols]) + bn_ref[h]
        m = jnp.maximum(jnp.max(sc, axis=-1, keepdims=True), jnp.max(sn, axis=-1, keepdims=True))
        pc = jnp.exp(sc - m)
        pn = jnp.exp(sn - m)
        l = jnp.sum(pc, axis=-1, keepdims=True) + jnp.sum(pn, axis=-1, keepdims=True)
        o = _dot_nt(pc.astype(BF16), cv_ref[h].astype(BF16)) + _dot(pn.astype(BF16), v_ref[:, cols])
        o_ref[:, cols] = (o / l).astype(BF16)


def _band_sample(q_tm, k_tm, v_tm, cache_kt, cache_vt, bias_c, bias_n, *, nb, s_len):
    width = q_tm.shape[1] // nb
    heads = cache_kt.shape[0] // nb
    col = pl.BlockSpec((s_len, width), lambda b: (0, b))
    cache = pl.BlockSpec((heads,) + cache_kt.shape[1:], lambda b: (b, 0, 0))
    return pl.pallas_call(
        _band_sample_kernel,
        grid=(nb,),
        in_specs=[col, col, col, cache, cache, _full(bias_c.shape), _full(bias_n.shape)],
        out_specs=col,
        out_shape=jax.ShapeDtypeStruct(q_tm.shape, BF16),
        compiler_params=_cparams(("arbitrary",)),
        name="band_sample",
    )(q_tm, k_tm, v_tm, cache_kt, cache_vt, bias_c, bias_n)


def _mem_kv_kernel(m_ref, g_ref, wk_ref, wv_ref, kg_ref, k_ref, v_ref, kb_ref, vb_ref):
    m = _rms(m_ref[...], g_ref[...]).astype(BF16)
    k = _dot(m, wk_ref[...])
    v = _dot(m, wv_ref[...])
    hd = kg_ref.shape[1]
    for h in range(k.shape[1] // hd):
        cols = slice(h * hd, (h + 1) * hd)
        kh = _rms(k[:, cols], kg_ref[...])
        k_ref[:, h, :] = kh
        v_ref[:, h, :] = v[:, cols]
        kb_ref[:, cols] = kh.astype(BF16)
    vb_ref[...] = v.astype(BF16)


def _mem_kv(mem2d, gain, wk, wv, kgain, *, tm):
    rows, d = mem2d.shape
    hd = kgain.shape[1]
    blk = pl.BlockSpec((tm, d), lambda i: (i, 0))
    hblk = pl.BlockSpec((tm, d // hd, hd), lambda i: (i, 0, 0))
    return pl.pallas_call(
        _mem_kv_kernel,
        grid=(rows // tm,),
        in_specs=[blk, _full((1, d)), _full(wk.shape), _full(wv.shape), _full(kgain.shape)],
        out_specs=(hblk, hblk, blk, blk),
        out_shape=(jax.ShapeDtypeStruct((rows, d // hd, hd), F32),
                   jax.ShapeDtypeStruct((rows, d // hd, hd), F32),
                   jax.ShapeDtypeStruct((rows, d), BF16), jax.ShapeDtypeStruct((rows, d), BF16)),
        compiler_params=_cparams(("arbitrary",)),
        name="mem_kv",
    )(mem2d, gain, wk, wv, kgain)


def _route(logits):
    lane = lax.broadcasted_iota(jnp.int32, logits.shape, 1).astype(F32)
    big = float(ROUTER_LANES)
    is_g = lane < N_GROUPS
    lg = jnp.where(is_g, logits, NEG_INF)
    gmax = jnp.max(lg, axis=-1, keepdims=True)
    p1 = 1.0 / jnp.sum(jnp.where(is_g, jnp.exp(lg - gmax), 0.0), axis=-1, keepdims=True)
    g_idx = jnp.min(jnp.where(lg == gmax, lane, big), axis=-1, keepdims=True)
    lo = ROUTER_E0 + g_idx * EXPERTS_PER_GROUP
    le = jnp.where((lane >= lo) & (lane < lo + EXPERTS_PER_GROUP), logits, NEG_INF)
    v1 = jnp.max(le, axis=-1, keepdims=True)
    i1 = jnp.min(jnp.where(le == v1, lane, big), axis=-1, keepdims=True)
    le2 = jnp.where(lane == i1, NEG_INF, le)
    v2 = jnp.max(le2, axis=-1, keepdims=True)
    i2 = jnp.min(jnp.where(le2 == v2, lane, big), axis=-1, keepdims=True)
    e2 = jnp.exp(v2 - v1)
    w1 = p1 / (1.0 + e2)
    w2 = p1 * e2 / (1.0 + e2)
    gates = jnp.where(lane == i1, w1, 0.0) + jnp.where(lane == i2, w2, 0.0)
    a = jnp.minimum(i1, i2) - lo
    b = jnp.maximum(i1, i2) - lo
    pair = a * (2 * EXPERTS_PER_GROUP - 1 - a) * 0.5 + (b - a - 1.0)
    cls = g_idx * PAIRS_PER_GROUP + pair
    w_a = jnp.where(i1 < i2, w1, w2)
    w_b = jnp.where(i1 < i2, w2, w1)
    return (gates + jnp.where(lane == ROUTE_CLS, cls, 0.0) + jnp.where(lane == ROUTE_WA, w_a, 0.0)
            + jnp.where(lane == ROUTE_WB, w_b, 0.0))


def _mix_mem_kernel(x_ref, ys_ref, ya_ref, mk_ref, mv_ref, wglu_ref, ons_ref, ona_ref, wout_ref,
                    nmem_ref, wq_ref, qg_ref, wo_ref, nffn_ref, wr_ref, rb_ref,
                    *outs, routed, nbat):
    if routed:
        x2_ref, o_scr = outs
    else:
        x2_ref, xn_ref, gate_ref, o_scr = outs
    tm = x_ref.shape[0]
    d = x_ref.shape[1] // nbat
    width = ys_ref.shape[1] // nbat
    n_mem = mk_ref.shape[0] // nbat

    def stacked(ref, w):
        return jnp.concatenate([ref[:, j * w:(j + 1) * w] for j in range(nbat)], axis=0)

    g = _dot(_gelu_tanh(stacked(ys_ref, width)).astype(BF16), wglu_ref[...])
    y_s = g[:, :width] * _sigmoid(g[:, width:])
    cat_s = _rms(y_s, ons_ref[...]).astype(BF16)
    cat_a = _rms(stacked(ya_ref, width).astype(F32), ona_ref[...]).astype(BF16)
    x1 = stacked(x_ref, d) + _dot(cat_s, wout_ref[0:width, :]) + _dot(cat_a, wout_ref[width:, :])

    q = _dot(_rms(x1, nmem_ref[...]).astype(BF16), wq_ref[...])
    hd = qg_ref.shape[1]
    scale = hd ** -0.5
    for j in range(nbat):
        rows = slice(j * tm, (j + 1) * tm)
        mem = slice(j * n_mem, (j + 1) * n_mem)
        for h in range(d // hd):
            cols = slice(h * hd, (h + 1) * hd)
            qh = _rms(q[rows, cols], qg_ref[...]).astype(BF16)
            s = _dot_nt(qh, mk_ref[mem, cols].astype(BF16)) * scale
            p = jnp.exp(s - jnp.max(s, axis=-1, keepdims=True))
            l = jnp.sum(p, axis=-1, keepdims=True)
            o_scr[rows, cols] = (_dot(p.astype(BF16), mv_ref[mem, cols].astype(BF16)) / l).astype(BF16)
    acc = x1 + _dot(o_scr[...], wo_ref[...])

    xn = _rms(acc, nffn_ref[...])
    xh = xn.astype(BF16)
    logits = _dot(xh, wr_ref[...]) + rb_ref[...]
    record = _route(logits)
    if routed:
        x2_ref[:, :d] = acc
        x2_ref[:, d:] = record
    else:
        for j in range(nbat):
            rows = slice(j * tm, (j + 1) * tm)
            x2_ref[:, j * d:(j + 1) * d] = acc[rows]
            xn_ref[:, j * d:(j + 1) * d] = xh[rows]
            gate_ref[:, j * ROUTER_LANES:(j + 1) * ROUTER_LANES] = record[rows]


def _mix_mem(x, ys, ya, mk, mv, weights, *, grid, tm, row_map, ssm_map, mem_map, routed, nbat=1):
    d = weights["w_out"].shape[1]
    width = weights["w_glu"].shape[0]
    names = ("w_glu", "out_norm_ssm", "out_norm_att", "w_out", "norm_mem", "w_mem_q", "mem_q_gain",
             "w_mem_o", "norm_ffn", "w_router", "router_bias")
    ws = [weights[n] for n in names]
    mem_rows = weights["n_mem"]
    xspec = pl.BlockSpec((tm, nbat * d), row_map)
    hspec = pl.BlockSpec((tm, nbat * width), row_map)
    sspec = pl.BlockSpec((tm, nbat * width), ssm_map)
    mspec = pl.BlockSpec((nbat * mem_rows, d), mem_map)
    n_col = x.shape[1] // d
    if routed:
        assert nbat == 1
        out_specs = pl.BlockSpec((tm, d + ROUTER_LANES), row_map)
        out_shape = jax.ShapeDtypeStruct((x.shape[0], n_col * (d + ROUTER_LANES)), F32)
    else:
        out_specs = (xspec, xspec, pl.BlockSpec((tm, nbat * ROUTER_LANES), row_map))
        out_shape = (jax.ShapeDtypeStruct(x.shape, F32), jax.ShapeDtypeStruct(x.shape, BF16),
                     jax.ShapeDtypeStruct((x.shape[0], n_col * ROUTER_LANES), F32))
    return pl.pallas_call(
        functools.partial(_mix_mem_kernel, routed=routed, nbat=nbat),
        grid=grid,
        in_specs=[xspec, sspec, hspec, mspec, mspec] + [_full(w.shape) for w in ws],
        out_specs=out_specs,
        out_shape=out_shape,
        scratch_shapes=[pltpu.VMEM((nbat * tm, d), BF16)],
        compiler_params=_cparams(("arbitrary",) * len(grid)),
        name="mix_mem",
    )(x, ys, ya, mk, mv, *ws)


def _moe_kernel(xn_ref, x2_ref, gate_ref, wg_ref, wu_ref, wd_ref, o_ref):
    e = pl.program_id(1)

    @pl.when(e == 0)
    def _():
        o_ref[...] = x2_ref[...]

    gates = gate_ref[...]
    lane = lax.broadcasted_iota(jnp.int32, gates.shape, 1)
    ge = jnp.sum(jnp.where(lane == e + ROUTER_E0, gates, 0.0), axis=-1, keepdims=True)
    xn = xn_ref[...]
    a = _dot(xn, wg_ref[...])
    h = a * _sigmoid(a) * _dot(xn, wu_ref[...])
    o_ref[...] += ge * _dot(h.astype(BF16), wd_ref[...])


def _moe(xn, x2, gates, wg, wu, wd, *, tm):
    rows, d = xn.shape
    n_exp, _, dff = wg.shape
    row = lambda i, e: (i, 0)
    return pl.pallas_call(
        _moe_kernel,
        grid=(rows // tm, n_exp),
        in_specs=[pl.BlockSpec((tm, d), row), pl.BlockSpec((tm, d), row),
                  pl.BlockSpec((tm, ROUTER_LANES), row),
                  pl.BlockSpec((None, d, dff), lambda i, e: (e, 0, 0)),
                  pl.BlockSpec((None, d, dff), lambda i, e: (e, 0, 0)),
                  pl.BlockSpec((None, dff, d), lambda i, e: (e, 0, 0))],
        out_specs=pl.BlockSpec((tm, d), row),
        out_shape=jax.ShapeDtypeStruct((rows, d), F32),
        compiler_params=_cparams(("arbitrary", "arbitrary")),
        name="moe",
    )(xn, x2, gates, wg, wu, wd)


def _moe_routed_kernel(tile_ref, ea_ref, eb_ref, lo_ref, hi_ref, valid_ref, tok_ref,
                       x_hbm, nffn_ref, wga_ref, wua_ref, wda_ref, wgb_ref, wub_ref, wdb_ref,
                       y_hbm, *scratch, tm, n_tiles):
    nb = MOE_BUFS
    xbuf, obuf = scratch[:nb], scratch[nb:2 * nb]
    acc_ref, gsem, ssem = scratch[2 * nb:]
    w = pl.program_id(0)
    t = tile_ref[w]
    lo = lo_ref[w]
    hi = hi_ref[w]
    valid = valid_ref[w] == 1
    d = y_hbm.shape[1]

    def row_copy_in(tile, s, r):
        tok = tok_ref[tile * tm + r]
        return pltpu.make_async_copy(x_hbm.at[pl.ds(tok, 1)], xbuf[s].at[pl.ds(r, 1)], gsem.at[s])

    def row_copy_out(tile, s, r):
        tok = tok_ref[tile * tm + r]
        return pltpu.make_async_copy(obuf[s].at[pl.ds(r, 1)], y_hbm.at[pl.ds(tok, 1)], ssem.at[s])

    def start_gather(tile, s):
        for r in range(tm):
            row_copy_in(tile, s, r).start()

    def start_scatter(tile, s):
        for r in range(tm):
            row_copy_out(tile, s, r).start()

    def wait_gather(s):
        pltpu.make_async_copy(x_hbm.at[pl.ds(0, tm)], xbuf[s], gsem.at[s]).wait()

    def wait_scatter(s):
        pltpu.make_async_copy(obuf[s], y_hbm.at[pl.ds(0, tm)], ssem.at[s]).wait()

    def compute(s, first):
        xe = xbuf[s][...]
        xn = _rms(xe[:, :d], nffn_ref[...]).astype(BF16)
        row = lax.broadcasted_iota(jnp.int32, (tm, 1), 0)
        inseg = jnp.logical_and(row >= lo, row < hi)
        w_a = jnp.where(inseg, xe[:, d + ROUTE_WA:d + ROUTE_WA + 1], 0.0)
        w_b = jnp.where(inseg, xe[:, d + ROUTE_WB:d + ROUTE_WB + 1], 0.0)
        a = _dot(xn, wga_ref[...])
        h_a = (a * _sigmoid(a) * _dot(xn, wua_ref[...])).astype(BF16)
        b = _dot(xn, wgb_ref[...])
        h_b = (b * _sigmoid(b) * _dot(xn, wub_ref[...])).astype(BF16)
        upd = w_a * _dot(h_a, wda_ref[...]) + w_b * _dot(h_b, wdb_ref[...])
        if first:
            acc_ref[...] = upd
        else:
            acc_ref[...] += upd

    def when(*conds):
        c = conds[0]
        for extra in conds[1:]:
            c = jnp.logical_and(c, extra)
        return pl.when(c)

    ahead = nb - 1

    @pl.when(w == 0)
    def _():
        for k in range(ahead):
            start_gather(k, k)

    is_first = jnp.logical_and(valid, lo == 0)
    is_last = jnp.logical_and(valid, hi == tm)
    last_tile = n_tiles - 1
    for s in range(nb):
        mine = t % nb == s
        nxt, prv = (s + ahead) % nb, (s - 1) % nb

        if s == 0:
            @when(is_first, t == 0)
            def _(nxt=nxt):
                wait_gather(0)
                start_gather(ahead, nxt)
                compute(0, True)

        @when(is_first, mine, t > 0, t + ahead <= last_tile)
        def _(s=s, nxt=nxt, prv=prv):
            wait_gather(s)
            start_gather(t + ahead, nxt)
            start_scatter(t - 1, prv)
            compute(s, True)

        @when(is_first, mine, t + ahead > last_tile)
        def _(s=s, prv=prv):
            wait_gather(s)
            start_scatter(t - 1, prv)
            compute(s, True)

        @when(valid, mine, lo > 0)
        def _(s=s):
            compute(s, False)

        @when(is_last, mine)
        def _(s=s):
            @pl.when(t >= nb)
            def _():
                wait_scatter(s)

            obuf[s][...] = xbuf[s][:, :d] + acc_ref[...]

            if s == last_tile % nb:
                @pl.when(t == last_tile)
                def _():
                    start_scatter(t, s)
                    for k in range(nb):
                        wait_scatter(k)


def _moe_routed(x2ext, plan, nffn, wg, wu, wd, *, tm):
    rows, de = x2ext.shape
    d = de - ROUTER_LANES
    n_exp, _, dff = wg.shape
    n_tiles = rows // tm
    assert rows % tm == 0 and n_tiles > 2 * MOE_BUFS
    n_items = plan[0].shape[0]
    ea = lambda w, tile, ea_, eb_, *_: (ea_[w], 0, 0)
    eb = lambda w, tile, ea_, eb_, *_: (eb_[w], 0, 0)
    grid_spec = pltpu.PrefetchScalarGridSpec(
        num_scalar_prefetch=len(plan),
        grid=(n_items,),
        in_specs=[pl.BlockSpec(memory_space=pl.ANY),
                  pl.BlockSpec((1, d), lambda w, *_: (0, 0)),
                  pl.BlockSpec((None, d, dff), ea), pl.BlockSpec((None, d, dff), ea),
                  pl.BlockSpec((None, dff, d), ea),
                  pl.BlockSpec((None, d, dff), eb), pl.BlockSpec((None, d, dff), eb),
                  pl.BlockSpec((None, dff, d), eb)],
        out_specs=pl.BlockSpec(memory_space=pl.ANY),
        scratch_shapes=([pltpu.VMEM((tm, de), F32)] * MOE_BUFS + [pltpu.VMEM((tm, d), F32)] * MOE_BUFS
                        + [pltpu.VMEM((tm, d), F32),
                           pltpu.SemaphoreType.DMA((MOE_BUFS,)), pltpu.SemaphoreType.DMA((MOE_BUFS,))]),
    )
    return pl.pallas_call(
        functools.partial(_moe_routed_kernel, tm=tm, n_tiles=n_tiles),
        grid_spec=grid_spec,
        out_shape=jax.ShapeDtypeStruct((rows, d), F32),
        compiler_params=_cparams(("arbitrary",)),
        name="moe_routed",
    )(*plan, x2ext, nffn, wg, wu, wd, wg, wu, wd)


def _route_plan(cls, *, tm):
    n = cls.shape[0]
    n_tiles = n // tm
    order = jnp.argsort(cls).astype(jnp.int32)
    classes = jnp.arange(N_CLASSES, dtype=jnp.int32)
    class_end = jnp.sum((cls[:, None] <= classes[None, :]).astype(jnp.int32), axis=0)
    class_start = jnp.concatenate([jnp.zeros((1,), jnp.int32), class_end[:-1]])
    bounds = jnp.concatenate([jnp.arange(n_tiles, dtype=jnp.int32) * tm,
                              jnp.where(class_end > class_start, class_start, n)])
    n_items = bounds.shape[0]
    idx = jnp.arange(n_items, dtype=jnp.int32)
    before = jnp.logical_or(bounds[None, :] < bounds[:, None],
                            jnp.logical_and(bounds[None, :] == bounds[:, None], idx[None, :] < idx[:, None]))
    rank = jnp.sum(before.astype(jnp.int32), axis=1)
    start = jnp.sum(jnp.where(rank[:, None] == idx[None, :], bounds[:, None], 0), axis=0)
    stop = jnp.concatenate([start[1:], jnp.full((1,), n, jnp.int32)])
    valid = stop > start
    tile = jnp.minimum(start // tm, n_tiles - 1)
    lo = start - tile * tm
    hi = stop - tile * tm
    c = jnp.minimum(jnp.sum((class_end[None, :] <= start[:, None]).astype(jnp.int32), axis=1),
                    N_CLASSES - 1)
    g = c // PAIRS_PER_GROUP
    pair = c % PAIRS_PER_GROUP
    a = jnp.zeros_like(pair)
    for k in range(1, EXPERTS_PER_GROUP - 1):
        a = a + (pair >= k * (2 * EXPERTS_PER_GROUP - 1 - k) // 2).astype(jnp.int32)
    b = pair - a * (2 * EXPERTS_PER_GROUP - 1 - a) // 2 + a + 1
    e_a = g * EXPERTS_PER_GROUP + a
    e_b = g * EXPERTS_PER_GROUP + b
    i32 = lambda v: v.astype(jnp.int32)
    return (i32(tile), i32(e_a), i32(e_b), i32(lo), i32(hi), i32(valid), order)


def _ssm_params(lam_re, lam_im, log_step, b_re, b_im, c_re, c_im):
    n_g, n_p = lam_re.shape
    step = jnp.exp(log_step.astype(F32))[:, None]
    mag = jnp.exp(lam_re * step)
    lb_re = mag * jnp.cos(lam_im * step)
    lb_im = mag * jnp.sin(lam_im * step)
    den = lam_re * lam_re + lam_im * lam_im
    f_re = ((lb_re - 1.0) * lam_re + lb_im * lam_im) / den
    f_im = (lb_im * lam_re - (lb_re - 1.0) * lam_im) / den
    bb_re = f_re[..., None] * b_re - f_im[..., None] * b_im
    bb_im = f_re[..., None] * b_im + f_im[..., None] * b_re
    eye = jnp.eye(n_g, dtype=F32)
    ns = n_g * n_p
    width = n_g * SSM_GROUP
    b_full = jnp.concatenate(
        [jnp.einsum("hg,gpc->hcgp", eye, bb_re).reshape(width, ns),
         jnp.einsum("hg,gpc->hcgp", eye, bb_im).reshape(width, ns)], axis=1)
    n_tiles = 2 * ns // MXU_N
    bc = jnp.stack([
        b_full[((j % (n_tiles // 2)) // 2) * LANES:((j % (n_tiles // 2)) // 2 + 1) * LANES,
               j * MXU_N:(j + 1) * MXU_N] for j in range(n_tiles)]).astype(BF16)
    c_full_re = jnp.einsum("gh,gcp->gphc", eye, c_re).reshape(ns, width)
    c_full_im = -jnp.einsum("gh,gcp->gphc", eye, c_im).reshape(ns, width)
    kt = ns * MXU_N // width
    tiles = range(width // MXU_N)
    cre = jnp.stack([c_full_re[n * kt:(n + 1) * kt, n * MXU_N:(n + 1) * MXU_N] for n in tiles]).astype(BF16)
    cim = jnp.stack([c_full_im[n * kt:(n + 1) * kt, n * MXU_N:(n + 1) * MXU_N] for n in tiles]).astype(BF16)
    return bc, cre, cim, lb_re.reshape(1, ns), lb_im.reshape(1, ns)


def _rel_bias(rel_bias, q0, n_q, n_k):
    n_r = n_q + n_k - 1
    dist = q0 + n_q - 1 - jnp.arange(n_r)
    r = rel_bias.astype(F32)[:, jnp.clip(dist, -REL_CLIP, REL_CLIP) + REL_CLIP]
    r = jnp.pad(r, ((0, 0), (0, 1)))
    rows = jnp.tile(r, (1, n_q))[:, :n_q * n_r].reshape(-1, n_q, n_r)
    return rows[:, :, n_q - 1:n_q - 1 + n_k]


def _layer(xp, xs, mem_p, ck, cv, s_re, s_im, cmk, cmv, w, dims):
    batch, seq, dec_batch, dec_seq, d = dims
    width = w["w_in"].shape[1] // 4
    ns = w["lam_re"].shape[1]
    n_mem = w["n_mem"]
    tm_p = 512
    nt_p = seq // tm_p
    n_dec = dec_batch * dec_seq

    u, q, k, v, kk, vk = _in_proj(xp, w["norm_mix"], w["w_in"], w["qg"], w["kg"], w["head_mean"],
                                  nb=batch, nt=nt_p, tm=tm_p, keep_transposed=True)
    y_ssm, pre, pim = _ssm_tokens(u, w["bc"], w["cre"], w["cim"], w["lam_re"], w["lam_im"], w["ssm_d"],
                                  nb=batch, tc=64)
    y_att = _band_prompt(q, k, v, w["bias_p"], nb=batch, t_len=seq)
    mk, mv, mkb, mvb = _mem_kv(mem_p, w["mem_in_norm"], w["w_mem_k"], w["w_mem_v"], w["mem_k_gain"], tm=512)
    tm_d = 512
    nt_d = seq // tm_d
    x2ext = _mix_mem(xp, y_ssm, y_att, mkb, mvb, w,
                     grid=(batch, nt_d), tm=tm_d,
                     row_map=lambda b, t: (b * nt_d + t, 0), ssm_map=lambda b, t: (b * nt_d + t, 0),
                     mem_map=lambda b, t: (b, 0), routed=True)
    plan = _route_plan(x2ext[:, d + ROUTE_CLS].astype(jnp.int32), tm=MOE_TILE)
    yp = _moe_routed(x2ext, plan, w["norm_ffn"], w["exp_w_gate"], w["exp_w_up"], w["exp_w_down"],
                     tm=MOE_TILE)

    us, qs, ks, vs, kks, vks = _in_proj(xs, w["norm_mix"], w["w_in"], w["qg"], w["kg"], w["head_mean"],
                                        nb=1, nt=1, tm=n_dec, keep_transposed=False)
    ys_ssm, sre, sim = _ssm(us, w["bc"], w["cre"], w["cim"], w["lam_re"], w["lam_im"], w["ssm_d"],
                            s_re, s_im, nb=dec_batch, tc=dec_seq)
    tmv = lambda a: a.reshape(dec_seq, dec_batch * a.shape[1])
    ys_att = _band_sample(tmv(qs), tmv(ks), tmv(vs), ck, cv, w["bias_sc"], w["bias_sn"],
                          nb=dec_batch, s_len=dec_seq)
    x2s, xns, gates_s = _mix_mem(tmv(xs), tmv(ys_ssm), ys_att, cmk, cmv, w,
                                 grid=(dec_batch // SAMPLE_STREAMS,), tm=dec_seq,
                                 row_map=lambda b: (0, b), ssm_map=lambda b: (0, b),
                                 mem_map=lambda b: (b, 0), routed=False, nbat=SAMPLE_STREAMS)
    ys = _moe(xns.reshape(n_dec, d), x2s.reshape(n_dec, d), gates_s.reshape(n_dec, ROUTER_LANES),
              w["exp_w_gate"], w["exp_w_up"], w["exp_w_down"], tm=n_dec)
    return yp, ys, (kk, vk, pre, pim, mk, mv), (kks, vks, sre, sim)


def kernel(x_prompt, x_sample, mem_prompt, cache_attn_k, cache_attn_v, state_ssm_re, state_ssm_im, cache_mem_k, cache_mem_v, norm_mix, w_in, ssm_lambda_re, ssm_lambda_im, ssm_log_step, ssm_b_re, ssm_b_im, ssm_c_re, ssm_c_im, ssm_d, w_glu, att_q_gain, att_k_gain, att_rel_bias, out_norm_ssm, out_norm_att, w_out, norm_mem, mem_in_norm, w_mem_q, w_mem_k, w_mem_v, w_mem_o, mem_q_gain, mem_k_gain, norm_ffn, router_g_w, router_g_b, router_e_w, router_e_b, exp_w_gate, exp_w_up, exp_w_down):
    depth = norm_mix.shape[0]
    batch, seq, d = x_prompt.shape
    dec_batch, dec_seq, _ = x_sample.shape
    n_mem = mem_prompt.shape[1]
    att_rows = cache_attn_k.shape[2]
    n_g, n_p = ssm_lambda_re.shape[1:]
    width = n_g * SSM_GROUP
    heads = width // ATT_HEAD_DIM
    ns = n_g * n_p
    assert seq % 512 == 0 and att_rows == PAD_ROWS and seq >= PAD_ROWS
    assert (dec_batch * dec_seq) % 8 == 0 and dec_seq % 16 == 0

    xp = x_prompt.reshape(batch * seq, d)
    xs = x_sample.transpose(1, 0, 2).reshape(dec_seq * dec_batch, d)
    mem_p = mem_prompt.reshape(batch * n_mem, d)
    row = lambda a: a.reshape(1, -1).astype(F32)
    head_mean = jnp.kron(jnp.eye(heads, dtype=F32),
                         jnp.full((ATT_HEAD_DIM, ATT_HEAD_DIM), 1.0 / ATT_HEAD_DIM, F32)).astype(BF16)
    pr = jnp.arange(PAIR_ROWS)[:, None]
    pc = jnp.arange(PAIR_BAND)[None, :]
    pair_ok = jnp.where(pr < CHUNK, pc < BAND, pc >= CHUNK)
    att_scale = ATT_HEAD_DIM ** -0.5

    p_out, s_out = [], []
    for l in range(depth):
        bc, cre, cim, lb_re, lb_im = _ssm_params(ssm_lambda_re[l], ssm_lambda_im[l], ssm_log_step[l],
                                                 ssm_b_re[l], ssm_b_im[l], ssm_c_re[l], ssm_c_im[l])
        w_router = jnp.concatenate(
            [router_g_w[l], router_e_w[l].transpose(1, 0, 2).reshape(d, N_GROUPS * EXPERTS_PER_GROUP),
             jnp.zeros((d, ROUTER_LANES - N_GROUPS * (1 + EXPERTS_PER_GROUP)), F32)], axis=1)
        r_bias = jnp.concatenate(
            [router_g_b[l], router_e_b[l].reshape(-1),
             jnp.zeros((ROUTER_LANES - N_GROUPS * (1 + EXPERTS_PER_GROUP),), F32)]).reshape(1, ROUTER_LANES)
        bias_s = _rel_bias(att_rel_bias[l], att_rows, dec_seq, att_rows + dec_seq)
        bias_p = jnp.where(pair_ok, _rel_bias(att_rel_bias[l], PAD_ROWS, PAIR_ROWS, PAIR_BAND), NEG_INF)
        w = dict(
            n_mem=n_mem,
            norm_mix=row(norm_mix[l]), w_in=w_in[l].astype(BF16),
            qg=row(jnp.tile(att_q_gain[l], heads) * att_scale), kg=row(jnp.tile(att_k_gain[l], heads)),
            head_mean=head_mean, bc=bc, cre=cre, cim=cim, lam_re=lb_re, lam_im=lb_im,
            ssm_d=row(ssm_d[l]), bias_p=bias_p,
            bias_sc=bias_s[:, :, :att_rows], bias_sn=bias_s[:, :, att_rows:],
            mem_in_norm=row(mem_in_norm[l]), w_mem_k=w_mem_k[l].astype(BF16),
            w_mem_v=w_mem_v[l].astype(BF16), mem_k_gain=row(mem_k_gain[l]),
            w_glu=w_glu[l].astype(BF16), out_norm_ssm=row(out_norm_ssm[l]),
            out_norm_att=row(out_norm_att[l]), w_out=w_out[l].astype(BF16), norm_mem=row(norm_mem[l]),
            w_mem_q=w_mem_q[l].astype(BF16), mem_q_gain=row(mem_q_gain[l]),
            w_mem_o=w_mem_o[l].astype(BF16), norm_ffn=row(norm_ffn[l]),
            w_router=w_router.astype(BF16), router_bias=r_bias,
            exp_w_gate=exp_w_gate[l].astype(BF16), exp_w_up=exp_w_up[l].astype(BF16),
            exp_w_down=exp_w_down[l].astype(BF16),
        )
        xp, xs, p_new, s_new = _layer(
            xp, xs, mem_p,
            cache_attn_k[l].transpose(0, 2, 3, 1).reshape(dec_batch * heads, ATT_HEAD_DIM, att_rows),
            cache_attn_v[l].transpose(0, 2, 3, 1).reshape(dec_batch * heads, ATT_HEAD_DIM, att_rows),
            state_ssm_re[l].reshape(dec_batch, ns), state_ssm_im[l].reshape(dec_batch, ns),
            cache_mem_k[l].reshape(dec_batch * n_mem, d), cache_mem_v[l].reshape(dec_batch * n_mem, d),
            w, (batch, seq, dec_batch, dec_seq, d))
        p_out.append(p_new)
        s_out.append(s_new)

    sdt = state_ssm_re.dtype
    keep = min(PAD_ROWS, seq)
    kv_p = lambda a: a.reshape(batch, heads, ATT_HEAD_DIM, keep).transpose(0, 3, 1, 2)
    kv_s = lambda a: a.reshape(dec_seq, dec_batch, heads, ATT_HEAD_DIM).transpose(1, 0, 2, 3)
    st = lambda a: a.reshape(a.shape[0], n_g, n_p).astype(sdt)
    mkv = lambda a: a.reshape(batch, n_mem, MEM_HEADS, d // MEM_HEADS)
    stack = lambda f, outs, i: jnp.stack([f(o[i]) for o in outs])
    yp = xp.reshape(batch, seq, d)
    ys = xs.reshape(dec_seq, dec_batch, d).transpose(1, 0, 2)
    return (yp, ys,
            stack(kv_p, p_out, 0), stack(kv_p, p_out, 1), stack(st, p_out, 2), stack(st, p_out, 3),
            stack(mkv, p_out, 4), stack(mkv, p_out, 5),
            stack(kv_s, s_out, 0), stack(kv_s, s_out, 1), stack(st, s_out, 2), stack(st, s_out, 3))
```
